```python
import jax, jax.numpy as jnp
from jax import lax
import numpy as np

D_MODEL = 1024
BATCH = 8
SEQ = 4096
DEPTH = 1
DEC_BATCH = 8
DEC_SEQ = 64
PAST_LEN = 1024

CHUNK = 64
N_META = 16
Q_BLOCK = 128
FOX_HEADS = 8
FOX_HEAD_DIM = 64
FOX_WIDTH = FOX_HEADS * FOX_HEAD_DIM
ML_HEADS = 4
ML_HEAD_DIM = 128
ML_WIDTH = ML_HEADS * ML_HEAD_DIM
CONV_WIDTH = 4
D_FF = ((-(-8 * D_MODEL // 3)) + 255) // 256 * 256
RMS_EPS = 1e-6
PAD_LOG_GATE = -1e30
SPLITS = (FOX_WIDTH, FOX_WIDTH, FOX_WIDTH, FOX_HEADS, ML_WIDTH, ML_WIDTH, ML_WIDTH, ML_HEADS, ML_HEADS, D_MODEL, D_MODEL)
SPLIT_IDX = tuple(int(i) for i in np.cumsum(SPLITS)[:-1])
D_IN = sum(SPLITS)

kernel_name = 'hybrid_fox_mlstm_stream_step'


def rmsnorm(x, g):
    xf = x.astype(jnp.float32)
    y = xf * lax.rsqrt(jnp.mean(xf * xf, axis=-1, keepdims=True) + RMS_EPS)
    return (y * g.astype(jnp.float32)).astype(x.dtype)


def causal_conv(x, prev, w, b):
    T = x.shape[1]
    xc = jnp.concatenate([prev.astype(x.dtype), x], axis=1)
    y = b + sum(xc[:, i:i + T] * w[i] for i in range(CONV_WIDTH))
    return y, xc[:, T:]


def fox_block(q, Fq, qpos, k, v, Fk):
    s = jnp.einsum('bqhd,bkhd->bhqk', q, k).astype(jnp.float32) * (FOX_HEAD_DIM ** -0.5)
    bias = jnp.transpose(Fq, (0, 2, 1))[:, :, :, None] - jnp.transpose(Fk, (0, 2, 1))[:, :, None, :]
    mask = jnp.arange(k.shape[1])[None, :] <= qpos[:, None]
    s = jnp.where(mask[None, None], s + bias, -jnp.inf)
    p = jax.nn.softmax(s, axis=-1).astype(v.dtype)
    return jnp.einsum('bhqk,bkhd->bqhd', p, v)


def fox_prompt(q, k, v, F):
    B, L, H, Dh = q.shape
    nb = -(-L // Q_BLOCK)
    pad = nb * Q_BLOCK - L
    qb = jnp.pad(q, ((0, 0), (0, pad), (0, 0), (0, 0))).reshape(B, nb, Q_BLOCK, H, Dh).swapaxes(0, 1)
    Fb = jnp.pad(F, ((0, 0), (0, pad), (0, 0))).reshape(B, nb, Q_BLOCK, H).swapaxes(0, 1)
    starts = jnp.arange(nb) * Q_BLOCK

    def one(args):
        qi, Fi, s0 = args
        return fox_block(qi, Fi, s0 + jnp.arange(Q_BLOCK), k, v, F)

    o = lax.map(one, (qb, Fb, starts))
    return o.swapaxes(0, 1).reshape(B, nb * Q_BLOCK, H, Dh)[:, :L]


def mlstm_chunk(state, inp):
    C, n, m = state
    q, k, v, ig, lf = inp
    T = q.shape[1]
    b = jnp.cumsum(lf, axis=1)
    D = b[:, :, None, :] - b[:, None, :, :] + ig[:, None, :, :]
    tril = jnp.tril(jnp.ones((T, T), dtype=bool))
    D = jnp.where(tril[None, :, :, None], D, -jnp.inf)
    m_inter = b + m[:, None, :]
    m_t = jnp.maximum(m_inter, jnp.max(D, axis=2))
    w_intra = jnp.exp(D - m_t[:, :, None, :])
    w_inter = jnp.exp(m_inter - m_t)
    A = w_intra * jnp.einsum('bthd,bshd->btsh', q, k)
    num = jnp.einsum('btsh,bshd->bthd', A, v) + w_inter[..., None] * jnp.einsum('bthk,bhkv->bthv', q, C)
    den = jnp.sum(A, axis=2) + w_inter * jnp.einsum('bthk,bhk->bth', q, n)
    h = num / jnp.maximum(jnp.abs(den), jnp.exp(-m_t))[..., None]
    bL = b[:, -1]
    g = bL[:, None, :] - b + ig
    m_new = jnp.maximum(bL + m, jnp.max(g, axis=1))
    wk = jnp.exp(g - m_new[:, None, :])
    decay = jnp.exp(bL + m - m_new)
    C_new = decay[..., None, None] * C + jnp.einsum('bth,bthk,bthv->bhkv', wk, k, v)
    n_new = decay[..., None] * n + jnp.einsum('bth,bthk->bhk', wk, k)
    return (C_new, n_new, m_new), h


def mlstm_prompt(q, k, v, ig, lf):
    B, L = q.shape[:2]
    pad = (-L) % CHUNK

    def padt(a, val):
        return jnp.pad(a, ((0, 0), (pad, 0)) + ((0, 0),) * (a.ndim - 2), constant_values=val)

    q, k, v, lf = padt(q, 0.0), padt(k, 0.0), padt(v, 0.0), padt(lf, 0.0)
    ig = padt(ig, PAD_LOG_GATE)
    nc = (L + pad) // CHUNK

    def to_chunks(a):
        return a.reshape((B, nc, CHUNK) + a.shape[2:]).swapaxes(0, 1)

    init = (jnp.zeros((B, ML_HEADS, ML_HEAD_DIM, ML_HEAD_DIM), jnp.float32),
            jnp.zeros((B, ML_HEADS, ML_HEAD_DIM), jnp.float32),
            jnp.zeros((B, ML_HEADS), jnp.float32))
    state, h = lax.scan(mlstm_chunk, init, (to_chunks(q), to_chunks(k), to_chunks(v), to_chunks(ig), to_chunks(lf)))
    h = h.swapaxes(0, 1).reshape(B, L + pad, ML_HEADS, ML_HEAD_DIM)[:, pad:]
    return state, h


def token_mixers(h, p, cache):
    B, T, _ = h.shape
    f32 = jnp.float32
    z = h @ p['w_in'] + p['b_in']
    fq, fk, fv, ff, mx, mv, mo, mi, mf, ga, gb = jnp.split(z, SPLIT_IDX, axis=-1)
    fq = rmsnorm(fq.reshape(B, T, FOX_HEADS, FOX_HEAD_DIM), p['g_fq'])
    fk = rmsnorm(fk.reshape(B, T, FOX_HEADS, FOX_HEAD_DIM), p['g_fk'])
    fv = fv.reshape(B, T, FOX_HEADS, FOX_HEAD_DIM)
    logf = jax.nn.log_sigmoid(ff.astype(f32))
    if cache is None:
        F = jnp.cumsum(logf, axis=1)
        a = fox_prompt(fq, fk, fv, F)
        conv_prev = jnp.zeros((B, CONV_WIDTH - 1, ML_WIDTH), h.dtype)
    else:
        P = cache['fox_k'].shape[1]
        k_all = jnp.concatenate([cache['fox_k'].astype(fk.dtype), fk], axis=1)
        v_all = jnp.concatenate([cache['fox_v'].astype(fv.dtype), fv], axis=1)
        F = jnp.cumsum(jnp.concatenate([cache['fox_logf'].astype(f32), logf], axis=1), axis=1)
        a = fox_block(fq, F[:, P:], P + jnp.arange(T), k_all, v_all, F)
        conv_prev = cache['conv']
    xc, conv_new = causal_conv(mx, conv_prev, p['w_conv'], p['b_conv'])
    xc = jax.nn.silu(xc)
    xch = xc.reshape(B, T, ML_HEADS, ML_HEAD_DIM)
    mq = jnp.einsum('bthd,hde->bthe', xch, p['w_mq']).astype(f32)
    mk = (jnp.einsum('bthd,hde->bthe', xch, p['w_mk']) * (ML_HEAD_DIM ** -0.5)).astype(f32)
    mvh = mv.reshape(B, T, ML_HEADS, ML_HEAD_DIM).astype(f32)
    ig = mi.astype(f32)
    lfm = jax.nn.log_sigmoid(mf.astype(f32))
    if cache is None:
        (C, n, m), hcell = mlstm_prompt(mq, mk, mvh, ig, lfm)
    else:
        st = (cache['C'].astype(f32), cache['n'].astype(f32), cache['m'].astype(f32))
        (C, n, m), hcell = mlstm_chunk(st, (mq, mk, mvh, ig, lfm))
    og = jax.nn.sigmoid(mo.astype(f32)).reshape(B, T, ML_HEADS, ML_HEAD_DIM)
    hm = rmsnorm(og * hcell, p['g_mnorm']).reshape(B, T, ML_WIDTH).astype(h.dtype) + p['skip_m'] * xc
    ya = a.reshape(B, T, FOX_WIDTH) @ p['w_fox_proj']
    yb = hm @ p['w_ml_proj']
    merged = jax.nn.sigmoid(ga) * ya + jax.nn.sigmoid(gb) * yb
    out = merged @ p['w_out']
    return out, (fk, fv, logf, C, n, m, conv_new)


def trunk_layer(x, p, cache):
    h = rmsnorm(x, p['g_pre_mix'])
    mix, st = token_mixers(h, p, cache)
    x = x + rmsnorm(mix, p['g_post_mix'])
    h = rmsnorm(x, p['g_pre_ffn'])
    g, u = jnp.split(h @ p['w_gate_up'], 2, axis=-1)
    f = (jax.nn.silu(g) * u) @ p['w_down']
    x = x + rmsnorm(f, p['g_post_ffn'])
    return x, st


def stack_layers(per_layer):
    return [jnp.stack(arrs) for arrs in zip(*per_layer)]


def setup_inputs(seed: int = 0) -> dict:
    key = jax.random.key(seed)
    ks = iter(jax.random.split(key, 48))

    def nrm(shape, s=1.0):
        return s * jax.random.normal(next(ks), shape, jnp.float32)

    b_in = jnp.concatenate([
        nrm((DEPTH, 3 * FOX_WIDTH), 0.02),
        2.0 + nrm((DEPTH, FOX_HEADS), 0.5),
        nrm((DEPTH, 3 * ML_WIDTH), 0.02),
        nrm((DEPTH, ML_HEADS), 0.1),
        jnp.linspace(3.0, 6.0, ML_HEADS)[None, :] + nrm((DEPTH, ML_HEADS), 0.1),
        nrm((DEPTH, 2 * D_MODEL), 0.02)], axis=-1)
    return {
        'x_prompt': nrm((BATCH, SEQ, D_MODEL)),
        'x_sample': nrm((DEC_BATCH, DEC_SEQ, D_MODEL)),
        'cache_fox_k': nrm((DEPTH, DEC_BATCH, PAST_LEN, FOX_HEADS, FOX_HEAD_DIM)),
        'cache_fox_v': nrm((DEPTH, DEC_BATCH, PAST_LEN, FOX_HEADS, FOX_HEAD_DIM)),
        'cache_fox_logf': jax.nn.log_sigmoid(2.0 + nrm((DEPTH, DEC_BATCH, PAST_LEN, FOX_HEADS), 0.5)),
        'state_mlstm_C': nrm((DEPTH, DEC_BATCH, ML_HEADS, ML_HEAD_DIM, ML_HEAD_DIM), 0.1),
        'state_mlstm_n': nrm((DEPTH, DEC_BATCH, ML_HEADS, ML_HEAD_DIM), 0.5),
        'state_mlstm_m': nrm((DEPTH, DEC_BATCH, ML_HEADS)),
        'state_mlstm_conv': nrm((DEPTH, DEC_BATCH, CONV_WIDTH - 1, ML_WIDTH)),
        'meta_tokens': nrm((N_META, D_MODEL)),
        'w_in': nrm((DEPTH, D_MODEL, D_IN), D_MODEL ** -0.5),
        'b_in': b_in,
        'g_fq': 1.0 + nrm((DEPTH, FOX_HEADS, FOX_HEAD_DIM), 0.05),
        'g_fk': 1.0 + nrm((DEPTH, FOX_HEADS, FOX_HEAD_DIM), 0.05),
        'w_conv': nrm((DEPTH, CONV_WIDTH, ML_WIDTH), CONV_WIDTH ** -0.5),
        'b_conv': nrm((DEPTH, ML_WIDTH), 0.02),
        'w_mq': nrm((DEPTH, ML_HEADS, ML_HEAD_DIM, ML_HEAD_DIM), ML_HEAD_DIM ** -0.5),
        'w_mk': nrm((DEPTH, ML_HEADS, ML_HEAD_DIM, ML_HEAD_DIM), ML_HEAD_DIM ** -0.5),
        'g_mnorm': 1.0 + nrm((DEPTH, ML_HEADS, ML_HEAD_DIM), 0.05),
        'skip_m': 1.0 + nrm((DEPTH, ML_WIDTH), 0.05),
        'w_fox_proj': nrm((DEPTH, FOX_WIDTH, D_MODEL), FOX_WIDTH ** -0.5),
        'w_ml_proj': nrm((DEPTH, ML_WIDTH, D_MODEL), ML_WIDTH ** -0.5),
        'w_out': nrm((DEPTH, D_MODEL, D_MODEL), D_MODEL ** -0.5),
        'g_pre_mix': 1.0 + nrm((DEPTH, D_MODEL), 0.05),
        'g_post_mix': 1.0 + nrm((DEPTH, D_MODEL), 0.05),
        'g_pre_ffn': 1.0 + nrm((DEPTH, D_MODEL), 0.05),
        'g_post_ffn': 1.0 + nrm((DEPTH, D_MODEL), 0.05),
        'w_gate_up': nrm((DEPTH, D_MODEL, 2 * D_FF), D_MODEL ** -0.5),
        'w_down': nrm((DEPTH, D_FF, D_MODEL), D_FF ** -0.5),
    }


def reference(x_prompt, x_sample, cache_fox_k, cache_fox_v, cache_fox_logf, state_mlstm_C, state_mlstm_n,
              state_mlstm_m, state_mlstm_conv, meta_tokens, w_in, b_in, g_fq, g_fk, w_conv, b_conv, w_mq, w_mk,
              g_mnorm, skip_m, w_fox_proj, w_ml_proj, w_out, g_pre_mix, g_post_mix, g_pre_ffn, g_post_ffn,
              w_gate_up, w_down):
    B = x_prompt.shape[0]
    meta = jnp.broadcast_to(meta_tokens.astype(x_prompt.dtype)[None], (B, N_META, meta_tokens.shape[-1]))
    xp = jnp.concatenate([meta, x_prompt], axis=1)
    xs = x_sample
    prompt_states, sample_states = [], []
    for l in range(DEPTH):
        p = dict(w_in=w_in[l], b_in=b_in[l], g_fq=g_fq[l], g_fk=g_fk[l], w_conv=w_conv[l], b_conv=b_conv[l],
                 w_mq=w_mq[l], w_mk=w_mk[l], g_mnorm=g_mnorm[l], skip_m=skip_m[l], w_fox_proj=w_fox_proj[l],
                 w_ml_proj=w_ml_proj[l], w_out=w_out[l], g_pre_mix=g_pre_mix[l], g_post_mix=g_post_mix[l],
                 g_pre_ffn=g_pre_ffn[l], g_post_ffn=g_post_ffn[l], w_gate_up=w_gate_up[l], w_down=w_down[l])
        xp, stp = trunk_layer(xp, p, None)
        cache = dict(fox_k=cache_fox_k[l], fox_v=cache_fox_v[l], fox_logf=cache_fox_logf[l],
                     C=state_mlstm_C[l], n=state_mlstm_n[l], m=state_mlstm_m[l], conv=state_mlstm_conv[l])
        xs, sts = trunk_layer(xs, p, cache)
        prompt_states.append(stp)
        sample_states.append(sts)
    fox_k_p, fox_v_p, fox_logf_p, mlstm_C_p, mlstm_n_p, mlstm_m_p, mlstm_conv_p = stack_layers(prompt_states)
    fox_k_s, fox_v_s, fox_logf_s, mlstm_C_s, mlstm_n_s, mlstm_m_s, mlstm_conv_s = stack_layers(sample_states)
    y_prompt = xp[:, N_META:]
    y_sample = xs
    return (y_prompt, y_sample, fox_k_p, fox_v_p, fox_logf_p, mlstm_C_p, mlstm_n_p, mlstm_m_p, mlstm_conv_p,
            fox_k_s, fox_v_s, fox_logf_s, mlstm_C_s, mlstm_n_s, mlstm_m_s, mlstm_conv_s)
```

```python
import functools

import jax
import jax.numpy as jnp
from jax import lax
from jax.experimental import pallas as pl
from jax.experimental.pallas import tpu as pltpu

F32 = jnp.float32
BF16 = jnp.bfloat16

N_META = 16
FOX_HEADS = 8
FOX_HEAD_DIM = 64
FOX_WIDTH = FOX_HEADS * FOX_HEAD_DIM
ML_HEADS = 4
ML_HEAD_DIM = 128
ML_WIDTH = ML_HEADS * ML_HEAD_DIM
CONV_WIDTH = 4
RMS_EPS = 1e-6
PAD_LOG_GATE = -1e30
NEG_BIG = -1e30

LANES = 128
GATE_FOX_F = 0
GATE_ML_I = 8
GATE_ML_F = 12
GATE_ML_F2 = 16

VMEM_LIMIT = 56 * 1024 * 1024


def _const_spec(shape):
    nd = len(shape)
    return pl.BlockSpec(shape, lambda *_: (0,) * nd, pipeline_mode=pl.Buffered(1))


def _rms(x, g):
    return x * lax.rsqrt(jnp.mean(x * x, axis=-1, keepdims=True) + RMS_EPS) * g


def _inproj_kernel(x_ref, gpre_ref, w_ref, b_ref, gq_ref, gk_ref, hs_ref,
                   q_ref, kf_ref, kb_ref, vf_ref, vb_ref, mx_ref, mv_ref, mo_ref, ga_ref, gb_ref, gt_ref,
                   *, d_model):
    fw, mw = FOX_WIDTH, ML_WIDTH
    offs = [0, fw, 2 * fw, 3 * fw, 3 * fw + mw, 3 * fw + 2 * mw, 3 * fw + 3 * mw,
            3 * fw + 3 * mw + d_model, 3 * fw + 3 * mw + 2 * d_model, 3 * fw + 3 * mw + 2 * d_model + LANES]
    h = _rms(x_ref[...], gpre_ref[...]).astype(BF16)

    def seg(i):
        a, b = offs[i], offs[i + 1]
        return jnp.dot(h, w_ref[:, a:b], preferred_element_type=F32) + b_ref[:, a:b]

    def headnorm(z, g):
        ms = jnp.dot((z * z).astype(BF16), hs_ref[...], preferred_element_type=F32) * (1.0 / FOX_HEAD_DIM)
        return z * lax.rsqrt(ms + RMS_EPS) * g

    q = headnorm(seg(0), gq_ref[...])
    q_ref[...] = (q * (FOX_HEAD_DIM ** -0.5)).astype(BF16)
    k = headnorm(seg(1), gk_ref[...])
    kf_ref[...] = k
    kb_ref[...] = k.astype(BF16)
    v = seg(2)
    vf_ref[...] = v
    vb_ref[...] = v.astype(BF16)
    mx_ref[...] = seg(3)
    mv_ref[...] = seg(4).astype(BF16)
    mo_ref[...] = seg(5)
    ga_ref[...] = seg(6)
    gb_ref[...] = seg(7)
    g = seg(8)
    lane = lax.broadcasted_iota(jnp.int32, g.shape, 1)
    raw = (lane >= GATE_ML_I) & (lane < GATE_ML_F)
    gt_ref[...] = jnp.where(raw, g, jax.nn.log_sigmoid(g))


def _inproj(x, gpre, w, b, gq, gk, hs, tm):
    n, d = x.shape
    dinp = w.shape[1]
    fw, mw = FOX_WIDTH, ML_WIDTH
    row = lambda c: pl.BlockSpec((tm, c), lambda i: (i, 0))
    outs = [(fw, BF16), (fw, F32), (fw, BF16), (fw, F32), (fw, BF16), (mw, F32), (mw, BF16), (mw, F32),
            (d, F32), (d, F32), (LANES, F32)]
    return pl.pallas_call(
        functools.partial(_inproj_kernel, d_model=d),
        grid=(n // tm,),
        in_specs=[row(d), _const_spec((1, d)), _const_spec((d, dinp)), _const_spec((1, dinp)),
                  _const_spec((1, fw)), _const_spec((1, fw)), _const_spec((fw, fw))],
        out_specs=[row(c) for c, _ in outs],
        out_shape=[jax.ShapeDtypeStruct((n, c), t) for c, t in outs],
        compiler_params=pltpu.CompilerParams(dimension_semantics=("arbitrary",), vmem_limit_bytes=VMEM_LIMIT),
        name="inproj",
    )(x, gpre, w, b, gq, gk, hs)


def _cumsum_kernel(g_ref, c0_ref, fcol_ref, frow_ref, carry, *, nsub):
    @pl.when(pl.program_id(1) == 0)
    def _():
        carry[...] = c0_ref[...]

    r = lax.broadcasted_iota(jnp.int32, (LANES, LANES), 0)
    c = lax.broadcasted_iota(jnp.int32, (LANES, LANES), 1)
    tril = (c <= r).astype(F32)
    cur = carry[...]
    for s in range(nsub):
        sl = slice(s * LANES, (s + 1) * LANES)
        f = jnp.dot(tril, g_ref[sl, :], preferred_element_type=F32, precision=lax.Precision.HIGHEST) + cur
        fcol_ref[sl, :] = f
        frow_ref[:, sl] = f.T[:FOX_HEADS, :]
        cur = f[LANES - 1:LANES, :]
    carry[...] = cur


def _cumsum(g, c0):
    bsz, t, _ = g.shape
    nblk = t // LANES
    nsub = max(d for d in range(1, 12) if nblk % d == 0)
    tb = nsub * LANES
    return pl.pallas_call(
        functools.partial(_cumsum_kernel, nsub=nsub),
        grid=(bsz, nblk // nsub),
        in_specs=[pl.BlockSpec((None, tb, LANES), lambda b, i: (b, i, 0)),
                  pl.BlockSpec((None, 1, LANES), lambda b, i: (b, 0, 0))],
        out_specs=[pl.BlockSpec((None, tb, LANES), lambda b, i: (b, i, 0)),
                   pl.BlockSpec((None, FOX_HEADS, tb), lambda b, i: (b, 0, i))],
        out_shape=[jax.ShapeDtypeStruct((bsz, t, LANES), F32), jax.ShapeDtypeStruct((bsz, FOX_HEADS, t), F32)],
        scratch_shapes=[pltpu.VMEM((1, LANES), F32)],
        compiler_params=pltpu.CompilerParams(dimension_semantics=("arbitrary", "arbitrary")),
        name="logf_cumsum",
    )(g, c0)


def _fox_kernel(q_ref, kp_ref, vp_ref, fpr_ref, k_ref, v_ref, fr_ref, fc_ref, o_ref, *, tq, pblk, n_pblk):
    p = pl.program_id(1)
    i = pl.program_id(2)
    lane = lax.broadcasted_iota(jnp.int32, (1, LANES), 1)
    q = q_ref[...]
    fc = fc_ref[...]
    ri = lax.broadcasted_iota(jnp.int32, (tq, tq), 0)
    ci = lax.broadcasted_iota(jnp.int32, (tq, tq), 1)
    causal = ci <= ri
    res = []
    for hh in range(2):
        h = 2 * p + hh
        half = (lane >= FOX_HEAD_DIM * hh) & (lane < FOX_HEAD_DIM * (hh + 1))
        qm = jnp.where(half, q, jnp.zeros_like(q))
        ft = jnp.sum(jnp.where(lane == h, fc, 0.0), axis=-1, keepdims=True)

        def upd(carry, kb, vb, fs, mask=None):
            m, l, acc = carry
            s = lax.dot_general(qm, kb, (((1,), (1,)), ((), ())), preferred_element_type=F32)
            s = s + (ft - fs)
            if mask is not None:
                s = jnp.where(mask, s, NEG_BIG)
            m_new = jnp.maximum(m, jnp.max(s, axis=-1, keepdims=True))
            alpha = jnp.exp(m - m_new)
            pm = jnp.exp(s - m_new)
            l = alpha * l + jnp.sum(pm, axis=-1, keepdims=True)
            acc = alpha * acc + jnp.dot(pm.astype(BF16), vb, preferred_element_type=F32)
            return m_new, l, acc

        carry = (jnp.full((tq, 1), NEG_BIG, F32), jnp.zeros((tq, 1), F32), jnp.zeros((tq, LANES), F32))
        for jb in range(n_pblk):
            sl = slice(jb * pblk, (jb + 1) * pblk)
            carry = upd(carry, kp_ref[sl, :], vp_ref[sl, :], fpr_ref[pl.ds(h, 1), sl])

        def body(j, carry):
            off = pl.multiple_of(j * tq, tq)
            return upd(carry, k_ref[pl.ds(off, tq), :], v_ref[pl.ds(off, tq), :], fr_ref[j, pl.ds(h, 1), :])

        carry = lax.fori_loop(0, i, body, carry)
        off = pl.multiple_of(i * tq, tq)
        m, l, acc = upd(carry, k_ref[pl.ds(off, tq), :], v_ref[pl.ds(off, tq), :], fr_ref[i, pl.ds(h, 1), :],
                        mask=causal)
        res.append(acc / l)
    o_ref[...] = jnp.where(lane < FOX_HEAD_DIM, res[0], res[1]).astype(o_ref.dtype)


def _fox(q, kp, vp, fpr, k, v, fr, fc, tq, pblk):
    bsz, t, _ = q.shape
    bp, plen, _ = kp.shape
    nq = t // tq
    n_pblk = plen // pblk
    frb = fr.reshape(bsz, FOX_HEADS, nq, tq).transpose(0, 2, 1, 3)
    pb = (lambda b: b) if bp == bsz else (lambda b: 0)
    return pl.pallas_call(
        functools.partial(_fox_kernel, tq=tq, pblk=pblk, n_pblk=n_pblk),
        grid=(bsz, FOX_HEADS // 2, nq),
        in_specs=[pl.BlockSpec((None, tq, LANES), lambda b, p, i: (b, i, p)),
                  pl.BlockSpec((None, plen, LANES), lambda b, p, i: (pb(b), 0, p)),
                  pl.BlockSpec((None, plen, LANES), lambda b, p, i: (pb(b), 0, p)),
                  pl.BlockSpec((None, FOX_HEADS, plen), lambda b, p, i: (pb(b), 0, 0)),
                  pl.BlockSpec((None, t, LANES), lambda b, p, i: (b, 0, p)),
                  pl.BlockSpec((None, t, LANES), lambda b, p, i: (b, 0, p)),
                  pl.BlockSpec((None, nq, FOX_HEADS, tq), lambda b, p, i: (b, 0, 0, 0)),
                  pl.BlockSpec((None, tq, LANES), lambda b, p, i: (b, i, 0))],
        out_specs=pl.BlockSpec((None, tq, LANES), lambda b, p, i: (b, i, p)),
        out_shape=jax.ShapeDtypeStruct((bsz, t, FOX_WIDTH), BF16),
        compiler_params=pltpu.CompilerParams(dimension_semantics=("arbitrary", "arbitrary", "arbitrary"),
                                             vmem_limit_bytes=VMEM_LIMIT),
        name="fox_attention",
    )(q, kp, vp, fpr, k, v, frb, fc)


def _mlstm_kernel(mx_ref, mv_ref, mo_ref, gt_ref, cp_ref, c0_ref, n0_ref, m0_ref,
                  wc_ref, bc_ref, wq_ref, wk_ref, gn_ref, sk_ref,
                  hm_ref, cout_ref, nout_ref, mout_ref,
                  xbuf, c_scr, n_scr, m_scr, *, tc, n_valid):
    c = pl.program_id(1)

    @pl.when(c == 0)
    def _():
        xbuf[0:8, :] = cp_ref[...]
        c_scr[...] = c0_ref[...]
        n_scr[...] = n0_ref[...]
        m_scr[...] = m0_ref[...]

    xbuf[8:8 + tc, :] = mx_ref[...]
    xc = bc_ref[...]
    for i in range(CONV_WIDTH):
        xc = xc + xbuf[5 + i:5 + i + tc, :] * wc_ref[i:i + 1, :]
    xc = xc * jax.nn.sigmoid(xc)
    tail = xbuf[tc:tc + 8, :]
    xbuf[0:8, :] = tail

    gt = gt_ref[...]
    lane = lax.broadcasted_iota(jnp.int32, (tc, LANES), 1)
    if n_valid is not None:
        rowi = lax.broadcasted_iota(jnp.int32, (tc, LANES), 0)
        is_i = (lane >= GATE_ML_I) & (lane < GATE_ML_F)
        gt = jnp.where(rowi < n_valid, gt, jnp.where(is_i, PAD_LOG_GATE, 0.0))
    ri = lax.broadcasted_iota(jnp.int32, (tc, tc), 0)
    ci = lax.broadcasted_iota(jnp.int32, (tc, tc), 1)
    causal = ci <= ri
    b_all = jnp.dot(causal.astype(F32), gt, preferred_element_type=F32, precision=lax.Precision.HIGHEST)

    for h in range(ML_HEADS):
        hs = slice(h * ML_HEAD_DIM, (h + 1) * ML_HEAD_DIM)
        sel_b = lane == GATE_ML_F + h
        sel_b2 = lane == GATE_ML_F2 + h
        sel_i = lane == GATE_ML_I + h
        lmat = jnp.where(sel_b, b_all, jnp.where(sel_i | sel_b2, 1.0, 0.0))
        rmat = jnp.where(sel_b, 1.0, jnp.where(sel_i, gt, jnp.where(sel_b2, -b_all, 0.0)))
        dmat = lax.dot_general(lmat, rmat, (((1,), (1,)), ((), ())), preferred_element_type=F32,
                               precision=lax.Precision.HIGHEST)
        dmat = jnp.where(causal, dmat, NEG_BIG)
        b_col = jnp.sum(jnp.where(sel_b, b_all, 0.0), axis=-1, keepdims=True)
        ig_col = jnp.sum(jnp.where(sel_i, gt, 0.0), axis=-1, keepdims=True)
        m_prev = m_scr[h:h + 1, 0:1]
        m_inter = b_col + m_prev
        m_t = jnp.maximum(m_inter, jnp.max(dmat, axis=-1, keepdims=True))
        w_intra = jnp.exp(dmat - m_t)
        w_inter = jnp.exp(m_inter - m_t)

        xh = xc[:, hs]
        xhb = xh.astype(BF16)
        q = jnp.dot(xhb, wq_ref[h], preferred_element_type=F32)
        k = jnp.dot(xhb, wk_ref[h], preferred_element_type=F32) * (ML_HEAD_DIM ** -0.5)
        qb = q.astype(BF16)
        kb = k.astype(BF16)
        v = mv_ref[:, hs]
        smat = lax.dot_general(qb, kb, (((1,), (1,)), ((), ())), preferred_element_type=F32)
        amat = w_intra * smat
        ch = c_scr[h]
        nh = n_scr[h:h + 1, :]
        num = (jnp.dot(amat.astype(BF16), v, preferred_element_type=F32)
               + w_inter * jnp.dot(qb, ch.astype(BF16), preferred_element_type=F32))
        den = jnp.sum(amat, axis=-1, keepdims=True) + w_inter * jnp.sum(q * nh, axis=-1, keepdims=True)
        hcell = num / jnp.maximum(jnp.abs(den), jnp.exp(-m_t))

        b_last = b_col[tc - 1:tc, :]
        g = b_last - b_col + ig_col
        m_new = jnp.maximum(b_last + m_prev, jnp.max(g, axis=0, keepdims=True))
        wk = jnp.exp(g - m_new)
        decay = jnp.exp(b_last + m_prev - m_new)
        kw = k * wk
        c_scr[h] = decay * ch + lax.dot_general(kw.astype(BF16), v, (((0,), (0,)), ((), ())),
                                                 preferred_element_type=F32)
        n_scr[h:h + 1, :] = decay * nh + jnp.sum(kw, axis=0, keepdims=True)
        m_scr[h:h + 1, :] = jnp.broadcast_to(m_new, (1, LANES))

        y = jax.nn.sigmoid(mo_ref[:, hs]) * hcell
        hm_ref[:, hs] = (_rms(y, gn_ref[:, hs]) + sk_ref[:, hs] * xh).astype(hm_ref.dtype)

    @pl.when(c == pl.num_programs(1) - 1)
    def _():
        cout_ref[...] = c_scr[...]
        nout_ref[...] = n_scr[...]
        mout_ref[...] = m_scr[...]


def _mlstm(mx, mv, mo, gt, cp, c0, n0, m0, wc, bc, wq, wk, gn, sk, tc, n_valid=None):
    bsz, t, _ = mx.shape
    bs = c0.shape[0]
    sb = (lambda b: b) if bs == bsz else (lambda b: 0)
    tok = lambda w: pl.BlockSpec((None, tc, w), lambda b, c: (b, c, 0))
    hd = ML_HEAD_DIM
    st_specs = [pl.BlockSpec((None, ML_HEADS, hd, hd), lambda b, c: (b, 0, 0, 0)),
                pl.BlockSpec((None, 8, LANES), lambda b, c: (b, 0, 0)),
                pl.BlockSpec((None, 8, LANES), lambda b, c: (b, 0, 0))]
    return pl.pallas_call(
        functools.partial(_mlstm_kernel, tc=tc, n_valid=n_valid),
        grid=(bsz, t // tc),
        in_specs=[tok(ML_WIDTH), tok(ML_WIDTH), tok(ML_WIDTH), tok(LANES),
                  pl.BlockSpec((None, 8, ML_WIDTH), lambda b, c: (sb(b), 0, 0)),
                  pl.BlockSpec((None, ML_HEADS, hd, hd), lambda b, c: (sb(b), 0, 0, 0)),
                  pl.BlockSpec((None, 8, LANES), lambda b, c: (sb(b), 0, 0)),
                  pl.BlockSpec((None, 8, LANES), lambda b, c: (sb(b), 0, 0)),
                  _const_spec((CONV_WIDTH, ML_WIDTH)), _const_spec((1, ML_WIDTH)),
                  _const_spec((ML_HEADS, hd, hd)), _const_spec((ML_HEADS, hd, hd)),
                  _const_spec((1, ML_WIDTH)), _const_spec((1, ML_WIDTH))],
        out_specs=[tok(ML_WIDTH)] + st_specs,
        out_shape=[jax.ShapeDtypeStruct((bsz, t, ML_WIDTH), BF16),
                   jax.ShapeDtypeStruct((bsz, ML_HEADS, hd, hd), F32),
                   jax.ShapeDtypeStruct((bsz, 8, LANES), F32),
                   jax.ShapeDtypeStruct((bsz, 8, LANES), F32)],
        scratch_shapes=[pltpu.VMEM((8 + tc, ML_WIDTH), F32), pltpu.VMEM((ML_HEADS, hd, hd), F32),
                        pltpu.VMEM((8, LANES), F32), pltpu.VMEM((8, LANES), F32)],
        compiler_params=pltpu.CompilerParams(dimension_semantics=("arbitrary", "arbitrary")),
        name="mlstm",
    )(mx, mv, mo, gt, cp, c0, n0, m0, wc, bc, wq, wk, gn, sk)


def _post_kernel(a_ref, hm_ref, ga_ref, gb_ref, x_ref, wfp_ref, wmp_ref, wo_ref, wgu_ref, wd_ref,
                 g1_ref, g2_ref, g3_ref, y_ref, *, d_ff, n_ffc):
    ya = jnp.dot(a_ref[...], wfp_ref[...], preferred_element_type=F32)
    yb = jnp.dot(hm_ref[...], wmp_ref[...], preferred_element_type=F32)
    merged = jax.nn.sigmoid(ga_ref[...]) * ya + jax.nn.sigmoid(gb_ref[...]) * yb
    mix = jnp.dot(merged.astype(BF16), wo_ref[...], preferred_element_type=F32)
    x1 = x_ref[...] + _rms(mix, g1_ref[...])
    h2 = _rms(x1, g2_ref[...]).astype(BF16)
    fc = d_ff // n_ffc
    f = None
    for j in range(n_ffc):
        g = jnp.dot(h2, wgu_ref[:, j * fc:(j + 1) * fc], preferred_element_type=F32)
        u = jnp.dot(h2, wgu_ref[:, d_ff + j * fc:d_ff + (j + 1) * fc], preferred_element_type=F32)
        t = (g * jax.nn.sigmoid(g) * u).astype(BF16)
        fj = jnp.dot(t, wd_ref[j * fc:(j + 1) * fc, :], preferred_element_type=F32)
        f = fj if f is None else f + fj
    y_ref[...] = x1 + _rms(f, g3_ref[...])


def _post(a, hm, ga, gb, x, wfp, wmp, wo, wgu, wd, g1, g2, g3, tm):
    n, d = x.shape
    d_ff = wd.shape[0]
    n_ffc = 2 if (d_ff // 2) % LANES == 0 else 1
    row = lambda c: pl.BlockSpec((tm, c), lambda i: (i, 0))
    return pl.pallas_call(
        functools.partial(_post_kernel, d_ff=d_ff, n_ffc=n_ffc),
        grid=(n // tm,),
        in_specs=[row(FOX_WIDTH), row(ML_WIDTH), row(d), row(d), row(d),
                  _const_spec(wfp.shape), _const_spec(wmp.shape), _const_spec(wo.shape),
                  _const_spec(wgu.shape), _const_spec(wd.shape),
                  _const_spec((1, d)), _const_spec((1, d)), _const_spec((1, d))],
        out_specs=row(d),
        out_shape=jax.ShapeDtypeStruct((n, d), F32),
        compiler_params=pltpu.CompilerParams(dimension_semantics=("arbitrary",), vmem_limit_bytes=VMEM_LIMIT),
        name="post_mix_ffn",
    )(a, hm, ga, gb, x, wfp, wmp, wo, wgu, wd, g1, g2, g3)


def _pad_rows(a, rows, axis):
    pad = [(0, 0)] * a.ndim
    pad[axis] = (0, rows - a.shape[axis])
    return jnp.pad(a, pad)


def _row_tile(n, pref):
    t = min(pref, n)
    while n % t:
        t //= 2
    return t


def kernel(x_prompt, x_sample, cache_fox_k, cache_fox_v, cache_fox_logf, state_mlstm_C, state_mlstm_n,
           state_mlstm_m, state_mlstm_conv, meta_tokens, w_in, b_in, g_fq, g_fk, w_conv, b_conv, w_mq, w_mk,
           g_mnorm, skip_m, w_fox_proj, w_ml_proj, w_out, g_pre_mix, g_post_mix, g_pre_ffn, g_post_ffn,
           w_gate_up, w_down):
    assert w_in.shape[0] == 1, "single-layer trunk"
    bsz, seq, d = x_prompt.shape
    dbsz, dseq, _ = x_sample.shape
    past = cache_fox_k.shape[2]
    fw, mw = FOX_WIDTH, ML_WIDTH

    w, b = w_in[0], b_in[0]
    o_ff = 3 * fw
    o_mx = o_ff + FOX_HEADS
    o_mi = o_mx + 3 * mw
    o_mf = o_mi + ML_HEADS
    o_ga = o_mf + ML_HEADS

    def relayout(a):
        gates = jnp.concatenate([a[..., o_ff:o_mx], a[..., o_mi:o_mf], a[..., o_mf:o_ga], a[..., o_mf:o_ga]], axis=-1)
        gates = _pad_rows(gates, LANES, gates.ndim - 1)
        return jnp.concatenate([a[..., :o_ff], a[..., o_mx:o_mi], a[..., o_ga:], gates], axis=-1)

    w_p = relayout(w).astype(BF16)
    b_p = relayout(b)[None, :]
    hid = jnp.arange(fw) // FOX_HEAD_DIM
    hsum = (hid[:, None] == hid[None, :]).astype(BF16)
    gq = g_fq[0].reshape(1, fw)
    gk = g_fk[0].reshape(1, fw)
    gpre = g_pre_mix[0][None, :]
    inproj = lambda x, tm: _inproj(x, gpre, w_p, b_p, gq, gk, hsum, tm)

    wc, bc = w_conv[0], b_conv[0][None, :]
    wq, wk = w_mq[0].astype(BF16), w_mk[0].astype(BF16)
    gn, sk = g_mnorm[0].reshape(1, mw), skip_m[0][None, :]
    mlstm = lambda *a, **kw: _mlstm(*a, wc, bc, wq, wk, gn, sk, **kw)

    def conv_hist(rows):
        return jnp.pad(rows, ((0, 0), (8 - (CONV_WIDTH - 1), 0), (0, 0)))

    def m_rows(m):
        return jnp.broadcast_to(_pad_rows(m, 8, 1)[:, :, None], (m.shape[0], 8, LANES))

    mt = 64
    xm = _pad_rows(meta_tokens.astype(F32), mt, 0)
    (_, kf_m, kb_m, vf_m, vb_m, mx_m, mv_m, mo_m, _, _, gt_m) = inproj(xm, mt)
    zc = jnp.zeros((1, ML_HEADS, ML_HEAD_DIM, ML_HEAD_DIM), F32)
    z8 = jnp.zeros((1, 8, LANES), F32)
    _, c_m, n_m, m_m = mlstm(mx_m[None], mv_m[None], mo_m[None], gt_m[None], jnp.zeros((1, 8, mw), F32),
                             zc, z8, z8, tc=mt, n_valid=N_META)

    xp = x_prompt.reshape(bsz * seq, d)
    tm = _row_tile(bsz * seq, 512)
    (q_p, kf_p, kb_p, vf_p, vb_p, mx_p, mv_p, mo_p, ga_p, gb_p, gt_p) = inproj(xp, tm)
    r3 = lambda a: a.reshape(bsz, seq, a.shape[-1])
    mx_p3 = r3(mx_p)
    hm_p, c_p, n_p, m_p = mlstm(mx_p3, r3(mv_p), r3(mo_p), r3(gt_p),
                                conv_hist(mx_m[None, N_META - (CONV_WIDTH - 1):N_META]), c_m, n_m, m_m,
                                tc=_row_tile(seq, 128))
    lp = N_META + seq
    lpad = -(-lp // LANES) * LANES
    g_all = jnp.concatenate([jnp.broadcast_to(gt_m[None, :N_META], (bsz, N_META, LANES)), r3(gt_p)], axis=1)
    fcol, frow = _cumsum(_pad_rows(g_all, lpad, 1), jnp.zeros((bsz, 1, LANES), F32))
    a_p = _fox(r3(q_p), kb_m[None, :N_META], vb_m[None, :N_META], frow[:1, :, :N_META],
               r3(kb_p), r3(vb_p), frow[:, :, N_META:lp], fcol[:, N_META:lp], tq=_row_tile(seq, 256), pblk=N_META)

    xs = x_sample.reshape(dbsz * dseq, d)
    (q_s, kf_s, kb_s, vf_s, vb_s, mx_s, mv_s, mo_s, ga_s, gb_s, gt_s) = inproj(xs, _row_tile(dbsz * dseq, 512))
    r3s = lambda a: a.reshape(dbsz, dseq, a.shape[-1])
    mx_s3 = r3s(mx_s)
    hm_s, c_s, n_s, m_s = mlstm(mx_s3, r3s(mv_s), r3s(mo_s), r3s(gt_s), conv_hist(state_mlstm_conv[0]),
                                state_mlstm_C[0], _pad_rows(state_mlstm_n[0], 8, 1), m_rows(state_mlstm_m[0]),
                                tc=dseq)
    ls = past + dseq
    lspad = -(-ls // LANES) * LANES
    g_alls = jnp.concatenate([_pad_rows(cache_fox_logf[0].astype(F32), LANES, 2), r3s(gt_s)], axis=1)
    fcol_s, frow_s = _cumsum(_pad_rows(g_alls, lspad, 1), jnp.zeros((dbsz, 1, LANES), F32))
    kc = cache_fox_k[0].reshape(dbsz, past, fw).astype(BF16)
    vc = cache_fox_v[0].reshape(dbsz, past, fw).astype(BF16)
    a_s = _fox(r3s(q_s), kc, vc, frow_s[:, :, :past], r3s(kb_s), r3s(vb_s), frow_s[:, :, past:ls],
               fcol_s[:, past:ls], tq=dseq, pblk=_row_tile(past, 512))

    wfp, wmp, wo = w_fox_proj[0].astype(BF16), w_ml_proj[0].astype(BF16), w_out[0].astype(BF16)
    wgu, wd = w_gate_up[0].astype(BF16), w_down[0].astype(BF16)
    g1, g2, g3 = g_post_mix[0][None, :], g_pre_ffn[0][None, :], g_post_ffn[0][None, :]
    post = lambda a, hm, ga, gb, x, tm: _post(a, hm, ga, gb, x, wfp, wmp, wo, wgu, wd, g1, g2, g3, tm)
    y_p = post(a_p.reshape(bsz * seq, fw), hm_p.reshape(bsz * seq, mw), ga_p, gb_p, xp, _row_tile(bsz * seq, 256))
    y_s = post(a_s.reshape(dbsz * dseq, fw), hm_s.reshape(dbsz * dseq, mw), ga_s, gb_s, xs,
               _row_tile(dbsz * dseq, 256))

    def with_meta(meta_rows, main):
        m = jnp.broadcast_to(meta_rows[None], (bsz,) + meta_rows.shape)
        return jnp.concatenate([m, main], axis=1)[None]

    heads = lambda a: a.reshape(a.shape[:-1] + (FOX_HEADS, FOX_HEAD_DIM))
    tail = CONV_WIDTH - 1
    return (y_p.reshape(bsz, seq, d), y_s.reshape(dbsz, dseq, d),
            heads(with_meta(kf_m[:N_META], r3(kf_p))), heads(with_meta(vf_m[:N_META], r3(vf_p))),
            with_meta(gt_m[:N_META, :FOX_HEADS], r3(gt_p)[..., :FOX_HEADS]),
            c_p[None], n_p[None, :, :ML_HEADS], m_p[None, :, :ML_HEADS, 0], mx_p3[None, :, seq - tail:],
            heads(r3s(kf_s))[None], heads(r3s(vf_s))[None], r3s(gt_s)[None, ..., :FOX_HEADS],
            c_s[None], n_s[None, :, :ML_HEADS], m_s[None, :, :ML_HEADS, 0], mx_s3[None, :, dseq - tail:])
```

```python
import functools

import jax
import jax.numpy as jnp
from jax import lax
from jax.experimental import pallas as pl
from jax.experimental.pallas import tpu as pltpu

F32 = jnp.float32
BF16 = jnp.bfloat16

N_META = 16
FOX_HEADS = 8
FOX_HEAD_DIM = 64
FOX_WIDTH = FOX_HEADS * FOX_HEAD_DIM
ML_HEADS = 4
ML_HEAD_DIM = 128
ML_WIDTH = ML_HEADS * ML_HEAD_DIM
CONV_WIDTH = 4
RMS_EPS = 1e-6
PAD_LOG_GATE = -1e30
NEG_BIG = -1e30

LANES = 128
GATE_FOX_F = 0
GATE_ML_I = 8
GATE_ML_F = 12
GATE_ML_F2 = 16

AUG_F = FOX_HEAD_DIM
AUG_1 = FOX_HEAD_DIM + 3
AUG_END = FOX_HEAD_DIM + 6

VMEM_LIMIT = 56 * 1024 * 1024
FOX_WIDE_KEYS = 1024


def _const_spec(shape):
    nd = len(shape)
    return pl.BlockSpec(shape, lambda *_: (0,) * nd, pipeline_mode=pl.Buffered(1))


def _rms(x, g):
    return x * lax.rsqrt(jnp.mean(x * x, axis=-1, keepdims=True) + RMS_EPS) * g


def _split3(x):
    hi = x.astype(BF16).astype(F32)
    r = x - hi
    mid = r.astype(BF16).astype(F32)
    lo = (r - mid).astype(BF16).astype(F32)
    return hi, mid, lo


def _cumsum_rows(lg, cur):
    tm = lg.shape[0]
    blk = min(tm, LANES)
    r = lax.broadcasted_iota(jnp.int32, (blk, blk), 0)
    c = lax.broadcasted_iota(jnp.int32, (blk, blk), 1)
    tril = (c <= r).astype(BF16)
    outs = []
    for s in range(tm // blk):
        hi, mid, lo = _split3(lg[s * blk:(s + 1) * blk])
        f = (jnp.dot(tril, lo.astype(BF16), preferred_element_type=F32)
             + jnp.dot(tril, mid.astype(BF16), preferred_element_type=F32)
             + jnp.dot(tril, hi.astype(BF16), preferred_element_type=F32)) + cur
        outs.append(f)
        cur = f[blk - 1:blk]
    return (outs[0] if len(outs) == 1 else jnp.concatenate(outs, axis=0)), cur


def _store_attention_operands(qn, kn, v, fcol, q_ref, k_ref, v_ref):
    tm = fcol.shape[0]
    lane = lax.broadcasted_iota(jnp.int32, (tm, LANES), 1)
    data = lane < FOX_HEAD_DIM
    for h in range(FOX_HEADS):
        hi, mid, lo = _split3(jnp.sum(jnp.where(lane == h, fcol, 0.0), axis=-1, keepdims=True))
        t, odd = divmod(h, 2)

        def head_tile(z):
            x = z[:, t * LANES:(t + 1) * LANES]
            return pltpu.roll(x, FOX_HEAD_DIM, 1) if odd else x

        hs = slice(h * LANES, (h + 1) * LANES)
        if q_ref is not None:
            aug = jnp.where(lane == AUG_F, hi, jnp.where(lane == AUG_F + 1, mid, jnp.where(
                lane == AUG_F + 2, lo, jnp.where(lane < AUG_END, 1.0, 0.0))))
            q_ref[:, hs] = jnp.where(data, head_tile(qn), aug).astype(BF16)
        aug = jnp.where(lane < AUG_1, 1.0, jnp.where(lane == AUG_1, -hi, jnp.where(
            lane == AUG_1 + 1, -mid, jnp.where(lane == AUG_1 + 2, -lo, 0.0))))
        k_ref[:, hs] = jnp.where(data, head_tile(kn), aug).astype(BF16)
        v_ref[:, hs] = jnp.where(data, head_tile(v), jnp.where(lane == FOX_HEAD_DIM, 1.0, 0.0)).astype(BF16)


def _inproj_kernel(x_ref, gpre_ref, w_ref, b_ref, gq_ref, gk_ref, hs_ref, c0_ref,
                   q_ref, kf_ref, kb_ref, vf_ref, vb_ref, mx_ref, mv_ref, mo_ref, ga_ref, gb_ref, gt_ref, fc_ref,
                   carry, *, d_model, tpb):
    fw, mw = FOX_WIDTH, ML_WIDTH
    offs = [0, fw, 2 * fw, 3 * fw, 3 * fw + mw, 3 * fw + 2 * mw, 3 * fw + 3 * mw,
            3 * fw + 3 * mw + d_model, 3 * fw + 3 * mw + 2 * d_model, 3 * fw + 3 * mw + 2 * d_model + LANES]

    @pl.when(pl.program_id(0) % tpb == 0)
    def _():
        carry[...] = c0_ref[...]

    h = _rms(x_ref[...], gpre_ref[...]).astype(BF16)

    def seg(i):
        a, b = offs[i], offs[i + 1]
        return jnp.dot(h, w_ref[:, a:b], preferred_element_type=F32) + b_ref[:, a:b]

    def headnorm(z, g):
        ms = jnp.dot((z * z).astype(BF16), hs_ref[...], preferred_element_type=F32) * (1.0 / FOX_HEAD_DIM)
        return z * lax.rsqrt(ms + RMS_EPS) * g

    g = seg(8)
    lane = lax.broadcasted_iota(jnp.int32, g.shape, 1)
    raw = (lane >= GATE_ML_I) & (lane < GATE_ML_F)
    gt = jnp.where(raw, g, jax.nn.log_sigmoid(g))
    gt_ref[...] = gt
    fcol, cur = _cumsum_rows(gt, carry[...])
    carry[...] = cur
    fc_ref[...] = fcol

    q = headnorm(seg(0), gq_ref[...]) * (FOX_HEAD_DIM ** -0.5)
    k = headnorm(seg(1), gk_ref[...])
    kf_ref[...] = k
    v = seg(2)
    vf_ref[...] = v
    _store_attention_operands(q, k, v, fcol, q_ref, kb_ref, vb_ref)
    mx_ref[...] = seg(3)
    mv_ref[...] = seg(4).astype(BF16)
    mo_ref[...] = seg(5)
    ga_ref[...] = seg(6)
    gb_ref[...] = seg(7)


def _inproj(x, gpre, w, b, gq, gk, hs, c0, tm, tpb):
    n, d = x.shape
    dinp = w.shape[1]
    fw, mw = FOX_WIDTH, ML_WIDTH
    aw = FOX_HEADS * LANES
    row = lambda c: pl.BlockSpec((tm, c), lambda i: (i, 0))
    cb = (lambda i: i // tpb) if c0.shape[0] > 1 else (lambda i: 0)
    outs = [(aw, BF16), (fw, F32), (aw, BF16), (fw, F32), (aw, BF16), (mw, F32), (mw, BF16), (mw, F32),
            (d, F32), (d, F32), (LANES, F32), (LANES, F32)]
    return pl.pallas_call(
        functools.partial(_inproj_kernel, d_model=d, tpb=tpb),
        grid=(n // tm,),
        in_specs=[row(d), _const_spec((1, d)), _const_spec((d, dinp)), _const_spec((1, dinp)),
                  _const_spec((1, fw)), _const_spec((1, fw)), _const_spec((fw, fw)),
                  pl.BlockSpec((None, 1, LANES), lambda i: (cb(i), 0, 0))],
        out_specs=[row(c) for c, _ in outs],
        out_shape=[jax.ShapeDtypeStruct((n, c), t) for c, t in outs],
        scratch_shapes=[pltpu.VMEM((1, LANES), F32)],
        compiler_params=pltpu.CompilerParams(dimension_semantics=("arbitrary",), vmem_limit_bytes=VMEM_LIMIT),
        name="inproj",
    )(x, gpre, w, b, gq, gk, hs, c0)


def _cacheprep_kernel(k_ref, v_ref, lf_ref, ka_ref, va_ref, fl_ref, carry):
    @pl.when(pl.program_id(1) == 0)
    def _():
        carry[...] = jnp.zeros_like(carry)

    fcol, cur = _cumsum_rows(lf_ref[...], carry[...])
    carry[...] = cur
    fl_ref[...] = cur
    _store_attention_operands(None, k_ref[...], v_ref[...], fcol, None, ka_ref, va_ref)


def _cacheprep(ck, cv, clf, tp):
    bsz, plen, _ = ck.shape
    aw = FOX_HEADS * LANES
    return pl.pallas_call(
        _cacheprep_kernel,
        grid=(bsz, plen // tp),
        in_specs=[pl.BlockSpec((None, tp, FOX_WIDTH), lambda b, i: (b, i, 0)),
                  pl.BlockSpec((None, tp, FOX_WIDTH), lambda b, i: (b, i, 0)),
                  pl.BlockSpec((None, tp, LANES), lambda b, i: (b, i, 0))],
        out_specs=[pl.BlockSpec((None, tp, aw), lambda b, i: (b, i, 0)),
                   pl.BlockSpec((None, tp, aw), lambda b, i: (b, i, 0)),
                   pl.BlockSpec((None, 1, LANES), lambda b, i: (b, 0, 0))],
        out_shape=[jax.ShapeDtypeStruct((bsz, plen, aw), BF16), jax.ShapeDtypeStruct((bsz, plen, aw), BF16),
                   jax.ShapeDtypeStruct((bsz, 1, LANES), F32)],
        scratch_shapes=[pltpu.VMEM((1, LANES), F32)],
        compiler_params=pltpu.CompilerParams(dimension_semantics=("arbitrary", "arbitrary")),
        name="cache_prep",
    )(ck, cv, clf)


def _fox_kernel(q_ref, kp_ref, vp_ref, k_ref, v_ref, o_ref, *, tq, tr, tkb, pblk, n_pblk):
    i = pl.program_id(2)
    nr = tq // tr
    chains = [(hh, r) for hh in range(2) for r in range(nr)]
    qs = [q_ref[r * tr:(r + 1) * tr, hh * LANES:(hh + 1) * LANES] for hh, r in chains]
    ri = lax.broadcasted_iota(jnp.int32, (tr, tr), 0)
    ci = lax.broadcasted_iota(jnp.int32, (tr, tr), 1)
    causal = ci <= ri

    def upd1(n, state, kb, vb, mask=None):
        m, acc = state
        hh = chains[n][0]
        hs = slice(hh * LANES, (hh + 1) * LANES)
        s = lax.dot_general(qs[n], kb[:, hs], (((1,), (1,)), ((), ())), preferred_element_type=F32)
        if mask is not None:
            s = jnp.where(mask, s, NEG_BIG)
        m_new = jnp.maximum(m, jnp.max(s, axis=-1, keepdims=True))
        alpha = jnp.exp(m - m_new)
        pm = jnp.exp(s - m_new)
        acc = alpha * acc + jnp.dot(pm.astype(BF16), vb[:, hs], preferred_element_type=F32)
        return m_new, acc

    def upd(carry, kb, vb):
        return tuple(upd1(n, carry[n], kb, vb) for n in range(len(chains)))

    carry = tuple((jnp.full((tr, 1), NEG_BIG, F32), jnp.zeros((tr, LANES), F32)) for _ in chains)
    for jb in range(n_pblk):
        sl = slice(jb * pblk, (jb + 1) * pblk)
        carry = upd(carry, kp_ref[sl, :], vp_ref[sl, :])

    def body_wide(j, carry):
        off = pl.multiple_of(j * tkb, tkb)
        return upd(carry, k_ref[pl.ds(off, tkb), :], v_ref[pl.ds(off, tkb), :])

    def body(j, carry):
        off = pl.multiple_of(j * tq, tq)
        return upd(carry, k_ref[pl.ds(off, tq), :], v_ref[pl.ds(off, tq), :])

    n_wide = i // (tkb // tq)
    carry = lax.fori_loop(0, n_wide, body_wide, carry)
    carry = lax.fori_loop(n_wide * (tkb // tq), i, body, carry)

    carry = list(carry)
    for c in range(nr):
        off = pl.multiple_of(i * tq + c * tr, tr)
        kb = k_ref[pl.ds(off, tr), :]
        vb = v_ref[pl.ds(off, tr), :]
        for n, (hh, r) in enumerate(chains):
            if r >= c:
                carry[n] = upd1(n, carry[n], kb, vb, mask=causal if r == c else None)

    lane = lax.broadcasted_iota(jnp.int32, (tr, LANES), 1)
    outs = []
    for n in range(len(chains)):
        acc = carry[n][1]
        l = jnp.sum(jnp.where(lane == FOX_HEAD_DIM, acc, 0.0), axis=-1, keepdims=True)
        outs.append(acc / l)
    for r in range(nr):
        o_ref[r * tr:(r + 1) * tr, :] = jnp.where(lane < FOX_HEAD_DIM, outs[r],
                                                  pltpu.roll(outs[nr + r], FOX_HEAD_DIM, 1)).astype(o_ref.dtype)


def _fox(q, kp, vp, k, v, tq, tr, pblk):
    bsz, t, _ = q.shape
    bp, plen, _ = kp.shape
    nq = t // tq
    n_pblk = plen // pblk
    pw = 2 * LANES
    pb = (lambda b: b) if bp == bsz else (lambda b: 0)
    return pl.pallas_call(
        functools.partial(_fox_kernel, tq=tq, tr=tr, tkb=_row_tile(t, FOX_WIDE_KEYS), pblk=pblk, n_pblk=n_pblk),
        grid=(bsz, FOX_HEADS // 2, nq),
        in_specs=[pl.BlockSpec((None, tq, pw), lambda b, p, i: (b, i, p)),
                  pl.BlockSpec((None, plen, pw), lambda b, p, i: (pb(b), 0, p)),
                  pl.BlockSpec((None, plen, pw), lambda b, p, i: (pb(b), 0, p)),
                  pl.BlockSpec((None, t, pw), lambda b, p, i: (b, 0, p)),
                  pl.BlockSpec((None, t, pw), lambda b, p, i: (b, 0, p))],
        out_specs=pl.BlockSpec((None, tq, LANES), lambda b, p, i: (b, i, p)),
        out_shape=jax.ShapeDtypeStruct((bsz, t, FOX_WIDTH), BF16),
        compiler_params=pltpu.CompilerParams(dimension_semantics=("arbitrary", "arbitrary", "arbitrary"),
                                             vmem_limit_bytes=VMEM_LIMIT),
        name="fox_attention",
    )(q, kp, vp, k, v)


def _mlstm_kernel(mx_ref, mv_ref, mo_ref, gt_ref, cp_ref, c0_ref, n0_ref, m0_ref,
                  wc_ref, bc_ref, wq_ref, wk_ref, gn_ref, sk_ref,
                  hm_ref, cout_ref, nout_ref, mout_ref,
                  xbuf, c_scr, n_scr, m_scr, *, tc, n_valid):
    c = pl.program_id(1)

    @pl.when(c == 0)
    def _():
        xbuf[0:8, :] = cp_ref[...]
        c_scr[...] = c0_ref[...]
        n_scr[...] = n0_ref[...]
        m_scr[...] = m0_ref[...]

    xbuf[8:8 + tc, :] = mx_ref[...]
    xc = bc_ref[...]
    for i in range(CONV_WIDTH):
        xc = xc + xbuf[5 + i:5 + i + tc, :] * wc_ref[i:i + 1, :]
    xc = xc * jax.nn.sigmoid(xc)
    tail = xbuf[tc:tc + 8, :]
    xbuf[0:8, :] = tail

    gt = gt_ref[...]
    lane = lax.broadcasted_iota(jnp.int32, (tc, LANES), 1)
    if n_valid is not None:
        rowi = lax.broadcasted_iota(jnp.int32, (tc, LANES), 0)
        is_i = (lane >= GATE_ML_I) & (lane < GATE_ML_F)
        gt = jnp.where(rowi < n_valid, gt, jnp.where(is_i, PAD_LOG_GATE, 0.0))
    ri = lax.broadcasted_iota(jnp.int32, (tc, tc), 0)
    ci = lax.broadcasted_iota(jnp.int32, (tc, tc), 1)
    causal = ci <= ri
    b_all = jnp.dot(causal.astype(F32), gt, preferred_element_type=F32, precision=lax.Precision.HIGHEST)

    for h in range(ML_HEADS):
        hs = slice(h * ML_HEAD_DIM, (h + 1) * ML_HEAD_DIM)
        sel_b = lane == GATE_ML_F + h
        sel_b2 = lane == GATE_ML_F2 + h
        sel_i = lane == GATE_ML_I + h
        lmat = jnp.where(sel_b, b_all, jnp.where(sel_i | sel_b2, 1.0, 0.0))
        rmat = jnp.where(sel_b, 1.0, jnp.where(sel_i, gt, jnp.where(sel_b2, -b_all, 0.0)))
        dmat = lax.dot_general(lmat, rmat, (((1,), (1,)), ((), ())), preferred_element_type=F32,
                               precision=lax.Precision.HIGHEST)
        dmat = jnp.where(causal, dmat, NEG_BIG)
        b_col = jnp.sum(jnp.where(sel_b, b_all, 0.0), axis=-1, keepdims=True)
        ig_col = jnp.sum(jnp.where(sel_i, gt, 0.0), axis=-1, keepdims=True)
        m_prev = m_scr[h:h + 1, 0:1]
        m_inter = b_col + m_prev
        m_t = jnp.maximum(m_inter, jnp.max(dmat, axis=-1, keepdims=True))
        w_intra = jnp.exp(dmat - m_t)
        w_inter = jnp.exp(m_inter - m_t)

        xh = xc[:, hs]
        xhb = xh.astype(BF16)
        q = jnp.dot(xhb, wq_ref[h], preferred_element_type=F32)
        k = jnp.dot(xhb, wk_ref[h], preferred_element_type=F32) * (ML_HEAD_DIM ** -0.5)
        qb = q.astype(BF16)
        kb = k.astype(BF16)
        v = mv_ref[:, hs]
        smat = lax.dot_general(qb, kb, (((1,), (1,)), ((), ())), preferred_element_type=F32)
        amat = w_intra * smat
        ch = c_scr[h]
        nh = n_scr[h:h + 1, :]
        num = (jnp.dot(amat.astype(BF16), v, preferred_element_type=F32)
               + w_inter * jnp.dot(qb, ch.astype(BF16), preferred_element_type=F32))
        den = jnp.sum(amat, axis=-1, keepdims=True) + w_inter * jnp.sum(q * nh, axis=-1, keepdims=True)
        hcell = num / jnp.maximum(jnp.abs(den), jnp.exp(-m_t))

        b_last = b_col[tc - 1:tc, :]
        g = b_last - b_col + ig_col
        m_new = jnp.maximum(b_last + m_prev, jnp.max(g, axis=0, keepdims=True))
        wk = jnp.exp(g - m_new)
        decay = jnp.exp(b_last + m_prev - m_new)
        kw = k * wk
        c_scr[h] = decay * ch + lax.dot_general(kw.astype(BF16), v, (((0,), (0,)), ((), ())),
                                                 preferred_element_type=F32)
        n_scr[h:h + 1, :] = decay * nh + jnp.sum(kw, axis=0, keepdims=True)
        m_scr[h:h + 1, :] = jnp.broadcast_to(m_new, (1, LANES))

        y = jax.nn.sigmoid(mo_ref[:, hs]) * hcell
        hm_ref[:, hs] = (_rms(y, gn_ref[:, hs]) + sk_ref[:, hs] * xh).astype(hm_ref.dtype)

    @pl.when(c == pl.num_programs(1) - 1)
    def _():
        cout_ref[...] = c_scr[...]
        nout_ref[...] = n_scr[...]
        mout_ref[...] = m_scr[...]


def _mlstm(mx, mv, mo, gt, cp, c0, n0, m0, wc, bc, wq, wk, gn, sk, tc, n_valid=None):
    bsz, t, _ = mx.shape
    bs = c0.shape[0]
    sb = (lambda b: b) if bs == bsz else (lambda b: 0)
    tok = lambda w: pl.BlockSpec((None, tc, w), lambda b, c: (b, c, 0))
    hd = ML_HEAD_DIM
    st_specs = [pl.BlockSpec((None, ML_HEADS, hd, hd), lambda b, c: (b, 0, 0, 0)),
                pl.BlockSpec((None, 8, LANES), lambda b, c: (b, 0, 0)),
                pl.BlockSpec((None, 8, LANES), lambda b, c: (b, 0, 0))]
    return pl.pallas_call(
        functools.partial(_mlstm_kernel, tc=tc, n_valid=n_valid),
        grid=(bsz, t // tc),
        in_specs=[tok(ML_WIDTH), tok(ML_WIDTH), tok(ML_WIDTH), tok(LANES),
                  pl.BlockSpec((None, 8, ML_WIDTH), lambda b, c: (sb(b), 0, 0)),
                  pl.BlockSpec((None, ML_HEADS, hd, hd), lambda b, c: (sb(b), 0, 0, 0)),
                  pl.BlockSpec((None, 8, LANES), lambda b, c: (sb(b), 0, 0)),
                  pl.BlockSpec((None, 8, LANES), lambda b, c: (sb(b), 0, 0)),
                  _const_spec((CONV_WIDTH, ML_WIDTH)), _const_spec((1, ML_WIDTH)),
                  _const_spec((ML_HEADS, hd, hd)), _const_spec((ML_HEADS, hd, hd)),
                  _const_spec((1, ML_WIDTH)), _const_spec((1, ML_WIDTH))],
        out_specs=[tok(ML_WIDTH)] + st_specs,
        out_shape=[jax.ShapeDtypeStruct((bsz, t, ML_WIDTH), BF16),
                   jax.ShapeDtypeStruct((bsz, ML_HEADS, hd, hd), F32),
                   jax.ShapeDtypeStruct((bsz, 8, LANES), F32),
                   jax.ShapeDtypeStruct((bsz, 8, LANES), F32)],
        scratch_shapes=[pltpu.VMEM((8 + tc, ML_WIDTH), F32), pltpu.VMEM((ML_HEADS, hd, hd), F32),
                        pltpu.VMEM((8, LANES), F32), pltpu.VMEM((8, LANES), F32)],
        compiler_params=pltpu.CompilerParams(dimension_semantics=("arbitrary", "arbitrary")),
        name="mlstm",
    )(mx, mv, mo, gt, cp, c0, n0, m0, wc, bc, wq, wk, gn, sk)


def _post_kernel(a_ref, hm_ref, ga_ref, gb_ref, x_ref, wfp_ref, wmp_ref, wo_ref, wgu_ref, wd_ref,
                 g1_ref, g2_ref, g3_ref, y_ref, *, d_ff, n_ffc):
    ya = jnp.dot(a_ref[...], wfp_ref[...], preferred_element_type=F32)
    yb = jnp.dot(hm_ref[...], wmp_ref[...], preferred_element_type=F32)
    merged = jax.nn.sigmoid(ga_ref[...]) * ya + jax.nn.sigmoid(gb_ref[...]) * yb
    mix = jnp.dot(merged.astype(BF16), wo_ref[...], preferred_element_type=F32)
    x1 = x_ref[...] + _rms(mix, g1_ref[...])
    h2 = _rms(x1, g2_ref[...]).astype(BF16)
    fc = d_ff // n_ffc
    f = None
    for j in range(n_ffc):
        g = jnp.dot(h2, wgu_ref[:, j * fc:(j + 1) * fc], preferred_element_type=F32)
        u = jnp.dot(h2, wgu_ref[:, d_ff + j * fc:d_ff + (j + 1) * fc], preferred_element_type=F32)
        t = (g * jax.nn.sigmoid(g) * u).astype(BF16)
        fj = jnp.dot(t, wd_ref[j * fc:(j + 1) * fc, :], preferred_element_type=F32)
        f = fj if f is None else f + fj
    y_ref[...] = x1 + _rms(f, g3_ref[...])


def _post(a, hm, ga, gb, x, wfp, wmp, wo, wgu, wd, g1, g2, g3, tm):
    n, d = x.shape
    d_ff = wd.shape[0]
    n_ffc = 2 if (d_ff // 2) % LANES == 0 else 1
    row = lambda c: pl.BlockSpec((tm, c), lambda i: (i, 0))
    return pl.pallas_call(
        functools.partial(_post_kernel, d_ff=d_ff, n_ffc=n_ffc),
        grid=(n // tm,),
        in_specs=[row(FOX_WIDTH), row(ML_WIDTH), row(d), row(d), row(d),
                  _const_spec(wfp.shape), _const_spec(wmp.shape), _const_spec(wo.shape),
                  _const_spec(wgu.shape), _const_spec(wd.shape),
                  _const_spec((1, d)), _const_spec((1, d)), _const_spec((1, d))],
        out_specs=row(d),
        out_shape=jax.ShapeDtypeStruct((n, d), F32),
        compiler_params=pltpu.CompilerParams(dimension_semantics=("arbitrary",), vmem_limit_bytes=VMEM_LIMIT),
        name="post_mix_ffn",
    )(a, hm, ga, gb, x, wfp, wmp, wo, wgu, wd, g1, g2, g3)


def _pad_rows(a, rows, axis):
    pad = [(0, 0)] * a.ndim
    pad[axis] = (0, rows - a.shape[axis])
    return jnp.pad(a, pad)


def _row_tile(n, pref):
    t = min(pref, n)
    while n % t:
        t //= 2
    return t


def kernel(x_prompt, x_sample, cache_fox_k, cache_fox_v, cache_fox_logf, state_mlstm_C, state_mlstm_n,
           state_mlstm_m, state_mlstm_conv, meta_tokens, w_in, b_in, g_fq, g_fk, w_conv, b_conv, w_mq, w_mk,
           g_mnorm, skip_m, w_fox_proj, w_ml_proj, w_out, g_pre_mix, g_post_mix, g_pre_ffn, g_post_ffn,
           w_gate_up, w_down):
    assert w_in.shape[0] == 1, "single-layer trunk"
    bsz, seq, d = x_prompt.shape
    dbsz, dseq, _ = x_sample.shape
    past = cache_fox_k.shape[2]
    fw, mw = FOX_WIDTH, ML_WIDTH

    w, b = w_in[0], b_in[0]
    o_ff = 3 * fw
    o_mx = o_ff + FOX_HEADS
    o_mi = o_mx + 3 * mw
    o_mf = o_mi + ML_HEADS
    o_ga = o_mf + ML_HEADS

    def relayout(a):
        gates = jnp.concatenate([a[..., o_ff:o_mx], a[..., o_mi:o_mf], a[..., o_mf:o_ga], a[..., o_mf:o_ga]], axis=-1)
        gates = _pad_rows(gates, LANES, gates.ndim - 1)
        return jnp.concatenate([a[..., :o_ff], a[..., o_mx:o_mi], a[..., o_ga:], gates], axis=-1)

    w_p = relayout(w).astype(BF16)
    b_p = relayout(b)[None, :]
    hid = jnp.arange(fw) // FOX_HEAD_DIM
    hsum = (hid[:, None] == hid[None, :]).astype(BF16)
    gq = g_fq[0].reshape(1, fw)
    gk = g_fk[0].reshape(1, fw)
    gpre = g_pre_mix[0][None, :]
    inproj = lambda x, c0, tm, tpb: _inproj(x, gpre, w_p, b_p, gq, gk, hsum, c0, tm, tpb)

    wc, bc = w_conv[0], b_conv[0][None, :]
    wq, wk = w_mq[0].astype(BF16), w_mk[0].astype(BF16)
    gn, sk = g_mnorm[0].reshape(1, mw), skip_m[0][None, :]
    mlstm = lambda *a, **kw: _mlstm(*a, wc, bc, wq, wk, gn, sk, **kw)

    def conv_hist(rows):
        return jnp.pad(rows, ((0, 0), (8 - (CONV_WIDTH - 1), 0), (0, 0)))

    def m_rows(m):
        return jnp.broadcast_to(_pad_rows(m, 8, 1)[:, :, None], (m.shape[0], 8, LANES))

    mt = 64
    xm = _pad_rows(meta_tokens.astype(F32), mt, 0)
    (_, kf_m, ka_m, vf_m, va_m, mx_m, mv_m, mo_m, _, _, gt_m, fc_m) = inproj(xm, jnp.zeros((1, 1, LANES), F32), mt, 1)
    zc = jnp.zeros((1, ML_HEADS, ML_HEAD_DIM, ML_HEAD_DIM), F32)
    z8 = jnp.zeros((1, 8, LANES), F32)
    _, c_m, n_m, m_m = mlstm(mx_m[None], mv_m[None], mo_m[None], gt_m[None], jnp.zeros((1, 8, mw), F32),
                             zc, z8, z8, tc=mt, n_valid=N_META)

    xp = x_prompt.reshape(bsz * seq, d)
    tm = _row_tile(seq, 512)
    (qa_p, kf_p, ka_p, vf_p, va_p, mx_p, mv_p, mo_p, ga_p, gb_p, gt_p, _) = inproj(
        xp, fc_m[None, N_META - 1:N_META], tm, seq // tm)
    r3 = lambda a: a.reshape(bsz, seq, a.shape[-1])
    mx_p3 = r3(mx_p)
    hm_p, c_p, n_p, m_p = mlstm(mx_p3, r3(mv_p), r3(mo_p), r3(gt_p),
                                conv_hist(mx_m[None, N_META - (CONV_WIDTH - 1):N_META]), c_m, n_m, m_m,
                                tc=_row_tile(seq, 128))
    a_p = _fox(r3(qa_p), ka_m[None, :N_META], va_m[None, :N_META], r3(ka_p), r3(va_p),
               tq=_row_tile(seq, 512), tr=_row_tile(seq, 256), pblk=N_META)

    kc, vc, f_c = _cacheprep(cache_fox_k[0].reshape(dbsz, past, fw), cache_fox_v[0].reshape(dbsz, past, fw),
                             _pad_rows(cache_fox_logf[0].astype(F32), LANES, 2), _row_tile(past, 512))
    xs = x_sample.reshape(dbsz * dseq, d)
    (qa_s, kf_s, ka_s, vf_s, va_s, mx_s, mv_s, mo_s, ga_s, gb_s, gt_s, _) = inproj(xs, f_c, dseq, 1)
    r3s = lambda a: a.reshape(dbsz, dseq, a.shape[-1])
    mx_s3 = r3s(mx_s)
    hm_s, c_s, n_s, m_s = mlstm(mx_s3, r3s(mv_s), r3s(mo_s), r3s(gt_s), conv_hist(state_mlstm_conv[0]),
                                state_mlstm_C[0], _pad_rows(state_mlstm_n[0], 8, 1), m_rows(state_mlstm_m[0]),
                                tc=dseq)
    a_s = _fox(r3s(qa_s), kc, vc, r3s(ka_s), r3s(va_s), tq=dseq, tr=dseq, pblk=_row_tile(past, 512))

    wfp, wmp, wo = w_fox_proj[0].astype(BF16), w_ml_proj[0].astype(BF16), w_out[0].astype(BF16)
    wgu, wd = w_gate_up[0].astype(BF16), w_down[0].astype(BF16)
    g1, g2, g3 = g_post_mix[0][None, :], g_pre_ffn[0][None, :], g_post_ffn[0][None, :]
    post = lambda a, hm, ga, gb, x, tm: _post(a, hm, ga, gb, x, wfp, wmp, wo, wgu, wd, g1, g2, g3, tm)
    y_p = post(a_p.reshape(bsz * seq, fw), hm_p.reshape(bsz * seq, mw), ga_p, gb_p, xp, _row_tile(bsz * seq, 256))
    y_s = post(a_s.reshape(dbsz * dseq, fw), hm_s.reshape(dbsz * dseq, mw), ga_s, gb_s, xs,
               _row_tile(dbsz * dseq, 256))

    def with_meta(meta_rows, main):
        m = jnp.broadcast_to(meta_rows[None], (bsz,) + meta_rows.shape)
        return jnp.concatenate([m, main], axis=1)[None]

    heads = lambda a: a.reshape(a.shape[:-1] + (FOX_HEADS, FOX_HEAD_DIM))
    tail = CONV_WIDTH - 1
    return (y_p.reshape(bsz, seq, d), y_s.reshape(dbsz, dseq, d),
            heads(with_meta(kf_m[:N_META], r3(kf_p))), heads(with_meta(vf_m[:N_META], r3(vf_p))),
            with_meta(gt_m[:N_META, :FOX_HEADS], r3(gt_p)[..., :FOX_HEADS]),
            c_p[None], n_p[None, :, :ML_HEADS], m_p[None, :, :ML_HEADS, 0], mx_p3[None, :, seq - tail:],
            heads(r3s(kf_s))[None], heads(r3s(vf_s))[None], r3s(gt_s)[None, ..., :FOX_HEADS],
            c_s[None], n_s[None, :, :ML_HEADS], m_s[None, :, :ML_HEADS, 0], mx_s3[None, :, dseq - tail:])
```

```python
import functools

import jax
import jax.numpy as jnp
from jax import lax
from jax.experimental import pallas as pl
from jax.experimental.pallas import tpu as pltpu

F32 = jnp.float32
BF16 = jnp.bfloat16

N_META = 16
FOX_HEADS = 8
FOX_HEAD_DIM = 64
FOX_WIDTH = FOX_HEADS * FOX_HEAD_DIM
ML_HEADS = 4
ML_HEAD_DIM = 128
ML_WIDTH = ML_HEADS * ML_HEAD_DIM
CONV_WIDTH = 4
RMS_EPS = 1e-6
PAD_LOG_GATE = -1e30
NEG_BIG = -1e30

LANES = 128
GATE_FOX_F = 0
GATE_ML_I = 8
GATE_ML_F = 12
GATE_ML_F2 = 16

AUG_F = FOX_HEAD_DIM
AUG_1 = FOX_HEAD_DIM + 3
AUG_END = FOX_HEAD_DIM + 6

VMEM_LIMIT = 56 * 1024 * 1024
FOX_WIDE_KEYS = 1024
MXU_TILE = 256
FF_CHUNKS = 3


def _const_spec(shape):
    nd = len(shape)
    return pl.BlockSpec(shape, lambda *_: (0,) * nd, pipeline_mode=pl.Buffered(1))


def _rms(x, g):
    return x * lax.rsqrt(jnp.mean(x * x, axis=-1, keepdims=True) + RMS_EPS) * g


def _split3(x):
    hi = x.astype(BF16).astype(F32)
    r = x - hi
    mid = r.astype(BF16).astype(F32)
    lo = (r - mid).astype(BF16).astype(F32)
    return hi, mid, lo


def _cumsum_rows(lg, cur):
    tm = lg.shape[0]
    blk = min(tm, LANES)
    r = lax.broadcasted_iota(jnp.int32, (blk, blk), 0)
    c = lax.broadcasted_iota(jnp.int32, (blk, blk), 1)
    tril = (c <= r).astype(BF16)
    outs = []
    for s in range(tm // blk):
        hi, mid, lo = _split3(lg[s * blk:(s + 1) * blk])
        f = (jnp.dot(tril, lo.astype(BF16), preferred_element_type=F32)
             + jnp.dot(tril, mid.astype(BF16), preferred_element_type=F32)
             + jnp.dot(tril, hi.astype(BF16), preferred_element_type=F32)) + cur
        outs.append(f)
        cur = f[blk - 1:blk]
    return (outs[0] if len(outs) == 1 else jnp.concatenate(outs, axis=0)), cur


def _store_attention_operands(qn, kn, v, fcol, q_ref, k_ref, v_ref):
    tm = fcol.shape[0]
    lane = lax.broadcasted_iota(jnp.int32, (tm, LANES), 1)
    data = lane < FOX_HEAD_DIM
    for h in range(FOX_HEADS):
        hi, mid, lo = _split3(jnp.sum(jnp.where(lane == h, fcol, 0.0), axis=-1, keepdims=True))
        t, odd = divmod(h, 2)

        def head_tile(z):
            x = z[:, t * LANES:(t + 1) * LANES]
            return pltpu.roll(x, FOX_HEAD_DIM, 1) if odd else x

        hs = slice(h * LANES, (h + 1) * LANES)
        if q_ref is not None:
            aug = jnp.where(lane == AUG_F, hi, jnp.where(lane == AUG_F + 1, mid, jnp.where(
                lane == AUG_F + 2, lo, jnp.where(lane < AUG_END, 1.0, 0.0))))
            q_ref[:, hs] = jnp.where(data, head_tile(qn), aug).astype(BF16)
        aug = jnp.where(lane < AUG_1, 1.0, jnp.where(lane == AUG_1, -hi, jnp.where(
            lane == AUG_1 + 1, -mid, jnp.where(lane == AUG_1 + 2, -lo, 0.0))))
        k_ref[:, hs] = jnp.where(data, head_tile(kn), aug).astype(BF16)
        v_ref[:, hs] = jnp.where(data, head_tile(v), jnp.where(lane == FOX_HEAD_DIM, 1.0, 0.0)).astype(BF16)


def _inproj_kernel(*refs, d_model, nt, kv_t):
    (x_ref, gpre_ref, w_ref, b_ref, gq_ref, gk_ref, hs_ref, c0_ref), rest = refs[:8], refs[8:]
    if kv_t:
        kmeta_ref, vmeta_ref, lmeta_ref = rest[:3]
        (q_ref, kf_ref, kb_ref, vf_ref, vb_ref, mx_ref, mv_ref, mo_ref, ga_ref, gb_ref, gt_ref, fc_ref, lt_ref,
         carry, kcar, vcar, lcar) = rest[3:]
    else:
        (q_ref, kf_ref, kb_ref, vf_ref, vb_ref, mx_ref, mv_ref, mo_ref, ga_ref, gb_ref, gt_ref, fc_ref, carry) = rest
    fw, mw = FOX_WIDTH, ML_WIDTH
    offs = [0, fw, 2 * fw, 3 * fw, 3 * fw + mw, 3 * fw + 2 * mw, 3 * fw + 3 * mw,
            3 * fw + 3 * mw + d_model, 3 * fw + 3 * mw + 2 * d_model, 3 * fw + 3 * mw + 2 * d_model + LANES]
    r = pl.program_id(1)

    @pl.when(r == 0)
    def _():
        carry[...] = c0_ref[...]
        if kv_t:
            kcar[...] = kmeta_ref[...]
            vcar[...] = vmeta_ref[...]
            lcar[...] = lmeta_ref[...]

    def store_shifted(xt, car, out_ref):
        sh = pltpu.roll(xt, N_META, 1)
        lane = lax.broadcasted_iota(jnp.int32, (xt.shape[0], LANES), 1)
        out_ref[:, :LANES] = jnp.where(lane < N_META, car[...], sh[:, :LANES])
        out_ref[:, LANES:] = sh[:, LANES:]
        car[...] = sh[:, :LANES]

    @pl.when(r < nt)
    def _():
        h = _rms(x_ref[...], gpre_ref[...]).astype(BF16)

        def seg(i):
            a, b = offs[i], offs[i + 1]
            return jnp.dot(h, w_ref[:, a:b], preferred_element_type=F32) + b_ref[:, a:b]

        def headnorm(z, g):
            ms = jnp.dot((z * z).astype(BF16), hs_ref[...], preferred_element_type=F32) * (1.0 / FOX_HEAD_DIM)
            return z * lax.rsqrt(ms + RMS_EPS) * g

        g = seg(8)
        lane = lax.broadcasted_iota(jnp.int32, g.shape, 1)
        raw = (lane >= GATE_ML_I) & (lane < GATE_ML_F)
        gt = jnp.where(raw, g, jax.nn.log_sigmoid(g))
        gt_ref[...] = gt
        fcol, cur = _cumsum_rows(gt, carry[...])
        carry[...] = cur
        fc_ref[...] = fcol

        q = headnorm(seg(0), gq_ref[...]) * (FOX_HEAD_DIM ** -0.5)
        k = headnorm(seg(1), gk_ref[...])
        v = seg(2)
        if kv_t:
            store_shifted(k.T, kcar, kf_ref)
            store_shifted(v.T, vcar, vf_ref)
            store_shifted(gt.T[:FOX_HEADS], lcar, lt_ref)
        else:
            kf_ref[...] = k
            vf_ref[...] = v
        _store_attention_operands(q, k, v, fcol, q_ref, kb_ref, vb_ref)
        mx_ref[...] = seg(3)
        mv_ref[...] = seg(4).astype(BF16)
        mo_ref[...] = jax.nn.sigmoid(seg(5)).astype(BF16)
        ga_ref[...] = jax.nn.sigmoid(seg(6)).astype(BF16)
        gb_ref[...] = jax.nn.sigmoid(seg(7)).astype(BF16)

    if kv_t:
        @pl.when(r == nt)
        def _():
            kf_ref[:, :LANES] = kcar[...]
            vf_ref[:, :LANES] = vcar[...]
            lt_ref[:, :LANES] = lcar[...]


def _inproj(x, gpre, w, b, gq, gk, hs, c0, tm, nt, meta_t=None):
    n, d = x.shape
    nb = n // (nt * tm)
    dinp = w.shape[1]
    fw, mw = FOX_WIDTH, ML_WIDTH
    aw = FOX_HEADS * LANES
    kv_t = meta_t is not None
    tile = lambda b, r: b * nt + jnp.minimum(r, nt - 1)
    row = lambda c: pl.BlockSpec((tm, c), lambda b, r: (tile(b, r), 0))
    cb = (lambda b: b) if c0.shape[0] > 1 else (lambda b: 0)
    lp = N_META + nt * tm
    tmin = lambda c: pl.BlockSpec((None, c, tm), lambda b, r: (b, 0, r))
    flat = lambda c, t: (row(c), jax.ShapeDtypeStruct((n, c), t))
    tok_minor = lambda c: (tmin(c), jax.ShapeDtypeStruct((nb, c, lp), F32))
    outs = [flat(aw, BF16), tok_minor(fw) if kv_t else flat(fw, F32), flat(aw, BF16),
            tok_minor(fw) if kv_t else flat(fw, F32), flat(aw, BF16), flat(mw, F32), flat(mw, BF16), flat(mw, BF16),
            flat(d, BF16), flat(d, BF16), flat(LANES, F32), flat(LANES, F32)]
    in_specs = [row(d), _const_spec((1, d)), _const_spec((d, dinp)), _const_spec((1, dinp)),
                _const_spec((1, fw)), _const_spec((1, fw)), _const_spec((fw, fw)),
                pl.BlockSpec((None, 1, LANES), lambda b, r: (cb(b), 0, 0))]
    scratch = [pltpu.VMEM((1, LANES), F32)]
    args = [x, gpre, w, b, gq, gk, hs, c0]
    if kv_t:
        outs.append(tok_minor(FOX_HEADS))
        in_specs += [_const_spec((fw, LANES)), _const_spec((fw, LANES)), _const_spec((FOX_HEADS, LANES))]
        scratch += [pltpu.VMEM((fw, LANES), F32), pltpu.VMEM((fw, LANES), F32), pltpu.VMEM((FOX_HEADS, LANES), F32)]
        args += list(meta_t)
    return pl.pallas_call(
        functools.partial(_inproj_kernel, d_model=d, nt=nt, kv_t=kv_t),
        grid=(nb, nt + (1 if kv_t else 0)),
        in_specs=in_specs,
        out_specs=[o[0] for o in outs],
        out_shape=[o[1] for o in outs],
        scratch_shapes=scratch,
        compiler_params=pltpu.CompilerParams(dimension_semantics=("arbitrary", "arbitrary"),
                                             vmem_limit_bytes=VMEM_LIMIT),
        name="inproj",
    )(*args)


def _cacheprep_kernel(k_ref, v_ref, lf_ref, ka_ref, va_ref, fl_ref, carry):
    @pl.when(pl.program_id(1) == 0)
    def _():
        carry[...] = jnp.zeros_like(carry)

    fcol, cur = _cumsum_rows(lf_ref[...], carry[...])
    carry[...] = cur
    fl_ref[...] = cur
    _store_attention_operands(None, k_ref[...], v_ref[...], fcol, None, ka_ref, va_ref)


def _cacheprep(ck, cv, clf, tp):
    bsz, plen, _ = ck.shape
    aw = FOX_HEADS * LANES
    return pl.pallas_call(
        _cacheprep_kernel,
        grid=(bsz, plen // tp),
        in_specs=[pl.BlockSpec((None, tp, FOX_WIDTH), lambda b, i: (b, i, 0)),
                  pl.BlockSpec((None, tp, FOX_WIDTH), lambda b, i: (b, i, 0)),
                  pl.BlockSpec((None, tp, LANES), lambda b, i: (b, i, 0))],
        out_specs=[pl.BlockSpec((None, tp, aw), lambda b, i: (b, i, 0)),
                   pl.BlockSpec((None, tp, aw), lambda b, i: (b, i, 0)),
                   pl.BlockSpec((None, 1, LANES), lambda b, i: (b, 0, 0))],
        out_shape=[jax.ShapeDtypeStruct((bsz, plen, aw), BF16), jax.ShapeDtypeStruct((bsz, plen, aw), BF16),
                   jax.ShapeDtypeStruct((bsz, 1, LANES), F32)],
        scratch_shapes=[pltpu.VMEM((1, LANES), F32)],
        compiler_params=pltpu.CompilerParams(dimension_semantics=("arbitrary", "arbitrary")),
        name="cache_prep",
    )(ck, cv, clf)


def _fox_kernel(q_ref, kp_ref, vp_ref, k_ref, v_ref, o_ref, *, tq, tr, tkb, pblk, n_pblk):
    i = pl.program_id(2)
    nr = tq // tr
    chains = [(hh, r) for hh in range(2) for r in range(nr)]
    qs = [q_ref[r * tr:(r + 1) * tr, hh * LANES:(hh + 1) * LANES] for hh, r in chains]
    ri = lax.broadcasted_iota(jnp.int32, (tr, tr), 0)
    ci = lax.broadcasted_iota(jnp.int32, (tr, tr), 1)
    causal = ci <= ri

    def upd1(n, state, kb, vb, mask=None):
        m, acc = state
        hh = chains[n][0]
        hs = slice(hh * LANES, (hh + 1) * LANES)
        s = lax.dot_general(qs[n], kb[:, hs], (((1,), (1,)), ((), ())), preferred_element_type=F32)
        if mask is not None:
            s = jnp.where(mask, s, NEG_BIG)
        m_new = jnp.maximum(m, jnp.max(s, axis=-1, keepdims=True))
        alpha = jnp.exp(m - m_new)
        pm = jnp.exp(s - m_new)
        acc = alpha * acc + jnp.dot(pm.astype(BF16), vb[:, hs], preferred_element_type=F32)
        return m_new, acc

    def upd(carry, kb, vb):
        return tuple(upd1(n, carry[n], kb, vb) for n in range(len(chains)))

    carry = tuple((jnp.full((tr, 1), NEG_BIG, F32), jnp.zeros((tr, LANES), F32)) for _ in chains)
    for jb in range(n_pblk):
        sl = slice(jb * pblk, (jb + 1) * pblk)
        carry = upd(carry, kp_ref[sl, :], vp_ref[sl, :])

    def body_wide(j, carry):
        off = pl.multiple_of(j * tkb, tkb)
        return upd(carry, k_ref[pl.ds(off, tkb), :], v_ref[pl.ds(off, tkb), :])

    def body(j, carry):
        off = pl.multiple_of(j * tq, tq)
        return upd(carry, k_ref[pl.ds(off, tq), :], v_ref[pl.ds(off, tq), :])

    n_wide = i // (tkb // tq)
    carry = lax.fori_loop(0, n_wide, body_wide, carry)
    carry = lax.fori_loop(n_wide * (tkb // tq), i, body, carry)

    carry = list(carry)
    for c in range(nr):
        off = pl.multiple_of(i * tq + c * tr, tr)
        kb = k_ref[pl.ds(off, tr), :]
        vb = v_ref[pl.ds(off, tr), :]
        for n, (hh, r) in enumerate(chains):
            if r >= c:
                carry[n] = upd1(n, carry[n], kb, vb, mask=causal if r == c else None)

    lane = lax.broadcasted_iota(jnp.int32, (tr, LANES), 1)
    outs = []
    for n in range(len(chains)):
        acc = carry[n][1]
        l = jnp.sum(jnp.where(lane == FOX_HEAD_DIM, acc, 0.0), axis=-1, keepdims=True)
        outs.append(acc / l)
    for r in range(nr):
        o_ref[r * tr:(r + 1) * tr, :] = jnp.where(lane < FOX_HEAD_DIM, outs[r],
                                                  pltpu.roll(outs[nr + r], FOX_HEAD_DIM, 1)).astype(o_ref.dtype)


def _fox(q, kp, vp, k, v, tq, tr, pblk):
    bsz, t, _ = q.shape
    bp, plen, _ = kp.shape
    nq = t // tq
    n_pblk = plen // pblk
    pw = 2 * LANES
    pb = (lambda b: b) if bp == bsz else (lambda b: 0)
    return pl.pallas_call(
        functools.partial(_fox_kernel, tq=tq, tr=tr, tkb=_row_tile(t, FOX_WIDE_KEYS), pblk=pblk, n_pblk=n_pblk),
        grid=(bsz, FOX_HEADS // 2, nq),
        in_specs=[pl.BlockSpec((None, tq, pw), lambda b, p, i: (b, i, p)),
                  pl.BlockSpec((None, plen, pw), lambda b, p, i: (pb(b), 0, p)),
                  pl.BlockSpec((None, plen, pw), lambda b, p, i: (pb(b), 0, p)),
                  pl.BlockSpec((None, t, pw), lambda b, p, i: (b, 0, p)),
                  pl.BlockSpec((None, t, pw), lambda b, p, i: (b, 0, p))],
        out_specs=pl.BlockSpec((None, tq, LANES), lambda b, p, i: (b, i, p)),
        out_shape=jax.ShapeDtypeStruct((bsz, t, FOX_WIDTH), BF16),
        compiler_params=pltpu.CompilerParams(dimension_semantics=("arbitrary", "arbitrary", "arbitrary"),
                                             vmem_limit_bytes=VMEM_LIMIT),
        name="fox_attention",
    )(q, kp, vp, k, v)


def _mlstm_kernel(mx_ref, mv_ref, mo_ref, gt_ref, cp_ref, c0_ref, n0_ref, m0_ref,
                  wc_ref, bc_ref, wq_ref, wk_ref, gn_ref, sk_ref,
                  hm_ref, cout_ref, nout_ref, mout_ref,
                  xbuf, c_scr, n_scr, m_scr, *, tc, n_valid):
    c = pl.program_id(1)

    @pl.when(c == 0)
    def _():
        xbuf[0:8, :] = cp_ref[...]
        c_scr[...] = c0_ref[...]
        n_scr[...] = n0_ref[...]
        m_scr[...] = m0_ref[...]

    xbuf[8:8 + tc, :] = mx_ref[...]
    xc = bc_ref[...]
    for i in range(CONV_WIDTH):
        xc = xc + xbuf[5 + i:5 + i + tc, :] * wc_ref[i:i + 1, :]
    xc = xc * jax.nn.sigmoid(xc)
    tail = xbuf[tc:tc + 8, :]
    xbuf[0:8, :] = tail

    gt = gt_ref[...]
    lane = lax.broadcasted_iota(jnp.int32, (tc, LANES), 1)
    if n_valid is not None:
        rowi = lax.broadcasted_iota(jnp.int32, (tc, LANES), 0)
        is_i = (lane >= GATE_ML_I) & (lane < GATE_ML_F)
        gt = jnp.where(rowi < n_valid, gt, jnp.where(is_i, PAD_LOG_GATE, 0.0))
    ri = lax.broadcasted_iota(jnp.int32, (tc, tc), 0)
    ci = lax.broadcasted_iota(jnp.int32, (tc, tc), 1)
    causal = ci <= ri
    b_all = jnp.dot(causal.astype(F32), gt, preferred_element_type=F32, precision=lax.Precision.HIGHEST)

    for h in range(ML_HEADS):
        hs = slice(h * ML_HEAD_DIM, (h + 1) * ML_HEAD_DIM)
        sel_b = lane == GATE_ML_F + h
        sel_b2 = lane == GATE_ML_F2 + h
        sel_i = lane == GATE_ML_I + h
        lmat = jnp.where(sel_b, b_all, jnp.where(sel_i | sel_b2, 1.0, 0.0))
        rmat = jnp.where(sel_b, 1.0, jnp.where(sel_i, gt, jnp.where(sel_b2, -b_all, 0.0)))
        dmat = lax.dot_general(lmat, rmat, (((1,), (1,)), ((), ())), preferred_element_type=F32,
                               precision=lax.Precision.HIGHEST)
        dmat = jnp.where(causal, dmat, NEG_BIG)
        b_col = jnp.sum(jnp.where(sel_b, b_all, 0.0), axis=-1, keepdims=True)
        ig_col = jnp.sum(jnp.where(sel_i, gt, 0.0), axis=-1, keepdims=True)
        m_prev = m_scr[h:h + 1, 0:1]
        m_inter = b_col + m_prev
        m_t = jnp.maximum(m_inter, jnp.max(dmat, axis=-1, keepdims=True))
        w_intra = jnp.exp(dmat - m_t)
        w_inter = jnp.exp(m_inter - m_t)

        xh = xc[:, hs]
        xhb = xh.astype(BF16)
        q = jnp.dot(xhb, wq_ref[h], preferred_element_type=F32)
        k = jnp.dot(xhb, wk_ref[h], preferred_element_type=F32) * (ML_HEAD_DIM ** -0.5)
        qb = q.astype(BF16)
        kb = k.astype(BF16)
        v = mv_ref[:, hs]
        smat = lax.dot_general(qb, kb, (((1,), (1,)), ((), ())), preferred_element_type=F32)
        amat = w_intra * smat
        ch = c_scr[h]
        nh = n_scr[h:h + 1, :]
        num = (jnp.dot(amat.astype(BF16), v, preferred_element_type=F32)
               + w_inter * jnp.dot(qb, ch.astype(BF16), preferred_element_type=F32))
        den = jnp.sum(amat, axis=-1, keepdims=True) + w_inter * jnp.sum(q * nh, axis=-1, keepdims=True)
        hcell = num / jnp.maximum(jnp.abs(den), jnp.exp(-m_t))

        b_last = b_col[tc - 1:tc, :]
        g = b_last - b_col + ig_col
        m_new = jnp.maximum(b_last + m_prev, jnp.max(g, axis=0, keepdims=True))
        wk = jnp.exp(g - m_new)
        decay = jnp.exp(b_last + m_prev - m_new)
        kw = k * wk
        c_scr[h] = decay * ch + lax.dot_general(kw.astype(BF16), v, (((0,), (0,)), ((), ())),
                                                 preferred_element_type=F32)
        n_scr[h:h + 1, :] = decay * nh + jnp.sum(kw, axis=0, keepdims=True)
        m_scr[h:h + 1, :] = jnp.broadcast_to(m_new, (1, LANES))

        y = mo_ref[:, hs].astype(F32) * hcell
        hm_ref[:, hs] = (_rms(y, gn_ref[:, hs]) + sk_ref[:, hs] * xh).astype(hm_ref.dtype)

    @pl.when(c == pl.num_programs(1) - 1)
    def _():
        cout_ref[...] = c_scr[...]
        nout_ref[...] = n_scr[...]
        mout_ref[...] = m_scr[...]


def _mlstm(mx, mv, mo, gt, cp, c0, n0, m0, wc, bc, wq, wk, gn, sk, tc, n_valid=None):
    bsz, t, _ = mx.shape
    bs = c0.shape[0]
    sb = (lambda b: b) if bs == bsz else (lambda b: 0)
    tok = lambda w: pl.BlockSpec((None, tc, w), lambda b, c: (b, c, 0))
    hd = ML_HEAD_DIM
    st_specs = [pl.BlockSpec((None, ML_HEADS, hd, hd), lambda b, c: (b, 0, 0, 0)),
                pl.BlockSpec((None, 8, LANES), lambda b, c: (b, 0, 0)),
                pl.BlockSpec((None, 8, LANES), lambda b, c: (b, 0, 0))]
    return pl.pallas_call(
        functools.partial(_mlstm_kernel, tc=tc, n_valid=n_valid),
        grid=(bsz, t // tc),
        in_specs=[tok(ML_WIDTH), tok(ML_WIDTH), tok(ML_WIDTH), tok(LANES),
                  pl.BlockSpec((None, 8, ML_WIDTH), lambda b, c: (sb(b), 0, 0)),
                  pl.BlockSpec((None, ML_HEADS, hd, hd), lambda b, c: (sb(b), 0, 0, 0)),
                  pl.BlockSpec((None, 8, LANES), lambda b, c: (sb(b), 0, 0)),
                  pl.BlockSpec((None, 8, LANES), lambda b, c: (sb(b), 0, 0)),
                  _const_spec((CONV_WIDTH, ML_WIDTH)), _const_spec((1, ML_WIDTH)),
                  _const_spec((ML_HEADS, hd, hd)), _const_spec((ML_HEADS, hd, hd)),
                  _const_spec((1, ML_WIDTH)), _const_spec((1, ML_WIDTH))],
        out_specs=[tok(ML_WIDTH)] + st_specs,
        out_shape=[jax.ShapeDtypeStruct((bsz, t, ML_WIDTH), BF16),
                   jax.ShapeDtypeStruct((bsz, ML_HEADS, hd, hd), F32),
                   jax.ShapeDtypeStruct((bsz, 8, LANES), F32),
                   jax.ShapeDtypeStruct((bsz, 8, LANES), F32)],
        scratch_shapes=[pltpu.VMEM((8 + tc, ML_WIDTH), F32), pltpu.VMEM((ML_HEADS, hd, hd), F32),
                        pltpu.VMEM((8, LANES), F32), pltpu.VMEM((8, LANES), F32)],
        compiler_params=pltpu.CompilerParams(dimension_semantics=("arbitrary", "arbitrary")),
        name="mlstm",
    )(mx, mv, mo, gt, cp, c0, n0, m0, wc, bc, wq, wk, gn, sk)


def _post_kernel(a_ref, hm_ref, ga_ref, gb_ref, x_ref, wfp_ref, wmp_ref, wo_ref, wgu_ref, wd_ref,
                 g1_ref, g2_ref, g3_ref, y_ref, *, d_ff, ff_cuts):
    ya = jnp.dot(a_ref[...], wfp_ref[...], preferred_element_type=F32)
    yb = jnp.dot(hm_ref[...], wmp_ref[...], preferred_element_type=F32)
    merged = ga_ref[...].astype(F32) * ya + gb_ref[...].astype(F32) * yb
    mix = jnp.dot(merged.astype(BF16), wo_ref[...], preferred_element_type=F32)
    x1 = x_ref[...] + _rms(mix, g1_ref[...])
    h2 = _rms(x1, g2_ref[...]).astype(BF16)
    f = None
    for lo, hi in zip(ff_cuts[:-1], ff_cuts[1:]):
        g = jnp.dot(h2, wgu_ref[:, lo:hi], preferred_element_type=F32)
        u = jnp.dot(h2, wgu_ref[:, d_ff + lo:d_ff + hi], preferred_element_type=F32)
        t = (g * jax.nn.sigmoid(g) * u).astype(BF16)
        fj = jnp.dot(t, wd_ref[lo:hi, :], preferred_element_type=F32)
        f = fj if f is None else f + fj
    y_ref[...] = x1 + _rms(f, g3_ref[...])


def _post(a, hm, ga, gb, x, wfp, wmp, wo, wgu, wd, g1, g2, g3, tm):
    n, d = x.shape
    d_ff = wd.shape[0]
    nmt = d_ff // MXU_TILE
    ff_cuts = tuple(MXU_TILE * ((nmt * j) // FF_CHUNKS) for j in range(FF_CHUNKS)) + (d_ff,) \
        if d_ff % MXU_TILE == 0 else (0, d_ff)
    row = lambda c: pl.BlockSpec((tm, c), lambda i: (i, 0))
    return pl.pallas_call(
        functools.partial(_post_kernel, d_ff=d_ff, ff_cuts=ff_cuts),
        grid=(n // tm,),
        in_specs=[row(FOX_WIDTH), row(ML_WIDTH), row(d), row(d), row(d),
                  _const_spec(wfp.shape), _const_spec(wmp.shape), _const_spec(wo.shape),
                  _const_spec(wgu.shape), _const_spec(wd.shape),
                  _const_spec((1, d)), _const_spec((1, d)), _const_spec((1, d))],
        out_specs=row(d),
        out_shape=jax.ShapeDtypeStruct((n, d), F32),
        compiler_params=pltpu.CompilerParams(dimension_semantics=("arbitrary",), vmem_limit_bytes=VMEM_LIMIT),
        name="post_mix_ffn",
    )(a, hm, ga, gb, x, wfp, wmp, wo, wgu, wd, g1, g2, g3)


def _pad_rows(a, rows, axis):
    pad = [(0, 0)] * a.ndim
    pad[axis] = (0, rows - a.shape[axis])
    return jnp.pad(a, pad)


def _row_tile(n, pref):
    t = min(pref, n)
    while n % t:
        t //= 2
    return t


def kernel(x_prompt, x_sample, cache_fox_k, cache_fox_v, cache_fox_logf, state_mlstm_C, state_mlstm_n,
           state_mlstm_m, state_mlstm_conv, meta_tokens, w_in, b_in, g_fq, g_fk, w_conv, b_conv, w_mq, w_mk,
           g_mnorm, skip_m, w_fox_proj, w_ml_proj, w_out, g_pre_mix, g_post_mix, g_pre_ffn, g_post_ffn,
           w_gate_up, w_down):
    assert w_in.shape[0] == 1, "single-layer trunk"
    bsz, seq, d = x_prompt.shape
    dbsz, dseq, _ = x_sample.shape
    past = cache_fox_k.shape[2]
    fw, mw = FOX_WIDTH, ML_WIDTH

    w, b = w_in[0], b_in[0]
    o_ff = 3 * fw
    o_mx = o_ff + FOX_HEADS
    o_mi = o_mx + 3 * mw
    o_mf = o_mi + ML_HEADS
    o_ga = o_mf + ML_HEADS

    def relayout(a):
        gates = jnp.concatenate([a[..., o_ff:o_mx], a[..., o_mi:o_mf], a[..., o_mf:o_ga], a[..., o_mf:o_ga]], axis=-1)
        gates = _pad_rows(gates, LANES, gates.ndim - 1)
        return jnp.concatenate([a[..., :o_ff], a[..., o_mx:o_mi], a[..., o_ga:], gates], axis=-1)

    w_p = relayout(w).astype(BF16)
    b_p = relayout(b)[None, :]
    hid = jnp.arange(fw) // FOX_HEAD_DIM
    hsum = (hid[:, None] == hid[None, :]).astype(BF16)
    gq = g_fq[0].reshape(1, fw)
    gk = g_fk[0].reshape(1, fw)
    gpre = g_pre_mix[0][None, :]
    inproj = lambda x, c0, tm, nt, **kw: _inproj(x, gpre, w_p, b_p, gq, gk, hsum, c0, tm, nt, **kw)

    wc, bc = w_conv[0], b_conv[0][None, :]
    wq, wk = w_mq[0].astype(BF16), w_mk[0].astype(BF16)
    gn, sk = g_mnorm[0].reshape(1, mw), skip_m[0][None, :]
    mlstm = lambda *a, **kw: _mlstm(*a, wc, bc, wq, wk, gn, sk, **kw)

    def conv_hist(rows):
        return jnp.pad(rows, ((0, 0), (8 - (CONV_WIDTH - 1), 0), (0, 0)))

    def m_rows(m):
        return jnp.broadcast_to(_pad_rows(m, 8, 1)[:, :, None], (m.shape[0], 8, LANES))

    mt = 64
    xm = _pad_rows(meta_tokens.astype(F32), mt, 0)
    (_, kf_m, ka_m, vf_m, va_m, mx_m, mv_m, mo_m, _, _, gt_m, fc_m) = inproj(xm, jnp.zeros((1, 1, LANES), F32), mt, 1)
    zc = jnp.zeros((1, ML_HEADS, ML_HEAD_DIM, ML_HEAD_DIM), F32)
    z8 = jnp.zeros((1, 8, LANES), F32)
    _, c_m, n_m, m_m = mlstm(mx_m[None], mv_m[None], mo_m[None], gt_m[None], jnp.zeros((1, 8, mw), F32),
                             zc, z8, z8, tc=mt, n_valid=N_META)

    xp = x_prompt.reshape(bsz * seq, d)
    tm = _row_tile(seq, 512)
    lanes_t = lambda rows: _pad_rows(rows[:N_META].T, LANES, 1)
    (qa_p, kt_p, ka_p, vt_p, va_p, mx_p, mv_p, mo_p, ga_p, gb_p, gt_p, _, lt_p) = inproj(
        xp, fc_m[None, N_META - 1:N_META], tm, seq // tm,
        meta_t=(lanes_t(kf_m), lanes_t(vf_m), lanes_t(gt_m[:, :FOX_HEADS])))
    r3 = lambda a: a.reshape(bsz, seq, a.shape[-1])
    mx_p3 = r3(mx_p)
    hm_p, c_p, n_p, m_p = mlstm(mx_p3, r3(mv_p), r3(mo_p), r3(gt_p),
                                conv_hist(mx_m[None, N_META - (CONV_WIDTH - 1):N_META]), c_m, n_m, m_m,
                                tc=_row_tile(seq, 256))
    a_p = _fox(r3(qa_p), ka_m[None, :N_META], va_m[None, :N_META], r3(ka_p), r3(va_p),
               tq=_row_tile(seq, 512), tr=_row_tile(seq, 256), pblk=N_META)

    kc, vc, f_c = _cacheprep(cache_fox_k[0].reshape(dbsz, past, fw), cache_fox_v[0].reshape(dbsz, past, fw),
                             _pad_rows(cache_fox_logf[0].astype(F32), LANES, 2), _row_tile(past, 512))
    xs = x_sample.reshape(dbsz * dseq, d)
    (qa_s, kf_s, ka_s, vf_s, va_s, mx_s, mv_s, mo_s, ga_s, gb_s, gt_s, _) = inproj(xs, f_c, dseq, 1)
    r3s = lambda a: a.reshape(dbsz, dseq, a.shape[-1])
    mx_s3 = r3s(mx_s)
    hm_s, c_s, n_s, m_s = mlstm(mx_s3, r3s(mv_s), r3s(mo_s), r3s(gt_s), conv_hist(state_mlstm_conv[0]),
                                state_mlstm_C[0], _pad_rows(state_mlstm_n[0], 8, 1), m_rows(state_mlstm_m[0]),
                                tc=dseq)
    a_s = _fox(r3s(qa_s), kc, vc, r3s(ka_s), r3s(va_s), tq=dseq, tr=dseq, pblk=_row_tile(past, 512))

    wfp, wmp, wo = w_fox_proj[0].astype(BF16), w_ml_proj[0].astype(BF16), w_out[0].astype(BF16)
    wgu, wd = w_gate_up[0].astype(BF16), w_down[0].astype(BF16)
    g1, g2, g3 = g_post_mix[0][None, :], g_pre_ffn[0][None, :], g_post_ffn[0][None, :]
    post = lambda a, hm, ga, gb, x, tm: _post(a, hm, ga, gb, x, wfp, wmp, wo, wgu, wd, g1, g2, g3, tm)
    y_p = post(a_p.reshape(bsz * seq, fw), hm_p.reshape(bsz * seq, mw), ga_p, gb_p, xp, _row_tile(bsz * seq, 512))
    y_s = post(a_s.reshape(dbsz * dseq, fw), hm_s.reshape(dbsz * dseq, mw), ga_s, gb_s, xs,
               _row_tile(dbsz * dseq, 512))

    def tokens_major(a):
        return a.reshape(bsz, FOX_HEADS, FOX_HEAD_DIM, a.shape[-1]).transpose(0, 3, 1, 2)[None]

    heads = lambda a: a.reshape(a.shape[:-1] + (FOX_HEADS, FOX_HEAD_DIM))
    tail = CONV_WIDTH - 1
    return (y_p.reshape(bsz, seq, d), y_s.reshape(dbsz, dseq, d),
            tokens_major(kt_p), tokens_major(vt_p), lt_p.transpose(0, 2, 1)[None],
            c_p[None], n_p[None, :, :ML_HEADS], m_p[None, :, :ML_HEADS, 0], mx_p3[None, :, seq - tail:],
            heads(r3s(kf_s))[None], heads(r3s(vf_s))[None], r3s(gt_s)[None, ..., :FOX_HEADS],
            c_s[None], n_s[None, :, :ML_HEADS], m_s[None, :, :ML_HEADS, 0], mx_s3[None, :, dseq - tail:])
```

```python
import functools

import jax
import jax.numpy as jnp
from jax import lax
from jax.experimental import pallas as pl
from jax.experimental.pallas import tpu as pltpu

F32 = jnp.float32
BF16 = jnp.bfloat16

N_META = 16
FOX_HEADS = 8
FOX_HEAD_DIM = 64
FOX_WIDTH = FOX_HEADS * FOX_HEAD_DIM
ML_HEADS = 4
ML_HEAD_DIM = 128
ML_WIDTH = ML_HEADS * ML_HEAD_DIM
CONV_WIDTH = 4
RMS_EPS = 1e-6
PAD_LOG_GATE = -1e30
NEG_BIG = -1e30

LANES = 128
GATE_FOX_F = 0
GATE_ML_I = 8
GATE_ML_F = 12
GATE_ML_F2 = 16

AUG_F = FOX_HEAD_DIM
AUG_1 = FOX_HEAD_DIM + 3
AUG_END = FOX_HEAD_DIM + 6

VMEM_LIMIT = 56 * 1024 * 1024
FOX_WIDE_KEYS = 1024
MXU_TILE = 256
FF_CHUNKS = 3


def _const_spec(shape):
    nd = len(shape)
    return pl.BlockSpec(shape, lambda *_: (0,) * nd, pipeline_mode=pl.Buffered(1))


def _rms(x, g):
    return x * lax.rsqrt(jnp.mean(x * x, axis=-1, keepdims=True) + RMS_EPS) * g


def _split3(x):
    hi = x.astype(BF16).astype(F32)
    r = x - hi
    mid = r.astype(BF16).astype(F32)
    lo = (r - mid).astype(BF16).astype(F32)
    return hi, mid, lo


def _cumsum_rows(lg, cur):
    tm = lg.shape[0]
    blk = min(tm, LANES)
    r = lax.broadcasted_iota(jnp.int32, (blk, blk), 0)
    c = lax.broadcasted_iota(jnp.int32, (blk, blk), 1)
    tril = (c <= r).astype(BF16)
    outs = []
    for s in range(tm // blk):
        hi, mid, lo = _split3(lg[s * blk:(s + 1) * blk])
        f = (jnp.dot(tril, lo.astype(BF16), preferred_element_type=F32)
             + jnp.dot(tril, mid.astype(BF16), preferred_element_type=F32)
             + jnp.dot(tril, hi.astype(BF16), preferred_element_type=F32)) + cur
        outs.append(f)
        cur = f[blk - 1:blk]
    return (outs[0] if len(outs) == 1 else jnp.concatenate(outs, axis=0)), cur


def _store_attention_operands(qn, kn, v, fcol, q_ref, k_ref, v_ref):
    tm = fcol.shape[0]
    lane = lax.broadcasted_iota(jnp.int32, (tm, LANES), 1)
    data = lane < FOX_HEAD_DIM
    for h in range(FOX_HEADS):
        hi, mid, lo = _split3(jnp.sum(jnp.where(lane == h, fcol, 0.0), axis=-1, keepdims=True))
        t, odd = divmod(h, 2)

        def head_tile(z):
            x = z[:, t * LANES:(t + 1) * LANES]
            return pltpu.roll(x, FOX_HEAD_DIM, 1) if odd else x

        hs = slice(h * LANES, (h + 1) * LANES)
        if q_ref is not None:
            aug = jnp.where(lane == AUG_F, hi, jnp.where(lane == AUG_F + 1, mid, jnp.where(
                lane == AUG_F + 2, lo, jnp.where(lane < AUG_END, 1.0, 0.0))))
            q_ref[:, hs] = jnp.where(data, head_tile(qn), aug).astype(BF16)
        aug = jnp.where(lane < AUG_1, 1.0, jnp.where(lane == AUG_1, -hi, jnp.where(
            lane == AUG_1 + 1, -mid, jnp.where(lane == AUG_1 + 2, -lo, 0.0))))
        k_ref[:, hs] = jnp.where(data, head_tile(kn), aug).astype(BF16)
        v_ref[:, hs] = jnp.where(data, head_tile(v), jnp.where(lane == FOX_HEAD_DIM, 1.0, 0.0)).astype(BF16)


def _inproj_kernel(*refs, d_model, nt, kv_t):
    (x_ref, gpre_ref, w_ref, b_ref, gq_ref, gk_ref, hs_ref, c0_ref), rest = refs[:8], refs[8:]
    if kv_t:
        kmeta_ref, vmeta_ref, lmeta_ref = rest[:3]
        (q_ref, kf_ref, kb_ref, vf_ref, vb_ref, mx_ref, mv_ref, mo_ref, ga_ref, gb_ref, gt_ref, fc_ref, lt_ref,
         carry, kcar, vcar, lcar) = rest[3:]
    else:
        (q_ref, kf_ref, kb_ref, vf_ref, vb_ref, mx_ref, mv_ref, mo_ref, ga_ref, gb_ref, gt_ref, fc_ref, carry) = rest
    fw, mw = FOX_WIDTH, ML_WIDTH
    offs = [0, fw, 2 * fw, 3 * fw, 3 * fw + mw, 3 * fw + 2 * mw, 3 * fw + 3 * mw,
            3 * fw + 3 * mw + d_model, 3 * fw + 3 * mw + 2 * d_model, 3 * fw + 3 * mw + 2 * d_model + LANES]
    r = pl.program_id(1)

    @pl.when(r == 0)
    def _():
        carry[...] = c0_ref[...]
        if kv_t:
            kcar[...] = kmeta_ref[...]
            vcar[...] = vmeta_ref[...]
            lcar[...] = lmeta_ref[...]

    def store_shifted(xt, car, out_ref):
        sh = pltpu.roll(xt, N_META, 1)
        lane = lax.broadcasted_iota(jnp.int32, (xt.shape[0], LANES), 1)
        out_ref[:, :LANES] = jnp.where(lane < N_META, car[...], sh[:, :LANES])
        out_ref[:, LANES:] = sh[:, LANES:]
        car[...] = sh[:, :LANES]

    @pl.when(r < nt)
    def _():
        h = _rms(x_ref[...], gpre_ref[...]).astype(BF16)

        def seg(i):
            a, b = offs[i], offs[i + 1]
            return jnp.dot(h, w_ref[:, a:b], preferred_element_type=F32) + b_ref[:, a:b]

        def headnorm(z, g):
            ms = jnp.dot((z * z).astype(BF16), hs_ref[...], preferred_element_type=F32) * (1.0 / FOX_HEAD_DIM)
            return z * lax.rsqrt(ms + RMS_EPS) * g

        g = seg(8)
        lane = lax.broadcasted_iota(jnp.int32, g.shape, 1)
        raw = (lane >= GATE_ML_I) & (lane < GATE_ML_F)
        gt = jnp.where(raw, g, jax.nn.log_sigmoid(g))
        gt_ref[...] = gt
        fcol, cur = _cumsum_rows(gt, carry[...])
        carry[...] = cur
        fc_ref[...] = fcol

        q = headnorm(seg(0), gq_ref[...]) * (FOX_HEAD_DIM ** -0.5)
        k = headnorm(seg(1), gk_ref[...])
        v = seg(2)
        if kv_t:
            store_shifted(k.T, kcar, kf_ref)
            store_shifted(v.T, vcar, vf_ref)
            store_shifted(gt.T[:FOX_HEADS], lcar, lt_ref)
        else:
            kf_ref[...] = k
            vf_ref[...] = v
        _store_attention_operands(q, k, v, fcol, q_ref, kb_ref, vb_ref)
        mx_ref[...] = seg(3)
        mv_ref[...] = seg(4).astype(BF16)
        mo_ref[...] = jax.nn.sigmoid(seg(5)).astype(BF16)
        ga_ref[...] = jax.nn.sigmoid(seg(6)).astype(BF16)
        gb_ref[...] = jax.nn.sigmoid(seg(7)).astype(BF16)

    if kv_t:
        @pl.when(r == nt)
        def _():
            kf_ref[:, :LANES] = kcar[...]
            vf_ref[:, :LANES] = vcar[...]
            lt_ref[:, :LANES] = lcar[...]


def _inproj(x, gpre, w, b, gq, gk, hs, c0, tm, nt, meta_t=None):
    n, d = x.shape
    nb = n // (nt * tm)
    dinp = w.shape[1]
    fw, mw = FOX_WIDTH, ML_WIDTH
    aw = FOX_HEADS * LANES
    kv_t = meta_t is not None
    tile = lambda b, r: b * nt + jnp.minimum(r, nt - 1)
    row = lambda c: pl.BlockSpec((tm, c), lambda b, r: (tile(b, r), 0))
    cb = (lambda b: b) if c0.shape[0] > 1 else (lambda b: 0)
    lp = N_META + nt * tm
    tmin = lambda c: pl.BlockSpec((None, c, tm), lambda b, r: (b, 0, r))
    flat = lambda c, t: (row(c), jax.ShapeDtypeStruct((n, c), t))
    tok_minor = lambda c: (tmin(c), jax.ShapeDtypeStruct((nb, c, lp), F32))
    outs = [flat(aw, BF16), tok_minor(fw) if kv_t else flat(fw, F32), flat(aw, BF16),
            tok_minor(fw) if kv_t else flat(fw, F32), flat(aw, BF16), flat(mw, F32), flat(mw, BF16), flat(mw, BF16),
            flat(d, BF16), flat(d, BF16), flat(LANES, F32), flat(LANES, F32)]
    in_specs = [row(d), _const_spec((1, d)), _const_spec((d, dinp)), _const_spec((1, dinp)),
                _const_spec((1, fw)), _const_spec((1, fw)), _const_spec((fw, fw)),
                pl.BlockSpec((None, 1, LANES), lambda b, r: (cb(b), 0, 0))]
    scratch = [pltpu.VMEM((1, LANES), F32)]
    args = [x, gpre, w, b, gq, gk, hs, c0]
    if kv_t:
        outs.append(tok_minor(FOX_HEADS))
        in_specs += [_const_spec((fw, LANES)), _const_spec((fw, LANES)), _const_spec((FOX_HEADS, LANES))]
        scratch += [pltpu.VMEM((fw, LANES), F32), pltpu.VMEM((fw, LANES), F32), pltpu.VMEM((FOX_HEADS, LANES), F32)]
        args += list(meta_t)
    return pl.pallas_call(
        functools.partial(_inproj_kernel, d_model=d, nt=nt, kv_t=kv_t),
        grid=(nb, nt + (1 if kv_t else 0)),
        in_specs=in_specs,
        out_specs=[o[0] for o in outs],
        out_shape=[o[1] for o in outs],
        scratch_shapes=scratch,
        compiler_params=pltpu.CompilerParams(dimension_semantics=("arbitrary", "arbitrary"),
                                             vmem_limit_bytes=VMEM_LIMIT),
        name="inproj",
    )(*args)


def _cacheprep_kernel(k_ref, v_ref, lf_ref, ka_ref, va_ref, fl_ref, carry):
    @pl.when(pl.program_id(1) == 0)
    def _():
        carry[...] = jnp.zeros_like(carry)

    fcol, cur = _cumsum_rows(lf_ref[...], carry[...])
    carry[...] = cur
    fl_ref[...] = cur
    _store_attention_operands(None, k_ref[...], v_ref[...], fcol, None, ka_ref, va_ref)


def _cacheprep(ck, cv, clf, tp):
    bsz, plen, _ = ck.shape
    aw = FOX_HEADS * LANES
    return pl.pallas_call(
        _cacheprep_kernel,
        grid=(bsz, plen // tp),
        in_specs=[pl.BlockSpec((None, tp, FOX_WIDTH), lambda b, i: (b, i, 0)),
                  pl.BlockSpec((None, tp, FOX_WIDTH), lambda b, i: (b, i, 0)),
                  pl.BlockSpec((None, tp, LANES), lambda b, i: (b, i, 0))],
        out_specs=[pl.BlockSpec((None, tp, aw), lambda b, i: (b, i, 0)),
                   pl.BlockSpec((None, tp, aw), lambda b, i: (b, i, 0)),
                   pl.BlockSpec((None, 1, LANES), lambda b, i: (b, 0, 0))],
        out_shape=[jax.ShapeDtypeStruct((bsz, plen, aw), BF16), jax.ShapeDtypeStruct((bsz, plen, aw), BF16),
                   jax.ShapeDtypeStruct((bsz, 1, LANES), F32)],
        scratch_shapes=[pltpu.VMEM((1, LANES), F32)],
        compiler_params=pltpu.CompilerParams(dimension_semantics=("arbitrary", "arbitrary")),
        name="cache_prep",
    )(ck, cv, clf)


def _fox_kernel(q_ref, kp_ref, vp_ref, k_ref, v_ref, o_ref, *, tq, tr, tkb, pblk, n_pblk, pvalid):
    i = pl.program_id(2)
    nr = tq // tr
    chains = [(hh, r) for hh in range(2) for r in range(nr)]
    qs = [q_ref[r * tr:(r + 1) * tr, hh * LANES:(hh + 1) * LANES] for hh, r in chains]

    def upd1(n, state, kb, vb, mask=None):
        m, acc = state
        hh = chains[n][0]
        hs = slice(hh * LANES, (hh + 1) * LANES)
        s = lax.dot_general(qs[n], kb[:, hs], (((1,), (1,)), ((), ())), preferred_element_type=F32)
        if mask is not None:
            s = jnp.where(mask, s, NEG_BIG)
        m_new = jnp.maximum(m, jnp.max(s, axis=-1, keepdims=True))
        alpha = jnp.exp(m - m_new)
        pm = jnp.exp(s - m_new)
        acc = alpha * acc + jnp.dot(pm.astype(BF16), vb[:, hs], preferred_element_type=F32)
        return m_new, acc

    def upd(carry, kb, vb):
        return tuple(upd1(n, carry[n], kb, vb) for n in range(len(chains)))

    carry = tuple((jnp.full((tr, 1), NEG_BIG, F32), jnp.zeros((tr, LANES), F32)) for _ in chains)
    if pvalid is None:
        for jb in range(n_pblk):
            sl = slice(jb * pblk, (jb + 1) * pblk)
            carry = upd(carry, kp_ref[sl, :], vp_ref[sl, :])

    def body_wide(j, carry):
        off = pl.multiple_of(j * tkb, tkb)
        return upd(carry, k_ref[pl.ds(off, tkb), :], v_ref[pl.ds(off, tkb), :])

    def body(j, carry):
        off = pl.multiple_of(j * tq, tq)
        return upd(carry, k_ref[pl.ds(off, tq), :], v_ref[pl.ds(off, tq), :])

    n_wide = i // (tkb // tq)
    carry = lax.fori_loop(0, n_wide, body_wide, carry)
    if tkb > tq:
        carry = lax.fori_loop(n_wide * (tkb // tq), i, body, carry)

    carry = list(carry)
    off = pl.multiple_of(i * tq, tq)
    for r in range(nr):
        width = (r + 1) * tr
        kb = k_ref[pl.ds(off, width), :]
        vb = v_ref[pl.ds(off, width), :]
        total = width
        if pvalid is not None:
            kb = jnp.concatenate([kb, kp_ref[...]], axis=0)
            vb = jnp.concatenate([vb, vp_ref[...]], axis=0)
            total = width + LANES
        rj = lax.broadcasted_iota(jnp.int32, (tr, total), 0)
        cj = lax.broadcasted_iota(jnp.int32, (tr, total), 1)
        mask = cj <= rj + r * tr
        if pvalid is not None:
            mask = (mask | (cj >= width)) & (cj < width + pvalid)
        for n, (hh, rr) in enumerate(chains):
            if rr == r:
                carry[n] = upd1(n, carry[n], kb, vb, mask=mask)

    lane = lax.broadcasted_iota(jnp.int32, (tr, LANES), 1)
    outs = []
    for n in range(len(chains)):
        acc = carry[n][1]
        l = jnp.sum(jnp.where(lane == FOX_HEAD_DIM, acc, 0.0), axis=-1, keepdims=True)
        outs.append(acc / l)
    for r in range(nr):
        o_ref[r * tr:(r + 1) * tr, :] = jnp.where(lane < FOX_HEAD_DIM, outs[r],
                                                  pltpu.roll(outs[nr + r], FOX_HEAD_DIM, 1)).astype(o_ref.dtype)


def _fox(q, kp, vp, k, v, tq, tr, pblk, pvalid=None):
    bsz, t, _ = q.shape
    bp, plen, _ = kp.shape
    nq = t // tq
    n_pblk = plen // pblk
    pw = 2 * LANES
    pb = (lambda b: b) if bp == bsz else (lambda b: 0)
    return pl.pallas_call(
        functools.partial(_fox_kernel, tq=tq, tr=tr, tkb=_row_tile(t, FOX_WIDE_KEYS), pblk=pblk, n_pblk=n_pblk,
                          pvalid=pvalid),
        grid=(bsz, FOX_HEADS // 2, nq),
        in_specs=[pl.BlockSpec((None, tq, pw), lambda b, p, i: (b, i, p)),
                  pl.BlockSpec((None, plen, pw), lambda b, p, i: (pb(b), 0, p)),
                  pl.BlockSpec((None, plen, pw), lambda b, p, i: (pb(b), 0, p)),
                  pl.BlockSpec((None, t, pw), lambda b, p, i: (b, 0, p)),
                  pl.BlockSpec((None, t, pw), lambda b, p, i: (b, 0, p))],
        out_specs=pl.BlockSpec((None, tq, LANES), lambda b, p, i: (b, i, p)),
        out_shape=jax.ShapeDtypeStruct((bsz, t, FOX_WIDTH), BF16),
        compiler_params=pltpu.CompilerParams(dimension_semantics=("arbitrary", "arbitrary", "arbitrary"),
                                             vmem_limit_bytes=VMEM_LIMIT),
        name="fox_attention",
    )(q, kp, vp, k, v)


def _mlstm_kernel(mx_ref, mv_ref, mo_ref, gt_ref, cp_ref, c0_ref, n0_ref, m0_ref,
                  wc_ref, bc_ref, wq_ref, wk_ref, gn_ref, sk_ref,
                  hm_ref, cout_ref, nout_ref, mout_ref,
                  xbuf, c_scr, n_scr, m_scr, *, tc, n_valid):
    c = pl.program_id(1)

    @pl.when(c == 0)
    def _():
        xbuf[0:8, :] = cp_ref[...]
        c_scr[...] = c0_ref[...]
        n_scr[...] = n0_ref[...]
        m_scr[...] = m0_ref[...]

    xbuf[8:8 + tc, :] = mx_ref[...]
    xc = bc_ref[...]
    for i in range(CONV_WIDTH):
        xc = xc + xbuf[5 + i:5 + i + tc, :] * wc_ref[i:i + 1, :]
    xc = xc * jax.nn.sigmoid(xc)
    tail = xbuf[tc:tc + 8, :]
    xbuf[0:8, :] = tail

    gt = gt_ref[...]
    lane = lax.broadcasted_iota(jnp.int32, (tc, LANES), 1)
    if n_valid is not None:
        rowi = lax.broadcasted_iota(jnp.int32, (tc, LANES), 0)
        is_i = (lane >= GATE_ML_I) & (lane < GATE_ML_F)
        gt = jnp.where(rowi < n_valid, gt, jnp.where(is_i, PAD_LOG_GATE, 0.0))
    ri = lax.broadcasted_iota(jnp.int32, (tc, tc), 0)
    ci = lax.broadcasted_iota(jnp.int32, (tc, tc), 1)
    causal = ci <= ri
    b_all = _cumsum_rows(gt, jnp.zeros((1, LANES), F32))[0]
    r_t = (pltpu.roll(gt, GATE_ML_F - GATE_ML_I, 1) - b_all).T
    heads = range(ML_HEADS)
    hsl = [slice(h * ML_HEAD_DIM, (h + 1) * ML_HEAD_DIM) for h in heads]
    b_col = [b_all[:, GATE_ML_F + h:GATE_ML_F + h + 1] for h in heads]
    ig_col = [gt[:, GATE_ML_I + h:GATE_ML_I + h + 1] for h in heads]
    m_prev = [m_scr[h:h + 1, 0:1] for h in heads]
    dmat = [jnp.where(causal, b_col[h] + r_t[GATE_ML_F + h:GATE_ML_F + h + 1, :], NEG_BIG) for h in heads]
    m_inter = [b_col[h] + m_prev[h] for h in heads]
    m_t = [jnp.maximum(m_inter[h], jnp.max(dmat[h], axis=-1, keepdims=True)) for h in heads]
    w_inter = [jnp.exp(m_inter[h] - m_t[h]) for h in heads]

    xh = [xc[:, hsl[h]] for h in heads]
    xhb = [x.astype(BF16) for x in xh]
    q = [jnp.dot(xhb[h], wq_ref[h], preferred_element_type=F32) for h in heads]
    k = [jnp.dot(xhb[h], wk_ref[h], preferred_element_type=F32) * (ML_HEAD_DIM ** -0.5) for h in heads]
    qb = [x.astype(BF16) for x in q]
    kb = [x.astype(BF16) for x in k]
    v = [mv_ref[:, hsl[h]] for h in heads]
    smat = [lax.dot_general(qb[h], kb[h], (((1,), (1,)), ((), ())), preferred_element_type=F32) for h in heads]
    amat = [jnp.exp(dmat[h] - m_t[h]) * smat[h] for h in heads]
    ch = [c_scr[h] for h in heads]
    nh = [n_scr[h:h + 1, :] for h in heads]
    num = [jnp.dot(amat[h].astype(BF16), v[h], preferred_element_type=F32)
           + w_inter[h] * jnp.dot(qb[h], ch[h].astype(BF16), preferred_element_type=F32) for h in heads]
    den = [jnp.sum(amat[h], axis=-1, keepdims=True) + w_inter[h] * jnp.sum(q[h] * nh[h], axis=-1, keepdims=True)
           for h in heads]
    hcell = [num[h] / jnp.maximum(jnp.abs(den[h]), jnp.exp(-m_t[h])) for h in heads]

    b_last = [b_col[h][tc - 1:tc, :] for h in heads]
    g = [b_last[h] - b_col[h] + ig_col[h] for h in heads]
    m_new = [jnp.maximum(b_last[h] + m_prev[h], jnp.max(g[h], axis=0, keepdims=True)) for h in heads]
    decay = [jnp.exp(b_last[h] + m_prev[h] - m_new[h]) for h in heads]
    kw = [k[h] * jnp.exp(g[h] - m_new[h]) for h in heads]
    for h in heads:
        c_scr[h] = decay[h] * ch[h] + lax.dot_general(kw[h].astype(BF16), v[h], (((0,), (0,)), ((), ())),
                                                       preferred_element_type=F32)
        n_scr[h:h + 1, :] = decay[h] * nh[h] + jnp.sum(kw[h], axis=0, keepdims=True)
        m_scr[h:h + 1, :] = jnp.broadcast_to(m_new[h], (1, LANES))
        y = mo_ref[:, hsl[h]].astype(F32) * hcell[h]
        hm_ref[:, hsl[h]] = (_rms(y, gn_ref[:, hsl[h]]) + sk_ref[:, hsl[h]] * xh[h]).astype(hm_ref.dtype)

    @pl.when(c == pl.num_programs(1) - 1)
    def _():
        cout_ref[...] = c_scr[...]
        nout_ref[...] = n_scr[...]
        mout_ref[...] = m_scr[...]


def _mlstm(mx, mv, mo, gt, cp, c0, n0, m0, wc, bc, wq, wk, gn, sk, tc, n_valid=None):
    bsz, t, _ = mx.shape
    bs = c0.shape[0]
    sb = (lambda b: b) if bs == bsz else (lambda b: 0)
    tok = lambda w: pl.BlockSpec((None, tc, w), lambda b, c: (b, c, 0))
    hd = ML_HEAD_DIM
    st_specs = [pl.BlockSpec((None, ML_HEADS, hd, hd), lambda b, c: (b, 0, 0, 0)),
                pl.BlockSpec((None, 8, LANES), lambda b, c: (b, 0, 0)),
                pl.BlockSpec((None, 8, LANES), lambda b, c: (b, 0, 0))]
    return pl.pallas_call(
        functools.partial(_mlstm_kernel, tc=tc, n_valid=n_valid),
        grid=(bsz, t // tc),
        in_specs=[tok(ML_WIDTH), tok(ML_WIDTH), tok(ML_WIDTH), tok(LANES),
                  pl.BlockSpec((None, 8, ML_WIDTH), lambda b, c: (sb(b), 0, 0)),
                  pl.BlockSpec((None, ML_HEADS, hd, hd), lambda b, c: (sb(b), 0, 0, 0)),
                  pl.BlockSpec((None, 8, LANES), lambda b, c: (sb(b), 0, 0)),
                  pl.BlockSpec((None, 8, LANES), lambda b, c: (sb(b), 0, 0)),
                  _const_spec((CONV_WIDTH, ML_WIDTH)), _const_spec((1, ML_WIDTH)),
                  _const_spec((ML_HEADS, hd, hd)), _const_spec((ML_HEADS, hd, hd)),
                  _const_spec((1, ML_WIDTH)), _const_spec((1, ML_WIDTH))],
        out_specs=[tok(ML_WIDTH)] + st_specs,
        out_shape=[jax.ShapeDtypeStruct((bsz, t, ML_WIDTH), BF16),
                   jax.ShapeDtypeStruct((bsz, ML_HEADS, hd, hd), F32),
                   jax.ShapeDtypeStruct((bsz, 8, LANES), F32),
                   jax.ShapeDtypeStruct((bsz, 8, LANES), F32)],
        scratch_shapes=[pltpu.VMEM((8 + tc, ML_WIDTH), F32), pltpu.VMEM((ML_HEADS, hd, hd), F32),
                        pltpu.VMEM((8, LANES), F32), pltpu.VMEM((8, LANES), F32)],
        compiler_params=pltpu.CompilerParams(dimension_semantics=("arbitrary", "arbitrary")),
        name="mlstm",
    )(mx, mv, mo, gt, cp, c0, n0, m0, wc, bc, wq, wk, gn, sk)


def _post_kernel(a_ref, hm_ref, ga_ref, gb_ref, x_ref, wfp_ref, wmp_ref, wo_ref, wgu_ref, wd_ref,
                 g1_ref, g2_ref, g3_ref, y_ref, *, d_ff, ff_cuts):
    ya = jnp.dot(a_ref[...], wfp_ref[...], preferred_element_type=F32)
    yb = jnp.dot(hm_ref[...], wmp_ref[...], preferred_element_type=F32)
    merged = ga_ref[...].astype(F32) * ya + gb_ref[...].astype(F32) * yb
    mix = jnp.dot(merged.astype(BF16), wo_ref[...], preferred_element_type=F32)
    x1 = x_ref[...] + _rms(mix, g1_ref[...])
    h2 = _rms(x1, g2_ref[...]).astype(BF16)
    f = None
    for lo, hi in zip(ff_cuts[:-1], ff_cuts[1:]):
        g = jnp.dot(h2, wgu_ref[:, lo:hi], preferred_element_type=F32)
        u = jnp.dot(h2, wgu_ref[:, d_ff + lo:d_ff + hi], preferred_element_type=F32)
        t = (g * jax.nn.sigmoid(g) * u).astype(BF16)
        fj = jnp.dot(t, wd_ref[lo:hi, :], preferred_element_type=F32)
        f = fj if f is None else f + fj
    y_ref[...] = x1 + _rms(f, g3_ref[...])


def _post(a, hm, ga, gb, x, wfp, wmp, wo, wgu, wd, g1, g2, g3, tm):
    n, d = x.shape
    d_ff = wd.shape[0]
    nmt = d_ff // MXU_TILE
    ff_cuts = tuple(MXU_TILE * ((nmt * j) // FF_CHUNKS) for j in range(FF_CHUNKS)) + (d_ff,) \
        if d_ff % MXU_TILE == 0 else (0, d_ff)
    row = lambda c: pl.BlockSpec((tm, c), lambda i: (i, 0))
    return pl.pallas_call(
        functools.partial(_post_kernel, d_ff=d_ff, ff_cuts=ff_cuts),
        grid=(n // tm,),
        in_specs=[row(FOX_WIDTH), row(ML_WIDTH), row(d), row(d), row(d),
                  _const_spec(wfp.shape), _const_spec(wmp.shape), _const_spec(wo.shape),
                  _const_spec(wgu.shape), _const_spec(wd.shape),
                  _const_spec((1, d)), _const_spec((1, d)), _const_spec((1, d))],
        out_specs=row(d),
        out_shape=jax.ShapeDtypeStruct((n, d), F32),
        compiler_params=pltpu.CompilerParams(dimension_semantics=("arbitrary",), vmem_limit_bytes=VMEM_LIMIT),
        name="post_mix_ffn",
    )(a, hm, ga, gb, x, wfp, wmp, wo, wgu, wd, g1, g2, g3)


def _pad_rows(a, rows, axis):
    pad = [(0, 0)] * a.ndim
    pad[axis] = (0, rows - a.shape[axis])
    return jnp.pad(a, pad)


def _row_tile(n, pref):
    t = min(pref, n)
    while n % t:
        t //= 2
    return t


def kernel(x_prompt, x_sample, cache_fox_k, cache_fox_v, cache_fox_logf, state_mlstm_C, state_mlstm_n,
           state_mlstm_m, state_mlstm_conv, meta_tokens, w_in, b_in, g_fq, g_fk, w_conv, b_conv, w_mq, w_mk,
           g_mnorm, skip_m, w_fox_proj, w_ml_proj, w_out, g_pre_mix, g_post_mix, g_pre_ffn, g_post_ffn,
           w_gate_up, w_down):
    assert w_in.shape[0] == 1, "single-layer trunk"
    bsz, seq, d = x_prompt.shape
    dbsz, dseq, _ = x_sample.shape
    past = cache_fox_k.shape[2]
    fw, mw = FOX_WIDTH, ML_WIDTH

    w, b = w_in[0], b_in[0]
    o_ff = 3 * fw
    o_mx = o_ff + FOX_HEADS
    o_mi = o_mx + 3 * mw
    o_mf = o_mi + ML_HEADS
    o_ga = o_mf + ML_HEADS

    def relayout(a):
        gates = jnp.concatenate([a[..., o_ff:o_mx], a[..., o_mi:o_mf], a[..., o_mf:o_ga], a[..., o_mf:o_ga]], axis=-1)
        gates = _pad_rows(gates, LANES, gates.ndim - 1)
        return jnp.concatenate([a[..., :o_ff], a[..., o_mx:o_mi], a[..., o_ga:], gates], axis=-1)

    w_p = relayout(w).astype(BF16)
    b_p = relayout(b)[None, :]
    hid = jnp.arange(fw) // FOX_HEAD_DIM
    hsum = (hid[:, None] == hid[None, :]).astype(BF16)
    gq = g_fq[0].reshape(1, fw)
    gk = g_fk[0].reshape(1, fw)
    gpre = g_pre_mix[0][None, :]
    inproj = lambda x, c0, tm, nt, **kw: _inproj(x, gpre, w_p, b_p, gq, gk, hsum, c0, tm, nt, **kw)

    wc, bc = w_conv[0], b_conv[0][None, :]
    wq, wk = w_mq[0].astype(BF16), w_mk[0].astype(BF16)
    gn, sk = g_mnorm[0].reshape(1, mw), skip_m[0][None, :]
    mlstm = lambda *a, **kw: _mlstm(*a, wc, bc, wq, wk, gn, sk, **kw)

    def conv_hist(rows):
        return jnp.pad(rows, ((0, 0), (8 - (CONV_WIDTH - 1), 0), (0, 0)))

    def m_rows(m):
        return jnp.broadcast_to(_pad_rows(m, 8, 1)[:, :, None], (m.shape[0], 8, LANES))

    mt = 64
    xm = _pad_rows(meta_tokens.astype(F32), mt, 0)
    (_, kf_m, ka_m, vf_m, va_m, mx_m, mv_m, mo_m, _, _, gt_m, fc_m) = inproj(xm, jnp.zeros((1, 1, LANES), F32), mt, 1)
    zc = jnp.zeros((1, ML_HEADS, ML_HEAD_DIM, ML_HEAD_DIM), F32)
    z8 = jnp.zeros((1, 8, LANES), F32)
    _, c_m, n_m, m_m = mlstm(mx_m[None], mv_m[None], mo_m[None], gt_m[None], jnp.zeros((1, 8, mw), F32),
                             zc, z8, z8, tc=mt, n_valid=N_META)

    xp = x_prompt.reshape(bsz * seq, d)
    tm = _row_tile(seq, 512)
    lanes_t = lambda rows: _pad_rows(rows[:N_META].T, LANES, 1)
    (qa_p, kt_p, ka_p, vt_p, va_p, mx_p, mv_p, mo_p, ga_p, gb_p, gt_p, _, lt_p) = inproj(
        xp, fc_m[None, N_META - 1:N_META], tm, seq // tm,
        meta_t=(lanes_t(kf_m), lanes_t(vf_m), lanes_t(gt_m[:, :FOX_HEADS])))
    r3 = lambda a: a.reshape(bsz, seq, a.shape[-1])
    mx_p3 = r3(mx_p)
    hm_p, c_p, n_p, m_p = mlstm(mx_p3, r3(mv_p), r3(mo_p), r3(gt_p),
                                conv_hist(mx_m[None, N_META - (CONV_WIDTH - 1):N_META]), c_m, n_m, m_m,
                                tc=_row_tile(seq, 256))
    a_p = _fox(r3(qa_p), _pad_rows(ka_m, LANES, 0)[None], _pad_rows(va_m, LANES, 0)[None], r3(ka_p), r3(va_p),
               tq=_row_tile(seq, 1024), tr=_row_tile(seq, 512), pblk=LANES, pvalid=N_META)

    kc, vc, f_c = _cacheprep(cache_fox_k[0].reshape(dbsz, past, fw), cache_fox_v[0].reshape(dbsz, past, fw),
                             _pad_rows(cache_fox_logf[0].astype(F32), LANES, 2), _row_tile(past, 512))
    xs = x_sample.reshape(dbsz * dseq, d)
    (qa_s, kf_s, ka_s, vf_s, va_s, mx_s, mv_s, mo_s, ga_s, gb_s, gt_s, _) = inproj(xs, f_c, dseq, 1)
    r3s = lambda a: a.reshape(dbsz, dseq, a.shape[-1])
    mx_s3 = r3s(mx_s)
    hm_s, c_s, n_s, m_s = mlstm(mx_s3, r3s(mv_s), r3s(mo_s), r3s(gt_s), conv_hist(state_mlstm_conv[0]),
                                state_mlstm_C[0], _pad_rows(state_mlstm_n[0], 8, 1), m_rows(state_mlstm_m[0]),
                                tc=dseq)
    a_s = _fox(r3s(qa_s), kc, vc, r3s(ka_s), r3s(va_s), tq=dseq, tr=dseq, pblk=_row_tile(past, 512))

    wfp, wmp, wo = w_fox_proj[0].astype(BF16), w_ml_proj[0].astype(BF16), w_out[0].astype(BF16)
    wgu, wd = w_gate_up[0].astype(BF16), w_down[0].astype(BF16)
    g1, g2, g3 = g_post_mix[0][None, :], g_pre_ffn[0][None, :], g_post_ffn[0][None, :]
    post = lambda a, hm, ga, gb, x, tm: _post(a, hm, ga, gb, x, wfp, wmp, wo, wgu, wd, g1, g2, g3, tm)
    y_p = post(a_p.reshape(bsz * seq, fw), hm_p.reshape(bsz * seq, mw), ga_p, gb_p, xp, _row_tile(bsz * seq, 512))
    y_s = post(a_s.reshape(dbsz * dseq, fw), hm_s.reshape(dbsz * dseq, mw), ga_s, gb_s, xs,
               _row_tile(dbsz * dseq, 512))

    def tokens_major(a):
        return a.reshape(bsz, FOX_HEADS, FOX_HEAD_DIM, a.shape[-1]).transpose(0, 3, 1, 2)[None]

    heads = lambda a: a.reshape(a.shape[:-1] + (FOX_HEADS, FOX_HEAD_DIM))
    tail = CONV_WIDTH - 1
    return (y_p.reshape(bsz, seq, d), y_s.reshape(dbsz, dseq, d),
            tokens_major(kt_p), tokens_major(vt_p), lt_p.transpose(0, 2, 1)[None],
            c_p[None], n_p[None, :, :ML_HEADS], m_p[None, :, :ML_HEADS, 0], mx_p3[None, :, seq - tail:],
            heads(r3s(kf_s))[None], heads(r3s(vf_s))[None], r3s(gt_s)[None, ..., :FOX_HEADS],
            c_s[None], n_s[None, :, :ML_HEADS], m_s[None, :, :ML_HEADS, 0], mx_s3[None, :, dseq - tail:])
```

```python
import functools

import jax
import jax.numpy as jnp
from jax import lax
from jax.experimental import pallas as pl
from jax.experimental.pallas import tpu as pltpu

F32 = jnp.float32
BF16 = jnp.bfloat16

N_META = 16
FOX_HEADS = 8
FOX_HEAD_DIM = 64
FOX_WIDTH = FOX_HEADS * FOX_HEAD_DIM
ML_HEADS = 4
ML_HEAD_DIM = 128
ML_WIDTH = ML_HEADS * ML_HEAD_DIM
CONV_WIDTH = 4
RMS_EPS = 1e-6
PAD_LOG_GATE = -1e30
NEG_BIG = -1e30

LANES = 128
GATE_FOX_F = 0
GATE_ML_I = 8
GATE_ML_F = 12
GATE_ML_F2 = 16

AUG_F = FOX_HEAD_DIM
AUG_1 = FOX_HEAD_DIM + 3
AUG_END = FOX_HEAD_DIM + 6

VMEM_LIMIT = 56 * 1024 * 1024
FOX_WIDE_KEYS = 1024
FOX_MAX_STATIC_BLOCKS = 4
MXU_TILE = 256
FF_CHUNKS = 3


def _const_spec(shape):
    nd = len(shape)
    return pl.BlockSpec(shape, lambda *_: (0,) * nd, pipeline_mode=pl.Buffered(1))


def _rms(x, g):
    return x * lax.rsqrt(jnp.mean(x * x, axis=-1, keepdims=True) + RMS_EPS) * g


def _split3(x):
    hi = x.astype(BF16).astype(F32)
    r = x - hi
    mid = r.astype(BF16).astype(F32)
    lo = (r - mid).astype(BF16).astype(F32)
    return hi, mid, lo


def _cumsum_rows(lg, cur):
    tm = lg.shape[0]
    blk = min(tm, LANES)
    r = lax.broadcasted_iota(jnp.int32, (blk, blk), 0)
    c = lax.broadcasted_iota(jnp.int32, (blk, blk), 1)
    tril = (c <= r).astype(BF16)
    outs = []
    for s in range(tm // blk):
        hi, mid, lo = _split3(lg[s * blk:(s + 1) * blk])
        f = (jnp.dot(tril, lo.astype(BF16), preferred_element_type=F32)
             + jnp.dot(tril, mid.astype(BF16), preferred_element_type=F32)
             + jnp.dot(tril, hi.astype(BF16), preferred_element_type=F32)) + cur
        outs.append(f)
        cur = f[blk - 1:blk]
    return (outs[0] if len(outs) == 1 else jnp.concatenate(outs, axis=0)), cur


def _store_attention_operands(qn, kn, v, fcol, q_ref, k_ref, v_ref):
    tm = fcol.shape[0]
    lane = lax.broadcasted_iota(jnp.int32, (tm, LANES), 1)
    data = lane < FOX_HEAD_DIM
    for h in range(FOX_HEADS):
        hi, mid, lo = _split3(jnp.sum(jnp.where(lane == h, fcol, 0.0), axis=-1, keepdims=True))
        t, odd = divmod(h, 2)

        def head_tile(z):
            x = z[:, t * LANES:(t + 1) * LANES]
            return pltpu.roll(x, FOX_HEAD_DIM, 1) if odd else x

        hs = slice(h * LANES, (h + 1) * LANES)
        if q_ref is not None:
            aug = jnp.where(lane == AUG_F, hi, jnp.where(lane == AUG_F + 1, mid, jnp.where(
                lane == AUG_F + 2, lo, jnp.where(lane < AUG_END, 1.0, 0.0))))
            q_ref[:, hs] = jnp.where(data, head_tile(qn), aug).astype(BF16)
        aug = jnp.where(lane < AUG_1, 1.0, jnp.where(lane == AUG_1, -hi, jnp.where(
            lane == AUG_1 + 1, -mid, jnp.where(lane == AUG_1 + 2, -lo, 0.0))))
        k_ref[:, hs] = jnp.where(data, head_tile(kn), aug).astype(BF16)
        v_ref[:, hs] = jnp.where(data, head_tile(v), jnp.where(lane == FOX_HEAD_DIM, 1.0, 0.0)).astype(BF16)


def _inproj_kernel(*refs, d_model, nt, kv_t):
    (x_ref, gpre_ref, w_ref, b_ref, gq_ref, gk_ref, hs_ref, c0_ref), rest = refs[:8], refs[8:]
    if kv_t:
        kmeta_ref, vmeta_ref, lmeta_ref = rest[:3]
        (q_ref, kf_ref, kb_ref, vf_ref, vb_ref, mx_ref, mv_ref, mo_ref, ga_ref, gb_ref, gt_ref, fc_ref, lt_ref,
         carry, kcar, vcar, lcar) = rest[3:]
    else:
        (q_ref, kf_ref, kb_ref, vf_ref, vb_ref, mx_ref, mv_ref, mo_ref, ga_ref, gb_ref, gt_ref, fc_ref, carry) = rest
    fw, mw = FOX_WIDTH, ML_WIDTH
    offs = [0, fw, 2 * fw, 3 * fw, 3 * fw + mw, 3 * fw + 2 * mw, 3 * fw + 3 * mw,
            3 * fw + 3 * mw + d_model, 3 * fw + 3 * mw + 2 * d_model, 3 * fw + 3 * mw + 2 * d_model + LANES]
    r = pl.program_id(1)

    @pl.when(r == 0)
    def _():
        carry[...] = c0_ref[...]
        if kv_t:
            kcar[...] = kmeta_ref[...]
            vcar[...] = vmeta_ref[...]
            lcar[...] = lmeta_ref[...]

    def store_shifted(xt, car, out_ref):
        sh = pltpu.roll(xt, N_META, 1)
        lane = lax.broadcasted_iota(jnp.int32, (xt.shape[0], LANES), 1)
        out_ref[:, :LANES] = jnp.where(lane < N_META, car[...], sh[:, :LANES])
        out_ref[:, LANES:] = sh[:, LANES:]
        car[...] = sh[:, :LANES]

    @pl.when(r < nt)
    def _():
        h = _rms(x_ref[...], gpre_ref[...]).astype(BF16)

        def seg(i):
            a, b = offs[i], offs[i + 1]
            return jnp.dot(h, w_ref[:, a:b], preferred_element_type=F32) + b_ref[:, a:b]

        def headnorm(z, g):
            ms = jnp.dot((z * z).astype(BF16), hs_ref[...], preferred_element_type=F32) * (1.0 / FOX_HEAD_DIM)
            return z * lax.rsqrt(ms + RMS_EPS) * g

        g = seg(8)
        lane = lax.broadcasted_iota(jnp.int32, g.shape, 1)
        raw = (lane >= GATE_ML_I) & (lane < GATE_ML_F)
        gt = jnp.where(raw, g, jax.nn.log_sigmoid(g))
        gt_ref[...] = gt
        fcol, cur = _cumsum_rows(gt, carry[...])
        carry[...] = cur
        fc_ref[...] = fcol

        q = headnorm(seg(0), gq_ref[...]) * (FOX_HEAD_DIM ** -0.5)
        k = headnorm(seg(1), gk_ref[...])
        v = seg(2)
        if kv_t:
            store_shifted(k.T, kcar, kf_ref)
            store_shifted(v.T, vcar, vf_ref)
            store_shifted(gt.T[:FOX_HEADS], lcar, lt_ref)
        else:
            kf_ref[...] = k
            vf_ref[...] = v
        _store_attention_operands(q, k, v, fcol, q_ref, kb_ref, vb_ref)
        mx_ref[...] = seg(3)
        mv_ref[...] = seg(4).astype(BF16)
        mo_ref[...] = jax.nn.sigmoid(seg(5)).astype(BF16)
        ga_ref[...] = jax.nn.sigmoid(seg(6)).astype(BF16)
        gb_ref[...] = jax.nn.sigmoid(seg(7)).astype(BF16)

    if kv_t:
        @pl.when(r == nt)
        def _():
            kf_ref[:, :LANES] = kcar[...]
            vf_ref[:, :LANES] = vcar[...]
            lt_ref[:, :LANES] = lcar[...]


def _inproj(x, gpre, w, b, gq, gk, hs, c0, tm, nt, meta_t=None):
    n, d = x.shape
    nb = n // (nt * tm)
    dinp = w.shape[1]
    fw, mw = FOX_WIDTH, ML_WIDTH
    aw = FOX_HEADS * LANES
    kv_t = meta_t is not None
    tile = lambda b, r: b * nt + jnp.minimum(r, nt - 1)
    row = lambda c: pl.BlockSpec((tm, c), lambda b, r: (tile(b, r), 0))
    cb = (lambda b: b) if c0.shape[0] > 1 else (lambda b: 0)
    lp = N_META + nt * tm
    tmin = lambda c: pl.BlockSpec((None, c, tm), lambda b, r: (b, 0, r))
    flat = lambda c, t: (row(c), jax.ShapeDtypeStruct((n, c), t))
    tok_minor = lambda c: (tmin(c), jax.ShapeDtypeStruct((nb, c, lp), F32))
    outs = [flat(aw, BF16), tok_minor(fw) if kv_t else flat(fw, F32), flat(aw, BF16),
            tok_minor(fw) if kv_t else flat(fw, F32), flat(aw, BF16), flat(mw, F32), flat(mw, BF16), flat(mw, BF16),
            flat(d, BF16), flat(d, BF16), flat(LANES, F32), flat(LANES, F32)]
    in_specs = [row(d), _const_spec((1, d)), _const_spec((d, dinp)), _const_spec((1, dinp)),
                _const_spec((1, fw)), _const_spec((1, fw)), _const_spec((fw, fw)),
                pl.BlockSpec((None, 1, LANES), lambda b, r: (cb(b), 0, 0))]
    scratch = [pltpu.VMEM((1, LANES), F32)]
    args = [x, gpre, w, b, gq, gk, hs, c0]
    if kv_t:
        outs.append(tok_minor(FOX_HEADS))
        in_specs += [_const_spec((fw, LANES)), _const_spec((fw, LANES)), _const_spec((FOX_HEADS, LANES))]
        scratch += [pltpu.VMEM((fw, LANES), F32), pltpu.VMEM((fw, LANES), F32), pltpu.VMEM((FOX_HEADS, LANES), F32)]
        args += list(meta_t)
    return pl.pallas_call(
        functools.partial(_inproj_kernel, d_model=d, nt=nt, kv_t=kv_t),
        grid=(nb, nt + (1 if kv_t else 0)),
        in_specs=in_specs,
        out_specs=[o[0] for o in outs],
        out_shape=[o[1] for o in outs],
        scratch_shapes=scratch,
        compiler_params=pltpu.CompilerParams(dimension_semantics=("arbitrary", "arbitrary"),
                                             vmem_limit_bytes=VMEM_LIMIT),
        name="inproj",
    )(*args)


def _cacheprep_kernel(k_ref, v_ref, lf_ref, ka_ref, va_ref, fl_ref, carry):
    @pl.when(pl.program_id(1) == 0)
    def _():
        carry[...] = jnp.zeros_like(carry)

    fcol, cur = _cumsum_rows(lf_ref[...], carry[...])
    carry[...] = cur
    fl_ref[...] = cur
    _store_attention_operands(None, k_ref[...], v_ref[...], fcol, None, ka_ref, va_ref)


def _cacheprep(ck, cv, clf, tp):
    bsz, plen, _ = ck.shape
    aw = FOX_HEADS * LANES
    return pl.pallas_call(
        _cacheprep_kernel,
        grid=(bsz, plen // tp),
        in_specs=[pl.BlockSpec((None, tp, FOX_WIDTH), lambda b, i: (b, i, 0)),
                  pl.BlockSpec((None, tp, FOX_WIDTH), lambda b, i: (b, i, 0)),
                  pl.BlockSpec((None, tp, LANES), lambda b, i: (b, i, 0))],
        out_specs=[pl.BlockSpec((None, tp, aw), lambda b, i: (b, i, 0)),
                   pl.BlockSpec((None, tp, aw), lambda b, i: (b, i, 0)),
                   pl.BlockSpec((None, 1, LANES), lambda b, i: (b, 0, 0))],
        out_shape=[jax.ShapeDtypeStruct((bsz, plen, aw), BF16), jax.ShapeDtypeStruct((bsz, plen, aw), BF16),
                   jax.ShapeDtypeStruct((bsz, 1, LANES), F32)],
        scratch_shapes=[pltpu.VMEM((1, LANES), F32)],
        compiler_params=pltpu.CompilerParams(dimension_semantics=("arbitrary", "arbitrary")),
        name="cache_prep",
    )(ck, cv, clf)


def _fox_kernel(q_ref, kp_ref, vp_ref, k_ref, v_ref, o_ref, *, tq, tr, tkb, nq, pblk, n_pblk, pvalid):
    i = pl.program_id(2)
    nr = tq // tr
    chains = [(hh, r) for hh in range(2) for r in range(nr)]
    qs = [q_ref[r * tr:(r + 1) * tr, hh * LANES:(hh + 1) * LANES] for hh, r in chains]

    def upd1(n, state, kb, vb, mask=None):
        m, acc = state
        hh = chains[n][0]
        hs = slice(hh * LANES, (hh + 1) * LANES)
        s = lax.dot_general(qs[n], kb[:, hs], (((1,), (1,)), ((), ())), preferred_element_type=F32)
        if mask is not None:
            cut = s.shape[1] - mask.shape[1]
            tail = jnp.where(mask, s[:, cut:], NEG_BIG)
            s = tail if cut == 0 else jnp.concatenate([s[:, :cut], tail], axis=1)
        m_new = jnp.maximum(m, jnp.max(s, axis=-1, keepdims=True))
        alpha = jnp.exp(m - m_new)
        pm = jnp.exp(s - m_new)
        acc = alpha * acc + jnp.dot(pm.astype(BF16), vb[:, hs], preferred_element_type=F32)
        return m_new, acc

    def upd(carry, kb, vb):
        return tuple(upd1(n, carry[n], kb, vb) for n in range(len(chains)))

    def start():
        carry = tuple((jnp.full((tr, 1), NEG_BIG, F32), jnp.zeros((tr, LANES), F32)) for _ in chains)
        if pvalid is None:
            for jb in range(n_pblk):
                sl = slice(jb * pblk, (jb + 1) * pblk)
                carry = upd(carry, kp_ref[sl, :], vp_ref[sl, :])
        return carry

    def finish(carry, off):
        tail_w = tr + (LANES if pvalid is not None else 0)
        rj = lax.broadcasted_iota(jnp.int32, (tr, tail_w), 0)
        cj = lax.broadcasted_iota(jnp.int32, (tr, tail_w), 1)
        mask = cj <= rj
        if pvalid is not None:
            mask = (mask | (cj >= tr)) & (cj < tr + pvalid)
        carry = list(carry)
        for r in range(nr):
            width = (r + 1) * tr
            kb = k_ref[pl.ds(off, width), :]
            vb = v_ref[pl.ds(off, width), :]
            if pvalid is not None:
                kb = jnp.concatenate([kb, kp_ref[...]], axis=0)
                vb = jnp.concatenate([vb, vp_ref[...]], axis=0)
            for n, (hh, rr) in enumerate(chains):
                if rr == r:
                    carry[n] = upd1(n, carry[n], kb, vb, mask=mask)
        lane = lax.broadcasted_iota(jnp.int32, (tr, LANES), 1)
        outs = []
        for n in range(len(chains)):
            acc = carry[n][1]
            l = jnp.sum(jnp.where(lane == FOX_HEAD_DIM, acc, 0.0), axis=-1, keepdims=True)
            outs.append(acc / l)
        for r in range(nr):
            o_ref[r * tr:(r + 1) * tr, :] = jnp.where(lane < FOX_HEAD_DIM, outs[r],
                                                      pltpu.roll(outs[nr + r], FOX_HEAD_DIM, 1)).astype(o_ref.dtype)

    if tkb == tq and nq <= FOX_MAX_STATIC_BLOCKS:
        for nw in range(nq):
            @pl.when(i == nw)
            def _():
                carry = start()
                for j in range(nw):
                    carry = upd(carry, k_ref[j * tkb:(j + 1) * tkb, :], v_ref[j * tkb:(j + 1) * tkb, :])
                finish(carry, nw * tq)
    else:
        def body_wide(j, carry):
            off = pl.multiple_of(j * tkb, tkb)
            return upd(carry, k_ref[pl.ds(off, tkb), :], v_ref[pl.ds(off, tkb), :])

        def body(j, carry):
            off = pl.multiple_of(j * tq, tq)
            return upd(carry, k_ref[pl.ds(off, tq), :], v_ref[pl.ds(off, tq), :])

        n_wide = i // (tkb // tq)
        carry = lax.fori_loop(0, n_wide, body_wide, start())
        if tkb > tq:
            carry = lax.fori_loop(n_wide * (tkb // tq), i, body, carry)
        finish(carry, pl.multiple_of(i * tq, tq))


def _fox(q, kp, vp, k, v, tq, tr, pblk, pvalid=None):
    bsz, t, _ = q.shape
    bp, plen, _ = kp.shape
    nq = t // tq
    n_pblk = plen // pblk
    pw = 2 * LANES
    pb = (lambda b: b) if bp == bsz else (lambda b: 0)
    return pl.pallas_call(
        functools.partial(_fox_kernel, tq=tq, tr=tr, tkb=_row_tile(t, FOX_WIDE_KEYS), nq=nq, pblk=pblk, n_pblk=n_pblk,
                          pvalid=pvalid),
        grid=(bsz, FOX_HEADS // 2, nq),
        in_specs=[pl.BlockSpec((None, tq, pw), lambda b, p, i: (b, i, p)),
                  pl.BlockSpec((None, plen, pw), lambda b, p, i: (pb(b), 0, p)),
                  pl.BlockSpec((None, plen, pw), lambda b, p, i: (pb(b), 0, p)),
                  pl.BlockSpec((None, t, pw), lambda b, p, i: (b, 0, p)),
                  pl.BlockSpec((None, t, pw), lambda b, p, i: (b, 0, p))],
        out_specs=pl.BlockSpec((None, tq, LANES), lambda b, p, i: (b, i, p)),
        out_shape=jax.ShapeDtypeStruct((bsz, t, FOX_WIDTH), BF16),
        compiler_params=pltpu.CompilerParams(dimension_semantics=("arbitrary", "arbitrary", "arbitrary"),
                                             vmem_limit_bytes=VMEM_LIMIT),
        name="fox_attention",
    )(q, kp, vp, k, v)


def _mlstm_kernel(mx_ref, mv_ref, mo_ref, gt_ref, cp_ref, c0_ref, n0_ref, m0_ref,
                  wc_ref, bc_ref, wq_ref, wk_ref, gn_ref, sk_ref,
                  hm_ref, cout_ref, nout_ref, mout_ref,
                  xbuf, c_scr, n_scr, m_scr, *, tc, n_valid, nb, shared_state):
    c = pl.program_id(1)

    @pl.when(c == 0)
    def _():
        for bi in range(nb):
            si = 0 if shared_state else bi
            xbuf[bi, 0:8, :] = cp_ref[si]
            c_scr[bi] = c0_ref[si]
            n_scr[bi] = n0_ref[si]
            m_scr[bi] = m0_ref[si]

    for bi in range(nb):
        _mlstm_chunk(mx_ref.at[bi], mv_ref.at[bi], mo_ref.at[bi], gt_ref.at[bi], wc_ref, bc_ref, wq_ref, wk_ref,
                     gn_ref, sk_ref, hm_ref.at[bi], xbuf.at[bi], c_scr.at[bi], n_scr.at[bi], m_scr.at[bi],
                     tc=tc, n_valid=n_valid)

    @pl.when(c == pl.num_programs(1) - 1)
    def _():
        cout_ref[...] = c_scr[...]
        nout_ref[...] = n_scr[...]
        mout_ref[...] = m_scr[...]


def _mlstm_chunk(mx_ref, mv_ref, mo_ref, gt_ref, wc_ref, bc_ref, wq_ref, wk_ref, gn_ref, sk_ref, hm_ref,
                 xbuf, c_scr, n_scr, m_scr, *, tc, n_valid):
    xbuf[8:8 + tc, :] = mx_ref[...]
    xc = bc_ref[...]
    for i in range(CONV_WIDTH):
        xc = xc + xbuf[5 + i:5 + i + tc, :] * wc_ref[i:i + 1, :]
    xc = xc * jax.nn.sigmoid(xc)
    tail = xbuf[tc:tc + 8, :]
    xbuf[0:8, :] = tail

    gt = gt_ref[...]
    lane = lax.broadcasted_iota(jnp.int32, (tc, LANES), 1)
    if n_valid is not None:
        rowi = lax.broadcasted_iota(jnp.int32, (tc, LANES), 0)
        is_i = (lane >= GATE_ML_I) & (lane < GATE_ML_F)
        gt = jnp.where(rowi < n_valid, gt, jnp.where(is_i, PAD_LOG_GATE, 0.0))
    ri = lax.broadcasted_iota(jnp.int32, (tc, tc), 0)
    ci = lax.broadcasted_iota(jnp.int32, (tc, tc), 1)
    causal = ci <= ri
    b_all = _cumsum_rows(gt, jnp.zeros((1, LANES), F32))[0]
    r_t = (pltpu.roll(gt, GATE_ML_F - GATE_ML_I, 1) - b_all).T
    heads = range(ML_HEADS)
    hsl = [slice(h * ML_HEAD_DIM, (h + 1) * ML_HEAD_DIM) for h in heads]
    b_col = [b_all[:, GATE_ML_F + h:GATE_ML_F + h + 1] for h in heads]
    ig_col = [gt[:, GATE_ML_I + h:GATE_ML_I + h + 1] for h in heads]
    m_prev = [m_scr[h:h + 1, 0:1] for h in heads]
    dmat = [jnp.where(causal, b_col[h] + r_t[GATE_ML_F + h:GATE_ML_F + h + 1, :], NEG_BIG) for h in heads]
    m_inter = [b_col[h] + m_prev[h] for h in heads]
    m_t = [jnp.maximum(m_inter[h], jnp.max(dmat[h], axis=-1, keepdims=True)) for h in heads]
    w_inter = [jnp.exp(m_inter[h] - m_t[h]) for h in heads]

    xh = [xc[:, hsl[h]] for h in heads]
    xhb = [x.astype(BF16) for x in xh]
    q = [jnp.dot(xhb[h], wq_ref[h], preferred_element_type=F32) for h in heads]
    k = [jnp.dot(xhb[h], wk_ref[h], preferred_element_type=F32) * (ML_HEAD_DIM ** -0.5) for h in heads]
    qb = [x.astype(BF16) for x in q]
    kb = [x.astype(BF16) for x in k]
    v = [mv_ref[:, hsl[h]] for h in heads]
    smat = [lax.dot_general(qb[h], kb[h], (((1,), (1,)), ((), ())), preferred_element_type=F32) for h in heads]
    amat = [jnp.exp(dmat[h] - m_t[h]) * smat[h] for h in heads]
    ch = [c_scr[h] for h in heads]
    nh = [n_scr[h:h + 1, :] for h in heads]
    num = [jnp.dot(amat[h].astype(BF16), v[h], preferred_element_type=F32)
           + w_inter[h] * jnp.dot(qb[h], ch[h].astype(BF16), preferred_element_type=F32) for h in heads]
    den = [jnp.sum(amat[h], axis=-1, keepdims=True) + w_inter[h] * jnp.sum(q[h] * nh[h], axis=-1, keepdims=True)
           for h in heads]
    hcell = [num[h] / jnp.maximum(jnp.abs(den[h]), jnp.exp(-m_t[h])) for h in heads]

    b_last = [b_col[h][tc - 1:tc, :] for h in heads]
    g = [b_last[h] - b_col[h] + ig_col[h] for h in heads]
    m_new = [jnp.maximum(b_last[h] + m_prev[h], jnp.max(g[h], axis=0, keepdims=True)) for h in heads]
    decay = [jnp.exp(b_last[h] + m_prev[h] - m_new[h]) for h in heads]
    kw = [k[h] * jnp.exp(g[h] - m_new[h]) for h in heads]
    for h in heads:
        c_scr[h] = decay[h] * ch[h] + lax.dot_general(kw[h].astype(BF16), v[h], (((0,), (0,)), ((), ())),
                                                       preferred_element_type=F32)
        n_scr[h:h + 1, :] = decay[h] * nh[h] + jnp.sum(kw[h], axis=0, keepdims=True)
        m_scr[h:h + 1, :] = jnp.broadcast_to(m_new[h], (1, LANES))
        y = mo_ref[:, hsl[h]].astype(F32) * hcell[h]
        hm_ref[:, hsl[h]] = (_rms(y, gn_ref[:, hsl[h]]) + sk_ref[:, hsl[h]] * xh[h]).astype(hm_ref.dtype)


def _mlstm(mx, mv, mo, gt, cp, c0, n0, m0, wc, bc, wq, wk, gn, sk, tc, nb=1, n_valid=None):
    bsz, t, _ = mx.shape
    shared = c0.shape[0] != bsz
    sn = 1 if shared else nb
    sb = (lambda b: 0) if shared else (lambda b: b)
    tok = lambda w: pl.BlockSpec((nb, tc, w), lambda b, c: (b, c, 0))
    hd = ML_HEAD_DIM
    st_specs = [pl.BlockSpec((nb, ML_HEADS, hd, hd), lambda b, c: (b, 0, 0, 0)),
                pl.BlockSpec((nb, 8, LANES), lambda b, c: (b, 0, 0)),
                pl.BlockSpec((nb, 8, LANES), lambda b, c: (b, 0, 0))]
    return pl.pallas_call(
        functools.partial(_mlstm_kernel, tc=tc, n_valid=n_valid, nb=nb, shared_state=shared),
        grid=(bsz // nb, t // tc),
        in_specs=[tok(ML_WIDTH), tok(ML_WIDTH), tok(ML_WIDTH), tok(LANES),
                  pl.BlockSpec((sn, 8, ML_WIDTH), lambda b, c: (sb(b), 0, 0)),
                  pl.BlockSpec((sn, ML_HEADS, hd, hd), lambda b, c: (sb(b), 0, 0, 0)),
                  pl.BlockSpec((sn, 8, LANES), lambda b, c: (sb(b), 0, 0)),
                  pl.BlockSpec((sn, 8, LANES), lambda b, c: (sb(b), 0, 0)),
                  _const_spec((CONV_WIDTH, ML_WIDTH)), _const_spec((1, ML_WIDTH)),
                  _const_spec((ML_HEADS, hd, hd)), _const_spec((ML_HEADS, hd, hd)),
                  _const_spec((1, ML_WIDTH)), _const_spec((1, ML_WIDTH))],
        out_specs=[tok(ML_WIDTH)] + st_specs,
        out_shape=[jax.ShapeDtypeStruct((bsz, t, ML_WIDTH), BF16),
                   jax.ShapeDtypeStruct((bsz, ML_HEADS, hd, hd), F32),
                   jax.ShapeDtypeStruct((bsz, 8, LANES), F32),
                   jax.ShapeDtypeStruct((bsz, 8, LANES), F32)],
        scratch_shapes=[pltpu.VMEM((nb, 8 + tc, ML_WIDTH), F32), pltpu.VMEM((nb, ML_HEADS, hd, hd), F32),
                        pltpu.VMEM((nb, 8, LANES), F32), pltpu.VMEM((nb, 8, LANES), F32)],
        compiler_params=pltpu.CompilerParams(dimension_semantics=("arbitrary", "arbitrary")),
        name="mlstm",
    )(mx, mv, mo, gt, cp, c0, n0, m0, wc, bc, wq, wk, gn, sk)


def _post_kernel(a_ref, hm_ref, ga_ref, gb_ref, x_ref, wfp_ref, wmp_ref, wo_ref, wgu_ref, wd_ref,
                 g1_ref, g2_ref, g3_ref, y_ref, *, d_ff, ff_cuts):
    ya = jnp.dot(a_ref[...], wfp_ref[...], preferred_element_type=F32)
    yb = jnp.dot(hm_ref[...], wmp_ref[...], preferred_element_type=F32)
    merged = ga_ref[...].astype(F32) * ya + gb_ref[...].astype(F32) * yb
    mix = jnp.dot(merged.astype(BF16), wo_ref[...], preferred_element_type=F32)
    x1 = x_ref[...] + _rms(mix, g1_ref[...])
    h2 = _rms(x1, g2_ref[...]).astype(BF16)
    f = None
    for lo, hi in zip(ff_cuts[:-1], ff_cuts[1:]):
        g = jnp.dot(h2, wgu_ref[:, lo:hi], preferred_element_type=F32)
        u = jnp.dot(h2, wgu_ref[:, d_ff + lo:d_ff + hi], preferred_element_type=F32)
        t = (g * jax.nn.sigmoid(g) * u).astype(BF16)
        fj = jnp.dot(t, wd_ref[lo:hi, :], preferred_element_type=F32)
        f = fj if f is None else f + fj
    y_ref[...] = x1 + _rms(f, g3_ref[...])


def _post(a, hm, ga, gb, x, wfp, wmp, wo, wgu, wd, g1, g2, g3, tm):
    n, d = x.shape
    d_ff = wd.shape[0]
    nmt = d_ff // MXU_TILE
    ff_cuts = tuple(MXU_TILE * ((nmt * j) // FF_CHUNKS) for j in range(FF_CHUNKS)) + (d_ff,) \
        if d_ff % MXU_TILE == 0 else (0, d_ff)
    row = lambda c: pl.BlockSpec((tm, c), lambda i: (i, 0))
    return pl.pallas_call(
        functools.partial(_post_kernel, d_ff=d_ff, ff_cuts=ff_cuts),
        grid=(n // tm,),
        in_specs=[row(FOX_WIDTH), row(ML_WIDTH), row(d), row(d), row(d),
                  _const_spec(wfp.shape), _const_spec(wmp.shape), _const_spec(wo.shape),
                  _const_spec(wgu.shape), _const_spec(wd.shape),
                  _const_spec((1, d)), _const_spec((1, d)), _const_spec((1, d))],
        out_specs=row(d),
        out_shape=jax.ShapeDtypeStruct((n, d), F32),
        compiler_params=pltpu.CompilerParams(dimension_semantics=("arbitrary",), vmem_limit_bytes=VMEM_LIMIT),
        name="post_mix_ffn",
    )(a, hm, ga, gb, x, wfp, wmp, wo, wgu, wd, g1, g2, g3)


def _pad_rows(a, rows, axis):
    pad = [(0, 0)] * a.ndim
    pad[axis] = (0, rows - a.shape[axis])
    return jnp.pad(a, pad)


def _row_tile(n, pref):
    t = min(pref, n)
    while n % t:
        t //= 2
    return t


def kernel(x_prompt, x_sample, cache_fox_k, cache_fox_v, cache_fox_logf, state_mlstm_C, state_mlstm_n,
           state_mlstm_m, state_mlstm_conv, meta_tokens, w_in, b_in, g_fq, g_fk, w_conv, b_conv, w_mq, w_mk,
           g_mnorm, skip_m, w_fox_proj, w_ml_proj, w_out, g_pre_mix, g_post_mix, g_pre_ffn, g_post_ffn,
           w_gate_up, w_down):
    assert w_in.shape[0] == 1, "single-layer trunk"
    bsz, seq, d = x_prompt.shape
    dbsz, dseq, _ = x_sample.shape
    past = cache_fox_k.shape[2]
    fw, mw = FOX_WIDTH, ML_WIDTH

    w, b = w_in[0], b_in[0]
    o_ff = 3 * fw
    o_mx = o_ff + FOX_HEADS
    o_mi = o_mx + 3 * mw
    o_mf = o_mi + ML_HEADS
    o_ga = o_mf + ML_HEADS

    def relayout(a):
        gates = jnp.concatenate([a[..., o_ff:o_mx], a[..., o_mi:o_mf], a[..., o_mf:o_ga], a[..., o_mf:o_ga]], axis=-1)
        gates = _pad_rows(gates, LANES, gates.ndim - 1)
        return jnp.concatenate([a[..., :o_ff], a[..., o_mx:o_mi], a[..., o_ga:], gates], axis=-1)

    w_p = relayout(w).astype(BF16)
    b_p = relayout(b)[None, :]
    hid = jnp.arange(fw) // FOX_HEAD_DIM
    hsum = (hid[:, None] == hid[None, :]).astype(BF16)
    gq = g_fq[0].reshape(1, fw)
    gk = g_fk[0].reshape(1, fw)
    gpre = g_pre_mix[0][None, :]
    inproj = lambda x, c0, tm, nt, **kw: _inproj(x, gpre, w_p, b_p, gq, gk, hsum, c0, tm, nt, **kw)

    wc, bc = w_conv[0], b_conv[0][None, :]
    wq, wk = w_mq[0].astype(BF16), w_mk[0].astype(BF16)
    gn, sk = g_mnorm[0].reshape(1, mw), skip_m[0][None, :]
    mlstm = lambda *a, **kw: _mlstm(*a, wc, bc, wq, wk, gn, sk, **kw)

    def conv_hist(rows):
        return jnp.pad(rows, ((0, 0), (8 - (CONV_WIDTH - 1), 0), (0, 0)))

    def m_rows(m):
        return jnp.broadcast_to(_pad_rows(m, 8, 1)[:, :, None], (m.shape[0], 8, LANES))

    mt = 64
    xm = _pad_rows(meta_tokens.astype(F32), mt, 0)
    (_, kf_m, ka_m, vf_m, va_m, mx_m, mv_m, mo_m, _, _, gt_m, fc_m) = inproj(xm, jnp.zeros((1, 1, LANES), F32), mt, 1)
    zc = jnp.zeros((1, ML_HEADS, ML_HEAD_DIM, ML_HEAD_DIM), F32)
    z8 = jnp.zeros((1, 8, LANES), F32)
    _, c_m, n_m, m_m = mlstm(mx_m[None], mv_m[None], mo_m[None], gt_m[None], jnp.zeros((1, 8, mw), F32),
                             zc, z8, z8, tc=mt, n_valid=N_META)

    xp = x_prompt.reshape(bsz * seq, d)
    tm = _row_tile(seq, 512)
    lanes_t = lambda rows: _pad_rows(rows[:N_META].T, LANES, 1)
    (qa_p, kt_p, ka_p, vt_p, va_p, mx_p, mv_p, mo_p, ga_p, gb_p, gt_p, _, lt_p) = inproj(
        xp, fc_m[None, N_META - 1:N_META], tm, seq // tm,
        meta_t=(lanes_t(kf_m), lanes_t(vf_m), lanes_t(gt_m[:, :FOX_HEADS])))
    r3 = lambda a: a.reshape(bsz, seq, a.shape[-1])
    mx_p3 = r3(mx_p)
    hm_p, c_p, n_p, m_p = mlstm(mx_p3, r3(mv_p), r3(mo_p), r3(gt_p),
                                conv_hist(mx_m[None, N_META - (CONV_WIDTH - 1):N_META]), c_m, n_m, m_m,
                                tc=_row_tile(seq, 256), nb=_row_tile(bsz, 2))
    a_p = _fox(r3(qa_p), _pad_rows(ka_m, LANES, 0)[None], _pad_rows(va_m, LANES, 0)[None], r3(ka_p), r3(va_p),
               tq=_row_tile(seq, 1024), tr=_row_tile(seq, 512), pblk=LANES, pvalid=N_META)

    kc, vc, f_c = _cacheprep(cache_fox_k[0].reshape(dbsz, past, fw), cache_fox_v[0].reshape(dbsz, past, fw),
                             _pad_rows(cache_fox_logf[0].astype(F32), LANES, 2), _row_tile(past, 512))
    xs = x_sample.reshape(dbsz * dseq, d)
    (qa_s, kf_s, ka_s, vf_s, va_s, mx_s, mv_s, mo_s, ga_s, gb_s, gt_s, _) = inproj(xs, f_c, dseq, 1)
    r3s = lambda a: a.reshape(dbsz, dseq, a.shape[-1])
    mx_s3 = r3s(mx_s)
    hm_s, c_s, n_s, m_s = mlstm(mx_s3, r3s(mv_s), r3s(mo_s), r3s(gt_s), conv_hist(state_mlstm_conv[0]),
                                state_mlstm_C[0], _pad_rows(state_mlstm_n[0], 8, 1), m_rows(state_mlstm_m[0]),
                                tc=dseq, nb=_row_tile(dbsz, 4))
    a_s = _fox(r3s(qa_s), kc, vc, r3s(ka_s), r3s(va_s), tq=dseq, tr=dseq, pblk=_row_tile(past, 512))

    wfp, wmp, wo = w_fox_proj[0].astype(BF16), w_ml_proj[0].astype(BF16), w_out[0].astype(BF16)
    wgu, wd = w_gate_up[0].astype(BF16), w_down[0].astype(BF16)
    g1, g2, g3 = g_post_mix[0][None, :], g_pre_ffn[0][None, :], g_post_ffn[0][None, :]
    post = lambda a, hm, ga, gb, x, tm: _post(a, hm, ga, gb, x, wfp, wmp, wo, wgu, wd, g1, g2, g3, tm)
    y_p = post(a_p.reshape(bsz * seq, fw), hm_p.reshape(bsz * seq, mw), ga_p, gb_p, xp, _row_tile(bsz * seq, 512))
    y_s = post(a_s.reshape(dbsz * dseq, fw), hm_s.reshape(dbsz * dseq, mw), ga_s, gb_s, xs,
               _row_tile(dbsz * dseq, 512))

    def tokens_major(a):
        return a.reshape(bsz, FOX_HEADS, FOX_HEAD_DIM, a.shape[-1]).transpose(0, 3, 1, 2)[None]

    heads = lambda a: a.reshape(a.shape[:-1] + (FOX_HEADS, FOX_HEAD_DIM))
    tail = CONV_WIDTH - 1
    return (y_p.reshape(bsz, seq, d), y_s.reshape(dbsz, dseq, d),
            tokens_major(kt_p), tokens_major(vt_p), lt_p.transpose(0, 2, 1)[None],
            c_p[None], n_p[None, :, :ML_HEADS], m_p[None, :, :ML_HEADS, 0], mx_p3[None, :, seq - tail:],
            heads(r3s(kf_s))[None], heads(r3s(vf_s))[None], r3s(gt_s)[None, ..., :FOX_HEADS],
            c_s[None], n_s[None, :, :ML_HEADS], m_s[None, :, :ML_HEADS, 0], mx_s3[None, :, dseq - tail:])
```

```python
import functools

import jax
import jax.numpy as jnp
from jax import lax
from jax.experimental import pallas as pl
from jax.experimental.pallas import tpu as pltpu

F32 = jnp.float32
BF16 = jnp.bfloat16

N_META = 16
FOX_HEADS = 8
FOX_HEAD_DIM = 64
FOX_WIDTH = FOX_HEADS * FOX_HEAD_DIM
ML_HEADS = 4
ML_HEAD_DIM = 128
ML_WIDTH = ML_HEADS * ML_HEAD_DIM
CONV_WIDTH = 4
RMS_EPS = 1e-6
PAD_LOG_GATE = -1e30
NEG_BIG = -1e30

LANES = 128
GATE_FOX_F = 0
GATE_ML_I = 8
GATE_ML_F = 12
GATE_ML_F2 = 16

AUG_STEP = FOX_HEADS
AUG_QF = FOX_HEAD_DIM
AUG_KF = FOX_HEAD_DIM + 24

VMEM_LIMIT = 56 * 1024 * 1024
FOX_WIDE_KEYS = 1024
FOX_MAX_STATIC_BLOCKS = 4
MXU_TILE = 256
FF_CHUNKS = 3


def _const_spec(shape):
    nd = len(shape)
    return pl.BlockSpec(shape, lambda *_: (0,) * nd, pipeline_mode=pl.Buffered(1))


def _rms(x, g):
    return x * lax.rsqrt(jnp.mean(x * x, axis=-1, keepdims=True) + RMS_EPS) * g


def _split3(x):
    hi = x.astype(BF16).astype(F32)
    r = x - hi
    mid = r.astype(BF16).astype(F32)
    lo = (r - mid).astype(BF16).astype(F32)
    return hi, mid, lo


def _cumsum_rows(lg, cur):
    tm = lg.shape[0]
    blk = min(tm, LANES)
    r = lax.broadcasted_iota(jnp.int32, (blk, blk), 0)
    c = lax.broadcasted_iota(jnp.int32, (blk, blk), 1)
    tril = (c <= r).astype(BF16)
    outs = []
    for s in range(tm // blk):
        hi, mid, lo = _split3(lg[s * blk:(s + 1) * blk])
        f = (jnp.dot(tril, lo.astype(BF16), preferred_element_type=F32)
             + jnp.dot(tril, mid.astype(BF16), preferred_element_type=F32)
             + jnp.dot(tril, hi.astype(BF16), preferred_element_type=F32)) + cur
        outs.append(f)
        cur = f[blk - 1:blk]
    return (outs[0] if len(outs) == 1 else jnp.concatenate(outs, axis=0)), cur


def _store_attention_operands(qn, kn, v, fcol, q_ref, k_ref, v_ref, fillers=()):
    tm = fcol.shape[0]
    lane = lax.broadcasted_iota(jnp.int32, (tm, LANES), 1)
    data = lane < FOX_HEAD_DIM
    hi, mid, lo = _split3(fcol)
    comb = jnp.where(lane < AUG_STEP, hi, jnp.where(lane < 2 * AUG_STEP, pltpu.roll(mid, AUG_STEP, 1),
                                                     pltpu.roll(lo, 2 * AUG_STEP, 1)))
    ncomb = -comb
    is_qf = (lane == AUG_QF) | (lane == AUG_QF + AUG_STEP) | (lane == AUG_QF + 2 * AUG_STEP)
    is_kf = (lane == AUG_KF) | (lane == AUG_KF + AUG_STEP) | (lane == AUG_KF + 2 * AUG_STEP)
    base_q = jnp.where(is_kf, 1.0, 0.0)
    base_k = jnp.where(is_qf, 1.0, 0.0)
    base_v = jnp.where(lane == FOX_HEAD_DIM, 1.0, 0.0)
    fillers = list(fillers)
    for h in range(FOX_HEADS):
        t, odd = divmod(h, 2)

        def head_tile(z):
            x = z[:, t * LANES:(t + 1) * LANES]
            return pltpu.roll(x, FOX_HEAD_DIM, 1) if odd else x

        hs = slice(h * LANES, (h + 1) * LANES)
        if q_ref is not None:
            aug = jnp.where(is_qf, pltpu.roll(comb, AUG_QF - h, 1), base_q)
            q_ref[:, hs] = jnp.where(data, head_tile(qn), aug).astype(BF16)
        aug = jnp.where(is_kf, pltpu.roll(ncomb, AUG_KF - h, 1), base_k)
        k_ref[:, hs] = jnp.where(data, head_tile(kn), aug).astype(BF16)
        v_ref[:, hs] = jnp.where(data, head_tile(v), base_v).astype(BF16)
        if fillers:
            fillers.pop(0)()
    for f in fillers:
        f()


def _inproj_kernel(*refs, d_model, nt, kv_t, conv_valid):
    (x_ref, gpre_ref, w_ref, b_ref, gq_ref, gk_ref, hs_ref, c0_ref, cp_ref, wc_ref, bc_ref), rest = refs[:11], refs[11:]
    if kv_t:
        kmeta_ref, vmeta_ref, lmeta_ref = rest[:3]
        (q_ref, kf_ref, kb_ref, vf_ref, vb_ref, xc_ref, mv_ref, mo_ref, ga_ref, gb_ref, gt_ref, fc_ref, ct_ref,
         lt_ref, carry, xbuf, kcar, vcar, lcar) = rest[3:]
    else:
        (q_ref, kf_ref, kb_ref, vf_ref, vb_ref, xc_ref, mv_ref, mo_ref, ga_ref, gb_ref, gt_ref, fc_ref, ct_ref,
         carry, xbuf) = rest
    fw, mw = FOX_WIDTH, ML_WIDTH
    offs = [0, fw, 2 * fw, 3 * fw, 3 * fw + mw, 3 * fw + 2 * mw, 3 * fw + 3 * mw,
            3 * fw + 3 * mw + d_model, 3 * fw + 3 * mw + 2 * d_model, 3 * fw + 3 * mw + 2 * d_model + LANES]
    r = pl.program_id(1)

    @pl.when(r == 0)
    def _():
        carry[...] = c0_ref[...]
        xbuf[0:8, :] = cp_ref[...]
        if kv_t:
            kcar[...] = kmeta_ref[...]
            vcar[...] = vmeta_ref[...]
            lcar[...] = lmeta_ref[...]

    def store_shifted(xt, car, out_ref):
        sh = pltpu.roll(xt, N_META, 1)
        lane = lax.broadcasted_iota(jnp.int32, (xt.shape[0], LANES), 1)
        out_ref[:, :LANES] = jnp.where(lane < N_META, car[...], sh[:, :LANES])
        out_ref[:, LANES:] = sh[:, LANES:]
        car[...] = sh[:, :LANES]

    @pl.when(r < nt)
    def _():
        h = _rms(x_ref[...], gpre_ref[...]).astype(BF16)

        def seg(i):
            a, b = offs[i], offs[i + 1]
            return jnp.dot(h, w_ref[:, a:b], preferred_element_type=F32) + b_ref[:, a:b]

        def headnorm(z, g):
            ms = jnp.dot((z * z).astype(BF16), hs_ref[...], preferred_element_type=F32) * (1.0 / FOX_HEAD_DIM)
            return z * lax.rsqrt(ms + RMS_EPS) * g

        g = seg(8)
        lane = lax.broadcasted_iota(jnp.int32, g.shape, 1)
        raw = (lane >= GATE_ML_I) & (lane < GATE_ML_F)
        gt = jnp.where(raw, g, jax.nn.log_sigmoid(g))
        gt_ref[...] = gt
        fcol, cur = _cumsum_rows(gt, carry[...])
        carry[...] = cur
        fc_ref[...] = fcol

        q = headnorm(seg(0), gq_ref[...]) * (FOX_HEAD_DIM ** -0.5)
        k = headnorm(seg(1), gk_ref[...])
        v = seg(2)
        def store_kv():
            if kv_t:
                store_shifted(k.T, kcar, kf_ref)
                store_shifted(v.T, vcar, vf_ref)
                store_shifted(gt.T[:FOX_HEADS], lcar, lt_ref)
            else:
                kf_ref[...] = k
                vf_ref[...] = v

        def store_xc():
            tm = x_ref.shape[0]
            xbuf[8:8 + tm, :] = seg(3)
            xc = bc_ref[...]
            for i in range(CONV_WIDTH):
                xc = xc + xbuf[5 + i:5 + i + tm, :] * wc_ref[i:i + 1, :]
            xc_ref[...] = (xc * jax.nn.sigmoid(xc)).astype(BF16)
            ct_ref[...] = xbuf[conv_valid:conv_valid + 8, :]
            xbuf[0:8, :] = xbuf[tm:tm + 8, :]

        def store_mv():
            mv_ref[...] = seg(4).astype(BF16)

        def store_mo():
            mo_ref[...] = jax.nn.sigmoid(seg(5)).astype(BF16)

        def store_ga():
            ga_ref[...] = jax.nn.sigmoid(seg(6)).astype(BF16)

        def store_gb():
            gb_ref[...] = jax.nn.sigmoid(seg(7)).astype(BF16)

        _store_attention_operands(q, k, v, fcol, q_ref, kb_ref, vb_ref,
                                  fillers=(store_xc, store_mv, store_mo, store_ga, store_gb, store_kv))

    if kv_t:
        @pl.when(r == nt)
        def _():
            kf_ref[:, :LANES] = kcar[...]
            vf_ref[:, :LANES] = vcar[...]
            lt_ref[:, :LANES] = lcar[...]


def _inproj(x, gpre, w, b, gq, gk, hs, c0, cp, wc, bc, tm, nt, meta_t=None, conv_valid=None):
    n, d = x.shape
    nb = n // (nt * tm)
    dinp = w.shape[1]
    fw, mw = FOX_WIDTH, ML_WIDTH
    aw = FOX_HEADS * LANES
    kv_t = meta_t is not None
    tile = lambda b, r: b * nt + jnp.minimum(r, nt - 1)
    row = lambda c: pl.BlockSpec((tm, c), lambda b, r: (tile(b, r), 0))
    cb = (lambda b: b) if c0.shape[0] > 1 else (lambda b: 0)
    pb = (lambda b: b) if cp.shape[0] > 1 else (lambda b: 0)
    lp = N_META + nt * tm
    tmin = lambda c: pl.BlockSpec((None, c, tm), lambda b, r: (b, 0, r))
    flat = lambda c, t: (row(c), jax.ShapeDtypeStruct((n, c), t))
    tok_minor = lambda c: (tmin(c), jax.ShapeDtypeStruct((nb, c, lp), F32))
    outs = [flat(aw, BF16), tok_minor(fw) if kv_t else flat(fw, F32), flat(aw, BF16),
            tok_minor(fw) if kv_t else flat(fw, F32), flat(aw, BF16), flat(mw, BF16), flat(mw, BF16), flat(mw, BF16),
            flat(d, BF16), flat(d, BF16), flat(LANES, F32), flat(LANES, F32),
            (pl.BlockSpec((None, 8, mw), lambda b, r: (b, 0, 0)), jax.ShapeDtypeStruct((nb, 8, mw), F32))]
    in_specs = [row(d), _const_spec((1, d)), _const_spec((d, dinp)), _const_spec((1, dinp)),
                _const_spec((1, fw)), _const_spec((1, fw)), _const_spec((fw, fw)),
                pl.BlockSpec((None, 1, LANES), lambda b, r: (cb(b), 0, 0)),
                pl.BlockSpec((None, 8, mw), lambda b, r: (pb(b), 0, 0)),
                _const_spec((CONV_WIDTH, mw)), _const_spec((1, mw))]
    scratch = [pltpu.VMEM((1, LANES), F32), pltpu.VMEM((8 + tm, mw), F32)]
    args = [x, gpre, w, b, gq, gk, hs, c0, cp, wc, bc]
    if kv_t:
        outs.append(tok_minor(FOX_HEADS))
        in_specs += [_const_spec((fw, LANES)), _const_spec((fw, LANES)), _const_spec((FOX_HEADS, LANES))]
        scratch += [pltpu.VMEM((fw, LANES), F32), pltpu.VMEM((fw, LANES), F32), pltpu.VMEM((FOX_HEADS, LANES), F32)]
        args += list(meta_t)
    return pl.pallas_call(
        functools.partial(_inproj_kernel, d_model=d, nt=nt, kv_t=kv_t,
                          conv_valid=tm if conv_valid is None else conv_valid),
        grid=(nb, nt + (1 if kv_t else 0)),
        in_specs=in_specs,
        out_specs=[o[0] for o in outs],
        out_shape=[o[1] for o in outs],
        scratch_shapes=scratch,
        compiler_params=pltpu.CompilerParams(dimension_semantics=("arbitrary", "arbitrary"),
                                             vmem_limit_bytes=VMEM_LIMIT),
        name="inproj",
    )(*args)


def _cacheprep_kernel(k_ref, v_ref, lf_ref, ka_ref, va_ref, fl_ref, carry):
    @pl.when(pl.program_id(1) == 0)
    def _():
        carry[...] = jnp.zeros_like(carry)

    fcol, cur = _cumsum_rows(lf_ref[...], carry[...])
    carry[...] = cur
    fl_ref[...] = cur
    _store_attention_operands(None, k_ref[...], v_ref[...], fcol, None, ka_ref, va_ref)


def _cacheprep(ck, cv, clf, tp):
    bsz, plen, _ = ck.shape
    aw = FOX_HEADS * LANES
    return pl.pallas_call(
        _cacheprep_kernel,
        grid=(bsz, plen // tp),
        in_specs=[pl.BlockSpec((None, tp, FOX_WIDTH), lambda b, i: (b, i, 0)),
                  pl.BlockSpec((None, tp, FOX_WIDTH), lambda b, i: (b, i, 0)),
                  pl.BlockSpec((None, tp, LANES), lambda b, i: (b, i, 0))],
        out_specs=[pl.BlockSpec((None, tp, aw), lambda b, i: (b, i, 0)),
                   pl.BlockSpec((None, tp, aw), lambda b, i: (b, i, 0)),
                   pl.BlockSpec((None, 1, LANES), lambda b, i: (b, 0, 0))],
        out_shape=[jax.ShapeDtypeStruct((bsz, plen, aw), BF16), jax.ShapeDtypeStruct((bsz, plen, aw), BF16),
                   jax.ShapeDtypeStruct((bsz, 1, LANES), F32)],
        scratch_shapes=[pltpu.VMEM((1, LANES), F32)],
        compiler_params=pltpu.CompilerParams(dimension_semantics=("arbitrary", "arbitrary")),
        name="cache_prep",
    )(ck, cv, clf)


def _fox_kernel(q_ref, kp_ref, vp_ref, k_ref, v_ref, o_ref, *, tq, tr, tkb, nq, pblk, n_pblk, pvalid):
    i = pl.program_id(2)
    nr = tq // tr
    chains = [(hh, r) for hh in range(2) for r in range(nr)]
    qs = [q_ref[r * tr:(r + 1) * tr, hh * LANES:(hh + 1) * LANES] for hh, r in chains]

    def upd1(n, state, kb, vb, mask=None):
        m, acc = state
        hh = chains[n][0]
        hs = slice(hh * LANES, (hh + 1) * LANES)
        s = lax.dot_general(qs[n], kb[:, hs], (((1,), (1,)), ((), ())), preferred_element_type=F32)
        if mask is not None:
            cut = s.shape[1] - mask.shape[1]
            tail = jnp.where(mask, s[:, cut:], NEG_BIG)
            s = tail if cut == 0 else jnp.concatenate([s[:, :cut], tail], axis=1)
        m_new = jnp.maximum(m, jnp.max(s, axis=-1, keepdims=True))
        alpha = jnp.exp(m - m_new)
        pm = jnp.exp(s - m_new)
        acc = alpha * acc + jnp.dot(pm.astype(BF16), vb[:, hs], preferred_element_type=F32)
        return m_new, acc

    def upd(carry, kb, vb):
        return tuple(upd1(n, carry[n], kb, vb) for n in range(len(chains)))

    def start():
        carry = tuple((jnp.full((tr, 1), NEG_BIG, F32), jnp.zeros((tr, LANES), F32)) for _ in chains)
        if pvalid is None:
            for jb in range(n_pblk):
                sl = slice(jb * pblk, (jb + 1) * pblk)
                carry = upd(carry, kp_ref[sl, :], vp_ref[sl, :])
        return carry

    def finish(carry, off):
        tail_w = tr + (LANES if pvalid is not None else 0)
        rj = lax.broadcasted_iota(jnp.int32, (tr, tail_w), 0)
        cj = lax.broadcasted_iota(jnp.int32, (tr, tail_w), 1)
        mask = cj <= rj
        if pvalid is not None:
            mask = (mask | (cj >= tr)) & (cj < tr + pvalid)
        carry = list(carry)
        for r in range(nr):
            width = (r + 1) * tr
            kb = k_ref[pl.ds(off, width), :]
            vb = v_ref[pl.ds(off, width), :]
            if pvalid is not None:
                kb = jnp.concatenate([kb, kp_ref[...]], axis=0)
                vb = jnp.concatenate([vb, vp_ref[...]], axis=0)
            for n, (hh, rr) in enumerate(chains):
                if rr == r:
                    carry[n] = upd1(n, carry[n], kb, vb, mask=mask)
        lane = lax.broadcasted_iota(jnp.int32, (tr, LANES), 1)
        outs = []
        for n in range(len(chains)):
            acc = carry[n][1]
            l = jnp.sum(jnp.where(lane == FOX_HEAD_DIM, acc, 0.0), axis=-1, keepdims=True)
            outs.append(acc / l)
        for r in range(nr):
            o_ref[r * tr:(r + 1) * tr, :] = jnp.where(lane < FOX_HEAD_DIM, outs[r],
                                                      pltpu.roll(outs[nr + r], FOX_HEAD_DIM, 1)).astype(o_ref.dtype)

    if tkb == tq and nq <= FOX_MAX_STATIC_BLOCKS:
        for nw in range(nq):
            @pl.when(i == nw)
            def _():
                carry = start()
                for j in range(nw):
                    carry = upd(carry, k_ref[j * tkb:(j + 1) * tkb, :], v_ref[j * tkb:(j + 1) * tkb, :])
                finish(carry, nw * tq)
    else:
        def body_wide(j, carry):
            off = pl.multiple_of(j * tkb, tkb)
            return upd(carry, k_ref[pl.ds(off, tkb), :], v_ref[pl.ds(off, tkb), :])

        def body(j, carry):
            off = pl.multiple_of(j * tq, tq)
            return upd(carry, k_ref[pl.ds(off, tq), :], v_ref[pl.ds(off, tq), :])

        n_wide = i // (tkb // tq)
        carry = lax.fori_loop(0, n_wide, body_wide, start())
        if tkb > tq:
            carry = lax.fori_loop(n_wide * (tkb // tq), i, body, carry)
        finish(carry, pl.multiple_of(i * tq, tq))


def _fox(q, kp, vp, k, v, tq, tr, pblk, pvalid=None):
    bsz, t, _ = q.shape
    bp, plen, _ = kp.shape
    nq = t // tq
    n_pblk = plen // pblk
    pw = 2 * LANES
    pb = (lambda b: b) if bp == bsz else (lambda b: 0)
    return pl.pallas_call(
        functools.partial(_fox_kernel, tq=tq, tr=tr, tkb=_row_tile(t, FOX_WIDE_KEYS), nq=nq, pblk=pblk, n_pblk=n_pblk,
                          pvalid=pvalid),
        grid=(bsz, FOX_HEADS // 2, nq),
        in_specs=[pl.BlockSpec((None, tq, pw), lambda b, p, i: (b, i, p)),
                  pl.BlockSpec((None, plen, pw), lambda b, p, i: (pb(b), 0, p)),
                  pl.BlockSpec((None, plen, pw), lambda b, p, i: (pb(b), 0, p)),
                  pl.BlockSpec((None, t, pw), lambda b, p, i: (b, 0, p)),
                  pl.BlockSpec((None, t, pw), lambda b, p, i: (b, 0, p))],
        out_specs=pl.BlockSpec((None, tq, LANES), lambda b, p, i: (b, i, p)),
        out_shape=jax.ShapeDtypeStruct((bsz, t, FOX_WIDTH), BF16),
        compiler_params=pltpu.CompilerParams(dimension_semantics=("arbitrary", "arbitrary", "arbitrary"),
                                             vmem_limit_bytes=VMEM_LIMIT),
        name="fox_attention",
    )(q, kp, vp, k, v)


def _mlstm_kernel(xc_ref, mv_ref, mo_ref, gt_ref, c0_ref, n0_ref, m0_ref, wq_ref, wk_ref, gn_ref, sk_ref,
                  hm_ref, cout_ref, nout_ref, mout_ref,
                  c_scr, n_scr, m_scr, *, tc, n_valid, nb, shared_state):
    c = pl.program_id(1)

    @pl.when(c == 0)
    def _():
        for bi in range(nb):
            si = 0 if shared_state else bi
            c_scr[bi] = c0_ref[si]
            n_scr[bi] = n0_ref[si]
            m_scr[bi] = m0_ref[si]

    for bi in range(nb):
        _mlstm_chunk(xc_ref.at[bi], mv_ref.at[bi], mo_ref.at[bi], gt_ref.at[bi], wq_ref, wk_ref,
                     gn_ref, sk_ref, hm_ref.at[bi], c_scr.at[bi], n_scr.at[bi], m_scr.at[bi],
                     tc=tc, n_valid=n_valid)

    @pl.when(c == pl.num_programs(1) - 1)
    def _():
        cout_ref[...] = c_scr[...]
        nout_ref[...] = n_scr[...]
        mout_ref[...] = m_scr[...]


def _mlstm_chunk(xc_ref, mv_ref, mo_ref, gt_ref, wq_ref, wk_ref, gn_ref, sk_ref, hm_ref,
                 c_scr, n_scr, m_scr, *, tc, n_valid):
    gt = gt_ref[...]
    lane = lax.broadcasted_iota(jnp.int32, (tc, LANES), 1)
    if n_valid is not None:
        rowi = lax.broadcasted_iota(jnp.int32, (tc, LANES), 0)
        is_i = (lane >= GATE_ML_I) & (lane < GATE_ML_F)
        gt = jnp.where(rowi < n_valid, gt, jnp.where(is_i, PAD_LOG_GATE, 0.0))
    ri = lax.broadcasted_iota(jnp.int32, (tc, tc), 0)
    ci = lax.broadcasted_iota(jnp.int32, (tc, tc), 1)
    causal = ci <= ri
    b_all = _cumsum_rows(gt, jnp.zeros((1, LANES), F32))[0]
    r_t = (pltpu.roll(gt, GATE_ML_F - GATE_ML_I, 1) - b_all).T
    heads = range(ML_HEADS)
    hsl = [slice(h * ML_HEAD_DIM, (h + 1) * ML_HEAD_DIM) for h in heads]
    b_col = [b_all[:, GATE_ML_F + h:GATE_ML_F + h + 1] for h in heads]
    ig_col = [gt[:, GATE_ML_I + h:GATE_ML_I + h + 1] for h in heads]
    m_prev = [m_scr[h:h + 1, 0:1] for h in heads]
    dmat = [jnp.where(causal, b_col[h] + r_t[GATE_ML_F + h:GATE_ML_F + h + 1, :], NEG_BIG) for h in heads]
    m_inter = [b_col[h] + m_prev[h] for h in heads]
    m_t = [jnp.maximum(m_inter[h], jnp.max(dmat[h], axis=-1, keepdims=True)) for h in heads]
    w_inter = [jnp.exp(m_inter[h] - m_t[h]) for h in heads]

    xhb = [xc_ref[:, hsl[h]] for h in heads]
    q = [jnp.dot(xhb[h], wq_ref[h], preferred_element_type=F32) for h in heads]
    k = [jnp.dot(xhb[h], wk_ref[h], preferred_element_type=F32) * (ML_HEAD_DIM ** -0.5) for h in heads]
    qb = [x.astype(BF16) for x in q]
    kb = [x.astype(BF16) for x in k]
    v = [mv_ref[:, hsl[h]] for h in heads]
    smat = [lax.dot_general(qb[h], kb[h], (((1,), (1,)), ((), ())), preferred_element_type=F32) for h in heads]
    amat = [jnp.exp(dmat[h] - m_t[h]) * smat[h] for h in heads]
    ch = [c_scr[h] for h in heads]
    nh = [n_scr[h:h + 1, :] for h in heads]
    num = [jnp.dot(amat[h].astype(BF16), v[h], preferred_element_type=F32)
           + w_inter[h] * jnp.dot(qb[h], ch[h].astype(BF16), preferred_element_type=F32) for h in heads]
    den = [jnp.sum(amat[h], axis=-1, keepdims=True) + w_inter[h] * jnp.sum(q[h] * nh[h], axis=-1, keepdims=True)
           for h in heads]
    hcell = [num[h] / jnp.maximum(jnp.abs(den[h]), jnp.exp(-m_t[h])) for h in heads]

    b_last = [b_col[h][tc - 1:tc, :] for h in heads]
    g = [b_last[h] - b_col[h] + ig_col[h] for h in heads]
    m_new = [jnp.maximum(b_last[h] + m_prev[h], jnp.max(g[h], axis=0, keepdims=True)) for h in heads]
    decay = [jnp.exp(b_last[h] + m_prev[h] - m_new[h]) for h in heads]
    kw = [k[h] * jnp.exp(g[h] - m_new[h]) for h in heads]
    for h in heads:
        c_scr[h] = decay[h] * ch[h] + lax.dot_general(kw[h].astype(BF16), v[h], (((0,), (0,)), ((), ())),
                                                       preferred_element_type=F32)
        n_scr[h:h + 1, :] = decay[h] * nh[h] + jnp.sum(kw[h], axis=0, keepdims=True)
        m_scr[h:h + 1, :] = jnp.broadcast_to(m_new[h], (1, LANES))
        y = mo_ref[:, hsl[h]].astype(F32) * hcell[h]
        hm_ref[:, hsl[h]] = (_rms(y, gn_ref[:, hsl[h]]) + sk_ref[:, hsl[h]] * xhb[h].astype(F32)).astype(hm_ref.dtype)


def _mlstm(xc, mv, mo, gt, c0, n0, m0, wq, wk, gn, sk, tc, nb=1, n_valid=None):
    bsz, t, _ = xc.shape
    shared = c0.shape[0] != bsz
    sn = 1 if shared else nb
    sb = (lambda b: 0) if shared else (lambda b: b)
    tok = lambda w: pl.BlockSpec((nb, tc, w), lambda b, c: (b, c, 0))
    hd = ML_HEAD_DIM
    st_specs = [pl.BlockSpec((nb, ML_HEADS, hd, hd), lambda b, c: (b, 0, 0, 0)),
                pl.BlockSpec((nb, 8, LANES), lambda b, c: (b, 0, 0)),
                pl.BlockSpec((nb, 8, LANES), lambda b, c: (b, 0, 0))]
    return pl.pallas_call(
        functools.partial(_mlstm_kernel, tc=tc, n_valid=n_valid, nb=nb, shared_state=shared),
        grid=(bsz // nb, t // tc),
        in_specs=[tok(ML_WIDTH), tok(ML_WIDTH), tok(ML_WIDTH), tok(LANES),
                  pl.BlockSpec((sn, ML_HEADS, hd, hd), lambda b, c: (sb(b), 0, 0, 0)),
                  pl.BlockSpec((sn, 8, LANES), lambda b, c: (sb(b), 0, 0)),
                  pl.BlockSpec((sn, 8, LANES), lambda b, c: (sb(b), 0, 0)),
                  _const_spec((ML_HEADS, hd, hd)), _const_spec((ML_HEADS, hd, hd)),
                  _const_spec((1, ML_WIDTH)), _const_spec((1, ML_WIDTH))],
        out_specs=[tok(ML_WIDTH)] + st_specs,
        out_shape=[jax.ShapeDtypeStruct((bsz, t, ML_WIDTH), BF16),
                   jax.ShapeDtypeStruct((bsz, ML_HEADS, hd, hd), F32),
                   jax.ShapeDtypeStruct((bsz, 8, LANES), F32),
                   jax.ShapeDtypeStruct((bsz, 8, LANES), F32)],
        scratch_shapes=[pltpu.VMEM((nb, ML_HEADS, hd, hd), F32),
                        pltpu.VMEM((nb, 8, LANES), F32), pltpu.VMEM((nb, 8, LANES), F32)],
        compiler_params=pltpu.CompilerParams(dimension_semantics=("arbitrary", "arbitrary")),
        name="mlstm",
    )(xc, mv, mo, gt, c0, n0, m0, wq, wk, gn, sk)


def _post_kernel(a_ref, hm_ref, ga_ref, gb_ref, x_ref, wfp_ref, wmp_ref, wo_ref, wgu_ref, wd_ref,
                 g1_ref, g2_ref, g3_ref, y_ref, *, d_ff, ff_cuts):
    ya = jnp.dot(a_ref[...], wfp_ref[...], preferred_element_type=F32)
    yb = jnp.dot(hm_ref[...], wmp_ref[...], preferred_element_type=F32)
    merged = ga_ref[...].astype(F32) * ya + gb_ref[...].astype(F32) * yb
    mix = jnp.dot(merged.astype(BF16), wo_ref[...], preferred_element_type=F32)
    x1 = x_ref[...] + _rms(mix, g1_ref[...])
    h2 = _rms(x1, g2_ref[...]).astype(BF16)
    f = None
    for lo, hi in zip(ff_cuts[:-1], ff_cuts[1:]):
        g = jnp.dot(h2, wgu_ref[:, lo:hi], preferred_element_type=F32)
        u = jnp.dot(h2, wgu_ref[:, d_ff + lo:d_ff + hi], preferred_element_type=F32)
        t = (g * jax.nn.sigmoid(g) * u).astype(BF16)
        fj = jnp.dot(t, wd_ref[lo:hi, :], preferred_element_type=F32)
        f = fj if f is None else f + fj
    y_ref[...] = x1 + _rms(f, g3_ref[...])


def _post(a, hm, ga, gb, x, wfp, wmp, wo, wgu, wd, g1, g2, g3, tm):
    n, d = x.shape
    d_ff = wd.shape[0]
    nmt = d_ff // MXU_TILE
    ff_cuts = tuple(MXU_TILE * ((nmt * j) // FF_CHUNKS) for j in range(FF_CHUNKS)) + (d_ff,) \
        if d_ff % MXU_TILE == 0 else (0, d_ff)
    row = lambda c: pl.BlockSpec((tm, c), lambda i: (i, 0))
    return pl.pallas_call(
        functools.partial(_post_kernel, d_ff=d_ff, ff_cuts=ff_cuts),
        grid=(n // tm,),
        in_specs=[row(FOX_WIDTH), row(ML_WIDTH), row(d), row(d), row(d),
                  _const_spec(wfp.shape), _const_spec(wmp.shape), _const_spec(wo.shape),
                  _const_spec(wgu.shape), _const_spec(wd.shape),
                  _const_spec((1, d)), _const_spec((1, d)), _const_spec((1, d))],
        out_specs=row(d),
        out_shape=jax.ShapeDtypeStruct((n, d), F32),
        compiler_params=pltpu.CompilerParams(dimension_semantics=("arbitrary",), vmem_limit_bytes=VMEM_LIMIT),
        name="post_mix_ffn",
    )(a, hm, ga, gb, x, wfp, wmp, wo, wgu, wd, g1, g2, g3)


def _pad_rows(a, rows, axis):
    pad = [(0, 0)] * a.ndim
    pad[axis] = (0, rows - a.shape[axis])
    return jnp.pad(a, pad)


def _row_tile(n, pref):
    t = min(pref, n)
    while n % t:
        t //= 2
    return t


def kernel(x_prompt, x_sample, cache_fox_k, cache_fox_v, cache_fox_logf, state_mlstm_C, state_mlstm_n,
           state_mlstm_m, state_mlstm_conv, meta_tokens, w_in, b_in, g_fq, g_fk, w_conv, b_conv, w_mq, w_mk,
           g_mnorm, skip_m, w_fox_proj, w_ml_proj, w_out, g_pre_mix, g_post_mix, g_pre_ffn, g_post_ffn,
           w_gate_up, w_down):
    assert w_in.shape[0] == 1, "single-layer trunk"
    bsz, seq, d = x_prompt.shape
    dbsz, dseq, _ = x_sample.shape
    past = cache_fox_k.shape[2]
    fw, mw = FOX_WIDTH, ML_WIDTH

    w, b = w_in[0], b_in[0]
    o_ff = 3 * fw
    o_mx = o_ff + FOX_HEADS
    o_mi = o_mx + 3 * mw
    o_mf = o_mi + ML_HEADS
    o_ga = o_mf + ML_HEADS

    def relayout(a):
        gates = jnp.concatenate([a[..., o_ff:o_mx], a[..., o_mi:o_mf], a[..., o_mf:o_ga], a[..., o_mf:o_ga]], axis=-1)
        gates = _pad_rows(gates, LANES, gates.ndim - 1)
        return jnp.concatenate([a[..., :o_ff], a[..., o_mx:o_mi], a[..., o_ga:], gates], axis=-1)

    w_p = relayout(w).astype(BF16)
    b_p = relayout(b)[None, :]
    hid = jnp.arange(fw) // FOX_HEAD_DIM
    hsum = (hid[:, None] == hid[None, :]).astype(BF16)
    gq = g_fq[0].reshape(1, fw)
    gk = g_fk[0].reshape(1, fw)
    gpre = g_pre_mix[0][None, :]
    wc, bc = w_conv[0], b_conv[0][None, :]
    inproj = lambda x, c0, cp, tm, nt, **kw: _inproj(x, gpre, w_p, b_p, gq, gk, hsum, c0, cp, wc, bc, tm, nt, **kw)

    wq, wk = w_mq[0].astype(BF16), w_mk[0].astype(BF16)
    gn, sk = g_mnorm[0].reshape(1, mw), skip_m[0][None, :]
    mlstm = lambda *a, **kw: _mlstm(*a, wq, wk, gn, sk, **kw)

    def conv_hist(rows):
        return jnp.pad(rows, ((0, 0), (8 - (CONV_WIDTH - 1), 0), (0, 0)))

    def m_rows(m):
        return jnp.broadcast_to(_pad_rows(m, 8, 1)[:, :, None], (m.shape[0], 8, LANES))

    mt = 64
    xm = _pad_rows(meta_tokens.astype(F32), mt, 0)
    (_, kf_m, ka_m, vf_m, va_m, xc_m, mv_m, mo_m, _, _, gt_m, fc_m, ct_m) = inproj(
        xm, jnp.zeros((1, 1, LANES), F32), jnp.zeros((1, 8, mw), F32), mt, 1, conv_valid=N_META)
    zc = jnp.zeros((1, ML_HEADS, ML_HEAD_DIM, ML_HEAD_DIM), F32)
    z8 = jnp.zeros((1, 8, LANES), F32)
    _, c_m, n_m, m_m = mlstm(xc_m[None], mv_m[None], mo_m[None], gt_m[None], zc, z8, z8, tc=mt, n_valid=N_META)

    xp = x_prompt.reshape(bsz * seq, d)
    tm = _row_tile(seq, 512)
    lanes_t = lambda rows: _pad_rows(rows[:N_META].T, LANES, 1)
    (qa_p, kt_p, ka_p, vt_p, va_p, xc_p, mv_p, mo_p, ga_p, gb_p, gt_p, _, ct_p, lt_p) = inproj(
        xp, fc_m[None, N_META - 1:N_META], ct_m, tm, seq // tm,
        meta_t=(lanes_t(kf_m), lanes_t(vf_m), lanes_t(gt_m[:, :FOX_HEADS])))
    r3 = lambda a: a.reshape(bsz, seq, a.shape[-1])
    hm_p, c_p, n_p, m_p = mlstm(r3(xc_p), r3(mv_p), r3(mo_p), r3(gt_p), c_m, n_m, m_m,
                                tc=_row_tile(seq, 256), nb=_row_tile(bsz, 2))
    a_p = _fox(r3(qa_p), _pad_rows(ka_m, LANES, 0)[None], _pad_rows(va_m, LANES, 0)[None], r3(ka_p), r3(va_p),
               tq=_row_tile(seq, 1024), tr=_row_tile(seq, 512), pblk=LANES, pvalid=N_META)

    kc, vc, f_c = _cacheprep(cache_fox_k[0].reshape(dbsz, past, fw), cache_fox_v[0].reshape(dbsz, past, fw),
                             _pad_rows(cache_fox_logf[0].astype(F32), LANES, 2), _row_tile(past, 512))
    xs = x_sample.reshape(dbsz * dseq, d)
    (qa_s, kf_s, ka_s, vf_s, va_s, xc_s, mv_s, mo_s, ga_s, gb_s, gt_s, _, ct_s) = inproj(
        xs, f_c, conv_hist(state_mlstm_conv[0]), dseq, 1)
    r3s = lambda a: a.reshape(dbsz, dseq, a.shape[-1])
    hm_s, c_s, n_s, m_s = mlstm(r3s(xc_s), r3s(mv_s), r3s(mo_s), r3s(gt_s), state_mlstm_C[0], _pad_rows(state_mlstm_n[0], 8, 1), m_rows(state_mlstm_m[0]),
                                tc=dseq, nb=_row_tile(dbsz, 4))
    a_s = _fox(r3s(qa_s), kc, vc, r3s(ka_s), r3s(va_s), tq=dseq, tr=dseq, pblk=_row_tile(past, 512))

    wfp, wmp, wo = w_fox_proj[0].astype(BF16), w_ml_proj[0].astype(BF16), w_out[0].astype(BF16)
    wgu, wd = w_gate_up[0].astype(BF16), w_down[0].astype(BF16)
    g1, g2, g3 = g_post_mix[0][None, :], g_pre_ffn[0][None, :], g_post_ffn[0][None, :]
    post = lambda a, hm, ga, gb, x, tm: _post(a, hm, ga, gb, x, wfp, wmp, wo, wgu, wd, g1, g2, g3, tm)
    y_p = post(a_p.reshape(bsz * seq, fw), hm_p.reshape(bsz * seq, mw), ga_p, gb_p, xp, _row_tile(bsz * seq, 512))
    y_s = post(a_s.reshape(dbsz * dseq, fw), hm_s.reshape(dbsz * dseq, mw), ga_s, gb_s, xs,
               _row_tile(dbsz * dseq, 512))

    def tokens_major(a):
        return a.reshape(bsz, FOX_HEADS, FOX_HEAD_DIM, a.shape[-1]).transpose(0, 3, 1, 2)[None]

    heads = lambda a: a.reshape(a.shape[:-1] + (FOX_HEADS, FOX_HEAD_DIM))
    tail = CONV_WIDTH - 1
    return (y_p.reshape(bsz, seq, d), y_s.reshape(dbsz, dseq, d),
            tokens_major(kt_p), tokens_major(vt_p), lt_p.transpose(0, 2, 1)[None],
            c_p[None], n_p[None, :, :ML_HEADS], m_p[None, :, :ML_HEADS, 0], ct_p[None, :, 8 - tail:],
            heads(r3s(kf_s))[None], heads(r3s(vf_s))[None], r3s(gt_s)[None, ..., :FOX_HEADS],
            c_s[None], n_s[None, :, :ML_HEADS], m_s[None, :, :ML_HEADS, 0], ct_s[None, :, 8 - tail:])
```

```python
import functools

import jax
import jax.numpy as jnp
from jax import lax
from jax.experimental import pallas as pl
from jax.experimental.pallas import tpu as pltpu

F32 = jnp.float32
BF16 = jnp.bfloat16

N_META = 16
FOX_HEADS = 8
FOX_HEAD_DIM = 64
FOX_WIDTH = FOX_HEADS * FOX_HEAD_DIM
ML_HEADS = 4
ML_HEAD_DIM = 128
ML_WIDTH = ML_HEADS * ML_HEAD_DIM
CONV_WIDTH = 4
RMS_EPS = 1e-6
PAD_LOG_GATE = -1e30
NEG_BIG = -1e30

LANES = 128
GATE_FOX_F = 0
GATE_ML_I = 8
GATE_ML_F = 12
GATE_ML_F2 = 16

AUG_STEP = FOX_HEADS
AUG_QF = FOX_HEAD_DIM
AUG_KF = FOX_HEAD_DIM + 24

VMEM_LIMIT = 56 * 1024 * 1024
FOX_WIDE_KEYS = 1024
FOX_MAX_STATIC_BLOCKS = 4
MXU_TILE = 256
FF_CHUNKS = 3


def _const_spec(shape):
    nd = len(shape)
    return pl.BlockSpec(shape, lambda *_: (0,) * nd, pipeline_mode=pl.Buffered(1))


def _rms(x, g):
    return x * lax.rsqrt(jnp.mean(x * x, axis=-1, keepdims=True) + RMS_EPS) * g


def _split3(x):
    hi = x.astype(BF16).astype(F32)
    r = x - hi
    mid = r.astype(BF16).astype(F32)
    lo = (r - mid).astype(BF16).astype(F32)
    return hi, mid, lo


def _cumsum_rows(lg, cur):
    tm = lg.shape[0]
    blk = min(tm, LANES)
    r = lax.broadcasted_iota(jnp.int32, (blk, blk), 0)
    c = lax.broadcasted_iota(jnp.int32, (blk, blk), 1)
    tril = (c <= r).astype(BF16)
    outs = []
    for s in range(tm // blk):
        hi, mid, lo = _split3(lg[s * blk:(s + 1) * blk])
        f = (jnp.dot(tril, lo.astype(BF16), preferred_element_type=F32)
             + jnp.dot(tril, mid.astype(BF16), preferred_element_type=F32)
             + jnp.dot(tril, hi.astype(BF16), preferred_element_type=F32)) + cur
        outs.append(f)
        cur = f[blk - 1:blk]
    return (outs[0] if len(outs) == 1 else jnp.concatenate(outs, axis=0)), cur


def _store_attention_operands(qn, kn, v, fcol, q_ref, k_ref, v_ref, fillers=()):
    tm = fcol.shape[0]
    lane = lax.broadcasted_iota(jnp.int32, (tm, LANES), 1)
    data = lane < FOX_HEAD_DIM
    hi, mid, lo = _split3(fcol)
    comb = jnp.where(lane < AUG_STEP, hi, jnp.where(lane < 2 * AUG_STEP, pltpu.roll(mid, AUG_STEP, 1),
                                                     pltpu.roll(lo, 2 * AUG_STEP, 1)))
    ncomb = -comb
    is_qf = (lane == AUG_QF) | (lane == AUG_QF + AUG_STEP) | (lane == AUG_QF + 2 * AUG_STEP)
    is_kf = (lane == AUG_KF) | (lane == AUG_KF + AUG_STEP) | (lane == AUG_KF + 2 * AUG_STEP)
    base_q = jnp.where(is_kf, 1.0, 0.0)
    base_k = jnp.where(is_qf, 1.0, 0.0)
    base_v = jnp.where(lane == FOX_HEAD_DIM, 1.0, 0.0)
    fillers = list(fillers)
    for h in range(FOX_HEADS):
        t, odd = divmod(h, 2)

        def head_tile(z):
            x = z[:, t * LANES:(t + 1) * LANES]
            return pltpu.roll(x, FOX_HEAD_DIM, 1) if odd else x

        hs = slice(h * LANES, (h + 1) * LANES)
        if q_ref is not None:
            aug = jnp.where(is_qf, pltpu.roll(comb, AUG_QF - h, 1), base_q)
            q_ref[:, hs] = jnp.where(data, head_tile(qn), aug).astype(BF16)
        aug = jnp.where(is_kf, pltpu.roll(ncomb, AUG_KF - h, 1), base_k)
        k_ref[:, hs] = jnp.where(data, head_tile(kn), aug).astype(BF16)
        v_ref[:, hs] = jnp.where(data, head_tile(v), base_v).astype(BF16)
        if fillers:
            fillers.pop(0)()
    for f in fillers:
        f()


def _inproj_kernel(*refs, d_model, nt, kv_t, conv_valid):
    (x_ref, gpre_ref, w_ref, b_ref, gq_ref, gk_ref, hs_ref, c0_ref, cp_ref, wc_ref, bc_ref), rest = refs[:11], refs[11:]
    if kv_t:
        kmeta_ref, vmeta_ref, lmeta_ref = rest[:3]
        (q_ref, kf_ref, kb_ref, vf_ref, vb_ref, xc_ref, mv_ref, mo_ref, ga_ref, gb_ref, gt_ref, fc_ref, ct_ref,
         lt_ref, carry, xbuf, kcar, vcar, lcar) = rest[3:]
    else:
        (q_ref, kf_ref, kb_ref, vf_ref, vb_ref, xc_ref, mv_ref, mo_ref, ga_ref, gb_ref, gt_ref, fc_ref, ct_ref,
         carry, xbuf) = rest
    fw, mw = FOX_WIDTH, ML_WIDTH
    offs = [0, fw, 2 * fw, 3 * fw, 3 * fw + mw, 3 * fw + 2 * mw, 3 * fw + 3 * mw,
            3 * fw + 3 * mw + d_model, 3 * fw + 3 * mw + 2 * d_model, 3 * fw + 3 * mw + 2 * d_model + LANES]
    r = pl.program_id(1)

    @pl.when(r == 0)
    def _():
        carry[...] = c0_ref[...]
        xbuf[0:8, :] = cp_ref[...]
        if kv_t:
            kcar[...] = kmeta_ref[...]
            vcar[...] = vmeta_ref[...]
            lcar[...] = lmeta_ref[...]

    def store_shifted(xt, car, out_ref):
        sh = pltpu.roll(xt, N_META, 1)
        lane = lax.broadcasted_iota(jnp.int32, (xt.shape[0], LANES), 1)
        out_ref[:, :LANES] = jnp.where(lane < N_META, car[...], sh[:, :LANES])
        out_ref[:, LANES:] = sh[:, LANES:]
        car[...] = sh[:, :LANES]

    @pl.when(r < nt)
    def _():
        h = _rms(x_ref[...], gpre_ref[...]).astype(BF16)

        def seg(i):
            a, b = offs[i], offs[i + 1]
            return jnp.dot(h, w_ref[:, a:b], preferred_element_type=F32) + b_ref[:, a:b]

        def headnorm(z, g):
            ms = jnp.dot((z * z).astype(BF16), hs_ref[...], preferred_element_type=F32) * (1.0 / FOX_HEAD_DIM)
            return z * lax.rsqrt(ms + RMS_EPS) * g

        g = seg(8)
        ga_ref[...] = seg(6).astype(BF16)
        lane = lax.broadcasted_iota(jnp.int32, g.shape, 1)
        raw = (lane >= GATE_ML_I) & (lane < GATE_ML_F)
        gt = jnp.where(raw, g, jax.nn.log_sigmoid(g))
        gt_ref[...] = gt
        fcol, cur = _cumsum_rows(gt, carry[...])
        carry[...] = cur
        fc_ref[...] = fcol

        v = seg(2)
        q = headnorm(seg(0), gq_ref[...]) * (FOX_HEAD_DIM ** -0.5)
        k = headnorm(seg(1), gk_ref[...])
        def store_kv():
            if kv_t:
                store_shifted(k.T, kcar, kf_ref)
                store_shifted(v.T, vcar, vf_ref)
                store_shifted(gt.T[:FOX_HEADS], lcar, lt_ref)
            else:
                kf_ref[...] = k
                vf_ref[...] = v

        def store_xc():
            tm = x_ref.shape[0]
            xbuf[8:8 + tm, :] = seg(3)
            xc = bc_ref[...]
            for i in range(CONV_WIDTH):
                xc = xc + xbuf[5 + i:5 + i + tm, :] * wc_ref[i:i + 1, :]
            xc_ref[...] = (xc * jax.nn.sigmoid(xc)).astype(BF16)
            ct_ref[...] = xbuf[conv_valid:conv_valid + 8, :]
            xbuf[0:8, :] = xbuf[tm:tm + 8, :]

        def store_mv():
            mv_ref[...] = seg(4).astype(BF16)

        def store_mo():
            mo_ref[...] = jax.nn.sigmoid(seg(5)).astype(BF16)

        def store_gb():
            gb_ref[...] = seg(7).astype(BF16)

        _store_attention_operands(q, k, v, fcol, q_ref, kb_ref, vb_ref,
                                  fillers=(store_xc, store_mv, store_mo, store_gb, store_kv))

    if kv_t:
        @pl.when(r == nt)
        def _():
            kf_ref[:, :LANES] = kcar[...]
            vf_ref[:, :LANES] = vcar[...]
            lt_ref[:, :LANES] = lcar[...]


def _inproj(x, gpre, w, b, gq, gk, hs, c0, cp, wc, bc, tm, nt, meta_t=None, conv_valid=None):
    n, d = x.shape
    nb = n // (nt * tm)
    dinp = w.shape[1]
    fw, mw = FOX_WIDTH, ML_WIDTH
    aw = FOX_HEADS * LANES
    kv_t = meta_t is not None
    tile = lambda b, r: b * nt + jnp.minimum(r, nt - 1)
    row = lambda c: pl.BlockSpec((tm, c), lambda b, r: (tile(b, r), 0))
    cb = (lambda b: b) if c0.shape[0] > 1 else (lambda b: 0)
    pb = (lambda b: b) if cp.shape[0] > 1 else (lambda b: 0)
    lp = N_META + nt * tm
    tmin = lambda c: pl.BlockSpec((None, c, tm), lambda b, r: (b, 0, r))
    flat = lambda c, t: (row(c), jax.ShapeDtypeStruct((n, c), t))
    tok_minor = lambda c: (tmin(c), jax.ShapeDtypeStruct((nb, c, lp), F32))
    outs = [flat(aw, BF16), tok_minor(fw) if kv_t else flat(fw, F32), flat(aw, BF16),
            tok_minor(fw) if kv_t else flat(fw, F32), flat(aw, BF16), flat(mw, BF16), flat(mw, BF16), flat(mw, BF16),
            flat(d, BF16), flat(d, BF16), flat(LANES, F32), flat(LANES, F32),
            (pl.BlockSpec((None, 8, mw), lambda b, r: (b, 0, 0)), jax.ShapeDtypeStruct((nb, 8, mw), F32))]
    in_specs = [row(d), _const_spec((1, d)), _const_spec((d, dinp)), _const_spec((1, dinp)),
                _const_spec((1, fw)), _const_spec((1, fw)), _const_spec((fw, fw)),
                pl.BlockSpec((None, 1, LANES), lambda b, r: (cb(b), 0, 0)),
                pl.BlockSpec((None, 8, mw), lambda b, r: (pb(b), 0, 0)),
                _const_spec((CONV_WIDTH, mw)), _const_spec((1, mw))]
    scratch = [pltpu.VMEM((1, LANES), F32), pltpu.VMEM((8 + tm, mw), F32)]
    args = [x, gpre, w, b, gq, gk, hs, c0, cp, wc, bc]
    if kv_t:
        outs.append(tok_minor(FOX_HEADS))
        in_specs += [_const_spec((fw, LANES)), _const_spec((fw, LANES)), _const_spec((FOX_HEADS, LANES))]
        scratch += [pltpu.VMEM((fw, LANES), F32), pltpu.VMEM((fw, LANES), F32), pltpu.VMEM((FOX_HEADS, LANES), F32)]
        args += list(meta_t)
    return pl.pallas_call(
        functools.partial(_inproj_kernel, d_model=d, nt=nt, kv_t=kv_t,
                          conv_valid=tm if conv_valid is None else conv_valid),
        grid=(nb, nt + (1 if kv_t else 0)),
        in_specs=in_specs,
        out_specs=[o[0] for o in outs],
        out_shape=[o[1] for o in outs],
        scratch_shapes=scratch,
        compiler_params=pltpu.CompilerParams(dimension_semantics=("arbitrary", "arbitrary"),
                                             vmem_limit_bytes=VMEM_LIMIT),
        name="inproj",
    )(*args)


def _cacheprep_kernel(k_ref, v_ref, lf_ref, ka_ref, va_ref, fl_ref, carry):
    @pl.when(pl.program_id(1) == 0)
    def _():
        carry[...] = jnp.zeros_like(carry)

    fcol, cur = _cumsum_rows(lf_ref[...], carry[...])
    carry[...] = cur
    fl_ref[...] = cur
    _store_attention_operands(None, k_ref[...], v_ref[...], fcol, None, ka_ref, va_ref)


def _cacheprep(ck, cv, clf, tp):
    bsz, plen, _ = ck.shape
    aw = FOX_HEADS * LANES
    return pl.pallas_call(
        _cacheprep_kernel,
        grid=(bsz, plen // tp),
        in_specs=[pl.BlockSpec((None, tp, FOX_WIDTH), lambda b, i: (b, i, 0)),
                  pl.BlockSpec((None, tp, FOX_WIDTH), lambda b, i: (b, i, 0)),
                  pl.BlockSpec((None, tp, LANES), lambda b, i: (b, i, 0))],
        out_specs=[pl.BlockSpec((None, tp, aw), lambda b, i: (b, i, 0)),
                   pl.BlockSpec((None, tp, aw), lambda b, i: (b, i, 0)),
                   pl.BlockSpec((None, 1, LANES), lambda b, i: (b, 0, 0))],
        out_shape=[jax.ShapeDtypeStruct((bsz, plen, aw), BF16), jax.ShapeDtypeStruct((bsz, plen, aw), BF16),
                   jax.ShapeDtypeStruct((bsz, 1, LANES), F32)],
        scratch_shapes=[pltpu.VMEM((1, LANES), F32)],
        compiler_params=pltpu.CompilerParams(dimension_semantics=("arbitrary", "arbitrary")),
        name="cache_prep",
    )(ck, cv, clf)


def _fox_kernel(q_ref, kp_ref, vp_ref, k_ref, v_ref, o_ref, *, tq, tr, tkb, nq, pblk, n_pblk, pvalid):
    i = pl.program_id(2)
    nr = tq // tr
    chains = [(hh, r) for hh in range(2) for r in range(nr)]
    qs = [q_ref[r * tr:(r + 1) * tr, hh * LANES:(hh + 1) * LANES] for hh, r in chains]

    def upd1(n, state, kb, vb, mask=None):
        m, acc = state
        hh = chains[n][0]
        hs = slice(hh * LANES, (hh + 1) * LANES)
        s = lax.dot_general(qs[n], kb[:, hs], (((1,), (1,)), ((), ())), preferred_element_type=F32)
        if mask is not None:
            cut = s.shape[1] - mask.shape[1]
            tail = jnp.where(mask, s[:, cut:], NEG_BIG)
            s = tail if cut == 0 else jnp.concatenate([s[:, :cut], tail], axis=1)
        m_new = jnp.maximum(m, jnp.max(s, axis=-1, keepdims=True))
        alpha = jnp.exp(m - m_new)
        pm = jnp.exp(s - m_new)
        acc = alpha * acc + jnp.dot(pm.astype(BF16), vb[:, hs], preferred_element_type=F32)
        return m_new, acc

    def upd(carry, kb, vb):
        return tuple(upd1(n, carry[n], kb, vb) for n in range(len(chains)))

    def start():
        carry = tuple((jnp.full((tr, 1), NEG_BIG, F32), jnp.zeros((tr, LANES), F32)) for _ in chains)
        if pvalid is None:
            for jb in range(n_pblk):
                sl = slice(jb * pblk, (jb + 1) * pblk)
                carry = upd(carry, kp_ref[sl, :], vp_ref[sl, :])
        return carry

    def finish(carry, off):
        tail_w = tr + (LANES if pvalid is not None else 0)
        rj = lax.broadcasted_iota(jnp.int32, (tr, tail_w), 0)
        cj = lax.broadcasted_iota(jnp.int32, (tr, tail_w), 1)
        mask = cj <= rj
        if pvalid is not None:
            mask = (mask | (cj >= tr)) & (cj < tr + pvalid)
        carry = list(carry)
        for r in range(nr):
            width = (r + 1) * tr
            kb = k_ref[pl.ds(off, width), :]
            vb = v_ref[pl.ds(off, width), :]
            if pvalid is not None:
                kb = jnp.concatenate([kb, kp_ref[...]], axis=0)
                vb = jnp.concatenate([vb, vp_ref[...]], axis=0)
            for n, (hh, rr) in enumerate(chains):
                if rr == r:
                    carry[n] = upd1(n, carry[n], kb, vb, mask=mask)
        lane = lax.broadcasted_iota(jnp.int32, (tr, LANES), 1)
        outs = []
        for n in range(len(chains)):
            acc = carry[n][1]
            l = jnp.sum(jnp.where(lane == FOX_HEAD_DIM, acc, 0.0), axis=-1, keepdims=True)
            outs.append(acc / l)
        for r in range(nr):
            o_ref[r * tr:(r + 1) * tr, :] = jnp.where(lane < FOX_HEAD_DIM, outs[r],
                                                      pltpu.roll(outs[nr + r], FOX_HEAD_DIM, 1)).astype(o_ref.dtype)

    if tq % tkb == 0 and nq <= FOX_MAX_STATIC_BLOCKS:
        for qi in range(nq):
            @pl.when(i == qi)
            def _():
                carry = start()
                for j in range(qi * tq // tkb):
                    carry = upd(carry, k_ref[j * tkb:(j + 1) * tkb, :], v_ref[j * tkb:(j + 1) * tkb, :])
                finish(carry, qi * tq)
    else:
        tkw = max(tkb, tq)

        def body_wide(j, carry):
            off = pl.multiple_of(j * tkw, tkw)
            return upd(carry, k_ref[pl.ds(off, tkw), :], v_ref[pl.ds(off, tkw), :])

        def body(j, carry):
            off = pl.multiple_of(j * tq, tq)
            return upd(carry, k_ref[pl.ds(off, tq), :], v_ref[pl.ds(off, tq), :])

        n_wide = i // (tkw // tq)
        carry = lax.fori_loop(0, n_wide, body_wide, start())
        if tkw > tq:
            carry = lax.fori_loop(n_wide * (tkw // tq), i, body, carry)
        finish(carry, pl.multiple_of(i * tq, tq))


def _fox(q, kp, vp, k, v, tq, tr, pblk, pvalid=None):
    bsz, t, _ = q.shape
    bp, plen, _ = kp.shape
    nq = t // tq
    n_pblk = plen // pblk
    pw = 2 * LANES
    pb = (lambda b: b) if bp == bsz else (lambda b: 0)
    return pl.pallas_call(
        functools.partial(_fox_kernel, tq=tq, tr=tr, tkb=_row_tile(t, FOX_WIDE_KEYS), nq=nq, pblk=pblk, n_pblk=n_pblk,
                          pvalid=pvalid),
        grid=(bsz, FOX_HEADS // 2, nq),
        in_specs=[pl.BlockSpec((None, tq, pw), lambda b, p, i: (b, i, p)),
                  pl.BlockSpec((None, plen, pw), lambda b, p, i: (pb(b), 0, p)),
                  pl.BlockSpec((None, plen, pw), lambda b, p, i: (pb(b), 0, p)),
                  pl.BlockSpec((None, t, pw), lambda b, p, i: (b, 0, p)),
                  pl.BlockSpec((None, t, pw), lambda b, p, i: (b, 0, p))],
        out_specs=pl.BlockSpec((None, tq, LANES), lambda b, p, i: (b, i, p)),
        out_shape=jax.ShapeDtypeStruct((bsz, t, FOX_WIDTH), BF16),
        compiler_params=pltpu.CompilerParams(dimension_semantics=("arbitrary", "arbitrary", "arbitrary"),
                                             vmem_limit_bytes=VMEM_LIMIT),
        name="fox_attention",
    )(q, kp, vp, k, v)


def _mlstm_kernel(xc_ref, mv_ref, mo_ref, gt_ref, c0_ref, n0_ref, m0_ref, wq_ref, wk_ref, gn_ref, sk_ref,
                  hm_ref, cout_ref, nout_ref, mout_ref,
                  c_scr, n_scr, m_scr, *, tc, n_valid, nb, shared_state):
    c = pl.program_id(1)

    @pl.when(c == 0)
    def _():
        for bi in range(nb):
            si = 0 if shared_state else bi
            c_scr[bi] = c0_ref[si]
            n_scr[bi] = n0_ref[si]
            m_scr[bi] = m0_ref[si]

    for bi in range(nb):
        _mlstm_chunk(xc_ref.at[bi], mv_ref.at[bi], mo_ref.at[bi], gt_ref.at[bi], wq_ref, wk_ref,
                     gn_ref, sk_ref, hm_ref.at[bi], c_scr.at[bi], n_scr.at[bi], m_scr.at[bi],
                     tc=tc, n_valid=n_valid)

    @pl.when(c == pl.num_programs(1) - 1)
    def _():
        cout_ref[...] = c_scr[...]
        nout_ref[...] = n_scr[...]
        mout_ref[...] = m_scr[...]


def _mlstm_chunk(xc_ref, mv_ref, mo_ref, gt_ref, wq_ref, wk_ref, gn_ref, sk_ref, hm_ref,
                 c_scr, n_scr, m_scr, *, tc, n_valid):
    gt = gt_ref[...]
    lane = lax.broadcasted_iota(jnp.int32, (tc, LANES), 1)
    if n_valid is not None:
        rowi = lax.broadcasted_iota(jnp.int32, (tc, LANES), 0)
        is_i = (lane >= GATE_ML_I) & (lane < GATE_ML_F)
        gt = jnp.where(rowi < n_valid, gt, jnp.where(is_i, PAD_LOG_GATE, 0.0))
    ri = lax.broadcasted_iota(jnp.int32, (tc, tc), 0)
    ci = lax.broadcasted_iota(jnp.int32, (tc, tc), 1)
    causal = ci <= ri
    b_all = _cumsum_rows(gt, jnp.zeros((1, LANES), F32))[0]
    r_t = (pltpu.roll(gt, GATE_ML_F - GATE_ML_I, 1) - b_all).T
    heads = range(ML_HEADS)
    hsl = [slice(h * ML_HEAD_DIM, (h + 1) * ML_HEAD_DIM) for h in heads]
    b_col = [b_all[:, GATE_ML_F + h:GATE_ML_F + h + 1] for h in heads]
    ig_col = [gt[:, GATE_ML_I + h:GATE_ML_I + h + 1] for h in heads]
    m_prev = [m_scr[h:h + 1, 0:1] for h in heads]
    dmat = [jnp.where(causal, b_col[h] + r_t[GATE_ML_F + h:GATE_ML_F + h + 1, :], NEG_BIG) for h in heads]
    m_inter = [b_col[h] + m_prev[h] for h in heads]
    m_t = [jnp.maximum(m_inter[h], jnp.max(dmat[h], axis=-1, keepdims=True)) for h in heads]
    w_inter = [jnp.exp(m_inter[h] - m_t[h]) for h in heads]

    xhb = [xc_ref[:, hsl[h]] for h in heads]
    q = [jnp.dot(xhb[h], wq_ref[h], preferred_element_type=F32) for h in heads]
    k = [jnp.dot(xhb[h], wk_ref[h], preferred_element_type=F32) * (ML_HEAD_DIM ** -0.5) for h in heads]
    qb = [x.astype(BF16) for x in q]
    kb = [x.astype(BF16) for x in k]
    v = [mv_ref[:, hsl[h]] for h in heads]
    smat = [lax.dot_general(qb[h], kb[h], (((1,), (1,)), ((), ())), preferred_element_type=F32) for h in heads]
    amat = [jnp.exp(dmat[h] - m_t[h]) * smat[h] for h in heads]
    ch = [c_scr[h] for h in heads]
    nh = [n_scr[h:h + 1, :] for h in heads]
    num = [jnp.dot(amat[h].astype(BF16), v[h], preferred_element_type=F32)
           + w_inter[h] * jnp.dot(qb[h], ch[h].astype(BF16), preferred_element_type=F32) for h in heads]
    den = [jnp.sum(amat[h], axis=-1, keepdims=True) + w_inter[h] * jnp.sum(q[h] * nh[h], axis=-1, keepdims=True)
           for h in heads]
    hcell = [num[h] / jnp.maximum(jnp.abs(den[h]), jnp.exp(-m_t[h])) for h in heads]

    b_last = [b_col[h][tc - 1:tc, :] for h in heads]
    g = [b_last[h] - b_col[h] + ig_col[h] for h in heads]
    m_new = [jnp.maximum(b_last[h] + m_prev[h], jnp.max(g[h], axis=0, keepdims=True)) for h in heads]
    decay = [jnp.exp(b_last[h] + m_prev[h] - m_new[h]) for h in heads]
    kw = [k[h] * jnp.exp(g[h] - m_new[h]) for h in heads]
    for h in heads:
        c_scr[h] = decay[h] * ch[h] + lax.dot_general(kw[h].astype(BF16), v[h], (((0,), (0,)), ((), ())),
                                                       preferred_element_type=F32)
        n_scr[h:h + 1, :] = decay[h] * nh[h] + jnp.sum(kw[h], axis=0, keepdims=True)
        m_scr[h:h + 1, :] = jnp.broadcast_to(m_new[h], (1, LANES))
        y = mo_ref[:, hsl[h]].astype(F32) * hcell[h]
        hm_ref[:, hsl[h]] = (_rms(y, gn_ref[:, hsl[h]]) + sk_ref[:, hsl[h]] * xhb[h].astype(F32)).astype(hm_ref.dtype)


def _mlstm(xc, mv, mo, gt, c0, n0, m0, wq, wk, gn, sk, tc, nb=1, n_valid=None):
    bsz, t, _ = xc.shape
    shared = c0.shape[0] != bsz
    sn = 1 if shared else nb
    sb = (lambda b: 0) if shared else (lambda b: b)
    tok = lambda w: pl.BlockSpec((nb, tc, w), lambda b, c: (b, c, 0))
    hd = ML_HEAD_DIM
    st_specs = [pl.BlockSpec((nb, ML_HEADS, hd, hd), lambda b, c: (b, 0, 0, 0)),
                pl.BlockSpec((nb, 8, LANES), lambda b, c: (b, 0, 0)),
                pl.BlockSpec((nb, 8, LANES), lambda b, c: (b, 0, 0))]
    return pl.pallas_call(
        functools.partial(_mlstm_kernel, tc=tc, n_valid=n_valid, nb=nb, shared_state=shared),
        grid=(bsz // nb, t // tc),
        in_specs=[tok(ML_WIDTH), tok(ML_WIDTH), tok(ML_WIDTH), tok(LANES),
                  pl.BlockSpec((sn, ML_HEADS, hd, hd), lambda b, c: (sb(b), 0, 0, 0)),
                  pl.BlockSpec((sn, 8, LANES), lambda b, c: (sb(b), 0, 0)),
                  pl.BlockSpec((sn, 8, LANES), lambda b, c: (sb(b), 0, 0)),
                  _const_spec((ML_HEADS, hd, hd)), _const_spec((ML_HEADS, hd, hd)),
                  _const_spec((1, ML_WIDTH)), _const_spec((1, ML_WIDTH))],
        out_specs=[tok(ML_WIDTH)] + st_specs,
        out_shape=[jax.ShapeDtypeStruct((bsz, t, ML_WIDTH), BF16),
                   jax.ShapeDtypeStruct((bsz, ML_HEADS, hd, hd), F32),
                   jax.ShapeDtypeStruct((bsz, 8, LANES), F32),
                   jax.ShapeDtypeStruct((bsz, 8, LANES), F32)],
        scratch_shapes=[pltpu.VMEM((nb, ML_HEADS, hd, hd), F32),
                        pltpu.VMEM((nb, 8, LANES), F32), pltpu.VMEM((nb, 8, LANES), F32)],
        compiler_params=pltpu.CompilerParams(dimension_semantics=("arbitrary", "arbitrary")),
        name="mlstm",
    )(xc, mv, mo, gt, c0, n0, m0, wq, wk, gn, sk)


def _post_kernel(a_ref, hm_ref, ga_ref, gb_ref, x_ref, wfp_ref, wmp_ref, wo_ref, wgu_ref, wd_ref,
                 g1_ref, g2_ref, g3_ref, y_ref, *, d_ff, ff_cuts):
    ya = jnp.dot(a_ref[...], wfp_ref[...], preferred_element_type=F32)
    yb = jnp.dot(hm_ref[...], wmp_ref[...], preferred_element_type=F32)
    merged = jax.nn.sigmoid(ga_ref[...].astype(F32)) * ya + jax.nn.sigmoid(gb_ref[...].astype(F32)) * yb
    mix = jnp.dot(merged.astype(BF16), wo_ref[...], preferred_element_type=F32)
    x1 = x_ref[...] + _rms(mix, g1_ref[...])
    h2 = _rms(x1, g2_ref[...]).astype(BF16)
    f = None
    for lo, hi in zip(ff_cuts[:-1], ff_cuts[1:]):
        g = jnp.dot(h2, wgu_ref[:, lo:hi], preferred_element_type=F32)
        u = jnp.dot(h2, wgu_ref[:, d_ff + lo:d_ff + hi], preferred_element_type=F32)
        t = (g * jax.nn.sigmoid(g) * u).astype(BF16)
        fj = jnp.dot(t, wd_ref[lo:hi, :], preferred_element_type=F32)
        f = fj if f is None else f + fj
    y_ref[...] = x1 + _rms(f, g3_ref[...])


def _post(a, hm, ga, gb, x, wfp, wmp, wo, wgu, wd, g1, g2, g3, tm):
    n, d = x.shape
    d_ff = wd.shape[0]
    nmt = d_ff // MXU_TILE
    ff_cuts = tuple(MXU_TILE * ((nmt * j) // FF_CHUNKS) for j in range(FF_CHUNKS)) + (d_ff,) \
        if d_ff % MXU_TILE == 0 else (0, d_ff)
    row = lambda c: pl.BlockSpec((tm, c), lambda i: (i, 0))
    return pl.pallas_call(
        functools.partial(_post_kernel, d_ff=d_ff, ff_cuts=ff_cuts),
        grid=(n // tm,),
        in_specs=[row(FOX_WIDTH), row(ML_WIDTH), row(d), row(d), row(d),
                  _const_spec(wfp.shape), _const_spec(wmp.shape), _const_spec(wo.shape),
                  _const_spec(wgu.shape), _const_spec(wd.shape),
                  _const_spec((1, d)), _const_spec((1, d)), _const_spec((1, d))],
        out_specs=row(d),
        out_shape=jax.ShapeDtypeStruct((n, d), F32),
        compiler_params=pltpu.CompilerParams(dimension_semantics=("arbitrary",), vmem_limit_bytes=VMEM_LIMIT),
        name="post_mix_ffn",
    )(a, hm, ga, gb, x, wfp, wmp, wo, wgu, wd, g1, g2, g3)


def _pad_rows(a, rows, axis):
    pad = [(0, 0)] * a.ndim
    pad[axis] = (0, rows - a.shape[axis])
    return jnp.pad(a, pad)


def _row_tile(n, pref):
    t = min(pref, n)
    while n % t:
        t //= 2
    return t


def kernel(x_prompt, x_sample, cache_fox_k, cache_fox_v, cache_fox_logf, state_mlstm_C, state_mlstm_n,
           state_mlstm_m, state_mlstm_conv, meta_tokens, w_in, b_in, g_fq, g_fk, w_conv, b_conv, w_mq, w_mk,
           g_mnorm, skip_m, w_fox_proj, w_ml_proj, w_out, g_pre_mix, g_post_mix, g_pre_ffn, g_post_ffn,
           w_gate_up, w_down):
    assert w_in.shape[0] == 1, "single-layer trunk"
    bsz, seq, d = x_prompt.shape
    dbsz, dseq, _ = x_sample.shape
    past = cache_fox_k.shape[2]
    fw, mw = FOX_WIDTH, ML_WIDTH

    w, b = w_in[0], b_in[0]
    o_ff = 3 * fw
    o_mx = o_ff + FOX_HEADS
    o_mi = o_mx + 3 * mw
    o_mf = o_mi + ML_HEADS
    o_ga = o_mf + ML_HEADS

    def relayout(a):
        gates = jnp.concatenate([a[..., o_ff:o_mx], a[..., o_mi:o_mf], a[..., o_mf:o_ga], a[..., o_mf:o_ga]], axis=-1)
        gates = _pad_rows(gates, LANES, gates.ndim - 1)
        return jnp.concatenate([a[..., :o_ff], a[..., o_mx:o_mi], a[..., o_ga:], gates], axis=-1)

    w_p = relayout(w).astype(BF16)
    b_p = relayout(b)[None, :]
    hid = jnp.arange(fw) // FOX_HEAD_DIM
    hsum = (hid[:, None] == hid[None, :]).astype(BF16)
    gq = g_fq[0].reshape(1, fw)
    gk = g_fk[0].reshape(1, fw)
    gpre = g_pre_mix[0][None, :]
    wc, bc = w_conv[0], b_conv[0][None, :]
    inproj = lambda x, c0, cp, tm, nt, **kw: _inproj(x, gpre, w_p, b_p, gq, gk, hsum, c0, cp, wc, bc, tm, nt, **kw)

    wq, wk = w_mq[0].astype(BF16), w_mk[0].astype(BF16)
    gn, sk = g_mnorm[0].reshape(1, mw), skip_m[0][None, :]
    mlstm = lambda *a, **kw: _mlstm(*a, wq, wk, gn, sk, **kw)

    def conv_hist(rows):
        return jnp.pad(rows, ((0, 0), (8 - (CONV_WIDTH - 1), 0), (0, 0)))

    def m_rows(m):
        return jnp.broadcast_to(_pad_rows(m, 8, 1)[:, :, None], (m.shape[0], 8, LANES))

    mt = 64
    xm = _pad_rows(meta_tokens.astype(F32), mt, 0)
    (_, kf_m, ka_m, vf_m, va_m, xc_m, mv_m, mo_m, _, _, gt_m, fc_m, ct_m) = inproj(
        xm, jnp.zeros((1, 1, LANES), F32), jnp.zeros((1, 8, mw), F32), mt, 1, conv_valid=N_META)
    zc = jnp.zeros((1, ML_HEADS, ML_HEAD_DIM, ML_HEAD_DIM), F32)
    z8 = jnp.zeros((1, 8, LANES), F32)
    _, c_m, n_m, m_m = mlstm(xc_m[None], mv_m[None], mo_m[None], gt_m[None], zc, z8, z8, tc=mt, n_valid=N_META)

    xp = x_prompt.reshape(bsz * seq, d)
    tm = _row_tile(seq, 512)
    lanes_t = lambda rows: _pad_rows(rows[:N_META].T, LANES, 1)
    (qa_p, kt_p, ka_p, vt_p, va_p, xc_p, mv_p, mo_p, ga_p, gb_p, gt_p, _, ct_p, lt_p) = inproj(
        xp, fc_m[None, N_META - 1:N_META], ct_m, tm, seq // tm,
        meta_t=(lanes_t(kf_m), lanes_t(vf_m), lanes_t(gt_m[:, :FOX_HEADS])))
    r3 = lambda a: a.reshape(bsz, seq, a.shape[-1])
    hm_p, c_p, n_p, m_p = mlstm(r3(xc_p), r3(mv_p), r3(mo_p), r3(gt_p), c_m, n_m, m_m,
                                tc=_row_tile(seq, 256), nb=_row_tile(bsz, 2))
    a_p = _fox(r3(qa_p), _pad_rows(ka_m, LANES, 0)[None], _pad_rows(va_m, LANES, 0)[None], r3(ka_p), r3(va_p),
               tq=_row_tile(seq, 2048), tr=_row_tile(seq, 512), pblk=LANES, pvalid=N_META)

    kc, vc, f_c = _cacheprep(cache_fox_k[0].reshape(dbsz, past, fw), cache_fox_v[0].reshape(dbsz, past, fw),
                             _pad_rows(cache_fox_logf[0].astype(F32), LANES, 2), _row_tile(past, 512))
    xs = x_sample.reshape(dbsz * dseq, d)
    (qa_s, kf_s, ka_s, vf_s, va_s, xc_s, mv_s, mo_s, ga_s, gb_s, gt_s, _, ct_s) = inproj(
        xs, f_c, conv_hist(state_mlstm_conv[0]), dseq, 1)
    r3s = lambda a: a.reshape(dbsz, dseq, a.shape[-1])
    hm_s, c_s, n_s, m_s = mlstm(r3s(xc_s), r3s(mv_s), r3s(mo_s), r3s(gt_s), state_mlstm_C[0], _pad_rows(state_mlstm_n[0], 8, 1), m_rows(state_mlstm_m[0]),
                                tc=dseq, nb=_row_tile(dbsz, 4))
    a_s = _fox(r3s(qa_s), kc, vc, r3s(ka_s), r3s(va_s), tq=dseq, tr=dseq, pblk=_row_tile(past, 512))

    wfp, wmp, wo = w_fox_proj[0].astype(BF16), w_ml_proj[0].astype(BF16), w_out[0].astype(BF16)
    wgu, wd = w_gate_up[0].astype(BF16), w_down[0].astype(BF16)
    g1, g2, g3 = g_post_mix[0][None, :], g_pre_ffn[0][None, :], g_post_ffn[0][None, :]
    post = lambda a, hm, ga, gb, x, tm: _post(a, hm, ga, gb, x, wfp, wmp, wo, wgu, wd, g1, g2, g3, tm)
    y_p = post(a_p.reshape(bsz * seq, fw), hm_p.reshape(bsz * seq, mw), ga_p, gb_p, xp, _row_tile(bsz * seq, 512))
    y_s = post(a_s.reshape(dbsz * dseq, fw), hm_s.reshape(dbsz * dseq, mw), ga_s, gb_s, xs,
               _row_tile(dbsz * dseq, 512))

    def tokens_major(a):
        return a.reshape(bsz, FOX_HEADS, FOX_HEAD_DIM, a.shape[-1]).transpose(0, 3, 1, 2)[None]

    heads = lambda a: a.reshape(a.shape[:-1] + (FOX_HEADS, FOX_HEAD_DIM))
    tail = CONV_WIDTH - 1
    return (y_p.reshape(bsz, seq, d), y_s.reshape(dbsz, dseq, d),
            tokens_major(kt_p), tokens_major(vt_p), lt_p.transpose(0, 2, 1)[None],
            c_p[None], n_p[None, :, :ML_HEADS], m_p[None, :, :ML_HEADS, 0], ct_p[None, :, 8 - tail:],
            heads(r3s(kf_s))[None], heads(r3s(vf_s))[None], r3s(gt_s)[None, ..., :FOX_HEADS],
            c_s[None], n_s[None, :, :ML_HEADS], m_s[None, :, :ML_HEADS, 0], ct_s[None, :, 8 - tail:])
```

```python
import functools

import jax
import jax.numpy as jnp
from jax import lax
from jax.experimental import pallas as pl
from jax.experimental.pallas import tpu as pltpu

F32 = jnp.float32
BF16 = jnp.bfloat16

N_META = 16
FOX_HEADS = 8
FOX_HEAD_DIM = 64
FOX_WIDTH = FOX_HEADS * FOX_HEAD_DIM
ML_HEADS = 4
ML_HEAD_DIM = 128
ML_WIDTH = ML_HEADS * ML_HEAD_DIM
CONV_WIDTH = 4
RMS_EPS = 1e-6
PAD_LOG_GATE = -1e30
NEG_BIG = -1e30

LANES = 128
GATE_FOX_F = 0
GATE_ML_I = 8
GATE_ML_F = 12
GATE_ML_F2 = 16

AUG_STEP = FOX_HEADS
AUG_QF = FOX_HEAD_DIM
AUG_KF = FOX_HEAD_DIM + 24

VMEM_LIMIT = 56 * 1024 * 1024
FOX_WIDE_KEYS = 1024
FOX_MAX_STATIC_BLOCKS = 4
MXU_TILE = 256
FF_CHUNKS = 3


def _const_spec(shape):
    nd = len(shape)
    return pl.BlockSpec(shape, lambda *_: (0,) * nd, pipeline_mode=pl.Buffered(1))


def _rms(x, g):
    return x * lax.rsqrt(jnp.mean(x * x, axis=-1, keepdims=True) + RMS_EPS) * g


def _split3(x):
    hi = x.astype(BF16).astype(F32)
    r = x - hi
    mid = r.astype(BF16).astype(F32)
    lo = (r - mid).astype(BF16).astype(F32)
    return hi, mid, lo


def _cumsum_rows(lg, cur):
    tm = lg.shape[0]
    blk = min(tm, LANES)
    r = lax.broadcasted_iota(jnp.int32, (blk, blk), 0)
    c = lax.broadcasted_iota(jnp.int32, (blk, blk), 1)
    tril = (c <= r).astype(BF16)
    outs = []
    for s in range(tm // blk):
        hi, mid, lo = _split3(lg[s * blk:(s + 1) * blk])
        f = (jnp.dot(tril, lo.astype(BF16), preferred_element_type=F32)
             + jnp.dot(tril, mid.astype(BF16), preferred_element_type=F32)
             + jnp.dot(tril, hi.astype(BF16), preferred_element_type=F32)) + cur
        outs.append(f)
        cur = f[blk - 1:blk]
    return (outs[0] if len(outs) == 1 else jnp.concatenate(outs, axis=0)), cur


def _store_attention_operands(qn, kn, v, fcol, q_ref, k_ref, v_ref, fillers=()):
    tm = fcol.shape[0]
    lane = lax.broadcasted_iota(jnp.int32, (tm, LANES), 1)
    data = lane < FOX_HEAD_DIM
    hi, mid, lo = _split3(fcol)
    comb = jnp.where(lane < AUG_STEP, hi, jnp.where(lane < 2 * AUG_STEP, pltpu.roll(mid, AUG_STEP, 1),
                                                     pltpu.roll(lo, 2 * AUG_STEP, 1)))
    ncomb = -comb
    is_qf = (lane == AUG_QF) | (lane == AUG_QF + AUG_STEP) | (lane == AUG_QF + 2 * AUG_STEP)
    is_kf = (lane == AUG_KF) | (lane == AUG_KF + AUG_STEP) | (lane == AUG_KF + 2 * AUG_STEP)
    base_q = jnp.where(is_kf, 1.0, 0.0)
    base_k = jnp.where(is_qf, 1.0, 0.0)
    base_v = jnp.where(lane == FOX_HEAD_DIM, 1.0, 0.0)
    fillers = list(fillers)
    for h in range(FOX_HEADS):
        t, odd = divmod(h, 2)

        def head_tile(z):
            if callable(z):
                return z(h)
            x = z[:, t * LANES:(t + 1) * LANES]
            return pltpu.roll(x, FOX_HEAD_DIM, 1) if odd else x

        hs = slice(h * LANES, (h + 1) * LANES)
        if q_ref is not None:
            aug = jnp.where(is_qf, pltpu.roll(comb, AUG_QF - h, 1), base_q)
            q_ref[:, hs] = jnp.where(data, head_tile(qn), aug).astype(BF16)
        aug = jnp.where(is_kf, pltpu.roll(ncomb, AUG_KF - h, 1), base_k)
        k_ref[:, hs] = jnp.where(data, head_tile(kn), aug).astype(BF16)
        v_ref[:, hs] = jnp.where(data, head_tile(v), base_v).astype(BF16)
        if fillers:
            fillers.pop(0)()
    for f in fillers:
        f()


def _inproj_kernel(*refs, d_model, nt, kv_t, conv_valid):
    (x_ref, gpre_ref, w_ref, b_ref, gq_ref, gk_ref, hs_ref, c0_ref, cp_ref, wc_ref, bc_ref), rest = refs[:11], refs[11:]
    if kv_t:
        kmeta_ref, vmeta_ref, lmeta_ref = rest[:3]
        (q_ref, kf_ref, kb_ref, vf_ref, vb_ref, xc_ref, mv_ref, mo_ref, ga_ref, gb_ref, gt_ref, fc_ref, ct_ref,
         lt_ref, carry, xbuf, kcar, vcar, lcar) = rest[3:]
    else:
        (q_ref, kf_ref, kb_ref, vf_ref, vb_ref, xc_ref, mv_ref, mo_ref, ga_ref, gb_ref, gt_ref, fc_ref, ct_ref,
         carry, xbuf) = rest
    fw, mw = FOX_WIDTH, ML_WIDTH
    offs = [0, fw, 2 * fw, 3 * fw, 3 * fw + mw, 3 * fw + 2 * mw, 3 * fw + 3 * mw,
            3 * fw + 3 * mw + d_model, 3 * fw + 3 * mw + 2 * d_model, 3 * fw + 3 * mw + 2 * d_model + LANES]
    r = pl.program_id(1)

    @pl.when(r == 0)
    def _():
        carry[...] = c0_ref[...]
        xbuf[0:8, :] = cp_ref[...]
        if kv_t:
            kcar[...] = kmeta_ref[...]
            vcar[...] = vmeta_ref[...]
            lcar[...] = lmeta_ref[...]

    def store_shifted(xt, car, out_ref):
        sh = pltpu.roll(xt, N_META, 1)
        lane = lax.broadcasted_iota(jnp.int32, (xt.shape[0], LANES), 1)
        out_ref[:, :LANES] = jnp.where(lane < N_META, car[...], sh[:, :LANES])
        out_ref[:, LANES:] = sh[:, LANES:]
        car[...] = sh[:, :LANES]

    @pl.when(r < nt)
    def _():
        h = _rms(x_ref[...], gpre_ref[...]).astype(BF16)

        def seg(i):
            a, b = offs[i], offs[i + 1]
            return jnp.dot(h, w_ref[:, a:b], preferred_element_type=F32) + b_ref[:, a:b]

        def headnorm(z, g):
            ms = jnp.dot((z * z).astype(BF16), hs_ref[...], preferred_element_type=F32) * (1.0 / FOX_HEAD_DIM)
            return z * lax.rsqrt(ms + RMS_EPS) * g

        g = seg(8)
        ga_ref[...] = seg(6).astype(BF16)
        lane = lax.broadcasted_iota(jnp.int32, g.shape, 1)
        raw = (lane >= GATE_ML_I) & (lane < GATE_ML_F)
        gt = jnp.where(raw, g, jax.nn.log_sigmoid(g))
        gt_ref[...] = gt
        fcol, cur = _cumsum_rows(gt, carry[...])
        carry[...] = cur
        fc_ref[...] = fcol

        v = seg(2)
        q = headnorm(seg(0), gq_ref[...]) * (FOX_HEAD_DIM ** -0.5)
        k = headnorm(seg(1), gk_ref[...])
        def store_kv():
            if kv_t:
                store_shifted(k.T, kcar, kf_ref)
                store_shifted(v.T, vcar, vf_ref)
                store_shifted(gt.T[:FOX_HEADS], lcar, lt_ref)
            else:
                kf_ref[...] = k
                vf_ref[...] = v

        def store_xc():
            tm = x_ref.shape[0]
            xbuf[8:8 + tm, :] = seg(3)
            xc = bc_ref[...]
            for i in range(CONV_WIDTH):
                xc = xc + xbuf[5 + i:5 + i + tm, :] * wc_ref[i:i + 1, :]
            xc_ref[...] = (xc * jax.nn.sigmoid(xc)).astype(BF16)
            ct_ref[...] = xbuf[conv_valid:conv_valid + 8, :]
            xbuf[0:8, :] = xbuf[tm:tm + 8, :]

        def store_mv():
            mv_ref[...] = seg(4).astype(BF16)

        def store_mo():
            mo_ref[...] = jax.nn.sigmoid(seg(5)).astype(BF16)

        def store_gb():
            gb_ref[...] = seg(7).astype(BF16)

        _store_attention_operands(q, k, v, fcol, q_ref, kb_ref, vb_ref,
                                  fillers=(store_xc, store_mv, store_mo, store_gb, store_kv))

    if kv_t:
        @pl.when(r == nt)
        def _():
            kf_ref[:, :LANES] = kcar[...]
            vf_ref[:, :LANES] = vcar[...]
            lt_ref[:, :LANES] = lcar[...]


def _inproj(x, gpre, w, b, gq, gk, hs, c0, cp, wc, bc, tm, nt, meta_t=None, conv_valid=None):
    n, d = x.shape
    nb = n // (nt * tm)
    dinp = w.shape[1]
    fw, mw = FOX_WIDTH, ML_WIDTH
    aw = FOX_HEADS * LANES
    kv_t = meta_t is not None
    tile = lambda b, r: b * nt + jnp.minimum(r, nt - 1)
    row = lambda c: pl.BlockSpec((tm, c), lambda b, r: (tile(b, r), 0))
    cb = (lambda b: b) if c0.shape[0] > 1 else (lambda b: 0)
    pb = (lambda b: b) if cp.shape[0] > 1 else (lambda b: 0)
    lp = N_META + nt * tm
    tmin = lambda c: pl.BlockSpec((None, c, tm), lambda b, r: (b, 0, r))
    flat = lambda c, t: (row(c), jax.ShapeDtypeStruct((n, c), t))
    tok_minor = lambda c: (tmin(c), jax.ShapeDtypeStruct((nb, c, lp), F32))
    outs = [flat(aw, BF16), tok_minor(fw) if kv_t else flat(fw, F32), flat(aw, BF16),
            tok_minor(fw) if kv_t else flat(fw, F32), flat(aw, BF16), flat(mw, BF16), flat(mw, BF16), flat(mw, BF16),
            flat(d, BF16), flat(d, BF16), flat(LANES, F32), flat(LANES, F32),
            (pl.BlockSpec((None, 8, mw), lambda b, r: (b, 0, 0)), jax.ShapeDtypeStruct((nb, 8, mw), F32))]
    in_specs = [row(d), _const_spec((1, d)), _const_spec((d, dinp)), _const_spec((1, dinp)),
                _const_spec((1, fw)), _const_spec((1, fw)), _const_spec((fw, fw)),
                pl.BlockSpec((None, 1, LANES), lambda b, r: (cb(b), 0, 0)),
                pl.BlockSpec((None, 8, mw), lambda b, r: (pb(b), 0, 0)),
                _const_spec((CONV_WIDTH, mw)), _const_spec((1, mw))]
    scratch = [pltpu.VMEM((1, LANES), F32), pltpu.VMEM((8 + tm, mw), F32)]
    args = [x, gpre, w, b, gq, gk, hs, c0, cp, wc, bc]
    if kv_t:
        outs.append(tok_minor(FOX_HEADS))
        in_specs += [_const_spec((fw, LANES)), _const_spec((fw, LANES)), _const_spec((FOX_HEADS, LANES))]
        scratch += [pltpu.VMEM((fw, LANES), F32), pltpu.VMEM((fw, LANES), F32), pltpu.VMEM((FOX_HEADS, LANES), F32)]
        args += list(meta_t)
    return pl.pallas_call(
        functools.partial(_inproj_kernel, d_model=d, nt=nt, kv_t=kv_t,
                          conv_valid=tm if conv_valid is None else conv_valid),
        grid=(nb, nt + (1 if kv_t else 0)),
        in_specs=in_specs,
        out_specs=[o[0] for o in outs],
        out_shape=[o[1] for o in outs],
        scratch_shapes=scratch,
        compiler_params=pltpu.CompilerParams(dimension_semantics=("arbitrary", "arbitrary"),
                                             vmem_limit_bytes=VMEM_LIMIT),
        name="inproj",
    )(*args)


def _cacheprep_kernel(k_ref, v_ref, lf_ref, ka_ref, va_ref, fl_ref, carry):
    @pl.when(pl.program_id(1) == 0)
    def _():
        carry[...] = jnp.zeros_like(carry)

    fcol, cur = _cumsum_rows(lf_ref[...], carry[...])
    carry[...] = cur
    fl_ref[...] = cur
    tp = fcol.shape[0]

    def head_rows(ref):
        def tile(h):
            x = ref[pl.ds(h, tp, stride=FOX_HEADS), :]
            return jnp.concatenate([x, jnp.zeros_like(x)], axis=1)
        return tile

    _store_attention_operands(None, head_rows(k_ref), head_rows(v_ref), fcol, None, ka_ref, va_ref)


def _cacheprep(ck, cv, clf, tp):
    bsz, plen, _ = clf.shape
    aw = FOX_HEADS * LANES
    return pl.pallas_call(
        _cacheprep_kernel,
        grid=(bsz, plen // tp),
        in_specs=[pl.BlockSpec((None, tp * FOX_HEADS, FOX_HEAD_DIM), lambda b, i: (b, i, 0)),
                  pl.BlockSpec((None, tp * FOX_HEADS, FOX_HEAD_DIM), lambda b, i: (b, i, 0)),
                  pl.BlockSpec((None, tp, LANES), lambda b, i: (b, i, 0))],
        out_specs=[pl.BlockSpec((None, tp, aw), lambda b, i: (b, i, 0)),
                   pl.BlockSpec((None, tp, aw), lambda b, i: (b, i, 0)),
                   pl.BlockSpec((None, 1, LANES), lambda b, i: (b, 0, 0))],
        out_shape=[jax.ShapeDtypeStruct((bsz, plen, aw), BF16), jax.ShapeDtypeStruct((bsz, plen, aw), BF16),
                   jax.ShapeDtypeStruct((bsz, 1, LANES), F32)],
        scratch_shapes=[pltpu.VMEM((1, LANES), F32)],
        compiler_params=pltpu.CompilerParams(dimension_semantics=("arbitrary", "arbitrary")),
        name="cache_prep",
    )(ck, cv, clf)


def _fox_kernel(q_ref, kp_ref, vp_ref, k_ref, v_ref, o_ref, *, tq, tr, tkb, nq, hps, pblk, n_pblk, pvalid):
    i = pl.program_id(2)
    nr = tq // tr
    chains = [(hh, r) for hh in range(hps) for r in range(nr)]
    qs = [q_ref[r * tr:(r + 1) * tr, hh * LANES:(hh + 1) * LANES] for hh, r in chains]

    def upd1(n, state, kb, vb, mask=None):
        m, acc = state
        hh = chains[n][0]
        hs = slice(hh * LANES, (hh + 1) * LANES)
        s = lax.dot_general(qs[n], kb[:, hs], (((1,), (1,)), ((), ())), preferred_element_type=F32)
        if mask is not None:
            cut = s.shape[1] - mask.shape[1]
            tail = jnp.where(mask, s[:, cut:], NEG_BIG)
            s = tail if cut == 0 else jnp.concatenate([s[:, :cut], tail], axis=1)
        m_new = jnp.maximum(m, jnp.max(s, axis=-1, keepdims=True))
        alpha = jnp.exp(m - m_new)
        pm = jnp.exp(s - m_new)
        acc = alpha * acc + jnp.dot(pm.astype(BF16), vb[:, hs], preferred_element_type=F32)
        return m_new, acc

    def upd(carry, kb, vb):
        return tuple(upd1(n, carry[n], kb, vb) for n in range(len(chains)))

    def start():
        carry = tuple((jnp.full((tr, 1), NEG_BIG, F32), jnp.zeros((tr, LANES), F32)) for _ in chains)
        if pvalid is None:
            for jb in range(n_pblk):
                sl = slice(jb * pblk, (jb + 1) * pblk)
                carry = upd(carry, kp_ref[sl, :], vp_ref[sl, :])
        return carry

    def finish(carry, off):
        tail_w = tr + (LANES if pvalid is not None else 0)
        rj = lax.broadcasted_iota(jnp.int32, (tr, tail_w), 0)
        cj = lax.broadcasted_iota(jnp.int32, (tr, tail_w), 1)
        mask = cj <= rj
        if pvalid is not None:
            mask = (mask | (cj >= tr)) & (cj < tr + pvalid)
        carry = list(carry)
        for r in reversed(range(nr)):
            width = (r + 1) * tr
            kb = k_ref[pl.ds(off, width), :]
            vb = v_ref[pl.ds(off, width), :]
            if pvalid is not None:
                kb = jnp.concatenate([kb, kp_ref[...]], axis=0)
                vb = jnp.concatenate([vb, vp_ref[...]], axis=0)
            for n, (hh, rr) in enumerate(chains):
                if rr == r:
                    carry[n] = upd1(n, carry[n], kb, vb, mask=mask)
        lane = lax.broadcasted_iota(jnp.int32, (tr, LANES), 1)
        outs = []
        for n in range(len(chains)):
            acc = carry[n][1]
            l = jnp.sum(jnp.where(lane == FOX_HEAD_DIM, acc, 0.0), axis=-1, keepdims=True)
            outs.append(acc / l)
        for hp in range(hps // 2):
            for r in range(nr):
                even, odd = outs[2 * hp * nr + r], outs[(2 * hp + 1) * nr + r]
                o_ref[r * tr:(r + 1) * tr, hp * LANES:(hp + 1) * LANES] = jnp.where(
                    lane < FOX_HEAD_DIM, even, pltpu.roll(odd, FOX_HEAD_DIM, 1)).astype(o_ref.dtype)

    if tq % tkb == 0 and nq <= FOX_MAX_STATIC_BLOCKS:
        for qi in range(nq):
            @pl.when(i == qi)
            def _():
                carry = start()
                for j in range(qi * tq // tkb):
                    carry = upd(carry, k_ref[j * tkb:(j + 1) * tkb, :], v_ref[j * tkb:(j + 1) * tkb, :])
                finish(carry, qi * tq)
    else:
        tkw = max(tkb, tq)

        def body_wide(j, carry):
            off = pl.multiple_of(j * tkw, tkw)
            return upd(carry, k_ref[pl.ds(off, tkw), :], v_ref[pl.ds(off, tkw), :])

        def body(j, carry):
            off = pl.multiple_of(j * tq, tq)
            return upd(carry, k_ref[pl.ds(off, tq), :], v_ref[pl.ds(off, tq), :])

        n_wide = i // (tkw // tq)
        carry = lax.fori_loop(0, n_wide, body_wide, start())
        if tkw > tq:
            carry = lax.fori_loop(n_wide * (tkw // tq), i, body, carry)
        finish(carry, pl.multiple_of(i * tq, tq))


def _fox(q, kp, vp, k, v, tq, tr, pblk, pvalid=None, hps=2):
    bsz, t, _ = q.shape
    bp, plen, _ = kp.shape
    nq = t // tq
    n_pblk = plen // pblk
    pw = hps * LANES
    pb = (lambda b: b) if bp == bsz else (lambda b: 0)
    return pl.pallas_call(
        functools.partial(_fox_kernel, tq=tq, tr=tr, tkb=_row_tile(t, FOX_WIDE_KEYS), nq=nq, hps=hps, pblk=pblk,
                          n_pblk=n_pblk, pvalid=pvalid),
        grid=(bsz, FOX_HEADS // hps, nq),
        in_specs=[pl.BlockSpec((None, tq, pw), lambda b, p, i: (b, i, p)),
                  pl.BlockSpec((None, plen, pw), lambda b, p, i: (pb(b), 0, p)),
                  pl.BlockSpec((None, plen, pw), lambda b, p, i: (pb(b), 0, p)),
                  pl.BlockSpec((None, t, pw), lambda b, p, i: (b, 0, p)),
                  pl.BlockSpec((None, t, pw), lambda b, p, i: (b, 0, p))],
        out_specs=pl.BlockSpec((None, tq, hps * FOX_HEAD_DIM), lambda b, p, i: (b, i, p)),
        out_shape=jax.ShapeDtypeStruct((bsz, t, FOX_WIDTH), BF16),
        compiler_params=pltpu.CompilerParams(dimension_semantics=("arbitrary", "arbitrary", "arbitrary"),
                                             vmem_limit_bytes=VMEM_LIMIT),
        name="fox_attention",
    )(q, kp, vp, k, v)


def _mlstm_kernel(xc_ref, mv_ref, mo_ref, gt_ref, c0_ref, n0_ref, m0_ref, wq_ref, wk_ref, gn_ref, sk_ref,
                  hm_ref, cout_ref, nout_ref, mout_ref,
                  c_scr, n_scr, m_scr, *, tc, n_valid, nb, shared_state):
    c = pl.program_id(1)

    @pl.when(c == 0)
    def _():
        for bi in range(nb):
            si = 0 if shared_state else bi
            c_scr[bi] = c0_ref[si]
            n_scr[bi] = n0_ref[si]
            m_scr[bi] = m0_ref[si]

    for bi in range(nb):
        _mlstm_chunk(xc_ref.at[bi], mv_ref.at[bi], mo_ref.at[bi], gt_ref.at[bi], wq_ref, wk_ref,
                     gn_ref, sk_ref, hm_ref.at[bi], c_scr.at[bi], n_scr.at[bi], m_scr.at[bi],
                     tc=tc, n_valid=n_valid)

    @pl.when(c == pl.num_programs(1) - 1)
    def _():
        cout_ref[...] = c_scr[...]
        nout_ref[...] = n_scr[...]
        mout_ref[...] = m_scr[...]


def _mlstm_chunk(xc_ref, mv_ref, mo_ref, gt_ref, wq_ref, wk_ref, gn_ref, sk_ref, hm_ref,
                 c_scr, n_scr, m_scr, *, tc, n_valid):
    gt = gt_ref[...]
    lane = lax.broadcasted_iota(jnp.int32, (tc, LANES), 1)
    if n_valid is not None:
        rowi = lax.broadcasted_iota(jnp.int32, (tc, LANES), 0)
        is_i = (lane >= GATE_ML_I) & (lane < GATE_ML_F)
        gt = jnp.where(rowi < n_valid, gt, jnp.where(is_i, PAD_LOG_GATE, 0.0))
    ri = lax.broadcasted_iota(jnp.int32, (tc, tc), 0)
    ci = lax.broadcasted_iota(jnp.int32, (tc, tc), 1)
    causal = ci <= ri
    b_all = _cumsum_rows(gt, jnp.zeros((1, LANES), F32))[0]
    r_t = (pltpu.roll(gt, GATE_ML_F - GATE_ML_I, 1) - b_all).T
    heads = range(ML_HEADS)
    hsl = [slice(h * ML_HEAD_DIM, (h + 1) * ML_HEAD_DIM) for h in heads]
    b_col = [b_all[:, GATE_ML_F + h:GATE_ML_F + h + 1] for h in heads]
    ig_col = [gt[:, GATE_ML_I + h:GATE_ML_I + h + 1] for h in heads]
    m_prev = [m_scr[h:h + 1, 0:1] for h in heads]
    dmat = [jnp.where(causal, b_col[h] + r_t[GATE_ML_F + h:GATE_ML_F + h + 1, :], NEG_BIG) for h in heads]
    m_inter = [b_col[h] + m_prev[h] for h in heads]
    m_t = [jnp.maximum(m_inter[h], jnp.max(dmat[h], axis=-1, keepdims=True)) for h in heads]
    w_inter = [jnp.exp(m_inter[h] - m_t[h]) for h in heads]

    xhb = [xc_ref[:, hsl[h]] for h in heads]
    q = [jnp.dot(xhb[h], wq_ref[h], preferred_element_type=F32) for h in heads]
    k = [jnp.dot(xhb[h], wk_ref[h], preferred_element_type=F32) * (ML_HEAD_DIM ** -0.5) for h in heads]
    qb = [x.astype(BF16) for x in q]
    kb = [x.astype(BF16) for x in k]
    v = [mv_ref[:, hsl[h]] for h in heads]
    smat = [lax.dot_general(qb[h], kb[h], (((1,), (1,)), ((), ())), preferred_element_type=F32) for h in heads]
    amat = [jnp.exp(dmat[h] - m_t[h]) * smat[h] for h in heads]
    ch = [c_scr[h] for h in heads]
    nh = [n_scr[h:h + 1, :] for h in heads]
    num = [jnp.dot(amat[h].astype(BF16), v[h], preferred_element_type=F32)
           + w_inter[h] * jnp.dot(qb[h], ch[h].astype(BF16), preferred_element_type=F32) for h in heads]
    den = [jnp.sum(amat[h], axis=-1, keepdims=True) + w_inter[h] * jnp.sum(q[h] * nh[h], axis=-1, keepdims=True)
           for h in heads]
    hcell = [num[h] / jnp.maximum(jnp.abs(den[h]), jnp.exp(-m_t[h])) for h in heads]

    b_last = [b_col[h][tc - 1:tc, :] for h in heads]
    g = [b_last[h] - b_col[h] + ig_col[h] for h in heads]
    m_new = [jnp.maximum(b_last[h] + m_prev[h], jnp.max(g[h], axis=0, keepdims=True)) for h in heads]
    decay = [jnp.exp(b_last[h] + m_prev[h] - m_new[h]) for h in heads]
    kw = [k[h] * jnp.exp(g[h] - m_new[h]) for h in heads]
    for h in heads:
        c_scr[h] = decay[h] * ch[h] + lax.dot_general(kw[h].astype(BF16), v[h], (((0,), (0,)), ((), ())),
                                                       preferred_element_type=F32)
        n_scr[h:h + 1, :] = decay[h] * nh[h] + jnp.sum(kw[h], axis=0, keepdims=True)
        m_scr[h:h + 1, :] = jnp.broadcast_to(m_new[h], (1, LANES))
        y = mo_ref[:, hsl[h]].astype(F32) * hcell[h]
        hm_ref[:, hsl[h]] = (_rms(y, gn_ref[:, hsl[h]]) + sk_ref[:, hsl[h]] * xhb[h].astype(F32)).astype(hm_ref.dtype)


def _mlstm(xc, mv, mo, gt, c0, n0, m0, wq, wk, gn, sk, tc, nb=1, n_valid=None):
    bsz, t, _ = xc.shape
    shared = c0.shape[0] != bsz
    sn = 1 if shared else nb
    sb = (lambda b: 0) if shared else (lambda b: b)
    tok = lambda w: pl.BlockSpec((nb, tc, w), lambda b, c: (b, c, 0))
    hd = ML_HEAD_DIM
    st_specs = [pl.BlockSpec((nb, ML_HEADS, hd, hd), lambda b, c: (b, 0, 0, 0)),
                pl.BlockSpec((nb, 8, LANES), lambda b, c: (b, 0, 0)),
                pl.BlockSpec((nb, 8, LANES), lambda b, c: (b, 0, 0))]
    return pl.pallas_call(
        functools.partial(_mlstm_kernel, tc=tc, n_valid=n_valid, nb=nb, shared_state=shared),
        grid=(bsz // nb, t // tc),
        in_specs=[tok(ML_WIDTH), tok(ML_WIDTH), tok(ML_WIDTH), tok(LANES),
                  pl.BlockSpec((sn, ML_HEADS, hd, hd), lambda b, c: (sb(b), 0, 0, 0)),
                  pl.BlockSpec((sn, 8, LANES), lambda b, c: (sb(b), 0, 0)),
                  pl.BlockSpec((sn, 8, LANES), lambda b, c: (sb(b), 0, 0)),
                  _const_spec((ML_HEADS, hd, hd)), _const_spec((ML_HEADS, hd, hd)),
                  _const_spec((1, ML_WIDTH)), _const_spec((1, ML_WIDTH))],
        out_specs=[tok(ML_WIDTH)] + st_specs,
        out_shape=[jax.ShapeDtypeStruct((bsz, t, ML_WIDTH), BF16),
                   jax.ShapeDtypeStruct((bsz, ML_HEADS, hd, hd), F32),
                   jax.ShapeDtypeStruct((bsz, 8, LANES), F32),
                   jax.ShapeDtypeStruct((bsz, 8, LANES), F32)],
        scratch_shapes=[pltpu.VMEM((nb, ML_HEADS, hd, hd), F32),
                        pltpu.VMEM((nb, 8, LANES), F32), pltpu.VMEM((nb, 8, LANES), F32)],
        compiler_params=pltpu.CompilerParams(dimension_semantics=("arbitrary", "arbitrary")),
        name="mlstm",
    )(xc, mv, mo, gt, c0, n0, m0, wq, wk, gn, sk)


def _post_kernel(a_ref, hm_ref, ga_ref, gb_ref, x_ref, wfp_ref, wmp_ref, wo_ref, wgu_ref, wd_ref,
                 g1_ref, g2_ref, g3_ref, y_ref, *, d_ff, ff_cuts):
    ya = jnp.dot(a_ref[...], wfp_ref[...], preferred_element_type=F32)
    yb = jnp.dot(hm_ref[...], wmp_ref[...], preferred_element_type=F32)
    merged = jax.nn.sigmoid(ga_ref[...].astype(F32)) * ya + jax.nn.sigmoid(gb_ref[...].astype(F32)) * yb
    mix = jnp.dot(merged.astype(BF16), wo_ref[...], preferred_element_type=F32)
    x1 = x_ref[...] + _rms(mix, g1_ref[...])
    h2 = _rms(x1, g2_ref[...]).astype(BF16)
    f = None
    for lo, hi in zip(ff_cuts[:-1], ff_cuts[1:]):
        g = jnp.dot(h2, wgu_ref[:, lo:hi], preferred_element_type=F32)
        u = jnp.dot(h2, wgu_ref[:, d_ff + lo:d_ff + hi], preferred_element_type=F32)
        t = (g * jax.nn.sigmoid(g) * u).astype(BF16)
        fj = jnp.dot(t, wd_ref[lo:hi, :], preferred_element_type=F32)
        f = fj if f is None else f + fj
    y_ref[...] = x1 + _rms(f, g3_ref[...])


def _post(a, hm, ga, gb, x, wfp, wmp, wo, wgu, wd, g1, g2, g3, tm):
    n, d = x.shape
    d_ff = wd.shape[0]
    nmt = d_ff // MXU_TILE
    ff_cuts = tuple(MXU_TILE * ((nmt * j) // FF_CHUNKS) for j in range(FF_CHUNKS)) + (d_ff,) \
        if d_ff % MXU_TILE == 0 else (0, d_ff)
    row = lambda c: pl.BlockSpec((tm, c), lambda i: (i, 0))
    return pl.pallas_call(
        functools.partial(_post_kernel, d_ff=d_ff, ff_cuts=ff_cuts),
        grid=(n // tm,),
        in_specs=[row(FOX_WIDTH), row(ML_WIDTH), row(d), row(d), row(d),
                  _const_spec(wfp.shape), _const_spec(wmp.shape), _const_spec(wo.shape),
                  _const_spec(wgu.shape), _const_spec(wd.shape),
                  _const_spec((1, d)), _const_spec((1, d)), _const_spec((1, d))],
        out_specs=row(d),
        out_shape=jax.ShapeDtypeStruct((n, d), F32),
        compiler_params=pltpu.CompilerParams(dimension_semantics=("arbitrary",), vmem_limit_bytes=VMEM_LIMIT),
        name="post_mix_ffn",
    )(a, hm, ga, gb, x, wfp, wmp, wo, wgu, wd, g1, g2, g3)


def _pad_rows(a, rows, axis):
    pad = [(0, 0)] * a.ndim
    pad[axis] = (0, rows - a.shape[axis])
    return jnp.pad(a, pad)


def _row_tile(n, pref):
    t = min(pref, n)
    while n % t:
        t //= 2
    return t


def kernel(x_prompt, x_sample, cache_fox_k, cache_fox_v, cache_fox_logf, state_mlstm_C, state_mlstm_n,
           state_mlstm_m, state_mlstm_conv, meta_tokens, w_in, b_in, g_fq, g_fk, w_conv, b_conv, w_mq, w_mk,
           g_mnorm, skip_m, w_fox_proj, w_ml_proj, w_out, g_pre_mix, g_post_mix, g_pre_ffn, g_post_ffn,
           w_gate_up, w_down):
    assert w_in.shape[0] == 1, "single-layer trunk"
    bsz, seq, d = x_prompt.shape
    dbsz, dseq, _ = x_sample.shape
    past = cache_fox_k.shape[2]
    fw, mw = FOX_WIDTH, ML_WIDTH

    w, b = w_in[0], b_in[0]
    o_ff = 3 * fw
    o_mx = o_ff + FOX_HEADS
    o_mi = o_mx + 3 * mw
    o_mf = o_mi + ML_HEADS
    o_ga = o_mf + ML_HEADS

    def relayout(a):
        gates = jnp.concatenate([a[..., o_ff:o_mx], a[..., o_mi:o_mf], a[..., o_mf:o_ga], a[..., o_mf:o_ga]], axis=-1)
        gates = _pad_rows(gates, LANES, gates.ndim - 1)
        return jnp.concatenate([a[..., :o_ff], a[..., o_mx:o_mi], a[..., o_ga:], gates], axis=-1)

    w_p = relayout(w).astype(BF16)
    b_p = relayout(b)[None, :]
    hid = jnp.arange(fw) // FOX_HEAD_DIM
    hsum = (hid[:, None] == hid[None, :]).astype(BF16)
    gq = g_fq[0].reshape(1, fw)
    gk = g_fk[0].reshape(1, fw)
    gpre = g_pre_mix[0][None, :]
    wc, bc = w_conv[0], b_conv[0][None, :]
    inproj = lambda x, c0, cp, tm, nt, **kw: _inproj(x, gpre, w_p, b_p, gq, gk, hsum, c0, cp, wc, bc, tm, nt, **kw)

    wq, wk = w_mq[0].astype(BF16), w_mk[0].astype(BF16)
    gn, sk = g_mnorm[0].reshape(1, mw), skip_m[0][None, :]
    mlstm = lambda *a, **kw: _mlstm(*a, wq, wk, gn, sk, **kw)

    def conv_hist(rows):
        return jnp.pad(rows, ((0, 0), (8 - (CONV_WIDTH - 1), 0), (0, 0)))

    def m_rows(m):
        return jnp.broadcast_to(_pad_rows(m, 8, 1)[:, :, None], (m.shape[0], 8, LANES))

    mt = 64
    xm = _pad_rows(meta_tokens.astype(F32), mt, 0)
    (_, kf_m, ka_m, vf_m, va_m, xc_m, mv_m, mo_m, _, _, gt_m, fc_m, ct_m) = inproj(
        xm, jnp.zeros((1, 1, LANES), F32), jnp.zeros((1, 8, mw), F32), mt, 1, conv_valid=N_META)
    zc = jnp.zeros((1, ML_HEADS, ML_HEAD_DIM, ML_HEAD_DIM), F32)
    z8 = jnp.zeros((1, 8, LANES), F32)
    _, c_m, n_m, m_m = mlstm(xc_m[None], mv_m[None], mo_m[None], gt_m[None], zc, z8, z8, tc=mt, n_valid=N_META)

    xp = x_prompt.reshape(bsz * seq, d)
    tm = _row_tile(seq, 512)
    lanes_t = lambda rows: _pad_rows(rows[:N_META].T, LANES, 1)
    (qa_p, kt_p, ka_p, vt_p, va_p, xc_p, mv_p, mo_p, ga_p, gb_p, gt_p, _, ct_p, lt_p) = inproj(
        xp, fc_m[None, N_META - 1:N_META], ct_m, tm, seq // tm,
        meta_t=(lanes_t(kf_m), lanes_t(vf_m), lanes_t(gt_m[:, :FOX_HEADS])))
    r3 = lambda a: a.reshape(bsz, seq, a.shape[-1])
    hm_p, c_p, n_p, m_p = mlstm(r3(xc_p), r3(mv_p), r3(mo_p), r3(gt_p), c_m, n_m, m_m,
                                tc=_row_tile(seq, 256), nb=_row_tile(bsz, 2))
    a_p = _fox(r3(qa_p), _pad_rows(ka_m, LANES, 0)[None], _pad_rows(va_m, LANES, 0)[None], r3(ka_p), r3(va_p),
               tq=_row_tile(seq, 2048), tr=_row_tile(seq, 512), pblk=LANES, pvalid=N_META)

    kc, vc, f_c = _cacheprep(cache_fox_k[0].reshape(dbsz, past * FOX_HEADS, FOX_HEAD_DIM),
                             cache_fox_v[0].reshape(dbsz, past * FOX_HEADS, FOX_HEAD_DIM),
                             _pad_rows(cache_fox_logf[0].astype(F32), LANES, 2), _row_tile(past, 512))
    xs = x_sample.reshape(dbsz * dseq, d)
    (qa_s, kf_s, ka_s, vf_s, va_s, xc_s, mv_s, mo_s, ga_s, gb_s, gt_s, _, ct_s) = inproj(
        xs, f_c, conv_hist(state_mlstm_conv[0]), dseq, 1)
    r3s = lambda a: a.reshape(dbsz, dseq, a.shape[-1])
    hm_s, c_s, n_s, m_s = mlstm(r3s(xc_s), r3s(mv_s), r3s(mo_s), r3s(gt_s), state_mlstm_C[0], _pad_rows(state_mlstm_n[0], 8, 1), m_rows(state_mlstm_m[0]),
                                tc=dseq, nb=_row_tile(dbsz, 4))
    a_s = _fox(r3s(qa_s), kc, vc, r3s(ka_s), r3s(va_s), tq=dseq, tr=dseq, pblk=_row_tile(past, 512), hps=FOX_HEADS)

    wfp, wmp, wo = w_fox_proj[0].astype(BF16), w_ml_proj[0].astype(BF16), w_out[0].astype(BF16)
    wgu, wd = w_gate_up[0].astype(BF16), w_down[0].astype(BF16)
    g1, g2, g3 = g_post_mix[0][None, :], g_pre_ffn[0][None, :], g_post_ffn[0][None, :]
    post = lambda a, hm, ga, gb, x, tm: _post(a, hm, ga, gb, x, wfp, wmp, wo, wgu, wd, g1, g2, g3, tm)
    y_p = post(a_p.reshape(bsz * seq, fw), hm_p.reshape(bsz * seq, mw), ga_p, gb_p, xp, _row_tile(bsz * seq, 512))
    y_s = post(a_s.reshape(dbsz * dseq, fw), hm_s.reshape(dbsz * dseq, mw), ga_s, gb_s, xs,
               _row_tile(dbsz * dseq, 512))

    def tokens_major(a):
        return a.reshape(bsz, FOX_HEADS, FOX_HEAD_DIM, a.shape[-1]).transpose(0, 3, 1, 2)[None]

    heads = lambda a: a.reshape(a.shape[:-1] + (FOX_HEADS, FOX_HEAD_DIM))
    tail = CONV_WIDTH - 1
    return (y_p.reshape(bsz, seq, d), y_s.reshape(dbsz, dseq, d),
            tokens_major(kt_p), tokens_major(vt_p), lt_p.transpose(0, 2, 1)[None],
            c_p[None], n_p[None, :, :ML_HEADS], m_p[None, :, :ML_HEADS, 0], ct_p[None, :, 8 - tail:],
            heads(r3s(kf_s))[None], heads(r3s(vf_s))[None], r3s(gt_s)[None, ..., :FOX_HEADS],
            c_s[None], n_s[None, :, :ML_HEADS], m_s[None, :, :ML_HEADS, 0], ct_s[None, :, 8 - tail:])
```

```python
import functools

import jax
import jax.numpy as jnp
from jax import lax
from jax.experimental import pallas as pl
from jax.experimental.pallas import tpu as pltpu

F32 = jnp.float32
BF16 = jnp.bfloat16

N_META = 16
FOX_HEADS = 8
FOX_HEAD_DIM = 64
FOX_WIDTH = FOX_HEADS * FOX_HEAD_DIM
ML_HEADS = 4
ML_HEAD_DIM = 128
ML_WIDTH = ML_HEADS * ML_HEAD_DIM
CONV_WIDTH = 4
RMS_EPS = 1e-6
PAD_LOG_GATE = -1e30
NEG_BIG = -1e30

LANES = 128
GATE_FOX_F = 0
GATE_ML_I = 8
GATE_ML_F = 12
GATE_ML_F2 = 16

AUG_STEP = FOX_HEADS
AUG_QF = FOX_HEAD_DIM
AUG_KF = FOX_HEAD_DIM + 24

VMEM_LIMIT = 56 * 1024 * 1024
FOX_WIDE_KEYS = 1024
FOX_MAX_STATIC_BLOCKS = 4
MXU_TILE = 256
FF_CHUNKS = 2


def _const_spec(shape):
    nd = len(shape)
    return pl.BlockSpec(shape, lambda *_: (0,) * nd, pipeline_mode=pl.Buffered(1))


def _rms(x, g):
    return x * lax.rsqrt(jnp.mean(x * x, axis=-1, keepdims=True) + RMS_EPS) * g


def _split3(x):
    hi = x.astype(BF16).astype(F32)
    r = x - hi
    mid = r.astype(BF16).astype(F32)
    lo = (r - mid).astype(BF16).astype(F32)
    return hi, mid, lo


def _cumsum_rows(lg, cur):
    tm = lg.shape[0]
    blk = min(tm, LANES)
    r = lax.broadcasted_iota(jnp.int32, (blk, blk), 0)
    c = lax.broadcasted_iota(jnp.int32, (blk, blk), 1)
    tril = (c <= r).astype(BF16)
    outs = []
    for s in range(tm // blk):
        hi, mid, lo = _split3(lg[s * blk:(s + 1) * blk])
        f = (jnp.dot(tril, lo.astype(BF16), preferred_element_type=F32)
             + jnp.dot(tril, mid.astype(BF16), preferred_element_type=F32)
             + jnp.dot(tril, hi.astype(BF16), preferred_element_type=F32)) + cur
        outs.append(f)
        cur = f[blk - 1:blk]
    return (outs[0] if len(outs) == 1 else jnp.concatenate(outs, axis=0)), cur


def _store_attention_operands(qn, kn, v, fcol, q_ref, k_ref, v_ref, fillers=()):
    tm = fcol.shape[0]
    lane = lax.broadcasted_iota(jnp.int32, (tm, LANES), 1)
    data = lane < FOX_HEAD_DIM
    hi, mid, lo = _split3(fcol)
    comb = jnp.where(lane < AUG_STEP, hi, jnp.where(lane < 2 * AUG_STEP, pltpu.roll(mid, AUG_STEP, 1),
                                                     pltpu.roll(lo, 2 * AUG_STEP, 1)))
    ncomb = -comb
    is_qf = (lane == AUG_QF) | (lane == AUG_QF + AUG_STEP) | (lane == AUG_QF + 2 * AUG_STEP)
    is_kf = (lane == AUG_KF) | (lane == AUG_KF + AUG_STEP) | (lane == AUG_KF + 2 * AUG_STEP)
    base_q = jnp.where(is_kf, 1.0, 0.0)
    base_k = jnp.where(is_qf, 1.0, 0.0)
    base_v = jnp.where(lane == FOX_HEAD_DIM, 1.0, 0.0)
    fillers = list(fillers)
    for h in range(FOX_HEADS):
        t, odd = divmod(h, 2)

        def head_tile(z):
            x = z[:, t * LANES:(t + 1) * LANES]
            return pltpu.roll(x, FOX_HEAD_DIM, 1) if odd else x

        hs = slice(h * LANES, (h + 1) * LANES)
        if q_ref is not None:
            aug = jnp.where(is_qf, pltpu.roll(comb, AUG_QF - h, 1), base_q)
            q_ref[:, hs] = jnp.where(data, head_tile(qn), aug).astype(BF16)
        aug = jnp.where(is_kf, pltpu.roll(ncomb, AUG_KF - h, 1), base_k)
        k_ref[:, hs] = jnp.where(data, head_tile(kn), aug).astype(BF16)
        v_ref[:, hs] = jnp.where(data, head_tile(v), base_v).astype(BF16)
        if fillers:
            fillers.pop(0)()
    for f in fillers:
        f()


def _inproj_kernel(*refs, d_model, nt, kv_t, conv_valid):
    (x_ref, gpre_ref, w_ref, b_ref, gq_ref, gk_ref, hs_ref, c0_ref, cp_ref, wc_ref, bc_ref), rest = refs[:11], refs[11:]
    if kv_t:
        kmeta_ref, vmeta_ref, lmeta_ref = rest[:3]
        (q_ref, kf_ref, kb_ref, vf_ref, vb_ref, xc_ref, mv_ref, mo_ref, ga_ref, gb_ref, gt_ref, fc_ref, ct_ref,
         lt_ref, carry, xbuf, kcar, vcar, lcar) = rest[3:]
    else:
        (q_ref, kf_ref, kb_ref, vf_ref, vb_ref, xc_ref, mv_ref, mo_ref, ga_ref, gb_ref, gt_ref, fc_ref, ct_ref,
         carry, xbuf) = rest
    fw, mw = FOX_WIDTH, ML_WIDTH
    offs = [0, fw, 2 * fw, 3 * fw, 3 * fw + mw, 3 * fw + 2 * mw, 3 * fw + 3 * mw,
            3 * fw + 3 * mw + d_model, 3 * fw + 3 * mw + 2 * d_model, 3 * fw + 3 * mw + 2 * d_model + LANES]
    r = pl.program_id(1)

    @pl.when(r == 0)
    def _():
        carry[...] = c0_ref[...]
        xbuf[0:8, :] = cp_ref[...]
        if kv_t:
            kcar[...] = kmeta_ref[...]
            vcar[...] = vmeta_ref[...]
            lcar[...] = lmeta_ref[...]

    def store_shifted(xt, car, out_ref):
        sh = pltpu.roll(xt, N_META, 1)
        lane = lax.broadcasted_iota(jnp.int32, (xt.shape[0], LANES), 1)
        out_ref[:, :LANES] = jnp.where(lane < N_META, car[...], sh[:, :LANES])
        out_ref[:, LANES:] = sh[:, LANES:]
        car[...] = sh[:, :LANES]

    @pl.when(r < nt)
    def _():
        h = _rms(x_ref[...], gpre_ref[...]).astype(BF16)

        def seg(i):
            a, b = offs[i], offs[i + 1]
            return jnp.dot(h, w_ref[:, a:b], preferred_element_type=F32) + b_ref[:, a:b]

        def headnorm(z, g):
            ms = jnp.dot((z * z).astype(BF16), hs_ref[...], preferred_element_type=F32) * (1.0 / FOX_HEAD_DIM)
            return z * lax.rsqrt(ms + RMS_EPS) * g

        g = seg(8)
        ga_ref[...] = seg(6).astype(BF16)
        lane = lax.broadcasted_iota(jnp.int32, g.shape, 1)
        raw = (lane >= GATE_ML_I) & (lane < GATE_ML_F)
        gt = jnp.where(raw, g, jax.nn.log_sigmoid(g))
        gt_ref[...] = gt
        fcol, cur = _cumsum_rows(gt, carry[...])
        carry[...] = cur
        fc_ref[...] = fcol

        v = seg(2)
        q = headnorm(seg(0), gq_ref[...]) * (FOX_HEAD_DIM ** -0.5)
        k = headnorm(seg(1), gk_ref[...])
        def store_kv():
            if kv_t:
                store_shifted(k.T, kcar, kf_ref)
                store_shifted(v.T, vcar, vf_ref)
                store_shifted(gt.T[:FOX_HEADS], lcar, lt_ref)
            else:
                kf_ref[...] = k
                vf_ref[...] = v

        def store_xc():
            tm = x_ref.shape[0]
            xbuf[8:8 + tm, :] = seg(3)
            xc = bc_ref[...]
            for i in range(CONV_WIDTH):
                xc = xc + xbuf[5 + i:5 + i + tm, :] * wc_ref[i:i + 1, :]
            xc_ref[...] = (xc * jax.nn.sigmoid(xc)).astype(BF16)
            ct_ref[...] = xbuf[conv_valid:conv_valid + 8, :]
            xbuf[0:8, :] = xbuf[tm:tm + 8, :]

        def store_mv():
            mv_ref[...] = seg(4).astype(BF16)

        def store_mo():
            mo_ref[...] = jax.nn.sigmoid(seg(5)).astype(BF16)

        def store_gb():
            gb_ref[...] = seg(7).astype(BF16)

        _store_attention_operands(q, k, v, fcol, q_ref, kb_ref, vb_ref,
                                  fillers=(store_xc, store_mv, store_mo, store_gb, store_kv))

    if kv_t:
        @pl.when(r == nt)
        def _():
            kf_ref[:, :LANES] = kcar[...]
            vf_ref[:, :LANES] = vcar[...]
            lt_ref[:, :LANES] = lcar[...]


def _inproj(x, gpre, w, b, gq, gk, hs, c0, cp, wc, bc, tm, nt, meta_t=None, conv_valid=None):
    n, d = x.shape
    nb = n // (nt * tm)
    dinp = w.shape[1]
    fw, mw = FOX_WIDTH, ML_WIDTH
    aw = FOX_HEADS * LANES
    kv_t = meta_t is not None
    tile = lambda b, r: b * nt + jnp.minimum(r, nt - 1)
    row = lambda c: pl.BlockSpec((tm, c), lambda b, r: (tile(b, r), 0))
    cb = (lambda b: b) if c0.shape[0] > 1 else (lambda b: 0)
    pb = (lambda b: b) if cp.shape[0] > 1 else (lambda b: 0)
    lp = N_META + nt * tm
    tmin = lambda c: pl.BlockSpec((None, c, tm), lambda b, r: (b, 0, r))
    flat = lambda c, t: (row(c), jax.ShapeDtypeStruct((n, c), t))
    tok_minor = lambda c: (tmin(c), jax.ShapeDtypeStruct((nb, c, lp), F32))
    outs = [flat(aw, BF16), tok_minor(fw) if kv_t else flat(fw, F32), flat(aw, BF16),
            tok_minor(fw) if kv_t else flat(fw, F32), flat(aw, BF16), flat(mw, BF16), flat(mw, BF16), flat(mw, BF16),
            flat(d, BF16), flat(d, BF16), flat(LANES, F32), flat(LANES, F32),
            (pl.BlockSpec((None, 8, mw), lambda b, r: (b, 0, 0)), jax.ShapeDtypeStruct((nb, 8, mw), F32))]
    in_specs = [row(d), _const_spec((1, d)), _const_spec((d, dinp)), _const_spec((1, dinp)),
                _const_spec((1, fw)), _const_spec((1, fw)), _const_spec((fw, fw)),
                pl.BlockSpec((None, 1, LANES), lambda b, r: (cb(b), 0, 0)),
                pl.BlockSpec((None, 8, mw), lambda b, r: (pb(b), 0, 0)),
                _const_spec((CONV_WIDTH, mw)), _const_spec((1, mw))]
    scratch = [pltpu.VMEM((1, LANES), F32), pltpu.VMEM((8 + tm, mw), F32)]
    args = [x, gpre, w, b, gq, gk, hs, c0, cp, wc, bc]
    if kv_t:
        outs.append(tok_minor(FOX_HEADS))
        in_specs += [_const_spec((fw, LANES)), _const_spec((fw, LANES)), _const_spec((FOX_HEADS, LANES))]
        scratch += [pltpu.VMEM((fw, LANES), F32), pltpu.VMEM((fw, LANES), F32), pltpu.VMEM((FOX_HEADS, LANES), F32)]
        args += list(meta_t)
    return pl.pallas_call(
        functools.partial(_inproj_kernel, d_model=d, nt=nt, kv_t=kv_t,
                          conv_valid=tm if conv_valid is None else conv_valid),
        grid=(nb, nt + (1 if kv_t else 0)),
        in_specs=in_specs,
        out_specs=[o[0] for o in outs],
        out_shape=[o[1] for o in outs],
        scratch_shapes=scratch,
        compiler_params=pltpu.CompilerParams(dimension_semantics=("arbitrary", "arbitrary"),
                                             vmem_limit_bytes=VMEM_LIMIT),
        name="inproj",
    )(*args)


def _cacheprep_kernel(k_ref, v_ref, lf_ref, ka_ref, va_ref, fl_ref, carry):
    @pl.when(pl.program_id(1) == 0)
    def _():
        carry[...] = jnp.zeros_like(carry)

    fcol, cur = _cumsum_rows(lf_ref[...], carry[...])
    carry[...] = cur
    fl_ref[...] = cur
    _store_attention_operands(None, k_ref[...], v_ref[...], fcol, None, ka_ref, va_ref)


def _cacheprep(ck, cv, clf, tp):
    bsz, plen, _ = ck.shape
    aw = FOX_HEADS * LANES
    return pl.pallas_call(
        _cacheprep_kernel,
        grid=(bsz, plen // tp),
        in_specs=[pl.BlockSpec((None, tp, FOX_WIDTH), lambda b, i: (b, i, 0)),
                  pl.BlockSpec((None, tp, FOX_WIDTH), lambda b, i: (b, i, 0)),
                  pl.BlockSpec((None, tp, LANES), lambda b, i: (b, i, 0))],
        out_specs=[pl.BlockSpec((None, tp, aw), lambda b, i: (b, i, 0)),
                   pl.BlockSpec((None, tp, aw), lambda b, i: (b, i, 0)),
                   pl.BlockSpec((None, 1, LANES), lambda b, i: (b, 0, 0))],
        out_shape=[jax.ShapeDtypeStruct((bsz, plen, aw), BF16), jax.ShapeDtypeStruct((bsz, plen, aw), BF16),
                   jax.ShapeDtypeStruct((bsz, 1, LANES), F32)],
        scratch_shapes=[pltpu.VMEM((1, LANES), F32)],
        compiler_params=pltpu.CompilerParams(dimension_semantics=("arbitrary", "arbitrary")),
        name="cache_prep",
    )(ck, cv, clf)


def _fox_kernel(q_ref, kp_ref, vp_ref, k_ref, v_ref, o_ref, *, tq, tr, tkb, nq, hps, pblk, n_pblk, pvalid):
    i = pl.program_id(2)
    nr = tq // tr
    chains = [(hh, r) for hh in range(hps) for r in range(nr)]
    qs = [q_ref[r * tr:(r + 1) * tr, hh * LANES:(hh + 1) * LANES] for hh, r in chains]

    def upd1(n, state, kb, vb, mask=None):
        m, acc = state
        hh = chains[n][0]
        hs = slice(hh * LANES, (hh + 1) * LANES)
        s = lax.dot_general(qs[n], kb[:, hs], (((1,), (1,)), ((), ())), preferred_element_type=F32)
        if mask is not None:
            cut = s.shape[1] - mask.shape[1]
            tail = jnp.where(mask, s[:, cut:], NEG_BIG)
            s = tail if cut == 0 else jnp.concatenate([s[:, :cut], tail], axis=1)
        m_new = jnp.maximum(m, jnp.max(s, axis=-1, keepdims=True))
        alpha = jnp.exp(m - m_new)
        pm = jnp.exp(s - m_new)
        acc = alpha * acc + jnp.dot(pm.astype(BF16), vb[:, hs], preferred_element_type=F32)
        return m_new, acc

    def upd(carry, kb, vb):
        return tuple(upd1(n, carry[n], kb, vb) for n in range(len(chains)))

    def start():
        carry = tuple((jnp.full((tr, 1), NEG_BIG, F32), jnp.zeros((tr, LANES), F32)) for _ in chains)
        if pvalid is None:
            for jb in range(n_pblk):
                sl = slice(jb * pblk, (jb + 1) * pblk)
                carry = upd(carry, kp_ref[sl, :], vp_ref[sl, :])
        return carry

    def finish(carry, off):
        tail_w = tr + (LANES if pvalid is not None else 0)
        rj = lax.broadcasted_iota(jnp.int32, (tr, tail_w), 0)
        cj = lax.broadcasted_iota(jnp.int32, (tr, tail_w), 1)
        mask = cj <= rj
        if pvalid is not None:
            mask = (mask | (cj >= tr)) & (cj < tr + pvalid)
        carry = list(carry)
        for r in reversed(range(nr)):
            width = (r + 1) * tr
            kb = k_ref[pl.ds(off, width), :]
            vb = v_ref[pl.ds(off, width), :]
            if pvalid is not None:
                kb = jnp.concatenate([kb, kp_ref[...]], axis=0)
                vb = jnp.concatenate([vb, vp_ref[...]], axis=0)
            for n, (hh, rr) in enumerate(chains):
                if rr == r:
                    carry[n] = upd1(n, carry[n], kb, vb, mask=mask)
        lane = lax.broadcasted_iota(jnp.int32, (tr, LANES), 1)
        outs = []
        for n in range(len(chains)):
            acc = carry[n][1]
            l = jnp.sum(jnp.where(lane == FOX_HEAD_DIM, acc, 0.0), axis=-1, keepdims=True)
            outs.append(acc / l)
        for hp in range(hps // 2):
            for r in range(nr):
                even, odd = outs[2 * hp * nr + r], outs[(2 * hp + 1) * nr + r]
                o_ref[r * tr:(r + 1) * tr, hp * LANES:(hp + 1) * LANES] = jnp.where(
                    lane < FOX_HEAD_DIM, even, pltpu.roll(odd, FOX_HEAD_DIM, 1)).astype(o_ref.dtype)

    if tq % tkb == 0 and nq <= FOX_MAX_STATIC_BLOCKS:
        for qi in range(nq):
            @pl.when(i == qi)
            def _():
                carry = start()
                for j in range(qi * tq // tkb):
                    carry = upd(carry, k_ref[j * tkb:(j + 1) * tkb, :], v_ref[j * tkb:(j + 1) * tkb, :])
                finish(carry, qi * tq)
    else:
        tkw = max(tkb, tq)

        def body_wide(j, carry):
            off = pl.multiple_of(j * tkw, tkw)
            return upd(carry, k_ref[pl.ds(off, tkw), :], v_ref[pl.ds(off, tkw), :])

        def body(j, carry):
            off = pl.multiple_of(j * tq, tq)
            return upd(carry, k_ref[pl.ds(off, tq), :], v_ref[pl.ds(off, tq), :])

        n_wide = i // (tkw // tq)
        carry = lax.fori_loop(0, n_wide, body_wide, start())
        if tkw > tq:
            carry = lax.fori_loop(n_wide * (tkw // tq), i, body, carry)
        finish(carry, pl.multiple_of(i * tq, tq))


def _fox(q, kp, vp, k, v, tq, tr, pblk, pvalid=None, hps=2):
    bsz, t, _ = q.shape
    bp, plen, _ = kp.shape
    nq = t // tq
    n_pblk = plen // pblk
    pw = hps * LANES
    pb = (lambda b: b) if bp == bsz else (lambda b: 0)
    return pl.pallas_call(
        functools.partial(_fox_kernel, tq=tq, tr=tr, tkb=_row_tile(t, FOX_WIDE_KEYS), nq=nq, hps=hps, pblk=pblk,
                          n_pblk=n_pblk, pvalid=pvalid),
        grid=(bsz, FOX_HEADS // hps, nq),
        in_specs=[pl.BlockSpec((None, tq, pw), lambda b, p, i: (b, i, p)),
                  pl.BlockSpec((None, plen, pw), lambda b, p, i: (pb(b), 0, p)),
                  pl.BlockSpec((None, plen, pw), lambda b, p, i: (pb(b), 0, p)),
                  pl.BlockSpec((None, t, pw), lambda b, p, i: (b, 0, p)),
                  pl.BlockSpec((None, t, pw), lambda b, p, i: (b, 0, p))],
        out_specs=pl.BlockSpec((None, tq, hps * FOX_HEAD_DIM), lambda b, p, i: (b, i, p)),
        out_shape=jax.ShapeDtypeStruct((bsz, t, FOX_WIDTH), BF16),
        compiler_params=pltpu.CompilerParams(dimension_semantics=("arbitrary", "arbitrary", "arbitrary"),
                                             vmem_limit_bytes=VMEM_LIMIT),
        name="fox_attention",
    )(q, kp, vp, k, v)


def _mlstm_kernel(xc_ref, mv_ref, mo_ref, gt_ref, c0_ref, n0_ref, m0_ref, wq_ref, wk_ref, gn_ref, sk_ref,
                  hm_ref, cout_ref, nout_ref, mout_ref,
                  c_scr, n_scr, m_scr, *, tc, n_valid, nb, shared_state):
    c = pl.program_id(1)

    @pl.when(c == 0)
    def _():
        for bi in range(nb):
            si = 0 if shared_state else bi
            c_scr[bi] = c0_ref[si]
            n_scr[bi] = n0_ref[si]
            m_scr[bi] = m0_ref[si]

    for bi in range(nb):
        _mlstm_chunk(xc_ref.at[bi], mv_ref.at[bi], mo_ref.at[bi], gt_ref.at[bi], wq_ref, wk_ref,
                     gn_ref, sk_ref, hm_ref.at[bi], c_scr.at[bi], n_scr.at[bi], m_scr.at[bi],
                     tc=tc, n_valid=n_valid)

    @pl.when(c == pl.num_programs(1) - 1)
    def _():
        cout_ref[...] = c_scr[...]
        nout_ref[...] = n_scr[...]
        mout_ref[...] = m_scr[...]


def _mlstm_chunk(xc_ref, mv_ref, mo_ref, gt_ref, wq_ref, wk_ref, gn_ref, sk_ref, hm_ref,
                 c_scr, n_scr, m_scr, *, tc, n_valid):
    gt = gt_ref[...]
    lane = lax.broadcasted_iota(jnp.int32, (tc, LANES), 1)
    if n_valid is not None:
        rowi = lax.broadcasted_iota(jnp.int32, (tc, LANES), 0)
        is_i = (lane >= GATE_ML_I) & (lane < GATE_ML_F)
        gt = jnp.where(rowi < n_valid, gt, jnp.where(is_i, PAD_LOG_GATE, 0.0))
    ri = lax.broadcasted_iota(jnp.int32, (tc, tc), 0)
    ci = lax.broadcasted_iota(jnp.int32, (tc, tc), 1)
    causal = ci <= ri
    b_all = _cumsum_rows(gt, jnp.zeros((1, LANES), F32))[0]
    r_t = (pltpu.roll(gt, GATE_ML_F - GATE_ML_I, 1) - b_all).T
    heads = range(ML_HEADS)
    hsl = [slice(h * ML_HEAD_DIM, (h + 1) * ML_HEAD_DIM) for h in heads]
    b_col = [b_all[:, GATE_ML_F + h:GATE_ML_F + h + 1] for h in heads]
    ig_col = [gt[:, GATE_ML_I + h:GATE_ML_I + h + 1] for h in heads]
    m_prev = [m_scr[h:h + 1, 0:1] for h in heads]
    dmat = [jnp.where(causal, b_col[h] + r_t[GATE_ML_F + h:GATE_ML_F + h + 1, :], NEG_BIG) for h in heads]
    m_inter = [b_col[h] + m_prev[h] for h in heads]
    m_t = [jnp.maximum(m_inter[h], jnp.max(dmat[h], axis=-1, keepdims=True)) for h in heads]
    w_inter = [jnp.exp(m_inter[h] - m_t[h]) for h in heads]

    xhb = [xc_ref[:, hsl[h]] for h in heads]
    q = [jnp.dot(xhb[h], wq_ref[h], preferred_element_type=F32) for h in heads]
    k = [jnp.dot(xhb[h], wk_ref[h], preferred_element_type=F32) * (ML_HEAD_DIM ** -0.5) for h in heads]
    qb = [x.astype(BF16) for x in q]
    kb = [x.astype(BF16) for x in k]
    v = [mv_ref[:, hsl[h]] for h in heads]
    smat = [lax.dot_general(qb[h], kb[h], (((1,), (1,)), ((), ())), preferred_element_type=F32) for h in heads]
    amat = [jnp.exp(dmat[h] - m_t[h]) * smat[h] for h in heads]
    ch = [c_scr[h] for h in heads]
    nh = [n_scr[h:h + 1, :] for h in heads]
    num = [jnp.dot(amat[h].astype(BF16), v[h], preferred_element_type=F32)
           + w_inter[h] * jnp.dot(qb[h], ch[h].astype(BF16), preferred_element_type=F32) for h in heads]
    den = [jnp.sum(amat[h], axis=-1, keepdims=True) + w_inter[h] * jnp.sum(q[h] * nh[h], axis=-1, keepdims=True)
           for h in heads]
    hcell = [num[h] / jnp.maximum(jnp.abs(den[h]), jnp.exp(-m_t[h])) for h in heads]

    b_last = [b_col[h][tc - 1:tc, :] for h in heads]
    g = [b_last[h] - b_col[h] + ig_col[h] for h in heads]
    m_new = [jnp.maximum(b_last[h] + m_prev[h], jnp.max(g[h], axis=0, keepdims=True)) for h in heads]
    decay = [jnp.exp(b_last[h] + m_prev[h] - m_new[h]) for h in heads]
    kw = [k[h] * jnp.exp(g[h] - m_new[h]) for h in heads]
    for h in heads:
        c_scr[h] = decay[h] * ch[h] + lax.dot_general(kw[h].astype(BF16), v[h], (((0,), (0,)), ((), ())),
                                                       preferred_element_type=F32)
        n_scr[h:h + 1, :] = decay[h] * nh[h] + jnp.sum(kw[h], axis=0, keepdims=True)
        m_scr[h:h + 1, :] = jnp.broadcast_to(m_new[h], (1, LANES))
        y = mo_ref[:, hsl[h]].astype(F32) * hcell[h]
        hm_ref[:, hsl[h]] = (_rms(y, gn_ref[:, hsl[h]]) + sk_ref[:, hsl[h]] * xhb[h].astype(F32)).astype(hm_ref.dtype)


def _mlstm(xc, mv, mo, gt, c0, n0, m0, wq, wk, gn, sk, tc, nb=1, n_valid=None):
    bsz, t, _ = xc.shape
    shared = c0.shape[0] != bsz
    sn = 1 if shared else nb
    sb = (lambda b: 0) if shared else (lambda b: b)
    tok = lambda w: pl.BlockSpec((nb, tc, w), lambda b, c: (b, c, 0))
    hd = ML_HEAD_DIM
    st_specs = [pl.BlockSpec((nb, ML_HEADS, hd, hd), lambda b, c: (b, 0, 0, 0)),
                pl.BlockSpec((nb, 8, LANES), lambda b, c: (b, 0, 0)),
                pl.BlockSpec((nb, 8, LANES), lambda b, c: (b, 0, 0))]
    return pl.pallas_call(
        functools.partial(_mlstm_kernel, tc=tc, n_valid=n_valid, nb=nb, shared_state=shared),
        grid=(bsz // nb, t // tc),
        in_specs=[tok(ML_WIDTH), tok(ML_WIDTH), tok(ML_WIDTH), tok(LANES),
                  pl.BlockSpec((sn, ML_HEADS, hd, hd), lambda b, c: (sb(b), 0, 0, 0)),
                  pl.BlockSpec((sn, 8, LANES), lambda b, c: (sb(b), 0, 0)),
                  pl.BlockSpec((sn, 8, LANES), lambda b, c: (sb(b), 0, 0)),
                  _const_spec((ML_HEADS, hd, hd)), _const_spec((ML_HEADS, hd, hd)),
                  _const_spec((1, ML_WIDTH)), _const_spec((1, ML_WIDTH))],
        out_specs=[tok(ML_WIDTH)] + st_specs,
        out_shape=[jax.ShapeDtypeStruct((bsz, t, ML_WIDTH), BF16),
                   jax.ShapeDtypeStruct((bsz, ML_HEADS, hd, hd), F32),
                   jax.ShapeDtypeStruct((bsz, 8, LANES), F32),
                   jax.ShapeDtypeStruct((bsz, 8, LANES), F32)],
        scratch_shapes=[pltpu.VMEM((nb, ML_HEADS, hd, hd), F32),
                        pltpu.VMEM((nb, 8, LANES), F32), pltpu.VMEM((nb, 8, LANES), F32)],
        compiler_params=pltpu.CompilerParams(dimension_semantics=("arbitrary", "arbitrary")),
        name="mlstm",
    )(xc, mv, mo, gt, c0, n0, m0, wq, wk, gn, sk)


def _post_kernel(a_ref, hm_ref, ga_ref, gb_ref, x_ref, wfp_ref, wmp_ref, wo_ref, wgu_ref, wd_ref,
                 g1_ref, g2_ref, g3_ref, y_ref, *, d_ff, ff_cuts):
    ya = jnp.dot(a_ref[...], wfp_ref[...], preferred_element_type=F32)
    yb = jnp.dot(hm_ref[...], wmp_ref[...], preferred_element_type=F32)
    merged = jax.nn.sigmoid(ga_ref[...].astype(F32)) * ya + jax.nn.sigmoid(gb_ref[...].astype(F32)) * yb
    mix = jnp.dot(merged.astype(BF16), wo_ref[...], preferred_element_type=F32)
    x1 = x_ref[...] + _rms(mix, g1_ref[...])
    h2 = _rms(x1, g2_ref[...]).astype(BF16)
    f = None
    for lo, hi in zip(ff_cuts[:-1], ff_cuts[1:]):
        g = jnp.dot(h2, wgu_ref[:, lo:hi], preferred_element_type=F32)
        u = jnp.dot(h2, wgu_ref[:, d_ff + lo:d_ff + hi], preferred_element_type=F32)
        t = (g * jax.nn.sigmoid(g) * u).astype(BF16)
        fj = jnp.dot(t, wd_ref[lo:hi, :], preferred_element_type=F32)
        f = fj if f is None else f + fj
    y_ref[...] = x1 + _rms(f, g3_ref[...])


def _post(a, hm, ga, gb, x, wfp, wmp, wo, wgu, wd, g1, g2, g3, tm):
    n, d = x.shape
    d_ff = wd.shape[0]
    nmt = d_ff // MXU_TILE
    ff_cuts = tuple(MXU_TILE * ((nmt * j) // FF_CHUNKS) for j in range(FF_CHUNKS)) + (d_ff,) \
        if d_ff % MXU_TILE == 0 else (0, d_ff)
    row = lambda c: pl.BlockSpec((tm, c), lambda i: (i, 0))
    return pl.pallas_call(
        functools.partial(_post_kernel, d_ff=d_ff, ff_cuts=ff_cuts),
        grid=(n // tm,),
        in_specs=[row(FOX_WIDTH), row(ML_WIDTH), row(d), row(d), row(d),
                  _const_spec(wfp.shape), _const_spec(wmp.shape), _const_spec(wo.shape),
                  _const_spec(wgu.shape), _const_spec(wd.shape),
                  _const_spec((1, d)), _const_spec((1, d)), _const_spec((1, d))],
        out_specs=row(d),
        out_shape=jax.ShapeDtypeStruct((n, d), F32),
        compiler_params=pltpu.CompilerParams(dimension_semantics=("arbitrary",), vmem_limit_bytes=VMEM_LIMIT),
        name="post_mix_ffn",
    )(a, hm, ga, gb, x, wfp, wmp, wo, wgu, wd, g1, g2, g3)


def _pad_rows(a, rows, axis):
    pad = [(0, 0)] * a.ndim
    pad[axis] = (0, rows - a.shape[axis])
    return jnp.pad(a, pad)


def _row_tile(n, pref):
    t = min(pref, n)
    while n % t:
        t //= 2
    return t


def kernel(x_prompt, x_sample, cache_fox_k, cache_fox_v, cache_fox_logf, state_mlstm_C, state_mlstm_n,
           state_mlstm_m, state_mlstm_conv, meta_tokens, w_in, b_in, g_fq, g_fk, w_conv, b_conv, w_mq, w_mk,
           g_mnorm, skip_m, w_fox_proj, w_ml_proj, w_out, g_pre_mix, g_post_mix, g_pre_ffn, g_post_ffn,
           w_gate_up, w_down):
    assert w_in.shape[0] == 1, "single-layer trunk"
    bsz, seq, d = x_prompt.shape
    dbsz, dseq, _ = x_sample.shape
    past = cache_fox_k.shape[2]
    fw, mw = FOX_WIDTH, ML_WIDTH

    w, b = w_in[0], b_in[0]
    o_ff = 3 * fw
    o_mx = o_ff + FOX_HEADS
    o_mi = o_mx + 3 * mw
    o_mf = o_mi + ML_HEADS
    o_ga = o_mf + ML_HEADS

    def relayout(a):
        gates = jnp.concatenate([a[..., o_ff:o_mx], a[..., o_mi:o_mf], a[..., o_mf:o_ga], a[..., o_mf:o_ga]], axis=-1)
        gates = _pad_rows(gates, LANES, gates.ndim - 1)
        return jnp.concatenate([a[..., :o_ff], a[..., o_mx:o_mi], a[..., o_ga:], gates], axis=-1)

    w_p = relayout(w).astype(BF16)
    b_p = relayout(b)[None, :]
    hid = jnp.arange(fw) // FOX_HEAD_DIM
    hsum = (hid[:, None] == hid[None, :]).astype(BF16)
    gq = g_fq[0].reshape(1, fw)
    gk = g_fk[0].reshape(1, fw)
    gpre = g_pre_mix[0][None, :]
    wc, bc = w_conv[0], b_conv[0][None, :]
    inproj = lambda x, c0, cp, tm, nt, **kw: _inproj(x, gpre, w_p, b_p, gq, gk, hsum, c0, cp, wc, bc, tm, nt, **kw)

    wq, wk = w_mq[0].astype(BF16), w_mk[0].astype(BF16)
    gn, sk = g_mnorm[0].reshape(1, mw), skip_m[0][None, :]
    mlstm = lambda *a, **kw: _mlstm(*a, wq, wk, gn, sk, **kw)

    def conv_hist(rows):
        return jnp.pad(rows, ((0, 0), (8 - (CONV_WIDTH - 1), 0), (0, 0)))

    def m_rows(m):
        return jnp.broadcast_to(_pad_rows(m, 8, 1)[:, :, None], (m.shape[0], 8, LANES))

    mt = 64
    xm = _pad_rows(meta_tokens.astype(F32), mt, 0)
    (_, kf_m, ka_m, vf_m, va_m, xc_m, mv_m, mo_m, _, _, gt_m, fc_m, ct_m) = inproj(
        xm, jnp.zeros((1, 1, LANES), F32), jnp.zeros((1, 8, mw), F32), mt, 1, conv_valid=N_META)
    zc = jnp.zeros((1, ML_HEADS, ML_HEAD_DIM, ML_HEAD_DIM), F32)
    z8 = jnp.zeros((1, 8, LANES), F32)
    _, c_m, n_m, m_m = mlstm(xc_m[None], mv_m[None], mo_m[None], gt_m[None], zc, z8, z8, tc=mt, n_valid=N_META)

    xp = x_prompt.reshape(bsz * seq, d)
    tm = _row_tile(seq, 512)
    lanes_t = lambda rows: _pad_rows(rows[:N_META].T, LANES, 1)
    (qa_p, kt_p, ka_p, vt_p, va_p, xc_p, mv_p, mo_p, ga_p, gb_p, gt_p, _, ct_p, lt_p) = inproj(
        xp, fc_m[None, N_META - 1:N_META], ct_m, tm, seq // tm,
        meta_t=(lanes_t(kf_m), lanes_t(vf_m), lanes_t(gt_m[:, :FOX_HEADS])))
    r3 = lambda a: a.reshape(bsz, seq, a.shape[-1])
    hm_p, c_p, n_p, m_p = mlstm(r3(xc_p), r3(mv_p), r3(mo_p), r3(gt_p), c_m, n_m, m_m,
                                tc=_row_tile(seq, 256), nb=_row_tile(bsz, 4))
    a_p = _fox(r3(qa_p), _pad_rows(ka_m, LANES, 0)[None], _pad_rows(va_m, LANES, 0)[None], r3(ka_p), r3(va_p),
               tq=_row_tile(seq, 2048), tr=_row_tile(seq, 512), pblk=LANES, pvalid=N_META)

    kc, vc, f_c = _cacheprep(cache_fox_k[0].reshape(dbsz, past, fw), cache_fox_v[0].reshape(dbsz, past, fw),
                             _pad_rows(cache_fox_logf[0].astype(F32), LANES, 2), _row_tile(past, 512))
    xs = x_sample.reshape(dbsz * dseq, d)
    (qa_s, kf_s, ka_s, vf_s, va_s, xc_s, mv_s, mo_s, ga_s, gb_s, gt_s, _, ct_s) = inproj(
        xs, f_c, conv_hist(state_mlstm_conv[0]), dseq, 1)
    r3s = lambda a: a.reshape(dbsz, dseq, a.shape[-1])
    hm_s, c_s, n_s, m_s = mlstm(r3s(xc_s), r3s(mv_s), r3s(mo_s), r3s(gt_s), state_mlstm_C[0], _pad_rows(state_mlstm_n[0], 8, 1), m_rows(state_mlstm_m[0]),
                                tc=dseq, nb=_row_tile(dbsz, 4))
    a_s = _fox(r3s(qa_s), kc, vc, r3s(ka_s), r3s(va_s), tq=dseq, tr=dseq, pblk=_row_tile(past, 512), hps=FOX_HEADS)

    wfp, wmp, wo = w_fox_proj[0].astype(BF16), w_ml_proj[0].astype(BF16), w_out[0].astype(BF16)
    wgu, wd = w_gate_up[0].astype(BF16), w_down[0].astype(BF16)
    g1, g2, g3 = g_post_mix[0][None, :], g_pre_ffn[0][None, :], g_post_ffn[0][None, :]
    post = lambda a, hm, ga, gb, x, tm: _post(a, hm, ga, gb, x, wfp, wmp, wo, wgu, wd, g1, g2, g3, tm)
    y_p = post(a_p.reshape(bsz * seq, fw), hm_p.reshape(bsz * seq, mw), ga_p, gb_p, xp, _row_tile(bsz * seq, 512))
    y_s = post(a_s.reshape(dbsz * dseq, fw), hm_s.reshape(dbsz * dseq, mw), ga_s, gb_s, xs,
               _row_tile(dbsz * dseq, 512))

    def tokens_major(a):
        return a.reshape(bsz, FOX_HEADS, FOX_HEAD_DIM, a.shape[-1]).transpose(0, 3, 1, 2)[None]

    heads = lambda a: a.reshape(a.shape[:-1] + (FOX_HEADS, FOX_HEAD_DIM))
    tail = CONV_WIDTH - 1
    return (y_p.reshape(bsz, seq, d), y_s.reshape(dbsz, dseq, d),
            tokens_major(kt_p), tokens_major(vt_p), lt_p.transpose(0, 2, 1)[None],
            c_p[None], n_p[None, :, :ML_HEADS], m_p[None, :, :ML_HEADS, 0], ct_p[None, :, 8 - tail:],
            heads(r3s(kf_s))[None], heads(r3s(vf_s))[None], r3s(gt_s)[None, ..., :FOX_HEADS],
            c_s[None], n_s[None, :, :ML_HEADS], m_s[None, :, :ML_HEADS, 0], ct_s[None, :, 8 - tail:])
```

```python
import functools

import jax
import jax.numpy as jnp
from jax import lax
from jax.experimental import pallas as pl
from jax.experimental.pallas import tpu as pltpu

F32 = jnp.float32
BF16 = jnp.bfloat16

N_META = 16
FOX_HEADS = 8
FOX_HEAD_DIM = 64
FOX_WIDTH = FOX_HEADS * FOX_HEAD_DIM
ML_HEADS = 4
ML_HEAD_DIM = 128
ML_WIDTH = ML_HEADS * ML_HEAD_DIM
CONV_WIDTH = 4
RMS_EPS = 1e-6
PAD_LOG_GATE = -1e30
NEG_BIG = -1e30

LANES = 128
GATE_FOX_F = 0
GATE_ML_I = 8
GATE_ML_F = 12
GATE_ML_F2 = 16

AUG_STEP = FOX_HEADS
AUG_QF = FOX_HEAD_DIM
AUG_KF = FOX_HEAD_DIM + 24

VMEM_LIMIT = 56 * 1024 * 1024
FOX_WIDE_KEYS = 1024
FOX_MAX_STATIC_BLOCKS = 4
MXU_TILE = 256
FF_CHUNKS = 2


def _const_spec(shape):
    nd = len(shape)
    return pl.BlockSpec(shape, lambda *_: (0,) * nd, pipeline_mode=pl.Buffered(1))


def _rms(x, g):
    return x * lax.rsqrt(jnp.mean(x * x, axis=-1, keepdims=True) + RMS_EPS) * g


def _split3(x):
    hi = x.astype(BF16).astype(F32)
    r = x - hi
    mid = r.astype(BF16).astype(F32)
    lo = (r - mid).astype(BF16).astype(F32)
    return hi, mid, lo


def _cumsum_rows(lg, cur):
    tm = lg.shape[0]
    blk = min(tm, LANES)
    r = lax.broadcasted_iota(jnp.int32, (blk, blk), 0)
    c = lax.broadcasted_iota(jnp.int32, (blk, blk), 1)
    tril = (c <= r).astype(BF16)
    outs = []
    for s in range(tm // blk):
        hi, mid, lo = _split3(lg[s * blk:(s + 1) * blk])
        f = (jnp.dot(tril, lo.astype(BF16), preferred_element_type=F32)
             + jnp.dot(tril, mid.astype(BF16), preferred_element_type=F32)
             + jnp.dot(tril, hi.astype(BF16), preferred_element_type=F32)) + cur
        outs.append(f)
        cur = f[blk - 1:blk]
    return (outs[0] if len(outs) == 1 else jnp.concatenate(outs, axis=0)), cur


def _store_attention_operands(qn, kn, v, fcol, q_ref, k_ref, v_ref, fillers=()):
    tm = fcol.shape[0]
    lane = lax.broadcasted_iota(jnp.int32, (tm, LANES), 1)
    data = lane < FOX_HEAD_DIM
    hi, mid, lo = _split3(fcol)
    comb = jnp.where(lane < AUG_STEP, hi, jnp.where(lane < 2 * AUG_STEP, pltpu.roll(mid, AUG_STEP, 1),
                                                     pltpu.roll(lo, 2 * AUG_STEP, 1)))
    ncomb = -comb
    is_qf = (lane == AUG_QF) | (lane == AUG_QF + AUG_STEP) | (lane == AUG_QF + 2 * AUG_STEP)
    is_kf = (lane == AUG_KF) | (lane == AUG_KF + AUG_STEP) | (lane == AUG_KF + 2 * AUG_STEP)
    base_q = jnp.where(is_kf, 1.0, 0.0)
    base_k = jnp.where(is_qf, 1.0, 0.0)
    base_v = jnp.where(lane == FOX_HEAD_DIM, 1.0, 0.0)
    fillers = list(fillers)
    for h in range(FOX_HEADS):
        t, odd = divmod(h, 2)

        def head_tile(z):
            x = z[:, t * LANES:(t + 1) * LANES]
            return pltpu.roll(x, FOX_HEAD_DIM, 1) if odd else x

        hs = slice(h * LANES, (h + 1) * LANES)
        if q_ref is not None:
            aug = jnp.where(is_qf, pltpu.roll(comb, AUG_QF - h, 1), base_q)
            q_ref[:, hs] = jnp.where(data, head_tile(qn), aug).astype(BF16)
        aug = jnp.where(is_kf, pltpu.roll(ncomb, AUG_KF - h, 1), base_k)
        k_ref[:, hs] = jnp.where(data, head_tile(kn), aug).astype(BF16)
        v_ref[:, hs] = jnp.where(data, head_tile(v), base_v).astype(BF16)
        if fillers:
            fillers.pop(0)()
    for f in fillers:
        f()


def _inproj_kernel(*refs, d_model, nt, kv_t, conv_valid):
    (x_ref, gpre_ref, w_ref, b_ref, gq_ref, gk_ref, hs_ref, c0_ref, cp_ref, wc_ref, bc_ref), rest = refs[:11], refs[11:]
    if kv_t:
        kmeta_ref, vmeta_ref, lmeta_ref = rest[:3]
        (q_ref, kf_ref, kb_ref, vf_ref, vb_ref, xc_ref, mv_ref, mo_ref, ga_ref, gb_ref, gt_ref, fc_ref, ct_ref,
         lt_ref, carry, xbuf, kcar, vcar, lcar) = rest[3:]
    else:
        (q_ref, kf_ref, kb_ref, vf_ref, vb_ref, xc_ref, mv_ref, mo_ref, ga_ref, gb_ref, gt_ref, fc_ref, ct_ref,
         carry, xbuf) = rest
    fw, mw = FOX_WIDTH, ML_WIDTH
    offs = [0, fw, 2 * fw, 3 * fw, 3 * fw + mw, 3 * fw + 2 * mw, 3 * fw + 3 * mw,
            3 * fw + 3 * mw + d_model, 3 * fw + 3 * mw + 2 * d_model, 3 * fw + 3 * mw + 2 * d_model + LANES]
    r = pl.program_id(1)

    @pl.when(r == 0)
    def _():
        carry[...] = c0_ref[...]
        xbuf[0:8, :] = cp_ref[...]
        if kv_t:
            kcar[...] = kmeta_ref[...]
            vcar[...] = vmeta_ref[...]
            lcar[...] = lmeta_ref[...]

    def store_shifted(xt, car, out_ref):
        sh = pltpu.roll(xt, N_META, 1)
        lane = lax.broadcasted_iota(jnp.int32, (xt.shape[0], LANES), 1)
        out_ref[:, :LANES] = jnp.where(lane < N_META, car[...], sh[:, :LANES])
        out_ref[:, LANES:] = sh[:, LANES:]
        car[...] = sh[:, :LANES]

    @pl.when(r < nt)
    def _():
        h = _rms(x_ref[...], gpre_ref[...]).astype(BF16)

        def seg(i):
            a, b = offs[i], offs[i + 1]
            return jnp.dot(h, w_ref[:, a:b], preferred_element_type=F32) + b_ref[:, a:b]

        def headnorm(z, g):
            ms = jnp.dot((z * z).astype(BF16), hs_ref[...], preferred_element_type=F32) * (1.0 / FOX_HEAD_DIM)
            return z * lax.rsqrt(ms + RMS_EPS) * g

        g = seg(8)
        ga_ref[...] = seg(6).astype(BF16)
        lane = lax.broadcasted_iota(jnp.int32, g.shape, 1)
        raw = (lane >= GATE_ML_I) & (lane < GATE_ML_F)
        gt = jnp.where(raw, g, jax.nn.log_sigmoid(g))
        gt_ref[...] = gt
        fcol, cur = _cumsum_rows(gt, carry[...])
        carry[...] = cur
        fc_ref[...] = fcol

        v = seg(2)
        q = headnorm(seg(0), gq_ref[...]) * (FOX_HEAD_DIM ** -0.5)
        k = headnorm(seg(1), gk_ref[...])
        def store_kv():
            if kv_t:
                store_shifted(k.T, kcar, kf_ref)
                store_shifted(v.T, vcar, vf_ref)
                store_shifted(gt.T[:FOX_HEADS], lcar, lt_ref)
            else:
                kf_ref[...] = k
                vf_ref[...] = v

        def store_xc():
            tm = x_ref.shape[0]
            xbuf[8:8 + tm, :] = seg(3)
            xc = bc_ref[...]
            for i in range(CONV_WIDTH):
                xc = xc + xbuf[5 + i:5 + i + tm, :] * wc_ref[i:i + 1, :]
            xc_ref[...] = (xc * jax.nn.sigmoid(xc)).astype(BF16)
            ct_ref[...] = xbuf[conv_valid:conv_valid + 8, :]
            xbuf[0:8, :] = xbuf[tm:tm + 8, :]

        def store_mv():
            mv_ref[...] = seg(4).astype(BF16)

        def store_mo():
            mo_ref[...] = jax.nn.sigmoid(seg(5)).astype(BF16)

        def store_gb():
            gb_ref[...] = seg(7).astype(BF16)

        _store_attention_operands(q, k, v, fcol, q_ref, kb_ref, vb_ref,
                                  fillers=(store_xc, store_mv, store_mo, store_gb, store_kv))

    if kv_t:
        @pl.when(r == nt)
        def _():
            kf_ref[:, :LANES] = kcar[...]
            vf_ref[:, :LANES] = vcar[...]
            lt_ref[:, :LANES] = lcar[...]


def _inproj(x, gpre, w, b, gq, gk, hs, c0, cp, wc, bc, tm, nt, meta_t=None, conv_valid=None):
    n, d = x.shape
    nb = n // (nt * tm)
    dinp = w.shape[1]
    fw, mw = FOX_WIDTH, ML_WIDTH
    aw = FOX_HEADS * LANES
    kv_t = meta_t is not None
    tile = lambda b, r: b * nt + jnp.minimum(r, nt - 1)
    row = lambda c: pl.BlockSpec((tm, c), lambda b, r: (tile(b, r), 0))
    cb = (lambda b: b) if c0.shape[0] > 1 else (lambda b: 0)
    pb = (lambda b: b) if cp.shape[0] > 1 else (lambda b: 0)
    lp = N_META + nt * tm
    tmin = lambda c: pl.BlockSpec((None, c, tm), lambda b, r: (b, 0, r))
    flat = lambda c, t: (row(c), jax.ShapeDtypeStruct((n, c), t))
    tok_minor = lambda c: (tmin(c), jax.ShapeDtypeStruct((nb, c, lp), F32))
    outs = [flat(aw, BF16), tok_minor(fw) if kv_t else flat(fw, F32), flat(aw, BF16),
            tok_minor(fw) if kv_t else flat(fw, F32), flat(aw, BF16), flat(mw, BF16), flat(mw, BF16), flat(mw, BF16),
            flat(d, BF16), flat(d, BF16), flat(LANES, F32), flat(LANES, F32),
            (pl.BlockSpec((None, 8, mw), lambda b, r: (b, 0, 0)), jax.ShapeDtypeStruct((nb, 8, mw), F32))]
    in_specs = [row(d), _const_spec((1, d)), _const_spec((d, dinp)), _const_spec((1, dinp)),
                _const_spec((1, fw)), _const_spec((1, fw)), _const_spec((fw, fw)),
                pl.BlockSpec((None, 1, LANES), lambda b, r: (cb(b), 0, 0)),
                pl.BlockSpec((None, 8, mw), lambda b, r: (pb(b), 0, 0)),
                _const_spec((CONV_WIDTH, mw)), _const_spec((1, mw))]
    scratch = [pltpu.VMEM((1, LANES), F32), pltpu.VMEM((8 + tm, mw), F32)]
    args = [x, gpre, w, b, gq, gk, hs, c0, cp, wc, bc]
    if kv_t:
        outs.append(tok_minor(FOX_HEADS))
        in_specs += [_const_spec((fw, LANES)), _const_spec((fw, LANES)), _const_spec((FOX_HEADS, LANES))]
        scratch += [pltpu.VMEM((fw, LANES), F32), pltpu.VMEM((fw, LANES), F32), pltpu.VMEM((FOX_HEADS, LANES), F32)]
        args += list(meta_t)
    return pl.pallas_call(
        functools.partial(_inproj_kernel, d_model=d, nt=nt, kv_t=kv_t,
                          conv_valid=tm if conv_valid is None else conv_valid),
        grid=(nb, nt + (1 if kv_t else 0)),
        in_specs=in_specs,
        out_specs=[o[0] for o in outs],
        out_shape=[o[1] for o in outs],
        scratch_shapes=scratch,
        compiler_params=pltpu.CompilerParams(dimension_semantics=("arbitrary", "arbitrary"),
                                             vmem_limit_bytes=VMEM_LIMIT),
        name="inproj",
    )(*args)


def _cacheprep_kernel(k_ref, v_ref, lf_ref, ka_ref, va_ref, fl_ref, carry):
    @pl.when(pl.program_id(1) == 0)
    def _():
        carry[...] = jnp.zeros_like(carry)

    fcol, cur = _cumsum_rows(lf_ref[...], carry[...])
    carry[...] = cur
    fl_ref[...] = cur
    _store_attention_operands(None, k_ref[...].T, v_ref[...].T, fcol, None, ka_ref, va_ref)


def _cacheprep(ck, cv, clf, tp):
    bsz, _, plen = ck.shape
    aw = FOX_HEADS * LANES
    return pl.pallas_call(
        _cacheprep_kernel,
        grid=(bsz, plen // tp),
        in_specs=[pl.BlockSpec((None, FOX_WIDTH, tp), lambda b, i: (b, 0, i)),
                  pl.BlockSpec((None, FOX_WIDTH, tp), lambda b, i: (b, 0, i)),
                  pl.BlockSpec((None, tp, LANES), lambda b, i: (b, i, 0))],
        out_specs=[pl.BlockSpec((None, tp, aw), lambda b, i: (b, i, 0)),
                   pl.BlockSpec((None, tp, aw), lambda b, i: (b, i, 0)),
                   pl.BlockSpec((None, 1, LANES), lambda b, i: (b, 0, 0))],
        out_shape=[jax.ShapeDtypeStruct((bsz, plen, aw), BF16), jax.ShapeDtypeStruct((bsz, plen, aw), BF16),
                   jax.ShapeDtypeStruct((bsz, 1, LANES), F32)],
        scratch_shapes=[pltpu.VMEM((1, LANES), F32)],
        compiler_params=pltpu.CompilerParams(dimension_semantics=("arbitrary", "arbitrary")),
        name="cache_prep",
    )(ck, cv, clf)


def _fox_kernel(q_ref, kp_ref, vp_ref, k_ref, v_ref, o_ref, *, tq, tr, tkb, nq, hps, pblk, n_pblk, pvalid):
    i = pl.program_id(2)
    nr = tq // tr
    chains = [(hh, r) for hh in range(hps) for r in range(nr)]
    qs = [q_ref[r * tr:(r + 1) * tr, hh * LANES:(hh + 1) * LANES] for hh, r in chains]

    def upd1(n, state, kb, vb, mask=None):
        m, acc = state
        hh = chains[n][0]
        hs = slice(hh * LANES, (hh + 1) * LANES)
        s = lax.dot_general(qs[n], kb[:, hs], (((1,), (1,)), ((), ())), preferred_element_type=F32)
        if mask is not None:
            cut = s.shape[1] - mask.shape[1]
            tail = jnp.where(mask, s[:, cut:], NEG_BIG)
            s = tail if cut == 0 else jnp.concatenate([s[:, :cut], tail], axis=1)
        m_new = jnp.maximum(m, jnp.max(s, axis=-1, keepdims=True))
        alpha = jnp.exp(m - m_new)
        pm = jnp.exp(s - m_new)
        acc = alpha * acc + jnp.dot(pm.astype(BF16), vb[:, hs], preferred_element_type=F32)
        return m_new, acc

    def upd(carry, kb, vb):
        return tuple(upd1(n, carry[n], kb, vb) for n in range(len(chains)))

    def start():
        carry = tuple((jnp.full((tr, 1), NEG_BIG, F32), jnp.zeros((tr, LANES), F32)) for _ in chains)
        if pvalid is None:
            for jb in range(n_pblk):
                sl = slice(jb * pblk, (jb + 1) * pblk)
                carry = upd(carry, kp_ref[sl, :], vp_ref[sl, :])
        return carry

    def finish(carry, off):
        tail_w = tr + (LANES if pvalid is not None else 0)
        rj = lax.broadcasted_iota(jnp.int32, (tr, tail_w), 0)
        cj = lax.broadcasted_iota(jnp.int32, (tr, tail_w), 1)
        mask = cj <= rj
        if pvalid is not None:
            mask = (mask | (cj >= tr)) & (cj < tr + pvalid)
        carry = list(carry)
        for r in reversed(range(nr)):
            width = (r + 1) * tr
            kb = k_ref[pl.ds(off, width), :]
            vb = v_ref[pl.ds(off, width), :]
            if pvalid is not None:
                kb = jnp.concatenate([kb, kp_ref[...]], axis=0)
                vb = jnp.concatenate([vb, vp_ref[...]], axis=0)
            for n, (hh, rr) in enumerate(chains):
                if rr == r:
                    carry[n] = upd1(n, carry[n], kb, vb, mask=mask)
        lane = lax.broadcasted_iota(jnp.int32, (tr, LANES), 1)
        outs = []
        for n in range(len(chains)):
            acc = carry[n][1]
            l = jnp.sum(jnp.where(lane == FOX_HEAD_DIM, acc, 0.0), axis=-1, keepdims=True)
            outs.append(acc / l)
        for hp in range(hps // 2):
            for r in range(nr):
                even, odd = outs[2 * hp * nr + r], outs[(2 * hp + 1) * nr + r]
                o_ref[r * tr:(r + 1) * tr, hp * LANES:(hp + 1) * LANES] = jnp.where(
                    lane < FOX_HEAD_DIM, even, pltpu.roll(odd, FOX_HEAD_DIM, 1)).astype(o_ref.dtype)

    if tq % tkb == 0 and nq <= FOX_MAX_STATIC_BLOCKS:
        for qi in range(nq):
            @pl.when(i == qi)
            def _():
                carry = start()
                for j in range(qi * tq // tkb):
                    carry = upd(carry, k_ref[j * tkb:(j + 1) * tkb, :], v_ref[j * tkb:(j + 1) * tkb, :])
                finish(carry, qi * tq)
    else:
        tkw = max(tkb, tq)

        def body_wide(j, carry):
            off = pl.multiple_of(j * tkw, tkw)
            return upd(carry, k_ref[pl.ds(off, tkw), :], v_ref[pl.ds(off, tkw), :])

        def body(j, carry):
            off = pl.multiple_of(j * tq, tq)
            return upd(carry, k_ref[pl.ds(off, tq), :], v_ref[pl.ds(off, tq), :])

        n_wide = i // (tkw // tq)
        carry = lax.fori_loop(0, n_wide, body_wide, start())
        if tkw > tq:
            carry = lax.fori_loop(n_wide * (tkw // tq), i, body, carry)
        finish(carry, pl.multiple_of(i * tq, tq))


def _fox(q, kp, vp, k, v, tq, tr, pblk, pvalid=None, hps=2):
    bsz, t, _ = q.shape
    bp, plen, _ = kp.shape
    nq = t // tq
    n_pblk = plen // pblk
    pw = hps * LANES
    pb = (lambda b: b) if bp == bsz else (lambda b: 0)
    return pl.pallas_call(
        functools.partial(_fox_kernel, tq=tq, tr=tr, tkb=_row_tile(t, FOX_WIDE_KEYS), nq=nq, hps=hps, pblk=pblk,
                          n_pblk=n_pblk, pvalid=pvalid),
        grid=(bsz, FOX_HEADS // hps, nq),
        in_specs=[pl.BlockSpec((None, tq, pw), lambda b, p, i: (b, i, p)),
                  pl.BlockSpec((None, plen, pw), lambda b, p, i: (pb(b), 0, p)),
                  pl.BlockSpec((None, plen, pw), lambda b, p, i: (pb(b), 0, p)),
                  pl.BlockSpec((None, t, pw), lambda b, p, i: (b, 0, p)),
                  pl.BlockSpec((None, t, pw), lambda b, p, i: (b, 0, p))],
        out_specs=pl.BlockSpec((None, tq, hps * FOX_HEAD_DIM), lambda b, p, i: (b, i, p)),
        out_shape=jax.ShapeDtypeStruct((bsz, t, FOX_WIDTH), BF16),
        compiler_params=pltpu.CompilerParams(dimension_semantics=("arbitrary", "arbitrary", "arbitrary"),
                                             vmem_limit_bytes=VMEM_LIMIT),
        name="fox_attention",
    )(q, kp, vp, k, v)


def _mlstm_kernel(xc_ref, mv_ref, mo_ref, gt_ref, c0_ref, n0_ref, m0_ref, wq_ref, wk_ref, gn_ref, sk_ref,
                  hm_ref, cout_ref, nout_ref, mout_ref,
                  c_scr, n_scr, m_scr, *, tc, n_valid, nb, shared_state):
    c = pl.program_id(1)

    @pl.when(c == 0)
    def _():
        for bi in range(nb):
            si = 0 if shared_state else bi
            c_scr[bi] = c0_ref[si]
            n_scr[bi] = n0_ref[si]
            m_scr[bi] = m0_ref[si]

    for bi in range(nb):
        _mlstm_chunk(xc_ref.at[bi], mv_ref.at[bi], mo_ref.at[bi], gt_ref.at[bi], wq_ref, wk_ref,
                     gn_ref, sk_ref, hm_ref.at[bi], c_scr.at[bi], n_scr.at[bi], m_scr.at[bi],
                     tc=tc, n_valid=n_valid)

    @pl.when(c == pl.num_programs(1) - 1)
    def _():
        cout_ref[...] = c_scr[...]
        nout_ref[...] = n_scr[...]
        mout_ref[...] = m_scr[...]


def _mlstm_chunk(xc_ref, mv_ref, mo_ref, gt_ref, wq_ref, wk_ref, gn_ref, sk_ref, hm_ref,
                 c_scr, n_scr, m_scr, *, tc, n_valid):
    gt = gt_ref[...]
    lane = lax.broadcasted_iota(jnp.int32, (tc, LANES), 1)
    if n_valid is not None:
        rowi = lax.broadcasted_iota(jnp.int32, (tc, LANES), 0)
        is_i = (lane >= GATE_ML_I) & (lane < GATE_ML_F)
        gt = jnp.where(rowi < n_valid, gt, jnp.where(is_i, PAD_LOG_GATE, 0.0))
    ri = lax.broadcasted_iota(jnp.int32, (tc, tc), 0)
    ci = lax.broadcasted_iota(jnp.int32, (tc, tc), 1)
    causal = ci <= ri
    b_all = _cumsum_rows(gt, jnp.zeros((1, LANES), F32))[0]
    r_t = (pltpu.roll(gt, GATE_ML_F - GATE_ML_I, 1) - b_all).T
    heads = range(ML_HEADS)
    hsl = [slice(h * ML_HEAD_DIM, (h + 1) * ML_HEAD_DIM) for h in heads]
    b_col = [b_all[:, GATE_ML_F + h:GATE_ML_F + h + 1] for h in heads]
    ig_col = [gt[:, GATE_ML_I + h:GATE_ML_I + h + 1] for h in heads]
    m_prev = [m_scr[h:h + 1, 0:1] for h in heads]
    dmat = [jnp.where(causal, b_col[h] + r_t[GATE_ML_F + h:GATE_ML_F + h + 1, :], NEG_BIG) for h in heads]
    m_inter = [b_col[h] + m_prev[h] for h in heads]
    m_t = [jnp.maximum(m_inter[h], jnp.max(dmat[h], axis=-1, keepdims=True)) for h in heads]
    w_inter = [jnp.exp(m_inter[h] - m_t[h]) for h in heads]

    xhb = [xc_ref[:, hsl[h]] for h in heads]
    q = [jnp.dot(xhb[h], wq_ref[h], preferred_element_type=F32) for h in heads]
    k = [jnp.dot(xhb[h], wk_ref[h], preferred_element_type=F32) * (ML_HEAD_DIM ** -0.5) for h in heads]
    qb = [x.astype(BF16) for x in q]
    kb = [x.astype(BF16) for x in k]
    v = [mv_ref[:, hsl[h]] for h in heads]
    smat = [lax.dot_general(qb[h], kb[h], (((1,), (1,)), ((), ())), preferred_element_type=F32) for h in heads]
    amat = [jnp.exp(dmat[h] - m_t[h]) * smat[h] for h in heads]
    ch = [c_scr[h] for h in heads]
    nh = [n_scr[h:h + 1, :] for h in heads]
    num = [jnp.dot(amat[h].astype(BF16), v[h], preferred_element_type=F32)
           + w_inter[h] * jnp.dot(qb[h], ch[h].astype(BF16), preferred_element_type=F32) for h in heads]
    den = [jnp.sum(amat[h], axis=-1, keepdims=True) + w_inter[h] * jnp.sum(q[h] * nh[h], axis=-1, keepdims=True)
           for h in heads]
    hcell = [num[h] / jnp.maximum(jnp.abs(den[h]), jnp.exp(-m_t[h])) for h in heads]

    b_last = [b_col[h][tc - 1:tc, :] for h in heads]
    g = [b_last[h] - b_col[h] + ig_col[h] for h in heads]
    m_new = [jnp.maximum(b_last[h] + m_prev[h], jnp.max(g[h], axis=0, keepdims=True)) for h in heads]
    decay = [jnp.exp(b_last[h] + m_prev[h] - m_new[h]) for h in heads]
    kw = [k[h] * jnp.exp(g[h] - m_new[h]) for h in heads]
    for h in heads:
        c_scr[h] = decay[h] * ch[h] + lax.dot_general(kw[h].astype(BF16), v[h], (((0,), (0,)), ((), ())),
                                                       preferred_element_type=F32)
        n_scr[h:h + 1, :] = decay[h] * nh[h] + jnp.sum(kw[h], axis=0, keepdims=True)
        m_scr[h:h + 1, :] = jnp.broadcast_to(m_new[h], (1, LANES))
        y = mo_ref[:, hsl[h]].astype(F32) * hcell[h]
        hm_ref[:, hsl[h]] = (_rms(y, gn_ref[:, hsl[h]]) + sk_ref[:, hsl[h]] * xhb[h].astype(F32)).astype(hm_ref.dtype)


def _mlstm(xc, mv, mo, gt, c0, n0, m0, wq, wk, gn, sk, tc, nb=1, n_valid=None):
    bsz, t, _ = xc.shape
    shared = c0.shape[0] != bsz
    sn = 1 if shared else nb
    sb = (lambda b: 0) if shared else (lambda b: b)
    tok = lambda w: pl.BlockSpec((nb, tc, w), lambda b, c: (b, c, 0))
    hd = ML_HEAD_DIM
    st_specs = [pl.BlockSpec((nb, ML_HEADS, hd, hd), lambda b, c: (b, 0, 0, 0)),
                pl.BlockSpec((nb, 8, LANES), lambda b, c: (b, 0, 0)),
                pl.BlockSpec((nb, 8, LANES), lambda b, c: (b, 0, 0))]
    return pl.pallas_call(
        functools.partial(_mlstm_kernel, tc=tc, n_valid=n_valid, nb=nb, shared_state=shared),
        grid=(bsz // nb, t // tc),
        in_specs=[tok(ML_WIDTH), tok(ML_WIDTH), tok(ML_WIDTH), tok(LANES),
                  pl.BlockSpec((sn, ML_HEADS, hd, hd), lambda b, c: (sb(b), 0, 0, 0)),
                  pl.BlockSpec((sn, 8, LANES), lambda b, c: (sb(b), 0, 0)),
                  pl.BlockSpec((sn, 8, LANES), lambda b, c: (sb(b), 0, 0)),
                  _const_spec((ML_HEADS, hd, hd)), _const_spec((ML_HEADS, hd, hd)),
                  _const_spec((1, ML_WIDTH)), _const_spec((1, ML_WIDTH))],
        out_specs=[tok(ML_WIDTH)] + st_specs,
        out_shape=[jax.ShapeDtypeStruct((bsz, t, ML_WIDTH), BF16),
                   jax.ShapeDtypeStruct((bsz, ML_HEADS, hd, hd), F32),
                   jax.ShapeDtypeStruct((bsz, 8, LANES), F32),
                   jax.ShapeDtypeStruct((bsz, 8, LANES), F32)],
        scratch_shapes=[pltpu.VMEM((nb, ML_HEADS, hd, hd), F32),
                        pltpu.VMEM((nb, 8, LANES), F32), pltpu.VMEM((nb, 8, LANES), F32)],
        compiler_params=pltpu.CompilerParams(dimension_semantics=("arbitrary", "arbitrary")),
        name="mlstm",
    )(xc, mv, mo, gt, c0, n0, m0, wq, wk, gn, sk)


def _post_kernel(a_ref, hm_ref, ga_ref, gb_ref, x_ref, wfp_ref, wmp_ref, wo_ref, wgu_ref, wd_ref,
                 g1_ref, g2_ref, g3_ref, y_ref, *, d_ff, ff_cuts):
    ya = jnp.dot(a_ref[...], wfp_ref[...], preferred_element_type=F32)
    yb = jnp.dot(hm_ref[...], wmp_ref[...], preferred_element_type=F32)
    merged = jax.nn.sigmoid(ga_ref[...].astype(F32)) * ya + jax.nn.sigmoid(gb_ref[...].astype(F32)) * yb
    mix = jnp.dot(merged.astype(BF16), wo_ref[...], preferred_element_type=F32)
    x1 = x_ref[...] + _rms(mix, g1_ref[...])
    h2 = _rms(x1, g2_ref[...]).astype(BF16)
    f = None
    for lo, hi in zip(ff_cuts[:-1], ff_cuts[1:]):
        g = jnp.dot(h2, wgu_ref[:, lo:hi], preferred_element_type=F32)
        u = jnp.dot(h2, wgu_ref[:, d_ff + lo:d_ff + hi], preferred_element_type=F32)
        t = (g * jax.nn.sigmoid(g) * u).astype(BF16)
        fj = jnp.dot(t, wd_ref[lo:hi, :], preferred_element_type=F32)
        f = fj if f is None else f + fj
    y_ref[...] = x1 + _rms(f, g3_ref[...])


def _post(a, hm, ga, gb, x, wfp, wmp, wo, wgu, wd, g1, g2, g3, tm):
    n, d = x.shape
    d_ff = wd.shape[0]
    nmt = d_ff // MXU_TILE
    ff_cuts = tuple(MXU_TILE * ((nmt * j) // FF_CHUNKS) for j in range(FF_CHUNKS)) + (d_ff,) \
        if d_ff % MXU_TILE == 0 else (0, d_ff)
    row = lambda c: pl.BlockSpec((tm, c), lambda i: (i, 0))
    return pl.pallas_call(
        functools.partial(_post_kernel, d_ff=d_ff, ff_cuts=ff_cuts),
        grid=(n // tm,),
        in_specs=[row(FOX_WIDTH), row(ML_WIDTH), row(d), row(d), row(d),
                  _const_spec(wfp.shape), _const_spec(wmp.shape), _const_spec(wo.shape),
                  _const_spec(wgu.shape), _const_spec(wd.shape),
                  _const_spec((1, d)), _const_spec((1, d)), _const_spec((1, d))],
        out_specs=row(d),
        out_shape=jax.ShapeDtypeStruct((n, d), F32),
        compiler_params=pltpu.CompilerParams(dimension_semantics=("arbitrary",), vmem_limit_bytes=VMEM_LIMIT),
        name="post_mix_ffn",
    )(a, hm, ga, gb, x, wfp, wmp, wo, wgu, wd, g1, g2, g3)


def _pad_rows(a, rows, axis):
    pad = [(0, 0)] * a.ndim
    pad[axis] = (0, rows - a.shape[axis])
    return jnp.pad(a, pad)


def _row_tile(n, pref):
    t = min(pref, n)
    while n % t:
        t //= 2
    return t


def kernel(x_prompt, x_sample, cache_fox_k, cache_fox_v, cache_fox_logf, state_mlstm_C, state_mlstm_n,
           state_mlstm_m, state_mlstm_conv, meta_tokens, w_in, b_in, g_fq, g_fk, w_conv, b_conv, w_mq, w_mk,
           g_mnorm, skip_m, w_fox_proj, w_ml_proj, w_out, g_pre_mix, g_post_mix, g_pre_ffn, g_post_ffn,
           w_gate_up, w_down):
    assert w_in.shape[0] == 1, "single-layer trunk"
    bsz, seq, d = x_prompt.shape
    dbsz, dseq, _ = x_sample.shape
    past = cache_fox_k.shape[2]
    fw, mw = FOX_WIDTH, ML_WIDTH

    w, b = w_in[0], b_in[0]
    o_ff = 3 * fw
    o_mx = o_ff + FOX_HEADS
    o_mi = o_mx + 3 * mw
    o_mf = o_mi + ML_HEADS
    o_ga = o_mf + ML_HEADS

    def relayout(a):
        gates = jnp.concatenate([a[..., o_ff:o_mx], a[..., o_mi:o_mf], a[..., o_mf:o_ga], a[..., o_mf:o_ga]], axis=-1)
        gates = _pad_rows(gates, LANES, gates.ndim - 1)
        return jnp.concatenate([a[..., :o_ff], a[..., o_mx:o_mi], a[..., o_ga:], gates], axis=-1)

    w_p = relayout(w).astype(BF16)
    b_p = relayout(b)[None, :]
    hid = jnp.arange(fw) // FOX_HEAD_DIM
    hsum = (hid[:, None] == hid[None, :]).astype(BF16)
    gq = g_fq[0].reshape(1, fw)
    gk = g_fk[0].reshape(1, fw)
    gpre = g_pre_mix[0][None, :]
    wc, bc = w_conv[0], b_conv[0][None, :]
    inproj = lambda x, c0, cp, tm, nt, **kw: _inproj(x, gpre, w_p, b_p, gq, gk, hsum, c0, cp, wc, bc, tm, nt, **kw)

    wq, wk = w_mq[0].astype(BF16), w_mk[0].astype(BF16)
    gn, sk = g_mnorm[0].reshape(1, mw), skip_m[0][None, :]
    mlstm = lambda *a, **kw: _mlstm(*a, wq, wk, gn, sk, **kw)

    def conv_hist(rows):
        return jnp.pad(rows, ((0, 0), (8 - (CONV_WIDTH - 1), 0), (0, 0)))

    def m_rows(m):
        return jnp.broadcast_to(_pad_rows(m, 8, 1)[:, :, None], (m.shape[0], 8, LANES))

    mt = 64
    xm = _pad_rows(meta_tokens.astype(F32), mt, 0)
    (_, kf_m, ka_m, vf_m, va_m, xc_m, mv_m, mo_m, _, _, gt_m, fc_m, ct_m) = inproj(
        xm, jnp.zeros((1, 1, LANES), F32), jnp.zeros((1, 8, mw), F32), mt, 1, conv_valid=N_META)
    zc = jnp.zeros((1, ML_HEADS, ML_HEAD_DIM, ML_HEAD_DIM), F32)
    z8 = jnp.zeros((1, 8, LANES), F32)
    _, c_m, n_m, m_m = mlstm(xc_m[None], mv_m[None], mo_m[None], gt_m[None], zc, z8, z8, tc=mt, n_valid=N_META)

    xp = x_prompt.reshape(bsz * seq, d)
    tm = _row_tile(seq, 512)
    lanes_t = lambda rows: _pad_rows(rows[:N_META].T, LANES, 1)
    (qa_p, kt_p, ka_p, vt_p, va_p, xc_p, mv_p, mo_p, ga_p, gb_p, gt_p, _, ct_p, lt_p) = inproj(
        xp, fc_m[None, N_META - 1:N_META], ct_m, tm, seq // tm,
        meta_t=(lanes_t(kf_m), lanes_t(vf_m), lanes_t(gt_m[:, :FOX_HEADS])))
    r3 = lambda a: a.reshape(bsz, seq, a.shape[-1])
    hm_p, c_p, n_p, m_p = mlstm(r3(xc_p), r3(mv_p), r3(mo_p), r3(gt_p), c_m, n_m, m_m,
                                tc=_row_tile(seq, 256), nb=_row_tile(bsz, 4))
    a_p = _fox(r3(qa_p), _pad_rows(ka_m, LANES, 0)[None], _pad_rows(va_m, LANES, 0)[None], r3(ka_p), r3(va_p),
               tq=_row_tile(seq, 2048), tr=_row_tile(seq, 512), pblk=LANES, pvalid=N_META)

    tok_minor = lambda c: c[0].transpose(0, 2, 3, 1).reshape(dbsz, fw, past)
    kc, vc, f_c = _cacheprep(tok_minor(cache_fox_k), tok_minor(cache_fox_v),
                             _pad_rows(cache_fox_logf[0].astype(F32), LANES, 2), _row_tile(past, 512))
    xs = x_sample.reshape(dbsz * dseq, d)
    (qa_s, kf_s, ka_s, vf_s, va_s, xc_s, mv_s, mo_s, ga_s, gb_s, gt_s, _, ct_s) = inproj(
        xs, f_c, conv_hist(state_mlstm_conv[0]), dseq, 1)
    r3s = lambda a: a.reshape(dbsz, dseq, a.shape[-1])
    hm_s, c_s, n_s, m_s = mlstm(r3s(xc_s), r3s(mv_s), r3s(mo_s), r3s(gt_s), state_mlstm_C[0], _pad_rows(state_mlstm_n[0], 8, 1), m_rows(state_mlstm_m[0]),
                                tc=dseq, nb=_row_tile(dbsz, 4))
    a_s = _fox(r3s(qa_s), kc, vc, r3s(ka_s), r3s(va_s), tq=dseq, tr=dseq, pblk=_row_tile(past, 512), hps=FOX_HEADS)

    wfp, wmp, wo = w_fox_proj[0].astype(BF16), w_ml_proj[0].astype(BF16), w_out[0].astype(BF16)
    wgu, wd = w_gate_up[0].astype(BF16), w_down[0].astype(BF16)
    g1, g2, g3 = g_post_mix[0][None, :], g_pre_ffn[0][None, :], g_post_ffn[0][None, :]
    post = lambda a, hm, ga, gb, x, tm: _post(a, hm, ga, gb, x, wfp, wmp, wo, wgu, wd, g1, g2, g3, tm)
    y_p = post(a_p.reshape(bsz * seq, fw), hm_p.reshape(bsz * seq, mw), ga_p, gb_p, xp, _row_tile(bsz * seq, 512))
    y_s = post(a_s.reshape(dbsz * dseq, fw), hm_s.reshape(dbsz * dseq, mw), ga_s, gb_s, xs,
               _row_tile(dbsz * dseq, 512))

    def tokens_major(a):
        return a.reshape(bsz, FOX_HEADS, FOX_HEAD_DIM, a.shape[-1]).transpose(0, 3, 1, 2)[None]

    heads = lambda a: a.reshape(a.shape[:-1] + (FOX_HEADS, FOX_HEAD_DIM))
    tail = CONV_WIDTH - 1
    return (y_p.reshape(bsz, seq, d), y_s.reshape(dbsz, dseq, d),
            tokens_major(kt_p), tokens_major(vt_p), lt_p.transpose(0, 2, 1)[None],
            c_p[None], n_p[None, :, :ML_HEADS], m_p[None, :, :ML_HEADS, 0], ct_p[None, :, 8 - tail:],
            heads(r3s(kf_s))[None], heads(r3s(vf_s))[None], r3s(gt_s)[None, ..., :FOX_HEADS],
            c_s[None], n_s[None, :, :ML_HEADS], m_s[None, :, :ML_HEADS, 0], ct_s[None, :, 8 - tail:])
```

```python
import functools

import jax
import jax.numpy as jnp
from jax import lax
from jax.experimental import pallas as pl
from jax.experimental.pallas import tpu as pltpu

F32 = jnp.float32
BF16 = jnp.bfloat16

N_META = 16
FOX_HEADS = 8
FOX_HEAD_DIM = 64
FOX_WIDTH = FOX_HEADS * FOX_HEAD_DIM
ML_HEADS = 4
ML_HEAD_DIM = 128
ML_WIDTH = ML_HEADS * ML_HEAD_DIM
CONV_WIDTH = 4
RMS_EPS = 1e-6
PAD_LOG_GATE = -1e30
NEG_BIG = -1e30

LANES = 128
GATE_FOX_F = 0
GATE_ML_I = 8
GATE_ML_F = 12
GATE_ML_F2 = 16

AUG_STEP = FOX_HEADS
AUG_QF = FOX_HEAD_DIM
AUG_KF = FOX_HEAD_DIM + 24

VMEM_LIMIT = 56 * 1024 * 1024
FOX_WIDE_KEYS = 1024
FOX_MAX_STATIC_BLOCKS = 4
MXU_TILE = 256
FF_CHUNKS = 2


def _const_spec(shape):
    nd = len(shape)
    return pl.BlockSpec(shape, lambda *_: (0,) * nd, pipeline_mode=pl.Buffered(1))


def _rms(x, g):
    return x * lax.rsqrt(jnp.mean(x * x, axis=-1, keepdims=True) + RMS_EPS) * g


def _split3(x):
    hi = x.astype(BF16).astype(F32)
    r = x - hi
    mid = r.astype(BF16).astype(F32)
    lo = (r - mid).astype(BF16).astype(F32)
    return hi, mid, lo


def _cumsum_rows(lg, cur):
    tm = lg.shape[0]
    blk = min(tm, LANES)
    r = lax.broadcasted_iota(jnp.int32, (blk, blk), 0)
    c = lax.broadcasted_iota(jnp.int32, (blk, blk), 1)
    tril = (c <= r).astype(BF16)
    outs = []
    for s in range(tm // blk):
        hi, mid, lo = _split3(lg[s * blk:(s + 1) * blk])
        f = (jnp.dot(tril, lo.astype(BF16), preferred_element_type=F32)
             + jnp.dot(tril, mid.astype(BF16), preferred_element_type=F32)
             + jnp.dot(tril, hi.astype(BF16), preferred_element_type=F32)) + cur
        outs.append(f)
        cur = f[blk - 1:blk]
    return (outs[0] if len(outs) == 1 else jnp.concatenate(outs, axis=0)), cur


def _store_attention_operands(qn, kn, v, fcol, q_ref, k_ref, v_ref, fillers=()):
    tm = fcol.shape[0]
    lane = lax.broadcasted_iota(jnp.int32, (tm, LANES), 1)
    data = lane < FOX_HEAD_DIM
    hi, mid, lo = _split3(fcol)
    comb = jnp.where(lane < AUG_STEP, hi, jnp.where(lane < 2 * AUG_STEP, pltpu.roll(mid, AUG_STEP, 1),
                                                     pltpu.roll(lo, 2 * AUG_STEP, 1)))
    ncomb = -comb
    is_qf = (lane == AUG_QF) | (lane == AUG_QF + AUG_STEP) | (lane == AUG_QF + 2 * AUG_STEP)
    is_kf = (lane == AUG_KF) | (lane == AUG_KF + AUG_STEP) | (lane == AUG_KF + 2 * AUG_STEP)
    base_q = jnp.where(is_kf, 1.0, 0.0)
    base_k = jnp.where(is_qf, 1.0, 0.0)
    base_v = jnp.where(lane == FOX_HEAD_DIM, 1.0, 0.0)
    fillers = list(fillers)
    for h in range(FOX_HEADS):
        t, odd = divmod(h, 2)

        def head_tile(z):
            x = z[:, t * LANES:(t + 1) * LANES]
            return pltpu.roll(x, FOX_HEAD_DIM, 1) if odd else x

        hs = slice(h * LANES, (h + 1) * LANES)
        if q_ref is not None:
            aug = jnp.where(is_qf, pltpu.roll(comb, AUG_QF - h, 1), base_q)
            q_ref[:, hs] = jnp.where(data, head_tile(qn), aug).astype(BF16)
        aug = jnp.where(is_kf, pltpu.roll(ncomb, AUG_KF - h, 1), base_k)
        k_ref[:, hs] = jnp.where(data, head_tile(kn), aug).astype(BF16)
        v_ref[:, hs] = jnp.where(data, head_tile(v), base_v).astype(BF16)
        if fillers:
            fillers.pop(0)()
    for f in fillers:
        f()


def _inproj_kernel(*refs, d_model, nt, kv_t, conv_valid):
    (x_ref, gpre_ref, w_ref, b_ref, gq_ref, gk_ref, hs_ref, c0_ref, cp_ref, wc_ref, bc_ref), rest = refs[:11], refs[11:]
    if kv_t:
        kmeta_ref, vmeta_ref, lmeta_ref = rest[:3]
        (q_ref, kf_ref, kb_ref, vf_ref, vb_ref, xc_ref, mv_ref, mo_ref, ga_ref, gb_ref, gt_ref, fc_ref, ct_ref,
         lt_ref, carry, xbuf, kcar, vcar, lcar) = rest[3:]
    else:
        (q_ref, kf_ref, kb_ref, vf_ref, vb_ref, xc_ref, mv_ref, mo_ref, ga_ref, gb_ref, gt_ref, fc_ref, ct_ref,
         carry, xbuf) = rest
    fw, mw = FOX_WIDTH, ML_WIDTH
    offs = [0, fw, 2 * fw, 3 * fw, 3 * fw + mw, 3 * fw + 2 * mw, 3 * fw + 3 * mw,
            3 * fw + 3 * mw + d_model, 3 * fw + 3 * mw + 2 * d_model, 3 * fw + 3 * mw + 2 * d_model + LANES]
    r = pl.program_id(1)

    @pl.when(r == 0)
    def _():
        carry[...] = c0_ref[...]
        xbuf[0:8, :] = cp_ref[...]
        if kv_t:
            kcar[...] = kmeta_ref[...]
            vcar[...] = vmeta_ref[...]
            lcar[...] = lmeta_ref[...]

    def store_shifted(xt, car, out_ref):
        sh = pltpu.roll(xt, N_META, 1)
        lane = lax.broadcasted_iota(jnp.int32, (xt.shape[0], LANES), 1)
        out_ref[:, :LANES] = jnp.where(lane < N_META, car[...], sh[:, :LANES])
        out_ref[:, LANES:] = sh[:, LANES:]
        car[...] = sh[:, :LANES]

    @pl.when(r < nt)
    def _():
        h = _rms(x_ref[...], gpre_ref[...]).astype(BF16)

        def seg(i):
            a, b = offs[i], offs[i + 1]
            return jnp.dot(h, w_ref[:, a:b], preferred_element_type=F32) + b_ref[:, a:b]

        def headnorm(z, g):
            ms = jnp.dot((z * z).astype(BF16), hs_ref[...], preferred_element_type=F32) * (1.0 / FOX_HEAD_DIM)
            return z * lax.rsqrt(ms + RMS_EPS) * g

        g = seg(8)
        ga_ref[...] = seg(6).astype(BF16)
        lane = lax.broadcasted_iota(jnp.int32, g.shape, 1)
        raw = (lane >= GATE_ML_I) & (lane < GATE_ML_F)
        gt = jnp.where(raw, g, jax.nn.log_sigmoid(g))
        gt_ref[...] = gt
        fcol, cur = _cumsum_rows(gt, carry[...])
        carry[...] = cur
        fc_ref[...] = fcol

        v = seg(2)
        q = headnorm(seg(0), gq_ref[...]) * (FOX_HEAD_DIM ** -0.5)
        k = headnorm(seg(1), gk_ref[...])
        def store_kv():
            if kv_t:
                store_shifted(k.T, kcar, kf_ref)
                store_shifted(v.T, vcar, vf_ref)
                store_shifted(gt.T[:FOX_HEADS], lcar, lt_ref)
            else:
                kf_ref[...] = k
                vf_ref[...] = v

        def store_xc():
            tm = x_ref.shape[0]
            xbuf[8:8 + tm, :] = seg(3)
            xc = bc_ref[...]
            for i in range(CONV_WIDTH):
                xc = xc + xbuf[5 + i:5 + i + tm, :] * wc_ref[i:i + 1, :]
            xc_ref[...] = (xc * jax.nn.sigmoid(xc)).astype(BF16)
            ct_ref[...] = xbuf[conv_valid:conv_valid + 8, :]
            xbuf[0:8, :] = xbuf[tm:tm + 8, :]

        def store_mv():
            mv_ref[...] = seg(4).astype(BF16)

        def store_mo():
            mo_ref[...] = jax.nn.sigmoid(seg(5)).astype(BF16)

        def store_gb():
            gb_ref[...] = seg(7).astype(BF16)

        _store_attention_operands(q, k, v, fcol, q_ref, kb_ref, vb_ref,
                                  fillers=(store_xc, store_mv, store_mo, store_gb, store_kv))

    if kv_t:
        @pl.when(r == nt)
        def _():
            kf_ref[:, :LANES] = kcar[...]
            vf_ref[:, :LANES] = vcar[...]
            lt_ref[:, :LANES] = lcar[...]


def _inproj(x, gpre, w, b, gq, gk, hs, c0, cp, wc, bc, tm, nt, meta_t=None, conv_valid=None):
    n, d = x.shape
    nb = n // (nt * tm)
    dinp = w.shape[1]
    fw, mw = FOX_WIDTH, ML_WIDTH
    aw = FOX_HEADS * LANES
    kv_t = meta_t is not None
    tile = lambda b, r: b * nt + jnp.minimum(r, nt - 1)
    row = lambda c: pl.BlockSpec((tm, c), lambda b, r: (tile(b, r), 0))
    cb = (lambda b: b) if c0.shape[0] > 1 else (lambda b: 0)
    pb = (lambda b: b) if cp.shape[0] > 1 else (lambda b: 0)
    lp = N_META + nt * tm
    tmin = lambda c: pl.BlockSpec((None, c, tm), lambda b, r: (b, 0, r))
    flat = lambda c, t: (row(c), jax.ShapeDtypeStruct((n, c), t))
    tok_minor = lambda c: (tmin(c), jax.ShapeDtypeStruct((nb, c, lp), F32))
    outs = [flat(aw, BF16), tok_minor(fw) if kv_t else flat(fw, F32), flat(aw, BF16),
            tok_minor(fw) if kv_t else flat(fw, F32), flat(aw, BF16), flat(mw, BF16), flat(mw, BF16), flat(mw, BF16),
            flat(d, BF16), flat(d, BF16), flat(LANES, F32), flat(LANES, F32),
            (pl.BlockSpec((None, 8, mw), lambda b, r: (b, 0, 0)), jax.ShapeDtypeStruct((nb, 8, mw), F32))]
    in_specs = [row(d), _const_spec((1, d)), _const_spec((d, dinp)), _const_spec((1, dinp)),
                _const_spec((1, fw)), _const_spec((1, fw)), _const_spec((fw, fw)),
                pl.BlockSpec((None, 1, LANES), lambda b, r: (cb(b), 0, 0)),
                pl.BlockSpec((None, 8, mw), lambda b, r: (pb(b), 0, 0)),
                _const_spec((CONV_WIDTH, mw)), _const_spec((1, mw))]
    scratch = [pltpu.VMEM((1, LANES), F32), pltpu.VMEM((8 + tm, mw), F32)]
    args = [x, gpre, w, b, gq, gk, hs, c0, cp, wc, bc]
    if kv_t:
        outs.append(tok_minor(FOX_HEADS))
        in_specs += [_const_spec((fw, LANES)), _const_spec((fw, LANES)), _const_spec((FOX_HEADS, LANES))]
        scratch += [pltpu.VMEM((fw, LANES), F32), pltpu.VMEM((fw, LANES), F32), pltpu.VMEM((FOX_HEADS, LANES), F32)]
        args += list(meta_t)
    return pl.pallas_call(
        functools.partial(_inproj_kernel, d_model=d, nt=nt, kv_t=kv_t,
                          conv_valid=tm if conv_valid is None else conv_valid),
        grid=(nb, nt + (1 if kv_t else 0)),
        in_specs=in_specs,
        out_specs=[o[0] for o in outs],
        out_shape=[o[1] for o in outs],
        scratch_shapes=scratch,
        compiler_params=pltpu.CompilerParams(dimension_semantics=("arbitrary", "arbitrary"),
                                             vmem_limit_bytes=VMEM_LIMIT),
        name="inproj",
    )(*args)


def _cacheprep_kernel(k_ref, v_ref, lf_ref, ka_ref, va_ref, fl_ref, carry):
    @pl.when(pl.program_id(1) == 0)
    def _():
        carry[...] = jnp.zeros_like(carry)

    fcol, cur = _cumsum_rows(lf_ref[...], carry[...])
    carry[...] = cur
    fl_ref[...] = cur
    _store_attention_operands(None, k_ref[...].T, v_ref[...].T, fcol, None, ka_ref, va_ref)


def _cacheprep(ck, cv, clf, tp):
    bsz, _, plen = ck.shape
    aw = FOX_HEADS * LANES
    return pl.pallas_call(
        _cacheprep_kernel,
        grid=(bsz, plen // tp),
        in_specs=[pl.BlockSpec((None, FOX_WIDTH, tp), lambda b, i: (b, 0, i)),
                  pl.BlockSpec((None, FOX_WIDTH, tp), lambda b, i: (b, 0, i)),
                  pl.BlockSpec((None, tp, LANES), lambda b, i: (b, i, 0))],
        out_specs=[pl.BlockSpec((None, tp, aw), lambda b, i: (b, i, 0)),
                   pl.BlockSpec((None, tp, aw), lambda b, i: (b, i, 0)),
                   pl.BlockSpec((None, 1, LANES), lambda b, i: (b, 0, 0))],
        out_shape=[jax.ShapeDtypeStruct((bsz, plen, aw), BF16), jax.ShapeDtypeStruct((bsz, plen, aw), BF16),
                   jax.ShapeDtypeStruct((bsz, 1, LANES), F32)],
        scratch_shapes=[pltpu.VMEM((1, LANES), F32)],
        compiler_params=pltpu.CompilerParams(dimension_semantics=("arbitrary", "arbitrary")),
        name="cache_prep",
    )(ck, cv, clf)


def _fox_kernel(q_ref, kp_ref, vp_ref, k_ref, v_ref, o_ref, *, tq, tr, tkb, nq, hps, pblk, n_pblk, pvalid):
    i = pl.program_id(2)
    nr = tq // tr
    chains = [(hh, r) for hh in range(hps) for r in range(nr)]
    qs = [q_ref[r * tr:(r + 1) * tr, hh * LANES:(hh + 1) * LANES] for hh, r in chains]

    def upd1(n, state, kb, vb, mask=None):
        m, acc = state
        hh = chains[n][0]
        hs = slice(hh * LANES, (hh + 1) * LANES)
        s = lax.dot_general(qs[n], kb[:, hs], (((1,), (1,)), ((), ())), preferred_element_type=F32)
        if mask is not None:
            cut = s.shape[1] - mask.shape[1]
            tail = jnp.where(mask, s[:, cut:], NEG_BIG)
            s = tail if cut == 0 else jnp.concatenate([s[:, :cut], tail], axis=1)
        m_new = jnp.maximum(m, jnp.max(s, axis=-1, keepdims=True))
        alpha = jnp.exp(m - m_new)
        pm = jnp.exp(s - m_new)
        acc = alpha * acc + jnp.dot(pm.astype(BF16), vb[:, hs], preferred_element_type=F32)
        return m_new, acc

    def upd(carry, kb, vb):
        return tuple(upd1(n, carry[n], kb, vb) for n in range(len(chains)))

    def start():
        carry = tuple((jnp.full((tr, 1), NEG_BIG, F32), jnp.zeros((tr, LANES), F32)) for _ in chains)
        if pvalid is None:
            for jb in range(n_pblk):
                sl = slice(jb * pblk, (jb + 1) * pblk)
                carry = upd(carry, kp_ref[sl, :], vp_ref[sl, :])
        return carry

    def finish(carry, off):
        tail_w = tr + (LANES if pvalid is not None else 0)
        rj = lax.broadcasted_iota(jnp.int32, (tr, tail_w), 0)
        cj = lax.broadcasted_iota(jnp.int32, (tr, tail_w), 1)
        mask = cj <= rj
        if pvalid is not None:
            mask = (mask | (cj >= tr)) & (cj < tr + pvalid)
        carry = list(carry)
        for r in reversed(range(nr)):
            width = (r + 1) * tr
            kb = k_ref[pl.ds(off, width), :]
            vb = v_ref[pl.ds(off, width), :]
            if pvalid is not None:
                kb = jnp.concatenate([kb, kp_ref[...]], axis=0)
                vb = jnp.concatenate([vb, vp_ref[...]], axis=0)
            for n, (hh, rr) in enumerate(chains):
                if rr == r:
                    carry[n] = upd1(n, carry[n], kb, vb, mask=mask)
        lane = lax.broadcasted_iota(jnp.int32, (tr, LANES), 1)
        outs = []
        for n in range(len(chains)):
            acc = carry[n][1]
            l = jnp.sum(jnp.where(lane == FOX_HEAD_DIM, acc, 0.0), axis=-1, keepdims=True)
            outs.append(acc / l)
        for hp in range(hps // 2):
            for r in range(nr):
                even, odd = outs[2 * hp * nr + r], outs[(2 * hp + 1) * nr + r]
                o_ref[r * tr:(r + 1) * tr, hp * LANES:(hp + 1) * LANES] = jnp.where(
                    lane < FOX_HEAD_DIM, even, pltpu.roll(odd, FOX_HEAD_DIM, 1)).astype(o_ref.dtype)

    if tq % tkb == 0 and nq <= FOX_MAX_STATIC_BLOCKS:
        for qi in range(nq):
            @pl.when(i == qi)
            def _():
                carry = start()
                for j in range(qi * tq // tkb):
                    carry = upd(carry, k_ref[j * tkb:(j + 1) * tkb, :], v_ref[j * tkb:(j + 1) * tkb, :])
                finish(carry, qi * tq)
    else:
        tkw = max(tkb, tq)

        def body_wide(j, carry):
            off = pl.multiple_of(j * tkw, tkw)
            return upd(carry, k_ref[pl.ds(off, tkw), :], v_ref[pl.ds(off, tkw), :])

        def body(j, carry):
            off = pl.multiple_of(j * tq, tq)
            return upd(carry, k_ref[pl.ds(off, tq), :], v_ref[pl.ds(off, tq), :])

        n_wide = i // (tkw // tq)
        carry = lax.fori_loop(0, n_wide, body_wide, start())
        if tkw > tq:
            carry = lax.fori_loop(n_wide * (tkw // tq), i, body, carry)
        finish(carry, pl.multiple_of(i * tq, tq))


def _fox(q, kp, vp, k, v, tq, tr, pblk, pvalid=None, hps=2):
    bsz, t, _ = q.shape
    bp, plen, _ = kp.shape
    nq = t // tq
    n_pblk = plen // pblk
    pw = hps * LANES
    pb = (lambda b: b) if bp == bsz else (lambda b: 0)
    return pl.pallas_call(
        functools.partial(_fox_kernel, tq=tq, tr=tr, tkb=_row_tile(t, FOX_WIDE_KEYS), nq=nq, hps=hps, pblk=pblk,
                          n_pblk=n_pblk, pvalid=pvalid),
        grid=(bsz, FOX_HEADS // hps, nq),
        in_specs=[pl.BlockSpec((None, tq, pw), lambda b, p, i: (b, i, p)),
                  pl.BlockSpec((None, plen, pw), lambda b, p, i: (pb(b), 0, p)),
                  pl.BlockSpec((None, plen, pw), lambda b, p, i: (pb(b), 0, p)),
                  pl.BlockSpec((None, t, pw), lambda b, p, i: (b, 0, p)),
                  pl.BlockSpec((None, t, pw), lambda b, p, i: (b, 0, p))],
        out_specs=pl.BlockSpec((None, tq, hps * FOX_HEAD_DIM), lambda b, p, i: (b, i, p)),
        out_shape=jax.ShapeDtypeStruct((bsz, t, FOX_WIDTH), BF16),
        compiler_params=pltpu.CompilerParams(dimension_semantics=("arbitrary", "arbitrary", "arbitrary"),
                                             vmem_limit_bytes=VMEM_LIMIT),
        name="fox_attention",
    )(q, kp, vp, k, v)


def _mlstm_kernel(xc_ref, mv_ref, mo_ref, gt_ref, c0_ref, n0_ref, m0_ref, wq_ref, wk_ref, gn_ref, sk_ref,
                  hm_ref, cout_ref, nout_ref, mout_ref,
                  c_scr, n_scr, m_scr, *, tc, n_valid, nb, shared_state):
    c = pl.program_id(1)

    @pl.when(c == 0)
    def _():
        for bi in range(nb):
            si = 0 if shared_state else bi
            c_scr[bi] = c0_ref[si]
            n_scr[bi] = n0_ref[si]
            m_scr[bi] = m0_ref[si]

    for bi in range(nb):
        _mlstm_chunk(xc_ref.at[bi], mv_ref.at[bi], mo_ref.at[bi], gt_ref.at[bi], wq_ref, wk_ref,
                     gn_ref, sk_ref, hm_ref.at[bi], c_scr.at[bi], n_scr.at[bi], m_scr.at[bi],
                     tc=tc, n_valid=n_valid)

    @pl.when(c == pl.num_programs(1) - 1)
    def _():
        cout_ref[...] = c_scr[...]
        nout_ref[...] = n_scr[...]
        mout_ref[...] = m_scr[...]


def _mlstm_chunk(xc_ref, mv_ref, mo_ref, gt_ref, wq_ref, wk_ref, gn_ref, sk_ref, hm_ref,
                 c_scr, n_scr, m_scr, *, tc, n_valid):
    gt = gt_ref[...]
    lane = lax.broadcasted_iota(jnp.int32, (tc, LANES), 1)
    if n_valid is not None:
        rowi = lax.broadcasted_iota(jnp.int32, (tc, LANES), 0)
        is_i = (lane >= GATE_ML_I) & (lane < GATE_ML_F)
        gt = jnp.where(rowi < n_valid, gt, jnp.where(is_i, PAD_LOG_GATE, 0.0))
    ri = lax.broadcasted_iota(jnp.int32, (tc, tc), 0)
    ci = lax.broadcasted_iota(jnp.int32, (tc, tc), 1)
    causal = ci <= ri
    b_all = _cumsum_rows(gt, jnp.zeros((1, LANES), F32))[0]
    r_t = (pltpu.roll(gt, GATE_ML_F - GATE_ML_I, 1) - b_all).T
    heads = range(ML_HEADS)
    hsl = [slice(h * ML_HEAD_DIM, (h + 1) * ML_HEAD_DIM) for h in heads]
    b_col = [b_all[:, GATE_ML_F + h:GATE_ML_F + h + 1] for h in heads]
    ig_col = [gt[:, GATE_ML_I + h:GATE_ML_I + h + 1] for h in heads]
    m_prev = [m_scr[h:h + 1, 0:1] for h in heads]
    dmat = [jnp.where(causal, b_col[h] + r_t[GATE_ML_F + h:GATE_ML_F + h + 1, :], NEG_BIG) for h in heads]
    m_inter = [b_col[h] + m_prev[h] for h in heads]
    m_t = [jnp.maximum(m_inter[h], jnp.max(dmat[h], axis=-1, keepdims=True)) for h in heads]
    w_inter = [jnp.exp(m_inter[h] - m_t[h]) for h in heads]

    xhb = [xc_ref[:, hsl[h]] for h in heads]
    q = [jnp.dot(xhb[h], wq_ref[h], preferred_element_type=F32) for h in heads]
    k = [jnp.dot(xhb[h], wk_ref[h], preferred_element_type=F32) * (ML_HEAD_DIM ** -0.5) for h in heads]
    qb = [x.astype(BF16) for x in q]
    kb = [x.astype(BF16) for x in k]
    v = [mv_ref[:, hsl[h]] for h in heads]
    smat = [lax.dot_general(qb[h], kb[h], (((1,), (1,)), ((), ())), preferred_element_type=F32) for h in heads]
    amat = [jnp.exp(dmat[h] - m_t[h]) * smat[h] for h in heads]
    ch = [c_scr[h] for h in heads]
    nh = [n_scr[h:h + 1, :] for h in heads]
    num = [jnp.dot(amat[h].astype(BF16), v[h], preferred_element_type=F32)
           + w_inter[h] * jnp.dot(qb[h], ch[h].astype(BF16), preferred_element_type=F32) for h in heads]
    den = [jnp.sum(amat[h], axis=-1, keepdims=True) + w_inter[h] * jnp.sum(q[h] * nh[h], axis=-1, keepdims=True)
           for h in heads]
    hcell = [num[h] / jnp.maximum(jnp.abs(den[h]), jnp.exp(-m_t[h])) for h in heads]

    b_last = [b_col[h][tc - 1:tc, :] for h in heads]
    g = [b_last[h] - b_col[h] + ig_col[h] for h in heads]
    m_new = [jnp.maximum(b_last[h] + m_prev[h], jnp.max(g[h], axis=0, keepdims=True)) for h in heads]
    decay = [jnp.exp(b_last[h] + m_prev[h] - m_new[h]) for h in heads]
    kw = [k[h] * jnp.exp(g[h] - m_new[h]) for h in heads]
    for h in heads:
        c_scr[h] = decay[h] * ch[h] + lax.dot_general(kw[h].astype(BF16), v[h], (((0,), (0,)), ((), ())),
                                                       preferred_element_type=F32)
        n_scr[h:h + 1, :] = decay[h] * nh[h] + jnp.sum(kw[h], axis=0, keepdims=True)
        m_scr[h:h + 1, :] = jnp.broadcast_to(m_new[h], (1, LANES))
        y = mo_ref[:, hsl[h]].astype(F32) * hcell[h]
        hm_ref[:, hsl[h]] = (_rms(y, gn_ref[:, hsl[h]]) + sk_ref[:, hsl[h]] * xhb[h].astype(F32)).astype(hm_ref.dtype)


def _mlstm(xc, mv, mo, gt, c0, n0, m0, wq, wk, gn, sk, tc, nb=1, n_valid=None):
    bsz, t, _ = xc.shape
    shared = c0.shape[0] != bsz
    sn = 1 if shared else nb
    sb = (lambda b: 0) if shared else (lambda b: b)
    tok = lambda w: pl.BlockSpec((nb, tc, w), lambda b, c: (b, c, 0))
    hd = ML_HEAD_DIM
    st_specs = [pl.BlockSpec((nb, ML_HEADS, hd, hd), lambda b, c: (b, 0, 0, 0)),
                pl.BlockSpec((nb, 8, LANES), lambda b, c: (b, 0, 0)),
                pl.BlockSpec((nb, 8, LANES), lambda b, c: (b, 0, 0))]
    return pl.pallas_call(
        functools.partial(_mlstm_kernel, tc=tc, n_valid=n_valid, nb=nb, shared_state=shared),
        grid=(bsz // nb, t // tc),
        in_specs=[tok(ML_WIDTH), tok(ML_WIDTH), tok(ML_WIDTH), tok(LANES),
                  pl.BlockSpec((sn, ML_HEADS, hd, hd), lambda b, c: (sb(b), 0, 0, 0)),
                  pl.BlockSpec((sn, 8, LANES), lambda b, c: (sb(b), 0, 0)),
                  pl.BlockSpec((sn, 8, LANES), lambda b, c: (sb(b), 0, 0)),
                  _const_spec((ML_HEADS, hd, hd)), _const_spec((ML_HEADS, hd, hd)),
                  _const_spec((1, ML_WIDTH)), _const_spec((1, ML_WIDTH))],
        out_specs=[tok(ML_WIDTH)] + st_specs,
        out_shape=[jax.ShapeDtypeStruct((bsz, t, ML_WIDTH), BF16),
                   jax.ShapeDtypeStruct((bsz, ML_HEADS, hd, hd), F32),
                   jax.ShapeDtypeStruct((bsz, 8, LANES), F32),
                   jax.ShapeDtypeStruct((bsz, 8, LANES), F32)],
        scratch_shapes=[pltpu.VMEM((nb, ML_HEADS, hd, hd), F32),
                        pltpu.VMEM((nb, 8, LANES), F32), pltpu.VMEM((nb, 8, LANES), F32)],
        compiler_params=pltpu.CompilerParams(dimension_semantics=("arbitrary", "arbitrary")),
        name="mlstm",
    )(xc, mv, mo, gt, c0, n0, m0, wq, wk, gn, sk)


def _post_kernel(a_ref, hm_ref, ga_ref, gb_ref, x_ref, wfp_ref, wmp_ref, wo_ref, wgu_ref, wd_ref,
                 g1_ref, g2_ref, g3_ref, y_ref, *, d_ff, ff_cuts):
    ya = jnp.dot(a_ref[...], wfp_ref[...], preferred_element_type=F32)
    yb = jnp.dot(hm_ref[...], wmp_ref[...], preferred_element_type=F32)
    merged = jax.nn.sigmoid(ga_ref[...].astype(F32)) * ya + jax.nn.sigmoid(gb_ref[...].astype(F32)) * yb
    mix = jnp.dot(merged.astype(BF16), wo_ref[...], preferred_element_type=F32)
    x1 = x_ref[...] + _rms(mix, g1_ref[...])
    h2 = _rms(x1, g2_ref[...]).astype(BF16)
    f = None
    for lo, hi in zip(ff_cuts[:-1], ff_cuts[1:]):
        g = jnp.dot(h2, wgu_ref[:, lo:hi], preferred_element_type=F32)
        u = jnp.dot(h2, wgu_ref[:, d_ff + lo:d_ff + hi], preferred_element_type=F32)
        t = (g * jax.nn.sigmoid(g) * u).astype(BF16)
        fj = jnp.dot(t, wd_ref[lo:hi, :], preferred_element_type=F32)
        f = fj if f is None else f + fj
    y_ref[...] = x1 + _rms(f, g3_ref[...])


def _post(a, hm, ga, gb, x, wfp, wmp, wo, wgu, wd, g1, g2, g3, tm):
    n, d = x.shape
    d_ff = wd.shape[0]
    nmt = d_ff // MXU_TILE
    ff_cuts = tuple(MXU_TILE * ((nmt * j) // FF_CHUNKS) for j in range(FF_CHUNKS)) + (d_ff,) \
        if d_ff % MXU_TILE == 0 else (0, d_ff)
    row = lambda c: pl.BlockSpec((tm, c), lambda i: (i, 0))
    return pl.pallas_call(
        functools.partial(_post_kernel, d_ff=d_ff, ff_cuts=ff_cuts),
        grid=(n // tm,),
        in_specs=[row(FOX_WIDTH), row(ML_WIDTH), row(d), row(d), row(d),
                  _const_spec(wfp.shape), _const_spec(wmp.shape), _const_spec(wo.shape),
                  _const_spec(wgu.shape), _const_spec(wd.shape),
                  _const_spec((1, d)), _const_spec((1, d)), _const_spec((1, d))],
        out_specs=row(d),
        out_shape=jax.ShapeDtypeStruct((n, d), F32),
        compiler_params=pltpu.CompilerParams(dimension_semantics=("arbitrary",), vmem_limit_bytes=VMEM_LIMIT),
        name="post_mix_ffn",
    )(a, hm, ga, gb, x, wfp, wmp, wo, wgu, wd, g1, g2, g3)


def _pad_rows(a, rows, axis):
    pad = [(0, 0)] * a.ndim
    pad[axis] = (0, rows - a.shape[axis])
    return jnp.pad(a, pad)


def _row_tile(n, pref):
    t = min(pref, n)
    while n % t:
        t //= 2
    return t


def kernel(x_prompt, x_sample, cache_fox_k, cache_fox_v, cache_fox_logf, state_mlstm_C, state_mlstm_n,
           state_mlstm_m, state_mlstm_conv, meta_tokens, w_in, b_in, g_fq, g_fk, w_conv, b_conv, w_mq, w_mk,
           g_mnorm, skip_m, w_fox_proj, w_ml_proj, w_out, g_pre_mix, g_post_mix, g_pre_ffn, g_post_ffn,
           w_gate_up, w_down):
    assert w_in.shape[0] == 1, "single-layer trunk"
    bsz, seq, d = x_prompt.shape
    dbsz, dseq, _ = x_sample.shape
    past = cache_fox_k.shape[2]
    fw, mw = FOX_WIDTH, ML_WIDTH

    w, b = w_in[0], b_in[0]
    o_ff = 3 * fw
    o_mx = o_ff + FOX_HEADS
    o_mi = o_mx + 3 * mw
    o_mf = o_mi + ML_HEADS
    o_ga = o_mf + ML_HEADS

    def relayout(a):
        gates = jnp.concatenate([a[..., o_ff:o_mx], a[..., o_mi:o_mf], a[..., o_mf:o_ga], a[..., o_mf:o_ga]], axis=-1)
        gates = _pad_rows(gates, LANES, gates.ndim - 1)
        return jnp.concatenate([a[..., :o_ff], a[..., o_mx:o_mi], a[..., o_ga:], gates], axis=-1)

    w_p = relayout(w).astype(BF16)
    b_p = relayout(b)[None, :]
    hid = jnp.arange(fw) // FOX_HEAD_DIM
    hsum = (hid[:, None] == hid[None, :]).astype(BF16)
    gq = g_fq[0].reshape(1, fw)
    gk = g_fk[0].reshape(1, fw)
    gpre = g_pre_mix[0][None, :]
    wc, bc = w_conv[0], b_conv[0][None, :]
    inproj = lambda x, c0, cp, tm, nt, **kw: _inproj(x, gpre, w_p, b_p, gq, gk, hsum, c0, cp, wc, bc, tm, nt, **kw)

    wq, wk = w_mq[0].astype(BF16), w_mk[0].astype(BF16)
    gn, sk = g_mnorm[0].reshape(1, mw), skip_m[0][None, :]
    mlstm = lambda *a, **kw: _mlstm(*a, wq, wk, gn, sk, **kw)

    def conv_hist(rows):
        return jnp.pad(rows, ((0, 0), (8 - (CONV_WIDTH - 1), 0), (0, 0)))

    def m_rows(m):
        return jnp.broadcast_to(_pad_rows(m, 8, 1)[:, :, None], (m.shape[0], 8, LANES))

    mt = 64
    xm = _pad_rows(meta_tokens.astype(F32), mt, 0)
    (_, kf_m, ka_m, vf_m, va_m, xc_m, mv_m, mo_m, _, _, gt_m, fc_m, ct_m) = inproj(
        xm, jnp.zeros((1, 1, LANES), F32), jnp.zeros((1, 8, mw), F32), mt, 1, conv_valid=N_META)
    zc = jnp.zeros((1, ML_HEADS, ML_HEAD_DIM, ML_HEAD_DIM), F32)
    z8 = jnp.zeros((1, 8, LANES), F32)
    _, c_m, n_m, m_m = mlstm(xc_m[None], mv_m[None], mo_m[None], gt_m[None], zc, z8, z8, tc=mt, n_valid=N_META)

    xp = x_prompt.reshape(bsz * seq, d)
    tm = _row_tile(seq, 512)
    lanes_t = lambda rows: _pad_rows(rows[:N_META].T, LANES, 1)
    (qa_p, kt_p, ka_p, vt_p, va_p, xc_p, mv_p, mo_p, ga_p, gb_p, gt_p, _, ct_p, lt_p) = inproj(
        xp, fc_m[None, N_META - 1:N_META], ct_m, tm, seq // tm,
        meta_t=(lanes_t(kf_m), lanes_t(vf_m), lanes_t(gt_m[:, :FOX_HEADS])))
    r3 = lambda a: a.reshape(bsz, seq, a.shape[-1])
    hm_p, c_p, n_p, m_p = mlstm(r3(xc_p), r3(mv_p), r3(mo_p), r3(gt_p), c_m, n_m, m_m,
                                tc=_row_tile(seq, 256), nb=_row_tile(bsz, 4))
    a_p = _fox(r3(qa_p), _pad_rows(ka_m, LANES, 0)[None], _pad_rows(va_m, LANES, 0)[None], r3(ka_p), r3(va_p),
               tq=_row_tile(seq, 4096), tr=_row_tile(seq, 512), pblk=LANES, pvalid=N_META)

    tok_minor = lambda c: c[0].transpose(0, 2, 3, 1).reshape(dbsz, fw, past)
    kc, vc, f_c = _cacheprep(tok_minor(cache_fox_k), tok_minor(cache_fox_v),
                             _pad_rows(cache_fox_logf[0].astype(F32), LANES, 2), _row_tile(past, 512))
    xs = x_sample.reshape(dbsz * dseq, d)
    (qa_s, kf_s, ka_s, vf_s, va_s, xc_s, mv_s, mo_s, ga_s, gb_s, gt_s, _, ct_s) = inproj(
        xs, f_c, conv_hist(state_mlstm_conv[0]), dseq, 1)
    r3s = lambda a: a.reshape(dbsz, dseq, a.shape[-1])
    hm_s, c_s, n_s, m_s = mlstm(r3s(xc_s), r3s(mv_s), r3s(mo_s), r3s(gt_s), state_mlstm_C[0], _pad_rows(state_mlstm_n[0], 8, 1), m_rows(state_mlstm_m[0]),
                                tc=dseq, nb=_row_tile(dbsz, 4))
    a_s = _fox(r3s(qa_s), kc, vc, r3s(ka_s), r3s(va_s), tq=dseq, tr=dseq, pblk=_row_tile(past, 512), hps=FOX_HEADS)

    wfp, wmp, wo = w_fox_proj[0].astype(BF16), w_ml_proj[0].astype(BF16), w_out[0].astype(BF16)
    wgu, wd = w_gate_up[0].astype(BF16), w_down[0].astype(BF16)
    g1, g2, g3 = g_post_mix[0][None, :], g_pre_ffn[0][None, :], g_post_ffn[0][None, :]
    post = lambda a, hm, ga, gb, x, tm: _post(a, hm, ga, gb, x, wfp, wmp, wo, wgu, wd, g1, g2, g3, tm)
    y_p = post(a_p.reshape(bsz * seq, fw), hm_p.reshape(bsz * seq, mw), ga_p, gb_p, xp, _row_tile(bsz * seq, 512))
    y_s = post(a_s.reshape(dbsz * dseq, fw), hm_s.reshape(dbsz * dseq, mw), ga_s, gb_s, xs,
               _row_tile(dbsz * dseq, 512))

    def tokens_major(a):
        return a.reshape(bsz, FOX_HEADS, FOX_HEAD_DIM, a.shape[-1]).transpose(0, 3, 1, 2)[None]

    heads = lambda a: a.reshape(a.shape[:-1] + (FOX_HEADS, FOX_HEAD_DIM))
    tail = CONV_WIDTH - 1
    return (y_p.reshape(bsz, seq, d), y_s.reshape(dbsz, dseq, d),
            tokens_major(kt_p), tokens_major(vt_p), lt_p.transpose(0, 2, 1)[None],
            c_p[None], n_p[None, :, :ML_HEADS], m_p[None, :, :ML_HEADS, 0], ct_p[None, :, 8 - tail:],
            heads(r3s(kf_s))[None], heads(r3s(vf_s))[None], r3s(gt_s)[None, ..., :FOX_HEADS],
            c_s[None], n_s[None, :, :ML_HEADS], m_s[None, :, :ML_HEADS, 0], ct_s[None, :, 8 - tail:])
```

```python
import functools

import jax
import jax.numpy as jnp
from jax import lax
from jax.experimental import pallas as pl
from jax.experimental.pallas import tpu as pltpu

F32 = jnp.float32
BF16 = jnp.bfloat16

N_META = 16
FOX_HEADS = 8
FOX_HEAD_DIM = 64
FOX_WIDTH = FOX_HEADS * FOX_HEAD_DIM
ML_HEADS = 4
ML_HEAD_DIM = 128
ML_WIDTH = ML_HEADS * ML_HEAD_DIM
CONV_WIDTH = 4
RMS_EPS = 1e-6
PAD_LOG_GATE = -1e30
NEG_BIG = -1e30

LANES = 128
GATE_ML_I = FOX_HEADS
GATE_ML_F = FOX_HEADS + ML_HEADS

AUG_STEP = FOX_HEADS
AUG_QF = FOX_HEAD_DIM
AUG_KF = FOX_HEAD_DIM + 24

VMEM_LIMIT = 56 * 1024 * 1024
FOX_WIDE_KEYS = 1024
FOX_MAX_STATIC_BLOCKS = 4
MXU_TILE = 256
FF_CHUNKS = 2


def _const_spec(shape):
    nd = len(shape)
    return pl.BlockSpec(shape, lambda *_: (0,) * nd, pipeline_mode=pl.Buffered(1))


def _rms(x, g):
    return x * lax.rsqrt(jnp.mean(x * x, axis=-1, keepdims=True) + RMS_EPS) * g


def _split3(x):
    hi = x.astype(BF16).astype(F32)
    r = x - hi
    mid = r.astype(BF16).astype(F32)
    lo = (r - mid).astype(BF16).astype(F32)
    return hi, mid, lo


def _cumsum_rows(lg, cur):
    tm = lg.shape[0]
    blk = min(tm, LANES)
    r = lax.broadcasted_iota(jnp.int32, (blk, blk), 0)
    c = lax.broadcasted_iota(jnp.int32, (blk, blk), 1)
    tril = (c <= r).astype(BF16)
    outs = []
    for s in range(tm // blk):
        hi, mid, lo = _split3(lg[s * blk:(s + 1) * blk])
        f = (jnp.dot(tril, lo.astype(BF16), preferred_element_type=F32)
             + jnp.dot(tril, mid.astype(BF16), preferred_element_type=F32)
             + jnp.dot(tril, hi.astype(BF16), preferred_element_type=F32)) + cur
        outs.append(f)
        cur = f[blk - 1:blk]
    return (outs[0] if len(outs) == 1 else jnp.concatenate(outs, axis=0)), cur


def _store_attention_operands(qn, kn, v, fcol, q_ref, k_ref, v_ref, fillers=()):
    tm = fcol.shape[0]
    lane = lax.broadcasted_iota(jnp.int32, (tm, LANES), 1)
    data = lane < FOX_HEAD_DIM
    hi, mid, lo = _split3(fcol)
    comb = jnp.where(lane < AUG_STEP, hi, jnp.where(lane < 2 * AUG_STEP, pltpu.roll(mid, AUG_STEP, 1),
                                                     pltpu.roll(lo, 2 * AUG_STEP, 1)))
    ncomb = -comb
    is_qf = (lane == AUG_QF) | (lane == AUG_QF + AUG_STEP) | (lane == AUG_QF + 2 * AUG_STEP)
    is_kf = (lane == AUG_KF) | (lane == AUG_KF + AUG_STEP) | (lane == AUG_KF + 2 * AUG_STEP)
    base_q = jnp.where(is_kf, 1.0, 0.0)
    base_k = jnp.where(is_qf, 1.0, 0.0)
    base_v = jnp.where(lane == FOX_HEAD_DIM, 1.0, 0.0)
    fillers = list(fillers)
    for h in range(FOX_HEADS):
        t, odd = divmod(h, 2)

        def head_tile(z):
            x = z[:, t * LANES:(t + 1) * LANES]
            return pltpu.roll(x, FOX_HEAD_DIM, 1) if odd else x

        hs = slice(h * LANES, (h + 1) * LANES)
        if q_ref is not None:
            aug = jnp.where(is_qf, pltpu.roll(comb, AUG_QF - h, 1), base_q)
            q_ref[:, hs] = jnp.where(data, head_tile(qn), aug).astype(BF16)
        aug = jnp.where(is_kf, pltpu.roll(ncomb, AUG_KF - h, 1), base_k)
        k_ref[:, hs] = jnp.where(data, head_tile(kn), aug).astype(BF16)
        v_ref[:, hs] = jnp.where(data, head_tile(v), base_v).astype(BF16)
        if fillers:
            fillers.pop(0)()
    for f in fillers:
        f()


def _inproj_kernel(*refs, d_model, nt, kv_t, conv_valid):
    (x_ref, gpre_ref, w_ref, b_ref, gq_ref, gk_ref, hs_ref, c0_ref, cp_ref, wc_ref, bc_ref), rest = refs[:11], refs[11:]
    if kv_t:
        kmeta_ref, vmeta_ref, lmeta_ref = rest[:3]
        (q_ref, kf_ref, kb_ref, vf_ref, vb_ref, xc_ref, mv_ref, mo_ref, ga_ref, gb_ref, gt_ref, fc_ref, ct_ref,
         lt_ref, carry, xbuf, kcar, vcar, lcar) = rest[3:]
    else:
        (q_ref, kf_ref, kb_ref, vf_ref, vb_ref, xc_ref, mv_ref, mo_ref, ga_ref, gb_ref, gt_ref, fc_ref, ct_ref,
         carry, xbuf) = rest
    fw, mw = FOX_WIDTH, ML_WIDTH
    offs = [0, fw, 2 * fw, 3 * fw, 3 * fw + mw, 3 * fw + 2 * mw, 3 * fw + 3 * mw,
            3 * fw + 3 * mw + d_model, 3 * fw + 3 * mw + 2 * d_model, 3 * fw + 3 * mw + 2 * d_model + LANES]
    r = pl.program_id(1)

    @pl.when(r == 0)
    def _():
        carry[...] = c0_ref[...]
        xbuf[0:8, :] = cp_ref[...]
        if kv_t:
            kcar[...] = kmeta_ref[...]
            vcar[...] = vmeta_ref[...]
            lcar[...] = lmeta_ref[...]

    def store_shifted(xt, car, out_ref):
        sh = pltpu.roll(xt, N_META, 1)
        lane = lax.broadcasted_iota(jnp.int32, (xt.shape[0], LANES), 1)
        out_ref[:, :LANES] = jnp.where(lane < N_META, car[...], sh[:, :LANES])
        out_ref[:, LANES:] = sh[:, LANES:]
        car[...] = sh[:, :LANES]

    @pl.when(r < nt)
    def _():
        h = _rms(x_ref[...], gpre_ref[...]).astype(BF16)

        def seg(i):
            a, b = offs[i], offs[i + 1]
            return jnp.dot(h, w_ref[:, a:b], preferred_element_type=F32) + b_ref[:, a:b]

        def headnorm(z, g):
            ms = jnp.dot((z * z).astype(BF16), hs_ref[...], preferred_element_type=F32) * (1.0 / FOX_HEAD_DIM)
            return z * lax.rsqrt(ms + RMS_EPS) * g

        g = seg(8)
        ga_ref[...] = seg(6).astype(BF16)
        lane = lax.broadcasted_iota(jnp.int32, g.shape, 1)
        raw = (lane >= GATE_ML_I) & (lane < GATE_ML_F)
        gt = jnp.where(raw, g, jax.nn.log_sigmoid(g))
        gt_ref[...] = gt
        fcol, cur = _cumsum_rows(gt, carry[...])
        carry[...] = cur
        fc_ref[...] = fcol

        v = seg(2)
        q = headnorm(seg(0), gq_ref[...]) * (FOX_HEAD_DIM ** -0.5)
        k = headnorm(seg(1), gk_ref[...])
        def store_kv():
            if kv_t:
                store_shifted(k.T, kcar, kf_ref)
                store_shifted(v.T, vcar, vf_ref)
                store_shifted(gt.T[:FOX_HEADS], lcar, lt_ref)
            else:
                kf_ref[...] = k
                vf_ref[...] = v

        def store_xc():
            tm = x_ref.shape[0]
            xbuf[8:8 + tm, :] = seg(3)
            xc = bc_ref[...]
            for i in range(CONV_WIDTH):
                xc = xc + xbuf[5 + i:5 + i + tm, :] * wc_ref[i:i + 1, :]
            xc_ref[...] = (xc * jax.nn.sigmoid(xc)).astype(BF16)
            ct_ref[...] = xbuf[conv_valid:conv_valid + 8, :]
            xbuf[0:8, :] = xbuf[tm:tm + 8, :]

        def store_mv():
            mv_ref[...] = seg(4).astype(BF16)

        def store_mo():
            mo_ref[...] = jax.nn.sigmoid(seg(5)).astype(BF16)

        def store_gb():
            gb_ref[...] = seg(7).astype(BF16)

        _store_attention_operands(q, k, v, fcol, q_ref, kb_ref, vb_ref,
                                  fillers=(store_xc, store_mv, store_mo, store_gb, store_kv))

    if kv_t:
        @pl.when(r == nt)
        def _():
            kf_ref[:, :LANES] = kcar[...]
            vf_ref[:, :LANES] = vcar[...]
            lt_ref[:, :LANES] = lcar[...]


def _inproj(x, gpre, w, b, gq, gk, hs, c0, cp, wc, bc, tm, nt, meta_t=None, conv_valid=None):
    n, d = x.shape
    nb = n // (nt * tm)
    dinp = w.shape[1]
    fw, mw = FOX_WIDTH, ML_WIDTH
    aw = FOX_HEADS * LANES
    kv_t = meta_t is not None
    tile = lambda b, r: b * nt + jnp.minimum(r, nt - 1)
    row = lambda c: pl.BlockSpec((tm, c), lambda b, r: (tile(b, r), 0))
    cb = (lambda b: b) if c0.shape[0] > 1 else (lambda b: 0)
    pb = (lambda b: b) if cp.shape[0] > 1 else (lambda b: 0)
    lp = N_META + nt * tm
    tmin = lambda c: pl.BlockSpec((None, c, tm), lambda b, r: (b, 0, r))
    flat = lambda c, t: (row(c), jax.ShapeDtypeStruct((n, c), t))
    tok_minor = lambda c: (tmin(c), jax.ShapeDtypeStruct((nb, c, lp), F32))
    outs = [flat(aw, BF16), tok_minor(fw) if kv_t else flat(fw, F32), flat(aw, BF16),
            tok_minor(fw) if kv_t else flat(fw, F32), flat(aw, BF16), flat(mw, BF16), flat(mw, BF16), flat(mw, BF16),
            flat(d, BF16), flat(d, BF16), flat(LANES, F32), flat(LANES, F32),
            (pl.BlockSpec((None, 8, mw), lambda b, r: (b, 0, 0)), jax.ShapeDtypeStruct((nb, 8, mw), F32))]
    in_specs = [row(d), _const_spec((1, d)), _const_spec((d, dinp)), _const_spec((1, dinp)),
                _const_spec((1, fw)), _const_spec((1, fw)), _const_spec((fw, fw)),
                pl.BlockSpec((None, 1, LANES), lambda b, r: (cb(b), 0, 0)),
                pl.BlockSpec((None, 8, mw), lambda b, r: (pb(b), 0, 0)),
                _const_spec((CONV_WIDTH, mw)), _const_spec((1, mw))]
    scratch = [pltpu.VMEM((1, LANES), F32), pltpu.VMEM((8 + tm, mw), F32)]
    args = [x, gpre, w, b, gq, gk, hs, c0, cp, wc, bc]
    if kv_t:
        outs.append(tok_minor(FOX_HEADS))
        in_specs += [_const_spec((fw, LANES)), _const_spec((fw, LANES)), _const_spec((FOX_HEADS, LANES))]
        scratch += [pltpu.VMEM((fw, LANES), F32), pltpu.VMEM((fw, LANES), F32), pltpu.VMEM((FOX_HEADS, LANES), F32)]
        args += list(meta_t)
    return pl.pallas_call(
        functools.partial(_inproj_kernel, d_model=d, nt=nt, kv_t=kv_t,
                          conv_valid=tm if conv_valid is None else conv_valid),
        grid=(nb, nt + (1 if kv_t else 0)),
        in_specs=in_specs,
        out_specs=[o[0] for o in outs],
        out_shape=[o[1] for o in outs],
        scratch_shapes=scratch,
        compiler_params=pltpu.CompilerParams(dimension_semantics=("arbitrary", "arbitrary"),
                                             vmem_limit_bytes=VMEM_LIMIT),
        name="inproj",
    )(*args)


def _cacheprep_kernel(k_ref, v_ref, lf_ref, ka_ref, va_ref, fl_ref, carry):
    @pl.when(pl.program_id(1) == 0)
    def _():
        carry[...] = jnp.zeros_like(carry)

    fcol, cur = _cumsum_rows(lf_ref[...], carry[...])
    carry[...] = cur
    fl_ref[...] = cur
    _store_attention_operands(None, k_ref[...].T, v_ref[...].T, fcol, None, ka_ref, va_ref)


def _cacheprep(ck, cv, clf, tp):
    bsz, _, plen = ck.shape
    aw = FOX_HEADS * LANES
    return pl.pallas_call(
        _cacheprep_kernel,
        grid=(bsz, plen // tp),
        in_specs=[pl.BlockSpec((None, FOX_WIDTH, tp), lambda b, i: (b, 0, i)),
                  pl.BlockSpec((None, FOX_WIDTH, tp), lambda b, i: (b, 0, i)),
                  pl.BlockSpec((None, tp, LANES), lambda b, i: (b, i, 0))],
        out_specs=[pl.BlockSpec((None, tp, aw), lambda b, i: (b, i, 0)),
                   pl.BlockSpec((None, tp, aw), lambda b, i: (b, i, 0)),
                   pl.BlockSpec((None, 1, LANES), lambda b, i: (b, 0, 0))],
        out_shape=[jax.ShapeDtypeStruct((bsz, plen, aw), BF16), jax.ShapeDtypeStruct((bsz, plen, aw), BF16),
                   jax.ShapeDtypeStruct((bsz, 1, LANES), F32)],
        scratch_shapes=[pltpu.VMEM((1, LANES), F32)],
        compiler_params=pltpu.CompilerParams(dimension_semantics=("arbitrary", "arbitrary")),
        name="cache_prep",
    )(ck, cv, clf)


def _fox_kernel(q_ref, kp_ref, vp_ref, k_ref, v_ref, o_ref, *, tq, tr, tkb, nq, hps, pblk, n_pblk, pvalid):
    i = pl.program_id(2)
    nr = tq // tr
    chains = [(hh, r) for hh in range(hps) for r in range(nr)]
    qs = [q_ref[r * tr:(r + 1) * tr, hh * LANES:(hh + 1) * LANES] for hh, r in chains]

    def upd1(n, state, kb, vb, mask=None):
        m, acc = state
        hh = chains[n][0]
        hs = slice(hh * LANES, (hh + 1) * LANES)
        s = lax.dot_general(qs[n], kb[:, hs], (((1,), (1,)), ((), ())), preferred_element_type=F32)
        if mask is not None:
            cut = s.shape[1] - mask.shape[1]
            tail = jnp.where(mask, s[:, cut:], NEG_BIG)
            s = tail if cut == 0 else jnp.concatenate([s[:, :cut], tail], axis=1)
        m_new = jnp.maximum(m, jnp.max(s, axis=-1, keepdims=True))
        alpha = jnp.exp(m - m_new)
        pm = jnp.exp(s - m_new)
        acc = alpha * acc + jnp.dot(pm.astype(BF16), vb[:, hs], preferred_element_type=F32)
        return m_new, acc

    def upd(carry, kb, vb):
        return tuple(upd1(n, carry[n], kb, vb) for n in range(len(chains)))

    def start():
        carry = tuple((jnp.full((tr, 1), NEG_BIG, F32), jnp.zeros((tr, LANES), F32)) for _ in chains)
        if pvalid is None:
            for jb in range(n_pblk):
                sl = slice(jb * pblk, (jb + 1) * pblk)
                carry = upd(carry, kp_ref[sl, :], vp_ref[sl, :])
        return carry

    def finish(carry, off):
        tail_w = tr + (LANES if pvalid is not None else 0)
        rj = lax.broadcasted_iota(jnp.int32, (tr, tail_w), 0)
        cj = lax.broadcasted_iota(jnp.int32, (tr, tail_w), 1)
        mask = cj <= rj
        if pvalid is not None:
            mask = (mask | (cj >= tr)) & (cj < tr + pvalid)
        carry = list(carry)
        for r in reversed(range(nr)):
            width = (r + 1) * tr
            kb = k_ref[pl.ds(off, width), :]
            vb = v_ref[pl.ds(off, width), :]
            if pvalid is not None:
                kb = jnp.concatenate([kb, kp_ref[...]], axis=0)
                vb = jnp.concatenate([vb, vp_ref[...]], axis=0)
            for n, (hh, rr) in enumerate(chains):
                if rr == r:
                    carry[n] = upd1(n, carry[n], kb, vb, mask=mask)
        lane = lax.broadcasted_iota(jnp.int32, (tr, LANES), 1)
        outs = []
        for n in range(len(chains)):
            acc = carry[n][1]
            l = jnp.sum(jnp.where(lane == FOX_HEAD_DIM, acc, 0.0), axis=-1, keepdims=True)
            outs.append(acc / l)
        for hp in range(hps // 2):
            for r in range(nr):
                even, odd = outs[2 * hp * nr + r], outs[(2 * hp + 1) * nr + r]
                o_ref[r * tr:(r + 1) * tr, hp * LANES:(hp + 1) * LANES] = jnp.where(
                    lane < FOX_HEAD_DIM, even, pltpu.roll(odd, FOX_HEAD_DIM, 1)).astype(o_ref.dtype)

    if tq % tkb == 0 and nq <= FOX_MAX_STATIC_BLOCKS:
        for qi in range(nq):
            @pl.when(i == qi)
            def _():
                carry = start()
                for j in range(qi * tq // tkb):
                    carry = upd(carry, k_ref[j * tkb:(j + 1) * tkb, :], v_ref[j * tkb:(j + 1) * tkb, :])
                finish(carry, qi * tq)
    else:
        tkw = max(tkb, tq)

        def body_wide(j, carry):
            off = pl.multiple_of(j * tkw, tkw)
            return upd(carry, k_ref[pl.ds(off, tkw), :], v_ref[pl.ds(off, tkw), :])

        def body(j, carry):
            off = pl.multiple_of(j * tq, tq)
            return upd(carry, k_ref[pl.ds(off, tq), :], v_ref[pl.ds(off, tq), :])

        n_wide = i // (tkw // tq)
        carry = lax.fori_loop(0, n_wide, body_wide, start())
        if tkw > tq:
            carry = lax.fori_loop(n_wide * (tkw // tq), i, body, carry)
        finish(carry, pl.multiple_of(i * tq, tq))


def _fox(q, kp, vp, k, v, tq, tr, pblk, pvalid=None, hps=2):
    bsz, t, _ = q.shape
    bp, plen, _ = kp.shape
    nq = t // tq
    n_pblk = plen // pblk
    pw = hps * LANES
    pb = (lambda b: b) if bp == bsz else (lambda b: 0)
    return pl.pallas_call(
        functools.partial(_fox_kernel, tq=tq, tr=tr, tkb=_row_tile(t, FOX_WIDE_KEYS), nq=nq, hps=hps, pblk=pblk,
                          n_pblk=n_pblk, pvalid=pvalid),
        grid=(bsz, FOX_HEADS // hps, nq),
        in_specs=[pl.BlockSpec((None, tq, pw), lambda b, p, i: (b, i, p)),
                  pl.BlockSpec((None, plen, pw), lambda b, p, i: (pb(b), 0, p)),
                  pl.BlockSpec((None, plen, pw), lambda b, p, i: (pb(b), 0, p)),
                  pl.BlockSpec((None, t, pw), lambda b, p, i: (b, 0, p)),
                  pl.BlockSpec((None, t, pw), lambda b, p, i: (b, 0, p))],
        out_specs=pl.BlockSpec((None, tq, hps * FOX_HEAD_DIM), lambda b, p, i: (b, i, p)),
        out_shape=jax.ShapeDtypeStruct((bsz, t, FOX_WIDTH), BF16),
        compiler_params=pltpu.CompilerParams(dimension_semantics=("arbitrary", "arbitrary", "arbitrary"),
                                             vmem_limit_bytes=VMEM_LIMIT),
        name="fox_attention",
    )(q, kp, vp, k, v)


def _mlstm_kernel(xc_ref, mv_ref, mo_ref, gt_ref, c0_ref, n0_ref, m0_ref, wq_ref, wk_ref, gn_ref, sk_ref,
                  hm_ref, cout_ref, nout_ref, mout_ref,
                  c_scr, n_scr, m_scr, *, tc, n_valid, nb, shared_state):
    c = pl.program_id(1)

    @pl.when(c == 0)
    def _():
        for bi in range(nb):
            si = 0 if shared_state else bi
            c_scr[bi] = c0_ref[si]
            n_scr[bi] = n0_ref[si]
            m_scr[bi] = m0_ref[si]

    for bi in range(nb):
        _mlstm_chunk(xc_ref.at[bi], mv_ref.at[bi], mo_ref.at[bi], gt_ref.at[bi], wq_ref, wk_ref,
                     gn_ref, sk_ref, hm_ref.at[bi], c_scr.at[bi], n_scr.at[bi], m_scr.at[bi],
                     tc=tc, n_valid=n_valid)

    @pl.when(c == pl.num_programs(1) - 1)
    def _():
        cout_ref[...] = c_scr[...]
        nout_ref[...] = n_scr[...]
        mout_ref[...] = m_scr[...]


def _mlstm_chunk(xc_ref, mv_ref, mo_ref, gt_ref, wq_ref, wk_ref, gn_ref, sk_ref, hm_ref,
                 c_scr, n_scr, m_scr, *, tc, n_valid):
    gt = gt_ref[...]
    lane = lax.broadcasted_iota(jnp.int32, (tc, LANES), 1)
    if n_valid is not None:
        rowi = lax.broadcasted_iota(jnp.int32, (tc, LANES), 0)
        is_i = (lane >= GATE_ML_I) & (lane < GATE_ML_F)
        gt = jnp.where(rowi < n_valid, gt, jnp.where(is_i, PAD_LOG_GATE, 0.0))
    ri = lax.broadcasted_iota(jnp.int32, (tc, tc), 0)
    ci = lax.broadcasted_iota(jnp.int32, (tc, tc), 1)
    causal = ci <= ri
    b_all = _cumsum_rows(gt, jnp.zeros((1, LANES), F32))[0]
    r_t = (pltpu.roll(gt, GATE_ML_F - GATE_ML_I, 1) - b_all).T
    heads = range(ML_HEADS)
    hsl = [slice(h * ML_HEAD_DIM, (h + 1) * ML_HEAD_DIM) for h in heads]
    b_col = [b_all[:, GATE_ML_F + h:GATE_ML_F + h + 1] for h in heads]
    ig_col = [gt[:, GATE_ML_I + h:GATE_ML_I + h + 1] for h in heads]
    m_prev = [m_scr[h:h + 1, 0:1] for h in heads]
    dmat = [jnp.where(causal, b_col[h] + r_t[GATE_ML_F + h:GATE_ML_F + h + 1, :], NEG_BIG) for h in heads]
    m_inter = [b_col[h] + m_prev[h] for h in heads]
    m_t = [jnp.maximum(m_inter[h], jnp.max(dmat[h], axis=-1, keepdims=True)) for h in heads]
    w_inter = [jnp.exp(m_inter[h] - m_t[h]) for h in heads]

    xhb = [xc_ref[:, hsl[h]] for h in heads]
    q = [jnp.dot(xhb[h], wq_ref[h], preferred_element_type=F32) for h in heads]
    k = [jnp.dot(xhb[h], wk_ref[h], preferred_element_type=F32) * (ML_HEAD_DIM ** -0.5) for h in heads]
    qb = [x.astype(BF16) for x in q]
    kb = [x.astype(BF16) for x in k]
    v = [mv_ref[:, hsl[h]] for h in heads]
    smat = [lax.dot_general(qb[h], kb[h], (((1,), (1,)), ((), ())), preferred_element_type=F32) for h in heads]
    amat = [jnp.exp(dmat[h] - m_t[h]) * smat[h] for h in heads]
    ch = [c_scr[h] for h in heads]
    nh = [n_scr[h:h + 1, :] for h in heads]
    num = [jnp.dot(amat[h].astype(BF16), v[h], preferred_element_type=F32)
           + w_inter[h] * jnp.dot(qb[h], ch[h].astype(BF16), preferred_element_type=F32) for h in heads]
    den = [jnp.sum(amat[h], axis=-1, keepdims=True) + w_inter[h] * jnp.sum(q[h] * nh[h], axis=-1, keepdims=True)
           for h in heads]
    hcell = [num[h] / jnp.maximum(jnp.abs(den[h]), jnp.exp(-m_t[h])) for h in heads]

    b_last = [b_col[h][tc - 1:tc, :] for h in heads]
    g = [b_last[h] - b_col[h] + ig_col[h] for h in heads]
    m_new = [jnp.maximum(b_last[h] + m_prev[h], jnp.max(g[h], axis=0, keepdims=True)) for h in heads]
    decay = [jnp.exp(b_last[h] + m_prev[h] - m_new[h]) for h in heads]
    kw = [k[h] * jnp.exp(g[h] - m_new[h]) for h in heads]
    for h in heads:
        c_scr[h] = decay[h] * ch[h] + lax.dot_general(kw[h].astype(BF16), v[h], (((0,), (0,)), ((), ())),
                                                       preferred_element_type=F32)
        n_scr[h:h + 1, :] = decay[h] * nh[h] + jnp.sum(kw[h], axis=0, keepdims=True)
        m_scr[h:h + 1, :] = jnp.broadcast_to(m_new[h], (1, LANES))
        y = mo_ref[:, hsl[h]].astype(F32) * hcell[h]
        hm_ref[:, hsl[h]] = (_rms(y, gn_ref[:, hsl[h]]) + sk_ref[:, hsl[h]] * xhb[h].astype(F32)).astype(hm_ref.dtype)


def _mlstm(xc, mv, mo, gt, c0, n0, m0, wq, wk, gn, sk, tc, nb=1, n_valid=None):
    bsz, t, _ = xc.shape
    shared = c0.shape[0] != bsz
    sn = 1 if shared else nb
    sb = (lambda b: 0) if shared else (lambda b: b)
    tok = lambda w: pl.BlockSpec((nb, tc, w), lambda b, c: (b, c, 0))
    hd = ML_HEAD_DIM
    st_specs = [pl.BlockSpec((nb, ML_HEADS, hd, hd), lambda b, c: (b, 0, 0, 0)),
                pl.BlockSpec((nb, 8, LANES), lambda b, c: (b, 0, 0)),
                pl.BlockSpec((nb, 8, LANES), lambda b, c: (b, 0, 0))]
    return pl.pallas_call(
        functools.partial(_mlstm_kernel, tc=tc, n_valid=n_valid, nb=nb, shared_state=shared),
        grid=(bsz // nb, t // tc),
        in_specs=[tok(ML_WIDTH), tok(ML_WIDTH), tok(ML_WIDTH), tok(LANES),
                  pl.BlockSpec((sn, ML_HEADS, hd, hd), lambda b, c: (sb(b), 0, 0, 0)),
                  pl.BlockSpec((sn, 8, LANES), lambda b, c: (sb(b), 0, 0)),
                  pl.BlockSpec((sn, 8, LANES), lambda b, c: (sb(b), 0, 0)),
                  _const_spec((ML_HEADS, hd, hd)), _const_spec((ML_HEADS, hd, hd)),
                  _const_spec((1, ML_WIDTH)), _const_spec((1, ML_WIDTH))],
        out_specs=[tok(ML_WIDTH)] + st_specs,
        out_shape=[jax.ShapeDtypeStruct((bsz, t, ML_WIDTH), BF16),
                   jax.ShapeDtypeStruct((bsz, ML_HEADS, hd, hd), F32),
                   jax.ShapeDtypeStruct((bsz, 8, LANES), F32),
                   jax.ShapeDtypeStruct((bsz, 8, LANES), F32)],
        scratch_shapes=[pltpu.VMEM((nb, ML_HEADS, hd, hd), F32),
                        pltpu.VMEM((nb, 8, LANES), F32), pltpu.VMEM((nb, 8, LANES), F32)],
        compiler_params=pltpu.CompilerParams(dimension_semantics=("arbitrary", "arbitrary")),
        name="mlstm",
    )(xc, mv, mo, gt, c0, n0, m0, wq, wk, gn, sk)


def _post_kernel(a_ref, hm_ref, ga_ref, gb_ref, x_ref, wfp_ref, wmp_ref, wo_ref, wgu_ref, wd_ref,
                 g1_ref, g2_ref, g3_ref, y_ref, *, d_ff, ff_cuts):
    ya = jnp.dot(a_ref[...], wfp_ref[...], preferred_element_type=F32)
    yb = jnp.dot(hm_ref[...], wmp_ref[...], preferred_element_type=F32)
    merged = jax.nn.sigmoid(ga_ref[...].astype(F32)) * ya + jax.nn.sigmoid(gb_ref[...].astype(F32)) * yb
    mix = jnp.dot(merged.astype(BF16), wo_ref[...], preferred_element_type=F32)
    x1 = x_ref[...] + _rms(mix, g1_ref[...])
    h2 = _rms(x1, g2_ref[...]).astype(BF16)
    f = None
    for lo, hi in zip(ff_cuts[:-1], ff_cuts[1:]):
        g = jnp.dot(h2, wgu_ref[:, lo:hi], preferred_element_type=F32)
        u = jnp.dot(h2, wgu_ref[:, d_ff + lo:d_ff + hi], preferred_element_type=F32)
        t = (g * jax.nn.sigmoid(g) * u).astype(BF16)
        fj = jnp.dot(t, wd_ref[lo:hi, :], preferred_element_type=F32)
        f = fj if f is None else f + fj
    y_ref[...] = x1 + _rms(f, g3_ref[...])


def _post(a, hm, ga, gb, x, wfp, wmp, wo, wgu, wd, g1, g2, g3, tm):
    n, d = x.shape
    d_ff = wd.shape[0]
    nmt = d_ff // MXU_TILE
    ff_cuts = tuple(MXU_TILE * ((nmt * j) // FF_CHUNKS) for j in range(FF_CHUNKS)) + (d_ff,) \
        if d_ff % MXU_TILE == 0 else (0, d_ff)
    row = lambda c: pl.BlockSpec((tm, c), lambda i: (i, 0))
    return pl.pallas_call(
        functools.partial(_post_kernel, d_ff=d_ff, ff_cuts=ff_cuts),
        grid=(n // tm,),
        in_specs=[row(FOX_WIDTH), row(ML_WIDTH), row(d), row(d), row(d),
                  _const_spec(wfp.shape), _const_spec(wmp.shape), _const_spec(wo.shape),
                  _const_spec(wgu.shape), _const_spec(wd.shape),
                  _const_spec((1, d)), _const_spec((1, d)), _const_spec((1, d))],
        out_specs=row(d),
        out_shape=jax.ShapeDtypeStruct((n, d), F32),
        compiler_params=pltpu.CompilerParams(dimension_semantics=("arbitrary",), vmem_limit_bytes=VMEM_LIMIT),
        name="post_mix_ffn",
    )(a, hm, ga, gb, x, wfp, wmp, wo, wgu, wd, g1, g2, g3)


def _pad_rows(a, rows, axis):
    pad = [(0, 0)] * a.ndim
    pad[axis] = (0, rows - a.shape[axis])
    return jnp.pad(a, pad)


def _row_tile(n, pref):
    t = min(pref, n)
    while n % t:
        t //= 2
    return t


def kernel(x_prompt, x_sample, cache_fox_k, cache_fox_v, cache_fox_logf, state_mlstm_C, state_mlstm_n,
           state_mlstm_m, state_mlstm_conv, meta_tokens, w_in, b_in, g_fq, g_fk, w_conv, b_conv, w_mq, w_mk,
           g_mnorm, skip_m, w_fox_proj, w_ml_proj, w_out, g_pre_mix, g_post_mix, g_pre_ffn, g_post_ffn,
           w_gate_up, w_down):
    assert w_in.shape[0] == 1, "single-layer trunk"
    bsz, seq, d = x_prompt.shape
    dbsz, dseq, _ = x_sample.shape
    past = cache_fox_k.shape[2]
    fw, mw = FOX_WIDTH, ML_WIDTH

    w, b = w_in[0], b_in[0]
    o_ff = 3 * fw
    o_mx = o_ff + FOX_HEADS
    o_mi = o_mx + 3 * mw
    o_mf = o_mi + ML_HEADS
    o_ga = o_mf + ML_HEADS

    def relayout(a):
        gates = jnp.concatenate([a[..., o_ff:o_mx], a[..., o_mi:o_mf], a[..., o_mf:o_ga]], axis=-1)
        gates = _pad_rows(gates, LANES, gates.ndim - 1)
        return jnp.concatenate([a[..., :o_ff], a[..., o_mx:o_mi], a[..., o_ga:], gates], axis=-1)

    w_p = relayout(w).astype(BF16)
    b_p = relayout(b)[None, :]
    hid = jnp.arange(fw) // FOX_HEAD_DIM
    hsum = (hid[:, None] == hid[None, :]).astype(BF16)
    gq = g_fq[0].reshape(1, fw)
    gk = g_fk[0].reshape(1, fw)
    gpre = g_pre_mix[0][None, :]
    wc, bc = w_conv[0], b_conv[0][None, :]
    inproj = lambda x, c0, cp, tm, nt, **kw: _inproj(x, gpre, w_p, b_p, gq, gk, hsum, c0, cp, wc, bc, tm, nt, **kw)

    wq, wk = w_mq[0].astype(BF16), w_mk[0].astype(BF16)
    gn, sk = g_mnorm[0].reshape(1, mw), skip_m[0][None, :]
    mlstm = lambda *a, **kw: _mlstm(*a, wq, wk, gn, sk, **kw)

    def conv_hist(rows):
        return jnp.pad(rows, ((0, 0), (8 - (CONV_WIDTH - 1), 0), (0, 0)))

    def m_rows(m):
        return jnp.broadcast_to(_pad_rows(m, 8, 1)[:, :, None], (m.shape[0], 8, LANES))

    mt = 64
    xm = _pad_rows(meta_tokens.astype(F32), mt, 0)
    (_, kf_m, ka_m, vf_m, va_m, xc_m, mv_m, mo_m, _, _, gt_m, fc_m, ct_m) = inproj(
        xm, jnp.zeros((1, 1, LANES), F32), jnp.zeros((1, 8, mw), F32), mt, 1, conv_valid=N_META)
    zc = jnp.zeros((1, ML_HEADS, ML_HEAD_DIM, ML_HEAD_DIM), F32)
    z8 = jnp.zeros((1, 8, LANES), F32)
    _, c_m, n_m, m_m = mlstm(xc_m[None], mv_m[None], mo_m[None], gt_m[None], zc, z8, z8, tc=mt, n_valid=N_META)

    xp = x_prompt.reshape(bsz * seq, d)
    tm = _row_tile(seq, 512)
    lanes_t = lambda rows: _pad_rows(rows[:N_META].T, LANES, 1)
    (qa_p, kt_p, ka_p, vt_p, va_p, xc_p, mv_p, mo_p, ga_p, gb_p, gt_p, _, ct_p, lt_p) = inproj(
        xp, fc_m[None, N_META - 1:N_META], ct_m, tm, seq // tm,
        meta_t=(lanes_t(kf_m), lanes_t(vf_m), lanes_t(gt_m[:, :FOX_HEADS])))
    r3 = lambda a: a.reshape(bsz, seq, a.shape[-1])
    hm_p, c_p, n_p, m_p = mlstm(r3(xc_p), r3(mv_p), r3(mo_p), r3(gt_p), c_m, n_m, m_m,
                                tc=_row_tile(seq, 256), nb=_row_tile(bsz, 4))
    a_p = _fox(r3(qa_p), _pad_rows(ka_m, LANES, 0)[None], _pad_rows(va_m, LANES, 0)[None], r3(ka_p), r3(va_p),
               tq=_row_tile(seq, 4096), tr=_row_tile(seq, 512), pblk=LANES, pvalid=N_META)

    tok_minor = lambda c: c[0].transpose(0, 2, 3, 1).reshape(dbsz, fw, past)
    kc, vc, f_c = _cacheprep(tok_minor(cache_fox_k), tok_minor(cache_fox_v),
                             _pad_rows(cache_fox_logf[0].astype(F32), LANES, 2), _row_tile(past, 512))
    xs = x_sample.reshape(dbsz * dseq, d)
    (qa_s, kf_s, ka_s, vf_s, va_s, xc_s, mv_s, mo_s, ga_s, gb_s, gt_s, _, ct_s) = inproj(
        xs, f_c, conv_hist(state_mlstm_conv[0]), dseq, 1)
    r3s = lambda a: a.reshape(dbsz, dseq, a.shape[-1])
    hm_s, c_s, n_s, m_s = mlstm(r3s(xc_s), r3s(mv_s), r3s(mo_s), r3s(gt_s), state_mlstm_C[0], _pad_rows(state_mlstm_n[0], 8, 1), m_rows(state_mlstm_m[0]),
                                tc=dseq, nb=_row_tile(dbsz, 4))
    a_s = _fox(r3s(qa_s), kc, vc, r3s(ka_s), r3s(va_s), tq=dseq, tr=dseq, pblk=_row_tile(past, 512), hps=FOX_HEADS)

    wfp, wmp, wo = w_fox_proj[0].astype(BF16), w_ml_proj[0].astype(BF16), w_out[0].astype(BF16)
    wgu, wd = w_gate_up[0].astype(BF16), w_down[0].astype(BF16)
    g1, g2, g3 = g_post_mix[0][None, :], g_pre_ffn[0][None, :], g_post_ffn[0][None, :]
    post = lambda a, hm, ga, gb, x, tm: _post(a, hm, ga, gb, x, wfp, wmp, wo, wgu, wd, g1, g2, g3, tm)
    y_p = post(a_p.reshape(bsz * seq, fw), hm_p.reshape(bsz * seq, mw), ga_p, gb_p, xp, _row_tile(bsz * seq, 512))
    y_s = post(a_s.reshape(dbsz * dseq, fw), hm_s.reshape(dbsz * dseq, mw), ga_s, gb_s, xs,
               _row_tile(dbsz * dseq, 512))

    def tokens_major(a):
        return a.reshape(bsz, FOX_HEADS, FOX_HEAD_DIM, a.shape[-1]).transpose(0, 3, 1, 2)[None]

    heads = lambda a: a.reshape(a.shape[:-1] + (FOX_HEADS, FOX_HEAD_DIM))
    tail = CONV_WIDTH - 1
    return (y_p.reshape(bsz, seq, d), y_s.reshape(dbsz, dseq, d),
            tokens_major(kt_p), tokens_major(vt_p), lt_p.transpose(0, 2, 1)[None],
            c_p[None], n_p[None, :, :ML_HEADS], m_p[None, :, :ML_HEADS, 0], ct_p[None, :, 8 - tail:],
            heads(r3s(kf_s))[None], heads(r3s(vf_s))[None], r3s(gt_s)[None, ..., :FOX_HEADS],
            c_s[None], n_s[None, :, :ML_HEADS], m_s[None, :, :ML_HEADS, 0], ct_s[None, :, 8 - tail:])
```

```python
import functools

import jax
import jax.numpy as jnp
from jax import lax
from jax.experimental import pallas as pl
from jax.experimental.pallas import tpu as pltpu

F32 = jnp.float32
BF16 = jnp.bfloat16

N_META = 16
FOX_HEADS = 8
FOX_HEAD_DIM = 64
FOX_WIDTH = FOX_HEADS * FOX_HEAD_DIM
ML_HEADS = 4
ML_HEAD_DIM = 128
ML_WIDTH = ML_HEADS * ML_HEAD_DIM
CONV_WIDTH = 4
RMS_EPS = 1e-6
PAD_LOG_GATE = -1e30
NEG_BIG = -1e30

LANES = 128
GATE_ML_I = FOX_HEADS
GATE_ML_F = FOX_HEADS + ML_HEADS

AUG_STEP = FOX_HEADS
AUG_QF = FOX_HEAD_DIM
AUG_KF = FOX_HEAD_DIM + 24

VMEM_LIMIT = 56 * 1024 * 1024
FOX_WIDE_KEYS = 1024
FOX_MAX_STATIC_BLOCKS = 4
MXU_TILE = 256
FF_CHUNKS = 2


def _const_spec(shape):
    nd = len(shape)
    return pl.BlockSpec(shape, lambda *_: (0,) * nd, pipeline_mode=pl.Buffered(1))


def _rms(x, g):
    return x * lax.rsqrt(jnp.mean(x * x, axis=-1, keepdims=True) + RMS_EPS) * g


def _split3(x):
    hi = x.astype(BF16).astype(F32)
    r = x - hi
    mid = r.astype(BF16).astype(F32)
    lo = (r - mid).astype(BF16).astype(F32)
    return hi, mid, lo


def _cumsum_rows(lg, cur):
    tm = lg.shape[0]
    blk = min(tm, LANES)
    r = lax.broadcasted_iota(jnp.int32, (blk, blk), 0)
    c = lax.broadcasted_iota(jnp.int32, (blk, blk), 1)
    tril = (c <= r).astype(BF16)
    outs = []
    for s in range(tm // blk):
        hi, mid, lo = _split3(lg[s * blk:(s + 1) * blk])
        f = (jnp.dot(tril, lo.astype(BF16), preferred_element_type=F32)
             + jnp.dot(tril, mid.astype(BF16), preferred_element_type=F32)
             + jnp.dot(tril, hi.astype(BF16), preferred_element_type=F32)) + cur
        outs.append(f)
        cur = f[blk - 1:blk]
    return (outs[0] if len(outs) == 1 else jnp.concatenate(outs, axis=0)), cur


def _store_attention_operands(qn, kn, v, fcol, q_ref, k_ref, v_ref, fillers=()):
    tm = fcol.shape[0]
    lane = lax.broadcasted_iota(jnp.int32, (tm, LANES), 1)
    data = lane < FOX_HEAD_DIM
    hi, mid, lo = _split3(fcol)
    comb = jnp.where(lane < AUG_STEP, hi, jnp.where(lane < 2 * AUG_STEP, pltpu.roll(mid, AUG_STEP, 1),
                                                     pltpu.roll(lo, 2 * AUG_STEP, 1)))
    ncomb = -comb
    is_qf = (lane == AUG_QF) | (lane == AUG_QF + AUG_STEP) | (lane == AUG_QF + 2 * AUG_STEP)
    is_kf = (lane == AUG_KF) | (lane == AUG_KF + AUG_STEP) | (lane == AUG_KF + 2 * AUG_STEP)
    base_q = jnp.where(is_kf, 1.0, 0.0)
    base_k = jnp.where(is_qf, 1.0, 0.0)
    base_v = jnp.where(lane == FOX_HEAD_DIM, 1.0, 0.0)
    fillers = list(fillers)
    for h in range(FOX_HEADS):
        t, odd = divmod(h, 2)

        def head_tile(z):
            x = z[:, t * LANES:(t + 1) * LANES]
            return pltpu.roll(x, FOX_HEAD_DIM, 1) if odd else x

        hs = slice(h * LANES, (h + 1) * LANES)
        if q_ref is not None:
            aug = jnp.where(is_qf, pltpu.roll(comb, AUG_QF - h, 1), base_q)
            q_ref[:, hs] = jnp.where(data, head_tile(qn), aug).astype(BF16)
        aug = jnp.where(is_kf, pltpu.roll(ncomb, AUG_KF - h, 1), base_k)
        k_ref[:, hs] = jnp.where(data, head_tile(kn), aug).astype(BF16)
        v_ref[:, hs] = jnp.where(data, head_tile(v), base_v).astype(BF16)
        if fillers:
            fillers.pop(0)()
    for f in fillers:
        f()


def _inproj_kernel(*refs, d_model, nt, kv_t, conv_valid):
    (x_ref, gpre_ref, w_ref, b_ref, gq_ref, gk_ref, hs_ref, c0_ref, cp_ref, wc_ref, bc_ref), rest = refs[:11], refs[11:]
    if kv_t:
        kmeta_ref, vmeta_ref, lmeta_ref = rest[:3]
        (q_ref, kf_ref, kb_ref, vf_ref, vb_ref, xc_ref, mv_ref, mo_ref, ga_ref, gb_ref, gt_ref, fc_ref, ct_ref,
         lt_ref, carry, xbuf, kcar, vcar, lcar) = rest[3:]
    else:
        (q_ref, kf_ref, kb_ref, vf_ref, vb_ref, xc_ref, mv_ref, mo_ref, ga_ref, gb_ref, gt_ref, fc_ref, ct_ref,
         carry, xbuf) = rest
    fw, mw = FOX_WIDTH, ML_WIDTH
    offs = [0, fw, 2 * fw, 3 * fw, 3 * fw + mw, 3 * fw + 2 * mw, 3 * fw + 3 * mw,
            3 * fw + 3 * mw + d_model, 3 * fw + 3 * mw + 2 * d_model, 3 * fw + 3 * mw + 2 * d_model + LANES]
    r = pl.program_id(1)

    @pl.when(r == 0)
    def _():
        carry[...] = c0_ref[...]
        xbuf[0:8, :] = cp_ref[...]
        if kv_t:
            kcar[...] = kmeta_ref[...]
            vcar[...] = vmeta_ref[...]
            lcar[...] = lmeta_ref[...]

    def store_shifted(xt, car, out_ref):
        sh = pltpu.roll(xt, N_META, 1)
        lane = lax.broadcasted_iota(jnp.int32, (xt.shape[0], LANES), 1)
        out_ref[:, :LANES] = jnp.where(lane < N_META, car[...], sh[:, :LANES])
        out_ref[:, LANES:] = sh[:, LANES:]
        car[...] = sh[:, :LANES]

    @pl.when(r < nt)
    def _():
        h = _rms(x_ref[...], gpre_ref[...]).astype(BF16)

        def seg(i):
            a, b = offs[i], offs[i + 1]
            return jnp.dot(h, w_ref[:, a:b], preferred_element_type=F32) + b_ref[:, a:b]

        def headnorm(z, g):
            ms = jnp.dot((z * z).astype(BF16), hs_ref[...], preferred_element_type=F32) * (1.0 / FOX_HEAD_DIM)
            return z * lax.rsqrt(ms + RMS_EPS) * g

        g = seg(8)
        ga_ref[...] = seg(6).astype(BF16)
        lane = lax.broadcasted_iota(jnp.int32, g.shape, 1)
        raw = (lane >= GATE_ML_I) & (lane < GATE_ML_F)
        gt = jnp.where(raw, g, jax.nn.log_sigmoid(g))
        gt_ref[...] = gt
        fcol, cur = _cumsum_rows(gt, carry[...])
        carry[...] = cur
        fc_ref[...] = fcol

        v = seg(2)
        q = headnorm(seg(0), gq_ref[...]) * (FOX_HEAD_DIM ** -0.5)
        k = headnorm(seg(1), gk_ref[...])
        def store_kv():
            if kv_t:
                store_shifted(k.T, kcar, kf_ref)
                store_shifted(v.T, vcar, vf_ref)
                store_shifted(gt.T[:FOX_HEADS], lcar, lt_ref)
            else:
                kf_ref[...] = k
                vf_ref[...] = v

        def store_xc():
            tm = x_ref.shape[0]
            xbuf[8:8 + tm, :] = seg(3)
            xc = bc_ref[...]
            for i in range(CONV_WIDTH):
                xc = xc + xbuf[5 + i:5 + i + tm, :] * wc_ref[i:i + 1, :]
            xc_ref[...] = (xc * jax.nn.sigmoid(xc)).astype(BF16)
            ct_ref[...] = xbuf[conv_valid:conv_valid + 8, :]
            xbuf[0:8, :] = xbuf[tm:tm + 8, :]

        def store_mv():
            mv_ref[...] = seg(4).astype(BF16)

        def store_mo():
            mo_ref[...] = jax.nn.sigmoid(seg(5)).astype(BF16)

        def store_gb():
            gb_ref[...] = seg(7).astype(BF16)

        _store_attention_operands(q, k, v, fcol, q_ref, kb_ref, vb_ref,
                                  fillers=(store_xc, store_mv, store_mo, store_gb, store_kv))

    if kv_t:
        @pl.when(r == nt)
        def _():
            kf_ref[:, :LANES] = kcar[...]
            vf_ref[:, :LANES] = vcar[...]
            lt_ref[:, :LANES] = lcar[...]


def _inproj(x, gpre, w, b, gq, gk, hs, c0, cp, wc, bc, tm, nt, meta_t=None, conv_valid=None):
    n, d = x.shape
    nb = n // (nt * tm)
    dinp = w.shape[1]
    fw, mw = FOX_WIDTH, ML_WIDTH
    aw = FOX_HEADS * LANES
    kv_t = meta_t is not None
    tile = lambda b, r: b * nt + jnp.minimum(r, nt - 1)
    row = lambda c: pl.BlockSpec((tm, c), lambda b, r: (tile(b, r), 0))
    cb = (lambda b: b) if c0.shape[0] > 1 else (lambda b: 0)
    pb = (lambda b: b) if cp.shape[0] > 1 else (lambda b: 0)
    lp = N_META + nt * tm
    tmin = lambda c: pl.BlockSpec((None, c, tm), lambda b, r: (b, 0, r))
    flat = lambda c, t: (row(c), jax.ShapeDtypeStruct((n, c), t))
    tok_minor = lambda c: (tmin(c), jax.ShapeDtypeStruct((nb, c, lp), F32))
    outs = [flat(aw, BF16), tok_minor(fw) if kv_t else flat(fw, F32), flat(aw, BF16),
            tok_minor(fw) if kv_t else flat(fw, F32), flat(aw, BF16), flat(mw, BF16), flat(mw, BF16), flat(mw, BF16),
            flat(d, BF16), flat(d, BF16), flat(LANES, F32), flat(LANES, F32),
            (pl.BlockSpec((None, 8, mw), lambda b, r: (b, 0, 0)), jax.ShapeDtypeStruct((nb, 8, mw), F32))]
    in_specs = [row(d), _const_spec((1, d)), _const_spec((d, dinp)), _const_spec((1, dinp)),
                _const_spec((1, fw)), _const_spec((1, fw)), _const_spec((fw, fw)),
                pl.BlockSpec((None, 1, LANES), lambda b, r: (cb(b), 0, 0)),
                pl.BlockSpec((None, 8, mw), lambda b, r: (pb(b), 0, 0)),
                _const_spec((CONV_WIDTH, mw)), _const_spec((1, mw))]
    scratch = [pltpu.VMEM((1, LANES), F32), pltpu.VMEM((8 + tm, mw), F32)]
    args = [x, gpre, w, b, gq, gk, hs, c0, cp, wc, bc]
    if kv_t:
        outs.append(tok_minor(FOX_HEADS))
        in_specs += [_const_spec((fw, LANES)), _const_spec((fw, LANES)), _const_spec((FOX_HEADS, LANES))]
        scratch += [pltpu.VMEM((fw, LANES), F32), pltpu.VMEM((fw, LANES), F32), pltpu.VMEM((FOX_HEADS, LANES), F32)]
        args += list(meta_t)
    return pl.pallas_call(
        functools.partial(_inproj_kernel, d_model=d, nt=nt, kv_t=kv_t,
                          conv_valid=tm if conv_valid is None else conv_valid),
        grid=(nb, nt + (1 if kv_t else 0)),
        in_specs=in_specs,
        out_specs=[o[0] for o in outs],
        out_shape=[o[1] for o in outs],
        scratch_shapes=scratch,
        compiler_params=pltpu.CompilerParams(dimension_semantics=("arbitrary", "arbitrary"),
                                             vmem_limit_bytes=VMEM_LIMIT),
        name="inproj",
    )(*args)


def _cacheprep_kernel(k_ref, v_ref, lf_ref, ka_ref, va_ref, fl_ref, carry):
    @pl.when(pl.program_id(1) == 0)
    def _():
        carry[...] = jnp.zeros_like(carry)

    fcol, cur = _cumsum_rows(lf_ref[...], carry[...])
    carry[...] = cur
    fl_ref[...] = cur
    _store_attention_operands(None, k_ref[...].T, v_ref[...].T, fcol, None, ka_ref, va_ref)


def _cacheprep(ck, cv, clf, tp):
    bsz, _, plen = ck.shape
    aw = FOX_HEADS * LANES
    return pl.pallas_call(
        _cacheprep_kernel,
        grid=(bsz, plen // tp),
        in_specs=[pl.BlockSpec((None, FOX_WIDTH, tp), lambda b, i: (b, 0, i)),
                  pl.BlockSpec((None, FOX_WIDTH, tp), lambda b, i: (b, 0, i)),
                  pl.BlockSpec((None, tp, LANES), lambda b, i: (b, i, 0))],
        out_specs=[pl.BlockSpec((None, tp, aw), lambda b, i: (b, i, 0)),
                   pl.BlockSpec((None, tp, aw), lambda b, i: (b, i, 0)),
                   pl.BlockSpec((None, 1, LANES), lambda b, i: (b, 0, 0))],
        out_shape=[jax.ShapeDtypeStruct((bsz, plen, aw), BF16), jax.ShapeDtypeStruct((bsz, plen, aw), BF16),
                   jax.ShapeDtypeStruct((bsz, 1, LANES), F32)],
        scratch_shapes=[pltpu.VMEM((1, LANES), F32)],
        compiler_params=pltpu.CompilerParams(dimension_semantics=("arbitrary", "arbitrary")),
        name="cache_prep",
    )(ck, cv, clf)


def _fox_kernel(q_ref, kp_ref, vp_ref, k_ref, v_ref, o_ref, *, tq, tr, tkb, nq, hps, pblk, n_pblk, pvalid):
    i = pl.program_id(2)
    nr = tq // tr
    chains = [(hh, r) for hh in range(hps) for r in range(nr)]
    qs = [q_ref[r * tr:(r + 1) * tr, hh * LANES:(hh + 1) * LANES] for hh, r in chains]

    def upd1(n, state, kb, vb, mask=None):
        m, acc = state
        hh = chains[n][0]
        hs = slice(hh * LANES, (hh + 1) * LANES)
        s = lax.dot_general(qs[n], kb[:, hs], (((1,), (1,)), ((), ())), preferred_element_type=F32)
        if mask is not None:
            cut = s.shape[1] - mask.shape[1]
            tail = jnp.where(mask, s[:, cut:], NEG_BIG)
            s = tail if cut == 0 else jnp.concatenate([s[:, :cut], tail], axis=1)
        m_new = jnp.maximum(m, jnp.max(s, axis=-1, keepdims=True))
        alpha = jnp.exp(m - m_new)
        pm = jnp.exp(s - m_new)
        acc = alpha * acc + jnp.dot(pm.astype(BF16), vb[:, hs], preferred_element_type=F32)
        return m_new, acc

    def upd(carry, kb, vb):
        return tuple(upd1(n, carry[n], kb, vb) for n in range(len(chains)))

    def start():
        carry = tuple((jnp.full((tr, 1), NEG_BIG, F32), jnp.zeros((tr, LANES), F32)) for _ in chains)
        if pvalid is None:
            for jb in range(n_pblk):
                sl = slice(jb * pblk, (jb + 1) * pblk)
                carry = upd(carry, kp_ref[sl, :], vp_ref[sl, :])
        return carry

    def finish(carry, off):
        tail_w = tr + (LANES if pvalid is not None else 0)
        rj = lax.broadcasted_iota(jnp.int32, (tr, tail_w), 0)
        cj = lax.broadcasted_iota(jnp.int32, (tr, tail_w), 1)
        mask = cj <= rj
        if pvalid is not None:
            mask = (mask | (cj >= tr)) & (cj < tr + pvalid)
        carry = list(carry)
        for r in reversed(range(nr)):
            width = (r + 1) * tr
            kb = k_ref[pl.ds(off, width), :]
            vb = v_ref[pl.ds(off, width), :]
            if pvalid is not None:
                kb = jnp.concatenate([kb, kp_ref[...]], axis=0)
                vb = jnp.concatenate([vb, vp_ref[...]], axis=0)
            for n, (hh, rr) in enumerate(chains):
                if rr == r:
                    carry[n] = upd1(n, carry[n], kb, vb, mask=mask)
        lane = lax.broadcasted_iota(jnp.int32, (tr, LANES), 1)
        outs = []
        for n in range(len(chains)):
            acc = carry[n][1]
            l = jnp.sum(jnp.where(lane == FOX_HEAD_DIM, acc, 0.0), axis=-1, keepdims=True)
            outs.append(acc / l)
        for hp in range(hps // 2):
            for r in range(nr):
                even, odd = outs[2 * hp * nr + r], outs[(2 * hp + 1) * nr + r]
                o_ref[r * tr:(r + 1) * tr, hp * LANES:(hp + 1) * LANES] = jnp.where(
                    lane < FOX_HEAD_DIM, even, pltpu.roll(odd, FOX_HEAD_DIM, 1)).astype(o_ref.dtype)

    if tq % tkb == 0 and nq <= FOX_MAX_STATIC_BLOCKS:
        for qi in range(nq):
            @pl.when(i == qi)
            def _():
                carry = start()
                for j in range(qi * tq // tkb):
                    carry = upd(carry, k_ref[j * tkb:(j + 1) * tkb, :], v_ref[j * tkb:(j + 1) * tkb, :])
                finish(carry, qi * tq)
    else:
        tkw = max(tkb, tq)

        def body_wide(j, carry):
            off = pl.multiple_of(j * tkw, tkw)
            return upd(carry, k_ref[pl.ds(off, tkw), :], v_ref[pl.ds(off, tkw), :])

        def body(j, carry):
            off = pl.multiple_of(j * tq, tq)
            return upd(carry, k_ref[pl.ds(off, tq), :], v_ref[pl.ds(off, tq), :])

        n_wide = i // (tkw // tq)
        carry = lax.fori_loop(0, n_wide, body_wide, start())
        if tkw > tq:
            carry = lax.fori_loop(n_wide * (tkw // tq), i, body, carry)
        finish(carry, pl.multiple_of(i * tq, tq))


def _fox(q, kp, vp, k, v, tq, tr, pblk, pvalid=None, hps=2):
    bsz, t, _ = q.shape
    bp, plen, _ = kp.shape
    nq = t // tq
    n_pblk = plen // pblk
    pw = hps * LANES
    pb = (lambda b: b) if bp == bsz else (lambda b: 0)
    return pl.pallas_call(
        functools.partial(_fox_kernel, tq=tq, tr=tr, tkb=_row_tile(t, FOX_WIDE_KEYS), nq=nq, hps=hps, pblk=pblk,
                          n_pblk=n_pblk, pvalid=pvalid),
        grid=(bsz, FOX_HEADS // hps, nq),
        in_specs=[pl.BlockSpec((None, tq, pw), lambda b, p, i: (b, i, p)),
                  pl.BlockSpec((None, plen, pw), lambda b, p, i: (pb(b), 0, p)),
                  pl.BlockSpec((None, plen, pw), lambda b, p, i: (pb(b), 0, p)),
                  pl.BlockSpec((None, t, pw), lambda b, p, i: (b, 0, p)),
                  pl.BlockSpec((None, t, pw), lambda b, p, i: (b, 0, p))],
        out_specs=pl.BlockSpec((None, tq, hps * FOX_HEAD_DIM), lambda b, p, i: (b, i, p)),
        out_shape=jax.ShapeDtypeStruct((bsz, t, FOX_WIDTH), BF16),
        compiler_params=pltpu.CompilerParams(dimension_semantics=("arbitrary", "arbitrary", "arbitrary"),
                                             vmem_limit_bytes=VMEM_LIMIT),
        name="fox_attention",
    )(q, kp, vp, k, v)


def _mlstm_kernel(xc_ref, mv_ref, mo_ref, gt_ref, c0_ref, n0_ref, m0_ref, wq_ref, wk_ref, gn_ref, sk_ref,
                  hm_ref, cout_ref, nout_ref, mout_ref,
                  c_scr, n_scr, m_scr, *, tc, n_valid, nb, shared_state):
    c = pl.program_id(1)

    @pl.when(c == 0)
    def _():
        for bi in range(nb):
            si = 0 if shared_state else bi
            c_scr[bi] = c0_ref[si]
            n_scr[bi] = n0_ref[si]
            m_scr[bi] = m0_ref[si]

    for bi in range(nb):
        _mlstm_chunk(xc_ref.at[bi], mv_ref.at[bi], mo_ref.at[bi], gt_ref.at[bi], wq_ref, wk_ref,
                     gn_ref, sk_ref, hm_ref.at[bi], c_scr.at[bi], n_scr.at[bi], m_scr.at[bi],
                     tc=tc, n_valid=n_valid)

    @pl.when(c == pl.num_programs(1) - 1)
    def _():
        cout_ref[...] = c_scr[...]
        nout_ref[...] = n_scr[...]
        mout_ref[...] = m_scr[...]


def _mlstm_chunk(xc_ref, mv_ref, mo_ref, gt_ref, wq_ref, wk_ref, gn_ref, sk_ref, hm_ref,
                 c_scr, n_scr, m_scr, *, tc, n_valid):
    gt = gt_ref[...]
    lane = lax.broadcasted_iota(jnp.int32, (tc, LANES), 1)
    if n_valid is not None:
        rowi = lax.broadcasted_iota(jnp.int32, (tc, LANES), 0)
        is_i = (lane >= GATE_ML_I) & (lane < GATE_ML_F)
        gt = jnp.where(rowi < n_valid, gt, jnp.where(is_i, PAD_LOG_GATE, 0.0))
    ri = lax.broadcasted_iota(jnp.int32, (tc, tc), 0)
    ci = lax.broadcasted_iota(jnp.int32, (tc, tc), 1)
    causal = ci <= ri
    b_all = _cumsum_rows(gt, jnp.zeros((1, LANES), F32))[0]
    r_t = (pltpu.roll(gt, GATE_ML_F - GATE_ML_I, 1) - b_all).T
    heads = range(ML_HEADS)
    hsl = [slice(h * ML_HEAD_DIM, (h + 1) * ML_HEAD_DIM) for h in heads]
    b_col = [b_all[:, GATE_ML_F + h:GATE_ML_F + h + 1] for h in heads]
    ig_col = [gt[:, GATE_ML_I + h:GATE_ML_I + h + 1] for h in heads]
    m_prev = [m_scr[h:h + 1, 0:1] for h in heads]
    dmat = [jnp.where(causal, b_col[h] + r_t[GATE_ML_F + h:GATE_ML_F + h + 1, :], NEG_BIG) for h in heads]
    m_inter = [b_col[h] + m_prev[h] for h in heads]
    m_t = [jnp.maximum(m_inter[h], jnp.max(dmat[h], axis=-1, keepdims=True)) for h in heads]
    w_inter = [jnp.exp(m_inter[h] - m_t[h]) for h in heads]

    xhb = [xc_ref[:, hsl[h]] for h in heads]
    q = [jnp.dot(xhb[h], wq_ref[h], preferred_element_type=F32) for h in heads]
    k = [jnp.dot(xhb[h], wk_ref[h], preferred_element_type=F32) * (ML_HEAD_DIM ** -0.5) for h in heads]
    qb = [x.astype(BF16) for x in q]
    kb = [x.astype(BF16) for x in k]
    v = [mv_ref[:, hsl[h]] for h in heads]
    smat = [lax.dot_general(qb[h], kb[h], (((1,), (1,)), ((), ())), preferred_element_type=F32) for h in heads]
    amat = [jnp.exp(dmat[h] - m_t[h]) * smat[h] for h in heads]
    ch = [c_scr[h] for h in heads]
    nh = [n_scr[h:h + 1, :] for h in heads]
    num = [jnp.dot(amat[h].astype(BF16), v[h], preferred_element_type=F32)
           + w_inter[h] * jnp.dot(qb[h], ch[h].astype(BF16), preferred_element_type=F32) for h in heads]
    den = [jnp.sum(amat[h], axis=-1, keepdims=True) + w_inter[h] * jnp.sum(q[h] * nh[h], axis=-1, keepdims=True)
           for h in heads]
    hcell = [num[h] / jnp.maximum(jnp.abs(den[h]), jnp.exp(-m_t[h])) for h in heads]

    b_last = [b_col[h][tc - 1:tc, :] for h in heads]
    g = [b_last[h] - b_col[h] + ig_col[h] for h in heads]
    m_new = [jnp.maximum(b_last[h] + m_prev[h], jnp.max(g[h], axis=0, keepdims=True)) for h in heads]
    decay = [jnp.exp(b_last[h] + m_prev[h] - m_new[h]) for h in heads]
    kw = [k[h] * jnp.exp(g[h] - m_new[h]) for h in heads]
    for h in heads:
        c_scr[h] = decay[h] * ch[h] + lax.dot_general(kw[h].astype(BF16), v[h], (((0,), (0,)), ((), ())),
                                                       preferred_element_type=F32)
        n_scr[h:h + 1, :] = decay[h] * nh[h] + jnp.sum(kw[h], axis=0, keepdims=True)
        m_scr[h:h + 1, :] = jnp.broadcast_to(m_new[h], (1, LANES))
        y = mo_ref[:, hsl[h]].astype(F32) * hcell[h]
        hm_ref[:, hsl[h]] = (_rms(y, gn_ref[:, hsl[h]]) + sk_ref[:, hsl[h]] * xhb[h].astype(F32)).astype(hm_ref.dtype)


def _mlstm(xc, mv, mo, gt, c0, n0, m0, wq, wk, gn, sk, tc, nb=1, n_valid=None):
    bsz, t, _ = xc.shape
    shared = c0.shape[0] != bsz
    sn = 1 if shared else nb
    sb = (lambda b: 0) if shared else (lambda b: b)
    tok = lambda w: pl.BlockSpec((nb, tc, w), lambda b, c: (b, c, 0))
    hd = ML_HEAD_DIM
    st_specs = [pl.BlockSpec((nb, ML_HEADS, hd, hd), lambda b, c: (b, 0, 0, 0)),
                pl.BlockSpec((nb, 8, LANES), lambda b, c: (b, 0, 0)),
                pl.BlockSpec((nb, 8, LANES), lambda b, c: (b, 0, 0))]
    return pl.pallas_call(
        functools.partial(_mlstm_kernel, tc=tc, n_valid=n_valid, nb=nb, shared_state=shared),
        grid=(bsz // nb, t // tc),
        in_specs=[tok(ML_WIDTH), tok(ML_WIDTH), tok(ML_WIDTH), tok(LANES),
                  pl.BlockSpec((sn, ML_HEADS, hd, hd), lambda b, c: (sb(b), 0, 0, 0)),
                  pl.BlockSpec((sn, 8, LANES), lambda b, c: (sb(b), 0, 0)),
                  pl.BlockSpec((sn, 8, LANES), lambda b, c: (sb(b), 0, 0)),
                  _const_spec((ML_HEADS, hd, hd)), _const_spec((ML_HEADS, hd, hd)),
                  _const_spec((1, ML_WIDTH)), _const_spec((1, ML_WIDTH))],
        out_specs=[tok(ML_WIDTH)] + st_specs,
        out_shape=[jax.ShapeDtypeStruct((bsz, t, ML_WIDTH), BF16),
                   jax.ShapeDtypeStruct((bsz, ML_HEADS, hd, hd), F32),
                   jax.ShapeDtypeStruct((bsz, 8, LANES), F32),
                   jax.ShapeDtypeStruct((bsz, 8, LANES), F32)],
        scratch_shapes=[pltpu.VMEM((nb, ML_HEADS, hd, hd), F32),
                        pltpu.VMEM((nb, 8, LANES), F32), pltpu.VMEM((nb, 8, LANES), F32)],
        compiler_params=pltpu.CompilerParams(dimension_semantics=("arbitrary", "arbitrary")),
        name="mlstm",
    )(xc, mv, mo, gt, c0, n0, m0, wq, wk, gn, sk)


def _post_kernel(a_ref, hm_ref, ga_ref, gb_ref, x_ref, wfp_ref, wmp_ref, wo_ref, wgu_ref, wd_ref,
                 g1_ref, g2_ref, g3_ref, y_ref, *, d_ff, ff_cuts):
    ya = jnp.dot(a_ref[...], wfp_ref[...], preferred_element_type=F32)
    yb = jnp.dot(hm_ref[...], wmp_ref[...], preferred_element_type=F32)
    merged = jax.nn.sigmoid(ga_ref[...].astype(F32)) * ya + jax.nn.sigmoid(gb_ref[...].astype(F32)) * yb
    mix = jnp.dot(merged.astype(BF16), wo_ref[...], preferred_element_type=F32)
    x1 = x_ref[...] + _rms(mix, g1_ref[...])
    h2 = _rms(x1, g2_ref[...]).astype(BF16)
    f = None
    for lo, hi in zip(ff_cuts[:-1], ff_cuts[1:]):
        g = jnp.dot(h2, wgu_ref[:, lo:hi], preferred_element_type=F32)
        u = jnp.dot(h2, wgu_ref[:, d_ff + lo:d_ff + hi], preferred_element_type=F32)
        t = (g * jax.nn.sigmoid(g) * u).astype(BF16)
        fj = jnp.dot(t, wd_ref[lo:hi, :], preferred_element_type=F32)
        f = fj if f is None else f + fj
    y_ref[...] = x1 + _rms(f, g3_ref[...])


def _post(a, hm, ga, gb, x, wfp, wmp, wo, wgu, wd, g1, g2, g3, tm):
    n, d = x.shape
    d_ff = wd.shape[0]
    nmt = d_ff // MXU_TILE
    ff_cuts = tuple(MXU_TILE * ((nmt * j) // FF_CHUNKS) for j in range(FF_CHUNKS)) + (d_ff,) \
        if d_ff % MXU_TILE == 0 else (0, d_ff)
    row = lambda c: pl.BlockSpec((tm, c), lambda i: (i, 0))
    return pl.pallas_call(
        functools.partial(_post_kernel, d_ff=d_ff, ff_cuts=ff_cuts),
        grid=(n // tm,),
        in_specs=[row(FOX_WIDTH), row(ML_WIDTH), row(d), row(d), row(d),
                  _const_spec(wfp.shape), _const_spec(wmp.shape), _const_spec(wo.shape),
                  _const_spec(wgu.shape), _const_spec(wd.shape),
                  _const_spec((1, d)), _const_spec((1, d)), _const_spec((1, d))],
        out_specs=row(d),
        out_shape=jax.ShapeDtypeStruct((n, d), F32),
        compiler_params=pltpu.CompilerParams(dimension_semantics=("arbitrary",), vmem_limit_bytes=VMEM_LIMIT),
        name="post_mix_ffn",
    )(a, hm, ga, gb, x, wfp, wmp, wo, wgu, wd, g1, g2, g3)


def _pad_rows(a, rows, axis):
    pad = [(0, 0)] * a.ndim
    pad[axis] = (0, rows - a.shape[axis])
    return jnp.pad(a, pad)


def _row_tile(n, pref):
    t = min(pref, n)
    while n % t:
        t //= 2
    return t


def kernel(x_prompt, x_sample, cache_fox_k, cache_fox_v, cache_fox_logf, state_mlstm_C, state_mlstm_n,
           state_mlstm_m, state_mlstm_conv, meta_tokens, w_in, b_in, g_fq, g_fk, w_conv, b_conv, w_mq, w_mk,
           g_mnorm, skip_m, w_fox_proj, w_ml_proj, w_out, g_pre_mix, g_post_mix, g_pre_ffn, g_post_ffn,
           w_gate_up, w_down):
    assert w_in.shape[0] == 1, "single-layer trunk"
    bsz, seq, d = x_prompt.shape
    dbsz, dseq, _ = x_sample.shape
    past = cache_fox_k.shape[2]
    fw, mw = FOX_WIDTH, ML_WIDTH

    w, b = w_in[0], b_in[0]
    o_ff = 3 * fw
    o_mx = o_ff + FOX_HEADS
    o_mi = o_mx + 3 * mw
    o_mf = o_mi + ML_HEADS
    o_ga = o_mf + ML_HEADS

    def relayout(a):
        gates = jnp.concatenate([a[..., o_ff:o_mx], a[..., o_mi:o_mf], a[..., o_mf:o_ga]], axis=-1)
        gates = _pad_rows(gates, LANES, gates.ndim - 1)
        return jnp.concatenate([a[..., :o_ff], a[..., o_mx:o_mi], a[..., o_ga:], gates], axis=-1)

    w_p = relayout(w).astype(BF16)
    b_p = relayout(b)[None, :]
    hid = jnp.arange(fw) // FOX_HEAD_DIM
    hsum = (hid[:, None] == hid[None, :]).astype(BF16)
    gq = g_fq[0].reshape(1, fw)
    gk = g_fk[0].reshape(1, fw)
    gpre = g_pre_mix[0][None, :]
    wc, bc = w_conv[0], b_conv[0][None, :]
    inproj = lambda x, c0, cp, tm, nt, **kw: _inproj(x, gpre, w_p, b_p, gq, gk, hsum, c0, cp, wc, bc, tm, nt, **kw)

    wq, wk = w_mq[0].astype(BF16), w_mk[0].astype(BF16)
    gn, sk = g_mnorm[0].reshape(1, mw), skip_m[0][None, :]
    mlstm = lambda *a, **kw: _mlstm(*a, wq, wk, gn, sk, **kw)

    def conv_hist(rows):
        return jnp.pad(rows, ((0, 0), (8 - (CONV_WIDTH - 1), 0), (0, 0)))

    def m_rows(m):
        return jnp.broadcast_to(_pad_rows(m, 8, 1)[:, :, None], (m.shape[0], 8, LANES))

    mt = 64
    xm = _pad_rows(meta_tokens.astype(F32), mt, 0)
    (_, kf_m, ka_m, vf_m, va_m, xc_m, mv_m, mo_m, _, _, gt_m, fc_m, ct_m) = inproj(
        xm, jnp.zeros((1, 1, LANES), F32), jnp.zeros((1, 8, mw), F32), mt, 1, conv_valid=N_META)
    zc = jnp.zeros((1, ML_HEADS, ML_HEAD_DIM, ML_HEAD_DIM), F32)
    z8 = jnp.zeros((1, 8, LANES), F32)
    _, c_m, n_m, m_m = mlstm(xc_m[None], mv_m[None], mo_m[None], gt_m[None], zc, z8, z8, tc=mt, n_valid=N_META)

    xp = x_prompt.reshape(bsz * seq, d)
    tm = _row_tile(seq, 512)
    lanes_t = lambda rows: _pad_rows(rows[:N_META].T, LANES, 1)
    (qa_p, kt_p, ka_p, vt_p, va_p, xc_p, mv_p, mo_p, ga_p, gb_p, gt_p, _, ct_p, lt_p) = inproj(
        xp, fc_m[None, N_META - 1:N_META], ct_m, tm, seq // tm,
        meta_t=(lanes_t(kf_m), lanes_t(vf_m), lanes_t(gt_m[:, :FOX_HEADS])))
    r3 = lambda a: a.reshape(bsz, seq, a.shape[-1])
    hm_p, c_p, n_p, m_p = mlstm(r3(xc_p), r3(mv_p), r3(mo_p), r3(gt_p), c_m, n_m, m_m,
                                tc=_row_tile(seq, 128), nb=_row_tile(bsz, 4))
    a_p = _fox(r3(qa_p), _pad_rows(ka_m, LANES, 0)[None], _pad_rows(va_m, LANES, 0)[None], r3(ka_p), r3(va_p),
               tq=_row_tile(seq, 4096), tr=_row_tile(seq, 512), pblk=LANES, pvalid=N_META)

    tok_minor = lambda c: c[0].transpose(0, 2, 3, 1).reshape(dbsz, fw, past)
    kc, vc, f_c = _cacheprep(tok_minor(cache_fox_k), tok_minor(cache_fox_v),
                             _pad_rows(cache_fox_logf[0].astype(F32), LANES, 2), _row_tile(past, 512))
    xs = x_sample.reshape(dbsz * dseq, d)
    (qa_s, kf_s, ka_s, vf_s, va_s, xc_s, mv_s, mo_s, ga_s, gb_s, gt_s, _, ct_s) = inproj(
        xs, f_c, conv_hist(state_mlstm_conv[0]), dseq, 1)
    r3s = lambda a: a.reshape(dbsz, dseq, a.shape[-1])
    hm_s, c_s, n_s, m_s = mlstm(r3s(xc_s), r3s(mv_s), r3s(mo_s), r3s(gt_s), state_mlstm_C[0], _pad_rows(state_mlstm_n[0], 8, 1), m_rows(state_mlstm_m[0]),
                                tc=dseq, nb=_row_tile(dbsz, 4))
    a_s = _fox(r3s(qa_s), kc, vc, r3s(ka_s), r3s(va_s), tq=dseq, tr=dseq, pblk=_row_tile(past, 512), hps=FOX_HEADS)

    wfp, wmp, wo = w_fox_proj[0].astype(BF16), w_ml_proj[0].astype(BF16), w_out[0].astype(BF16)
    wgu, wd = w_gate_up[0].astype(BF16), w_down[0].astype(BF16)
    g1, g2, g3 = g_post_mix[0][None, :], g_pre_ffn[0][None, :], g_post_ffn[0][None, :]
    post = lambda a, hm, ga, gb, x, tm: _post(a, hm, ga, gb, x, wfp, wmp, wo, wgu, wd, g1, g2, g3, tm)
    y_p = post(a_p.reshape(bsz * seq, fw), hm_p.reshape(bsz * seq, mw), ga_p, gb_p, xp, _row_tile(bsz * seq, 512))
    y_s = post(a_s.reshape(dbsz * dseq, fw), hm_s.reshape(dbsz * dseq, mw), ga_s, gb_s, xs,
               _row_tile(dbsz * dseq, 512))

    def tokens_major(a):
        return a.reshape(bsz, FOX_HEADS, FOX_HEAD_DIM, a.shape[-1]).transpose(0, 3, 1, 2)[None]

    heads = lambda a: a.reshape(a.shape[:-1] + (FOX_HEADS, FOX_HEAD_DIM))
    tail = CONV_WIDTH - 1
    return (y_p.reshape(bsz, seq, d), y_s.reshape(dbsz, dseq, d),
            tokens_major(kt_p), tokens_major(vt_p), lt_p.transpose(0, 2, 1)[None],
            c_p[None], n_p[None, :, :ML_HEADS], m_p[None, :, :ML_HEADS, 0], ct_p[None, :, 8 - tail:],
            heads(r3s(kf_s))[None], heads(r3s(vf_s))[None], r3s(gt_s)[None, ..., :FOX_HEADS],
            c_s[None], n_s[None, :, :ML_HEADS], m_s[None, :, :ML_HEADS, 0], ct_s[None, :, 8 - tail:])
```

```python
import functools

import jax
import jax.numpy as jnp
from jax import lax
from jax.experimental import pallas as pl
from jax.experimental.pallas import tpu as pltpu

F32 = jnp.float32
BF16 = jnp.bfloat16

N_META = 16
FOX_HEADS = 8
FOX_HEAD_DIM = 64
FOX_WIDTH = FOX_HEADS * FOX_HEAD_DIM
ML_HEADS = 4
ML_HEAD_DIM = 128
ML_WIDTH = ML_HEADS * ML_HEAD_DIM
CONV_WIDTH = 4
RMS_EPS = 1e-6
PAD_LOG_GATE = -1e30
NEG_BIG = -1e30

LANES = 128
GATE_ML_I = FOX_HEADS
GATE_ML_F = FOX_HEADS + ML_HEADS

AUG_STEP = FOX_HEADS
AUG_QF = FOX_HEAD_DIM
AUG_KF = FOX_HEAD_DIM + 24

VMEM_LIMIT = 56 * 1024 * 1024
FOX_WIDE_KEYS = 1024
FOX_MAX_STATIC_BLOCKS = 4
MXU_TILE = 256
FF_CHUNKS = 2


def _const_spec(shape):
    nd = len(shape)
    return pl.BlockSpec(shape, lambda *_: (0,) * nd, pipeline_mode=pl.Buffered(1))


def _rms(x, g):
    return x * lax.rsqrt(jnp.mean(x * x, axis=-1, keepdims=True) + RMS_EPS) * g


def _split3(x):
    hi = x.astype(BF16).astype(F32)
    r = x - hi
    mid = r.astype(BF16).astype(F32)
    lo = (r - mid).astype(BF16).astype(F32)
    return hi, mid, lo


def _cumsum_rows(lg, cur):
    tm = lg.shape[0]
    blk = min(tm, LANES)
    r = lax.broadcasted_iota(jnp.int32, (blk, blk), 0)
    c = lax.broadcasted_iota(jnp.int32, (blk, blk), 1)
    tril = (c <= r).astype(BF16)
    outs = []
    for s in range(tm // blk):
        hi, mid, lo = _split3(lg[s * blk:(s + 1) * blk])
        f = (jnp.dot(tril, lo.astype(BF16), preferred_element_type=F32)
             + jnp.dot(tril, mid.astype(BF16), preferred_element_type=F32)
             + jnp.dot(tril, hi.astype(BF16), preferred_element_type=F32)) + cur
        outs.append(f)
        cur = f[blk - 1:blk]
    return (outs[0] if len(outs) == 1 else jnp.concatenate(outs, axis=0)), cur


def _store_attention_operands(qn, kn, v, fcol, q_ref, k_ref, v_ref, fillers=()):
    tm = fcol.shape[0]
    lane = lax.broadcasted_iota(jnp.int32, (tm, LANES), 1)
    data = lane < FOX_HEAD_DIM
    hi, mid, lo = _split3(fcol)
    comb = jnp.where(lane < AUG_STEP, hi, jnp.where(lane < 2 * AUG_STEP, pltpu.roll(mid, AUG_STEP, 1),
                                                     pltpu.roll(lo, 2 * AUG_STEP, 1)))
    ncomb = -comb
    is_qf = (lane == AUG_QF) | (lane == AUG_QF + AUG_STEP) | (lane == AUG_QF + 2 * AUG_STEP)
    is_kf = (lane == AUG_KF) | (lane == AUG_KF + AUG_STEP) | (lane == AUG_KF + 2 * AUG_STEP)
    base_q = jnp.where(is_kf, 1.0, 0.0)
    base_k = jnp.where(is_qf, 1.0, 0.0)
    base_v = jnp.where(lane == FOX_HEAD_DIM, 1.0, 0.0)
    fillers = list(fillers)
    for h in range(FOX_HEADS):
        t, odd = divmod(h, 2)

        def head_tile(z):
            x = z[:, t * LANES:(t + 1) * LANES]
            return pltpu.roll(x, FOX_HEAD_DIM, 1) if odd else x

        hs = slice(h * LANES, (h + 1) * LANES)
        if q_ref is not None:
            aug = jnp.where(is_qf, pltpu.roll(comb, AUG_QF - h, 1), base_q)
            q_ref[:, hs] = jnp.where(data, head_tile(qn), aug).astype(BF16)
        aug = jnp.where(is_kf, pltpu.roll(ncomb, AUG_KF - h, 1), base_k)
        k_ref[:, hs] = jnp.where(data, head_tile(kn), aug).astype(BF16)
        v_ref[:, hs] = jnp.where(data, head_tile(v), base_v).astype(BF16)
        if fillers:
            fillers.pop(0)()
    for f in fillers:
        f()


def _inproj_kernel(*refs, d_model, nt, kv_t, conv_valid):
    (x_ref, gpre_ref, w_ref, b_ref, gq_ref, gk_ref, hs_ref, c0_ref, cp_ref, wc_ref, bc_ref), rest = refs[:11], refs[11:]
    if kv_t:
        kmeta_ref, vmeta_ref, lmeta_ref = rest[:3]
        (q_ref, kf_ref, kb_ref, vf_ref, vb_ref, xc_ref, mv_ref, mo_ref, ga_ref, gb_ref, gt_ref, fc_ref, ct_ref,
         lt_ref, carry, xbuf, kcar, vcar, lcar) = rest[3:]
    else:
        (q_ref, kf_ref, kb_ref, vf_ref, vb_ref, xc_ref, mv_ref, mo_ref, ga_ref, gb_ref, gt_ref, fc_ref, ct_ref,
         carry, xbuf) = rest
    fw, mw = FOX_WIDTH, ML_WIDTH
    offs = [0, fw, 2 * fw, 3 * fw, 3 * fw + mw, 3 * fw + 2 * mw, 3 * fw + 3 * mw,
            3 * fw + 3 * mw + d_model, 3 * fw + 3 * mw + 2 * d_model, 3 * fw + 3 * mw + 2 * d_model + LANES]
    r = pl.program_id(1)

    @pl.when(r == 0)
    def _():
        carry[...] = c0_ref[...]
        xbuf[0:8, :] = cp_ref[...]
        if kv_t:
            kcar[...] = kmeta_ref[...]
            vcar[...] = vmeta_ref[...]
            lcar[...] = lmeta_ref[...]

    def store_shifted(xt, car, out_ref):
        sh = pltpu.roll(xt, N_META, 1)
        lane = lax.broadcasted_iota(jnp.int32, (xt.shape[0], LANES), 1)
        out_ref[:, :LANES] = jnp.where(lane < N_META, car[...], sh[:, :LANES])
        out_ref[:, LANES:] = sh[:, LANES:]
        car[...] = sh[:, :LANES]

    @pl.when(r < nt)
    def _():
        h = _rms(x_ref[...], gpre_ref[...]).astype(BF16)

        def seg(i):
            a, b = offs[i], offs[i + 1]
            return jnp.dot(h, w_ref[:, a:b], preferred_element_type=F32) + b_ref[:, a:b]

        def headnorm(z, g):
            ms = jnp.dot((z * z).astype(BF16), hs_ref[...], preferred_element_type=F32) * (1.0 / FOX_HEAD_DIM)
            return z * lax.rsqrt(ms + RMS_EPS) * g

        g = seg(8)
        ga_ref[...] = seg(6).astype(BF16)
        lane = lax.broadcasted_iota(jnp.int32, g.shape, 1)
        raw = (lane >= GATE_ML_I) & (lane < GATE_ML_F)
        gt = jnp.where(raw, g, jax.nn.log_sigmoid(g))
        gt_ref[...] = gt
        fcol, cur = _cumsum_rows(gt, carry[...])
        carry[...] = cur
        fc_ref[...] = fcol

        v = seg(2)
        q = headnorm(seg(0), gq_ref[...]) * (FOX_HEAD_DIM ** -0.5)
        k = headnorm(seg(1), gk_ref[...])
        def store_kv():
            if kv_t:
                store_shifted(k.T, kcar, kf_ref)
                store_shifted(v.T, vcar, vf_ref)
                store_shifted(gt.T[:FOX_HEADS], lcar, lt_ref)
            else:
                kf_ref[...] = k
                vf_ref[...] = v

        def store_xc():
            tm = x_ref.shape[0]
            xbuf[8:8 + tm, :] = seg(3)
            xc = bc_ref[...]
            for i in range(CONV_WIDTH):
                xc = xc + xbuf[5 + i:5 + i + tm, :] * wc_ref[i:i + 1, :]
            xc_ref[...] = (xc * jax.nn.sigmoid(xc)).astype(BF16)
            ct_ref[...] = xbuf[conv_valid:conv_valid + 8, :]
            xbuf[0:8, :] = xbuf[tm:tm + 8, :]

        def store_mv():
            mv_ref[...] = seg(4).astype(BF16)

        def store_mo():
            mo_ref[...] = jax.nn.sigmoid(seg(5)).astype(BF16)

        def store_gb():
            gb_ref[...] = seg(7).astype(BF16)

        _store_attention_operands(q, k, v, fcol, q_ref, kb_ref, vb_ref,
                                  fillers=(store_xc, store_mv, store_mo, store_gb, store_kv))

    if kv_t:
        @pl.when(r == nt)
        def _():
            kf_ref[:, :LANES] = kcar[...]
            vf_ref[:, :LANES] = vcar[...]
            lt_ref[:, :LANES] = lcar[...]


def _inproj(x, gpre, w, b, gq, gk, hs, c0, cp, wc, bc, tm, nt, meta_t=None, conv_valid=None):
    n, d = x.shape
    nb = n // (nt * tm)
    dinp = w.shape[1]
    fw, mw = FOX_WIDTH, ML_WIDTH
    aw = FOX_HEADS * LANES
    kv_t = meta_t is not None
    tile = lambda b, r: b * nt + jnp.minimum(r, nt - 1)
    row = lambda c: pl.BlockSpec((tm, c), lambda b, r: (tile(b, r), 0))
    cb = (lambda b: b) if c0.shape[0] > 1 else (lambda b: 0)
    pb = (lambda b: b) if cp.shape[0] > 1 else (lambda b: 0)
    lp = N_META + nt * tm
    tmin = lambda c: pl.BlockSpec((None, c, tm), lambda b, r: (b, 0, r))
    flat = lambda c, t: (row(c), jax.ShapeDtypeStruct((n, c), t))
    tok_minor = lambda c: (tmin(c), jax.ShapeDtypeStruct((nb, c, lp), F32))
    outs = [flat(aw, BF16), tok_minor(fw) if kv_t else flat(fw, F32), flat(aw, BF16),
            tok_minor(fw) if kv_t else flat(fw, F32), flat(aw, BF16), flat(mw, BF16), flat(mw, BF16), flat(mw, BF16),
            flat(d, BF16), flat(d, BF16), flat(LANES, F32), flat(LANES, F32),
            (pl.BlockSpec((None, 8, mw), lambda b, r: (b, 0, 0)), jax.ShapeDtypeStruct((nb, 8, mw), F32))]
    in_specs = [row(d), _const_spec((1, d)), _const_spec((d, dinp)), _const_spec((1, dinp)),
                _const_spec((1, fw)), _const_spec((1, fw)), _const_spec((fw, fw)),
                pl.BlockSpec((None, 1, LANES), lambda b, r: (cb(b), 0, 0)),
                pl.BlockSpec((None, 8, mw), lambda b, r: (pb(b), 0, 0)),
                _const_spec((CONV_WIDTH, mw)), _const_spec((1, mw))]
    scratch = [pltpu.VMEM((1, LANES), F32), pltpu.VMEM((8 + tm, mw), F32)]
    args = [x, gpre, w, b, gq, gk, hs, c0, cp, wc, bc]
    if kv_t:
        outs.append(tok_minor(FOX_HEADS))
        in_specs += [_const_spec((fw, LANES)), _const_spec((fw, LANES)), _const_spec((FOX_HEADS, LANES))]
        scratch += [pltpu.VMEM((fw, LANES), F32), pltpu.VMEM((fw, LANES), F32), pltpu.VMEM((FOX_HEADS, LANES), F32)]
        args += list(meta_t)
    return pl.pallas_call(
        functools.partial(_inproj_kernel, d_model=d, nt=nt, kv_t=kv_t,
                          conv_valid=tm if conv_valid is None else conv_valid),
        grid=(nb, nt + (1 if kv_t else 0)),
        in_specs=in_specs,
        out_specs=[o[0] for o in outs],
        out_shape=[o[1] for o in outs],
        scratch_shapes=scratch,
        compiler_params=pltpu.CompilerParams(dimension_semantics=("arbitrary", "arbitrary"),
                                             vmem_limit_bytes=VMEM_LIMIT),
        name="inproj",
    )(*args)


def _cacheprep_kernel(k_ref, v_ref, lf_ref, ka_ref, va_ref, fl_ref, carry):
    @pl.when(pl.program_id(1) == 0)
    def _():
        carry[...] = jnp.zeros_like(carry)

    fcol, cur = _cumsum_rows(lf_ref[...], carry[...])
    carry[...] = cur
    fl_ref[...] = cur
    _store_attention_operands(None, k_ref[...].T, v_ref[...].T, fcol, None, ka_ref, va_ref)


def _cacheprep(ck, cv, clf, tp):
    bsz, _, plen = ck.shape
    aw = FOX_HEADS * LANES
    return pl.pallas_call(
        _cacheprep_kernel,
        grid=(bsz, plen // tp),
        in_specs=[pl.BlockSpec((None, FOX_WIDTH, tp), lambda b, i: (b, 0, i)),
                  pl.BlockSpec((None, FOX_WIDTH, tp), lambda b, i: (b, 0, i)),
                  pl.BlockSpec((None, tp, LANES), lambda b, i: (b, i, 0))],
        out_specs=[pl.BlockSpec((None, tp, aw), lambda b, i: (b, i, 0)),
                   pl.BlockSpec((None, tp, aw), lambda b, i: (b, i, 0)),
                   pl.BlockSpec((None, 1, LANES), lambda b, i: (b, 0, 0))],
        out_shape=[jax.ShapeDtypeStruct((bsz, plen, aw), BF16), jax.ShapeDtypeStruct((bsz, plen, aw), BF16),
                   jax.ShapeDtypeStruct((bsz, 1, LANES), F32)],
        scratch_shapes=[pltpu.VMEM((1, LANES), F32)],
        compiler_params=pltpu.CompilerParams(dimension_semantics=("arbitrary", "arbitrary")),
        name="cache_prep",
    )(ck, cv, clf)


def _fox_kernel(q_ref, kp_ref, vp_ref, k_ref, v_ref, o_ref, *, tq, tr, tkb, nq, hps, pblk, n_pblk, pvalid):
    i = pl.program_id(2)
    nr = tq // tr
    chains = [(hh, r) for hh in range(hps) for r in range(nr)]
    qs = [q_ref[r * tr:(r + 1) * tr, hh * LANES:(hh + 1) * LANES] for hh, r in chains]

    def upd1(n, state, kb, vb, mask=None):
        m, acc = state
        hh = chains[n][0]
        hs = slice(hh * LANES, (hh + 1) * LANES)
        s = lax.dot_general(qs[n], kb[:, hs], (((1,), (1,)), ((), ())), preferred_element_type=F32)
        if mask is not None:
            cut = s.shape[1] - mask.shape[1]
            tail = jnp.where(mask, s[:, cut:], NEG_BIG)
            s = tail if cut == 0 else jnp.concatenate([s[:, :cut], tail], axis=1)
        m_new = jnp.maximum(m, jnp.max(s, axis=-1, keepdims=True))
        alpha = jnp.exp(m - m_new)
        pm = jnp.exp(s - m_new)
        acc = alpha * acc + jnp.dot(pm.astype(BF16), vb[:, hs], preferred_element_type=F32)
        return m_new, acc

    def upd(carry, kb, vb):
        return tuple(upd1(n, carry[n], kb, vb) for n in range(len(chains)))

    def start():
        carry = tuple((jnp.full((tr, 1), NEG_BIG, F32), jnp.zeros((tr, LANES), F32)) for _ in chains)
        if pvalid is None:
            for jb in range(n_pblk):
                sl = slice(jb * pblk, (jb + 1) * pblk)
                carry = upd(carry, kp_ref[sl, :], vp_ref[sl, :])
        return carry

    def finish(carry, off):
        tail_w = tr + (LANES if pvalid is not None else 0)
        rj = lax.broadcasted_iota(jnp.int32, (tr, tail_w), 0)
        cj = lax.broadcasted_iota(jnp.int32, (tr, tail_w), 1)
        mask = cj <= rj
        if pvalid is not None:
            mask = (mask | (cj >= tr)) & (cj < tr + pvalid)
        carry = list(carry)
        for r in reversed(range(nr)):
            width = (r + 1) * tr
            kb = k_ref[pl.ds(off, width), :]
            vb = v_ref[pl.ds(off, width), :]
            if pvalid is not None:
                kb = jnp.concatenate([kb, kp_ref[...]], axis=0)
                vb = jnp.concatenate([vb, vp_ref[...]], axis=0)
            for n, (hh, rr) in enumerate(chains):
                if rr == r:
                    carry[n] = upd1(n, carry[n], kb, vb, mask=mask)
        lane = lax.broadcasted_iota(jnp.int32, (tr, LANES), 1)
        outs = []
        for n in range(len(chains)):
            acc = carry[n][1]
            l = jnp.sum(jnp.where(lane == FOX_HEAD_DIM, acc, 0.0), axis=-1, keepdims=True)
            outs.append(acc / l)
        for hp in range(hps // 2):
            for r in range(nr):
                even, odd = outs[2 * hp * nr + r], outs[(2 * hp + 1) * nr + r]
                o_ref[r * tr:(r + 1) * tr, hp * LANES:(hp + 1) * LANES] = jnp.where(
                    lane < FOX_HEAD_DIM, even, pltpu.roll(odd, FOX_HEAD_DIM, 1)).astype(o_ref.dtype)

    if tq % tkb == 0 and nq <= FOX_MAX_STATIC_BLOCKS:
        for qi in range(nq):
            @pl.when(i == qi)
            def _():
                carry = start()
                for j in range(qi * tq // tkb):
                    carry = upd(carry, k_ref[j * tkb:(j + 1) * tkb, :], v_ref[j * tkb:(j + 1) * tkb, :])
                finish(carry, qi * tq)
    else:
        tkw = max(tkb, tq)

        def body_wide(j, carry):
            off = pl.multiple_of(j * tkw, tkw)
            return upd(carry, k_ref[pl.ds(off, tkw), :], v_ref[pl.ds(off, tkw), :])

        def body(j, carry):
            off = pl.multiple_of(j * tq, tq)
            return upd(carry, k_ref[pl.ds(off, tq), :], v_ref[pl.ds(off, tq), :])

        n_wide = i // (tkw // tq)
        carry = lax.fori_loop(0, n_wide, body_wide, start())
        if tkw > tq:
            carry = lax.fori_loop(n_wide * (tkw // tq), i, body, carry)
        finish(carry, pl.multiple_of(i * tq, tq))


def _fox(q, kp, vp, k, v, tq, tr, pblk, pvalid=None, hps=2):
    bsz, t, _ = q.shape
    bp, plen, _ = kp.shape
    nq = t // tq
    n_pblk = plen // pblk
    pw = hps * LANES
    pb = (lambda b: b) if bp == bsz else (lambda b: 0)
    return pl.pallas_call(
        functools.partial(_fox_kernel, tq=tq, tr=tr, tkb=_row_tile(t, FOX_WIDE_KEYS), nq=nq, hps=hps, pblk=pblk,
                          n_pblk=n_pblk, pvalid=pvalid),
        grid=(bsz, FOX_HEADS // hps, nq),
        in_specs=[pl.BlockSpec((None, tq, pw), lambda b, p, i: (b, i, p)),
                  pl.BlockSpec((None, plen, pw), lambda b, p, i: (pb(b), 0, p)),
                  pl.BlockSpec((None, plen, pw), lambda b, p, i: (pb(b), 0, p)),
                  pl.BlockSpec((None, t, pw), lambda b, p, i: (b, 0, p)),
                  pl.BlockSpec((None, t, pw), lambda b, p, i: (b, 0, p))],
        out_specs=pl.BlockSpec((None, tq, hps * FOX_HEAD_DIM), lambda b, p, i: (b, i, p)),
        out_shape=jax.ShapeDtypeStruct((bsz, t, FOX_WIDTH), BF16),
        compiler_params=pltpu.CompilerParams(dimension_semantics=("arbitrary", "arbitrary", "arbitrary"),
                                             vmem_limit_bytes=VMEM_LIMIT),
        name="fox_attention",
    )(q, kp, vp, k, v)


def _mlstm_kernel(xc_ref, mv_ref, mo_ref, gt_ref, c0_ref, n0_ref, m0_ref, wq_ref, wk_ref, gn_ref, sk_ref,
                  hm_ref, cout_ref, nout_ref, mout_ref,
                  c_scr, n_scr, m_scr, *, tc, n_valid, nb, shared_state):
    c = pl.program_id(1)

    @pl.when(c == 0)
    def _():
        for bi in range(nb):
            si = 0 if shared_state else bi
            c_scr[bi] = c0_ref[si]
            n_scr[bi] = n0_ref[si]
            m_scr[bi] = m0_ref[si]

    for bi in range(nb):
        _mlstm_chunk(xc_ref.at[bi], mv_ref.at[bi], mo_ref.at[bi], gt_ref.at[bi], wq_ref, wk_ref,
                     gn_ref, sk_ref, hm_ref.at[bi], c_scr.at[bi], n_scr.at[bi], m_scr.at[bi],
                     tc=tc, n_valid=n_valid)

    @pl.when(c == pl.num_programs(1) - 1)
    def _():
        cout_ref[...] = c_scr[...]
        nout_ref[...] = n_scr[...]
        mout_ref[...] = m_scr[...]


def _mlstm_chunk(xc_ref, mv_ref, mo_ref, gt_ref, wq_ref, wk_ref, gn_ref, sk_ref, hm_ref,
                 c_scr, n_scr, m_scr, *, tc, n_valid):
    gt = gt_ref[...]
    lane = lax.broadcasted_iota(jnp.int32, (tc, LANES), 1)
    if n_valid is not None:
        rowi = lax.broadcasted_iota(jnp.int32, (tc, LANES), 0)
        is_i = (lane >= GATE_ML_I) & (lane < GATE_ML_F)
        gt = jnp.where(rowi < n_valid, gt, jnp.where(is_i, PAD_LOG_GATE, 0.0))
    ri = lax.broadcasted_iota(jnp.int32, (tc, tc), 0)
    ci = lax.broadcasted_iota(jnp.int32, (tc, tc), 1)
    causal = ci <= ri
    b_all = _cumsum_rows(gt, jnp.zeros((1, LANES), F32))[0]
    r_t = (pltpu.roll(gt, GATE_ML_F - GATE_ML_I, 1) - b_all).T
    heads = range(ML_HEADS)
    hsl = [slice(h * ML_HEAD_DIM, (h + 1) * ML_HEAD_DIM) for h in heads]
    b_col = [b_all[:, GATE_ML_F + h:GATE_ML_F + h + 1] for h in heads]
    ig_col = [gt[:, GATE_ML_I + h:GATE_ML_I + h + 1] for h in heads]
    m_prev = [m_scr[h:h + 1, 0:1] for h in heads]
    dmat = [jnp.where(causal, b_col[h] + r_t[GATE_ML_F + h:GATE_ML_F + h + 1, :], NEG_BIG) for h in heads]
    m_inter = [b_col[h] + m_prev[h] for h in heads]
    m_t = [jnp.maximum(m_inter[h], jnp.max(dmat[h], axis=-1, keepdims=True)) for h in heads]
    w_inter = [jnp.exp(m_inter[h] - m_t[h]) for h in heads]

    xhb = [xc_ref[:, hsl[h]] for h in heads]
    q = [jnp.dot(xhb[h], wq_ref[h], preferred_element_type=F32) for h in heads]
    k = [jnp.dot(xhb[h], wk_ref[h], preferred_element_type=F32) * (ML_HEAD_DIM ** -0.5) for h in heads]
    qb = [x.astype(BF16) for x in q]
    kb = [x.astype(BF16) for x in k]
    v = [mv_ref[:, hsl[h]] for h in heads]
    smat = [lax.dot_general(qb[h], kb[h], (((1,), (1,)), ((), ())), preferred_element_type=F32) for h in heads]
    amat = [jnp.exp(dmat[h] - m_t[h]) * smat[h] for h in heads]
    ch = [c_scr[h] for h in heads]
    nh = [n_scr[h:h + 1, :] for h in heads]
    num = [jnp.dot(amat[h].astype(BF16), v[h], preferred_element_type=F32)
           + w_inter[h] * jnp.dot(qb[h], ch[h].astype(BF16), preferred_element_type=F32) for h in heads]
    den = [jnp.sum(amat[h], axis=-1, keepdims=True) + w_inter[h] * jnp.sum(q[h] * nh[h], axis=-1, keepdims=True)
           for h in heads]
    hcell = [num[h] / jnp.maximum(jnp.abs(den[h]), jnp.exp(-m_t[h])) for h in heads]

    b_last = [b_col[h][tc - 1:tc, :] for h in heads]
    g = [b_last[h] - b_col[h] + ig_col[h] for h in heads]
    m_new = [jnp.maximum(b_last[h] + m_prev[h], jnp.max(g[h], axis=0, keepdims=True)) for h in heads]
    decay = [jnp.exp(b_last[h] + m_prev[h] - m_new[h]) for h in heads]
    kw = [k[h] * jnp.exp(g[h] - m_new[h]) for h in heads]
    for h in heads:
        c_scr[h] = decay[h] * ch[h] + lax.dot_general(kw[h].astype(BF16), v[h], (((0,), (0,)), ((), ())),
                                                       preferred_element_type=F32)
        n_scr[h:h + 1, :] = decay[h] * nh[h] + jnp.sum(kw[h], axis=0, keepdims=True)
        m_scr[h:h + 1, :] = jnp.broadcast_to(m_new[h], (1, LANES))
        y = mo_ref[:, hsl[h]].astype(F32) * hcell[h]
        hm_ref[:, hsl[h]] = (_rms(y, gn_ref[:, hsl[h]]) + sk_ref[:, hsl[h]] * xhb[h].astype(F32)).astype(hm_ref.dtype)


def _mlstm(xc, mv, mo, gt, c0, n0, m0, wq, wk, gn, sk, tc, nb=1, n_valid=None):
    bsz, t, _ = xc.shape
    shared = c0.shape[0] != bsz
    sn = 1 if shared else nb
    sb = (lambda b: 0) if shared else (lambda b: b)
    tok = lambda w: pl.BlockSpec((nb, tc, w), lambda b, c: (b, c, 0))
    hd = ML_HEAD_DIM
    st_specs = [pl.BlockSpec((nb, ML_HEADS, hd, hd), lambda b, c: (b, 0, 0, 0)),
                pl.BlockSpec((nb, 8, LANES), lambda b, c: (b, 0, 0)),
                pl.BlockSpec((nb, 8, LANES), lambda b, c: (b, 0, 0))]
    return pl.pallas_call(
        functools.partial(_mlstm_kernel, tc=tc, n_valid=n_valid, nb=nb, shared_state=shared),
        grid=(bsz // nb, t // tc),
        in_specs=[tok(ML_WIDTH), tok(ML_WIDTH), tok(ML_WIDTH), tok(LANES),
                  pl.BlockSpec((sn, ML_HEADS, hd, hd), lambda b, c: (sb(b), 0, 0, 0)),
                  pl.BlockSpec((sn, 8, LANES), lambda b, c: (sb(b), 0, 0)),
                  pl.BlockSpec((sn, 8, LANES), lambda b, c: (sb(b), 0, 0)),
                  _const_spec((ML_HEADS, hd, hd)), _const_spec((ML_HEADS, hd, hd)),
                  _const_spec((1, ML_WIDTH)), _const_spec((1, ML_WIDTH))],
        out_specs=[tok(ML_WIDTH)] + st_specs,
        out_shape=[jax.ShapeDtypeStruct((bsz, t, ML_WIDTH), BF16),
                   jax.ShapeDtypeStruct((bsz, ML_HEADS, hd, hd), F32),
                   jax.ShapeDtypeStruct((bsz, 8, LANES), F32),
                   jax.ShapeDtypeStruct((bsz, 8, LANES), F32)],
        scratch_shapes=[pltpu.VMEM((nb, ML_HEADS, hd, hd), F32),
                        pltpu.VMEM((nb, 8, LANES), F32), pltpu.VMEM((nb, 8, LANES), F32)],
        compiler_params=pltpu.CompilerParams(dimension_semantics=("arbitrary", "arbitrary")),
        name="mlstm",
    )(xc, mv, mo, gt, c0, n0, m0, wq, wk, gn, sk)


def _post_kernel(a_ref, hm_ref, ga_ref, gb_ref, x_ref, wfp_ref, wmp_ref, wo_ref, wgu_ref, wd_ref,
                 g1_ref, g2_ref, g3_ref, y_ref, *, d_ff, ff_cuts):
    ya = jnp.dot(a_ref[...], wfp_ref[...], preferred_element_type=F32)
    yb = jnp.dot(hm_ref[...], wmp_ref[...], preferred_element_type=F32)
    merged = jax.nn.sigmoid(ga_ref[...].astype(F32)) * ya + jax.nn.sigmoid(gb_ref[...].astype(F32)) * yb
    mix = jnp.dot(merged.astype(BF16), wo_ref[...], preferred_element_type=F32)
    x1 = x_ref[...] + _rms(mix, g1_ref[...])
    h2 = _rms(x1, g2_ref[...]).astype(BF16)
    f = None
    for lo, hi in zip(ff_cuts[:-1], ff_cuts[1:]):
        g = jnp.dot(h2, wgu_ref[:, lo:hi], preferred_element_type=F32)
        u = jnp.dot(h2, wgu_ref[:, d_ff + lo:d_ff + hi], preferred_element_type=F32)
        t = (g * jax.nn.sigmoid(g) * u).astype(BF16)
        fj = jnp.dot(t, wd_ref[lo:hi, :], preferred_element_type=F32)
        f = fj if f is None else f + fj
    y_ref[...] = x1 + _rms(f, g3_ref[...])


def _post(a, hm, ga, gb, x, wfp, wmp, wo, wgu, wd, g1, g2, g3, tm):
    n, d = x.shape
    d_ff = wd.shape[0]
    nmt = d_ff // MXU_TILE
    ff_cuts = tuple(MXU_TILE * ((nmt * j) // FF_CHUNKS) for j in range(FF_CHUNKS)) + (d_ff,) \
        if d_ff % MXU_TILE == 0 else (0, d_ff)
    row = lambda c: pl.BlockSpec((tm, c), lambda i: (i, 0))
    return pl.pallas_call(
        functools.partial(_post_kernel, d_ff=d_ff, ff_cuts=ff_cuts),
        grid=(n // tm,),
        in_specs=[row(FOX_WIDTH), row(ML_WIDTH), row(d), row(d), row(d),
                  _const_spec(wfp.shape), _const_spec(wmp.shape), _const_spec(wo.shape),
                  _const_spec(wgu.shape), _const_spec(wd.shape),
                  _const_spec((1, d)), _const_spec((1, d)), _const_spec((1, d))],
        out_specs=row(d),
        out_shape=jax.ShapeDtypeStruct((n, d), F32),
        compiler_params=pltpu.CompilerParams(dimension_semantics=("arbitrary",), vmem_limit_bytes=VMEM_LIMIT),
        name="post_mix_ffn",
    )(a, hm, ga, gb, x, wfp, wmp, wo, wgu, wd, g1, g2, g3)


def _pad_rows(a, rows, axis):
    pad = [(0, 0)] * a.ndim
    pad[axis] = (0, rows - a.shape[axis])
    return jnp.pad(a, pad)


def _row_tile(n, pref):
    t = min(pref, n)
    while n % t:
        t //= 2
    return t


def kernel(x_prompt, x_sample, cache_fox_k, cache_fox_v, cache_fox_logf, state_mlstm_C, state_mlstm_n,
           state_mlstm_m, state_mlstm_conv, meta_tokens, w_in, b_in, g_fq, g_fk, w_conv, b_conv, w_mq, w_mk,
           g_mnorm, skip_m, w_fox_proj, w_ml_proj, w_out, g_pre_mix, g_post_mix, g_pre_ffn, g_post_ffn,
           w_gate_up, w_down):
    assert w_in.shape[0] == 1, "single-layer trunk"
    bsz, seq, d = x_prompt.shape
    dbsz, dseq, _ = x_sample.shape
    past = cache_fox_k.shape[2]
    fw, mw = FOX_WIDTH, ML_WIDTH

    w, b = w_in[0], b_in[0]
    o_ff = 3 * fw
    o_mx = o_ff + FOX_HEADS
    o_mi = o_mx + 3 * mw
    o_mf = o_mi + ML_HEADS
    o_ga = o_mf + ML_HEADS

    def relayout(a):
        gates = jnp.concatenate([a[..., o_ff:o_mx], a[..., o_mi:o_mf], a[..., o_mf:o_ga]], axis=-1)
        gates = _pad_rows(gates, LANES, gates.ndim - 1)
        return jnp.concatenate([a[..., :o_ff], a[..., o_mx:o_mi], a[..., o_ga:], gates], axis=-1)

    w_p = relayout(w).astype(BF16)
    b_p = relayout(b)[None, :]
    hid = jnp.arange(fw) // FOX_HEAD_DIM
    hsum = (hid[:, None] == hid[None, :]).astype(BF16)
    gq = g_fq[0].reshape(1, fw)
    gk = g_fk[0].reshape(1, fw)
    gpre = g_pre_mix[0][None, :]
    wc, bc = w_conv[0], b_conv[0][None, :]
    inproj = lambda x, c0, cp, tm, nt, **kw: _inproj(x, gpre, w_p, b_p, gq, gk, hsum, c0, cp, wc, bc, tm, nt, **kw)

    wq, wk = w_mq[0].astype(BF16), w_mk[0].astype(BF16)
    gn, sk = g_mnorm[0].reshape(1, mw), skip_m[0][None, :]
    mlstm = lambda *a, **kw: _mlstm(*a, wq, wk, gn, sk, **kw)

    def conv_hist(rows):
        return jnp.pad(rows, ((0, 0), (8 - (CONV_WIDTH - 1), 0), (0, 0)))

    def m_rows(m):
        return jnp.broadcast_to(_pad_rows(m, 8, 1)[:, :, None], (m.shape[0], 8, LANES))

    mt = 64
    xm = _pad_rows(meta_tokens.astype(F32), mt, 0)
    (_, kf_m, ka_m, vf_m, va_m, xc_m, mv_m, mo_m, _, _, gt_m, fc_m, ct_m) = inproj(
        xm, jnp.zeros((1, 1, LANES), F32), jnp.zeros((1, 8, mw), F32), mt, 1, conv_valid=N_META)
    zc = jnp.zeros((1, ML_HEADS, ML_HEAD_DIM, ML_HEAD_DIM), F32)
    z8 = jnp.zeros((1, 8, LANES), F32)
    _, c_m, n_m, m_m = mlstm(xc_m[None], mv_m[None], mo_m[None], gt_m[None], zc, z8, z8, tc=mt, n_valid=N_META)

    xp = x_prompt.reshape(bsz * seq, d)
    tm = _row_tile(seq, 512)
    lanes_t = lambda rows: _pad_rows(rows[:N_META].T, LANES, 1)
    (qa_p, kt_p, ka_p, vt_p, va_p, xc_p, mv_p, mo_p, ga_p, gb_p, gt_p, _, ct_p, lt_p) = inproj(
        xp, fc_m[None, N_META - 1:N_META], ct_m, tm, seq // tm,
        meta_t=(lanes_t(kf_m), lanes_t(vf_m), lanes_t(gt_m[:, :FOX_HEADS])))
    r3 = lambda a: a.reshape(bsz, seq, a.shape[-1])
    hm_p, c_p, n_p, m_p = mlstm(r3(xc_p), r3(mv_p), r3(mo_p), r3(gt_p), c_m, n_m, m_m,
                                tc=_row_tile(seq, 512), nb=_row_tile(bsz, 4))
    a_p = _fox(r3(qa_p), _pad_rows(ka_m, LANES, 0)[None], _pad_rows(va_m, LANES, 0)[None], r3(ka_p), r3(va_p),
               tq=_row_tile(seq, 4096), tr=_row_tile(seq, 512), pblk=LANES, pvalid=N_META)

    tok_minor = lambda c: c[0].transpose(0, 2, 3, 1).reshape(dbsz, fw, past)
    kc, vc, f_c = _cacheprep(tok_minor(cache_fox_k), tok_minor(cache_fox_v),
                             _pad_rows(cache_fox_logf[0].astype(F32), LANES, 2), _row_tile(past, 512))
    xs = x_sample.reshape(dbsz * dseq, d)
    (qa_s, kf_s, ka_s, vf_s, va_s, xc_s, mv_s, mo_s, ga_s, gb_s, gt_s, _, ct_s) = inproj(
        xs, f_c, conv_hist(state_mlstm_conv[0]), dseq, 1)
    r3s = lambda a: a.reshape(dbsz, dseq, a.shape[-1])
    hm_s, c_s, n_s, m_s = mlstm(r3s(xc_s), r3s(mv_s), r3s(mo_s), r3s(gt_s), state_mlstm_C[0], _pad_rows(state_mlstm_n[0], 8, 1), m_rows(state_mlstm_m[0]),
                                tc=dseq, nb=_row_tile(dbsz, 4))
    a_s = _fox(r3s(qa_s), kc, vc, r3s(ka_s), r3s(va_s), tq=dseq, tr=dseq, pblk=_row_tile(past, 512), hps=FOX_HEADS)

    wfp, wmp, wo = w_fox_proj[0].astype(BF16), w_ml_proj[0].astype(BF16), w_out[0].astype(BF16)
    wgu, wd = w_gate_up[0].astype(BF16), w_down[0].astype(BF16)
    g1, g2, g3 = g_post_mix[0][None, :], g_pre_ffn[0][None, :], g_post_ffn[0][None, :]
    post = lambda a, hm, ga, gb, x, tm: _post(a, hm, ga, gb, x, wfp, wmp, wo, wgu, wd, g1, g2, g3, tm)
    y_p = post(a_p.reshape(bsz * seq, fw), hm_p.reshape(bsz * seq, mw), ga_p, gb_p, xp, _row_tile(bsz * seq, 512))
    y_s = post(a_s.reshape(dbsz * dseq, fw), hm_s.reshape(dbsz * dseq, mw), ga_s, gb_s, xs,
               _row_tile(dbsz * dseq, 512))

    def tokens_major(a):
        return a.reshape(bsz, FOX_HEADS, FOX_HEAD_DIM, a.shape[-1]).transpose(0, 3, 1, 2)[None]

    heads = lambda a: a.reshape(a.shape[:-1] + (FOX_HEADS, FOX_HEAD_DIM))
    tail = CONV_WIDTH - 1
    return (y_p.reshape(bsz, seq, d), y_s.reshape(dbsz, dseq, d),
            tokens_major(kt_p), tokens_major(vt_p), lt_p.transpose(0, 2, 1)[None],
            c_p[None], n_p[None, :, :ML_HEADS], m_p[None, :, :ML_HEADS, 0], ct_p[None, :, 8 - tail:],
            heads(r3s(kf_s))[None], heads(r3s(vf_s))[None], r3s(gt_s)[None, ..., :FOX_HEADS],
            c_s[None], n_s[None, :, :ML_HEADS], m_s[None, :, :ML_HEADS, 0], ct_s[None, :, 8 - tail:])
```

```python
import functools

import jax
import jax.numpy as jnp
from jax import lax
from jax.experimental import pallas as pl
from jax.experimental.pallas import tpu as pltpu

F32 = jnp.float32
BF16 = jnp.bfloat16

N_META = 16
FOX_HEADS = 8
FOX_HEAD_DIM = 64
FOX_WIDTH = FOX_HEADS * FOX_HEAD_DIM
ML_HEADS = 4
ML_HEAD_DIM = 128
ML_WIDTH = ML_HEADS * ML_HEAD_DIM
CONV_WIDTH = 4
RMS_EPS = 1e-6
PAD_LOG_GATE = -1e30
NEG_BIG = -1e30

LANES = 128
GATE_ML_I = FOX_HEADS
GATE_ML_F = FOX_HEADS + ML_HEADS

AUG_STEP = FOX_HEADS
AUG_QF = FOX_HEAD_DIM
AUG_KF = FOX_HEAD_DIM + 24

VMEM_LIMIT = 56 * 1024 * 1024
FOX_WIDE_KEYS = 1024
FOX_MAX_STATIC_BLOCKS = 4
MXU_TILE = 256
FF_CHUNKS = 2


def _const_spec(shape):
    nd = len(shape)
    return pl.BlockSpec(shape, lambda *_: (0,) * nd, pipeline_mode=pl.Buffered(1))


def _rms(x, g):
    return x * lax.rsqrt(jnp.mean(x * x, axis=-1, keepdims=True) + RMS_EPS) * g


def _split3(x):
    hi = x.astype(BF16).astype(F32)
    r = x - hi
    mid = r.astype(BF16).astype(F32)
    lo = (r - mid).astype(BF16).astype(F32)
    return hi, mid, lo


def _cumsum_rows(lg, cur):
    tm = lg.shape[0]
    blk = min(tm, LANES)
    r = lax.broadcasted_iota(jnp.int32, (blk, blk), 0)
    c = lax.broadcasted_iota(jnp.int32, (blk, blk), 1)
    tril = (c <= r).astype(BF16)
    outs = []
    for s in range(tm // blk):
        hi, mid, lo = _split3(lg[s * blk:(s + 1) * blk])
        f = (jnp.dot(tril, lo.astype(BF16), preferred_element_type=F32)
             + jnp.dot(tril, mid.astype(BF16), preferred_element_type=F32)
             + jnp.dot(tril, hi.astype(BF16), preferred_element_type=F32)) + cur
        outs.append(f)
        cur = f[blk - 1:blk]
    return (outs[0] if len(outs) == 1 else jnp.concatenate(outs, axis=0)), cur


def _store_attention_operands(qn, kn, v, fcol, q_ref, k_ref, v_ref, fillers=()):
    tm = fcol.shape[0]
    lane = lax.broadcasted_iota(jnp.int32, (tm, LANES), 1)
    data = lane < FOX_HEAD_DIM
    hi, mid, lo = _split3(fcol)
    comb = jnp.where(lane < AUG_STEP, hi, jnp.where(lane < 2 * AUG_STEP, pltpu.roll(mid, AUG_STEP, 1),
                                                     pltpu.roll(lo, 2 * AUG_STEP, 1)))
    ncomb = -comb
    is_qf = (lane == AUG_QF) | (lane == AUG_QF + AUG_STEP) | (lane == AUG_QF + 2 * AUG_STEP)
    is_kf = (lane == AUG_KF) | (lane == AUG_KF + AUG_STEP) | (lane == AUG_KF + 2 * AUG_STEP)
    base_q = jnp.where(is_kf, 1.0, 0.0)
    base_k = jnp.where(is_qf, 1.0, 0.0)
    base_v = jnp.where(lane == FOX_HEAD_DIM, 1.0, 0.0)
    fillers = list(fillers)
    for h in range(FOX_HEADS):
        t, odd = divmod(h, 2)

        def head_tile(z):
            x = z[:, t * LANES:(t + 1) * LANES]
            return pltpu.roll(x, FOX_HEAD_DIM, 1) if odd else x

        hs = slice(h * LANES, (h + 1) * LANES)
        if q_ref is not None:
            aug = jnp.where(is_qf, pltpu.roll(comb, AUG_QF - h, 1), base_q)
            q_ref[:, hs] = jnp.where(data, head_tile(qn), aug).astype(BF16)
        aug = jnp.where(is_kf, pltpu.roll(ncomb, AUG_KF - h, 1), base_k)
        k_ref[:, hs] = jnp.where(data, head_tile(kn), aug).astype(BF16)
        v_ref[:, hs] = jnp.where(data, head_tile(v), base_v).astype(BF16)
        if fillers:
            fillers.pop(0)()
    for f in fillers:
        f()


def _inproj_kernel(*refs, d_model, nt, kv_t, conv_valid):
    (x_ref, gpre_ref, w_ref, b_ref, gq_ref, gk_ref, hs_ref, c0_ref, cp_ref, wc_ref, bc_ref), rest = refs[:11], refs[11:]
    if kv_t:
        kmeta_ref, vmeta_ref, lmeta_ref = rest[:3]
        (q_ref, kf_ref, kb_ref, vf_ref, vb_ref, xc_ref, mv_ref, mo_ref, ga_ref, gb_ref, gt_ref, fc_ref, ct_ref,
         lt_ref, carry, xbuf, kcar, vcar, lcar) = rest[3:]
    else:
        (q_ref, kf_ref, kb_ref, vf_ref, vb_ref, xc_ref, mv_ref, mo_ref, ga_ref, gb_ref, gt_ref, fc_ref, ct_ref,
         carry, xbuf) = rest
    fw, mw = FOX_WIDTH, ML_WIDTH
    offs = [0, fw, 2 * fw, 3 * fw, 3 * fw + mw, 3 * fw + 2 * mw, 3 * fw + 3 * mw,
            3 * fw + 3 * mw + d_model, 3 * fw + 3 * mw + 2 * d_model, 3 * fw + 3 * mw + 2 * d_model + LANES]
    r = pl.program_id(1)

    @pl.when(r == 0)
    def _():
        carry[...] = c0_ref[...]
        xbuf[0:8, :] = cp_ref[...]
        if kv_t:
            kcar[...] = kmeta_ref[...]
            vcar[...] = vmeta_ref[...]
            lcar[...] = lmeta_ref[...]

    def store_shifted(xt, car, out_ref):
        sh = pltpu.roll(xt, N_META, 1)
        lane = lax.broadcasted_iota(jnp.int32, (xt.shape[0], LANES), 1)
        out_ref[:, :LANES] = jnp.where(lane < N_META, car[...], sh[:, :LANES])
        out_ref[:, LANES:] = sh[:, LANES:]
        car[...] = sh[:, :LANES]

    @pl.when(r < nt)
    def _():
        h = _rms(x_ref[...], gpre_ref[...]).astype(BF16)

        def seg(i):
            a, b = offs[i], offs[i + 1]
            return jnp.dot(h, w_ref[:, a:b], preferred_element_type=F32) + b_ref[:, a:b]

        def headnorm(z, g):
            ms = jnp.dot((z * z).astype(BF16), hs_ref[...], preferred_element_type=F32) * (1.0 / FOX_HEAD_DIM)
            return z * lax.rsqrt(ms + RMS_EPS) * g

        g = seg(8)
        ga_ref[...] = seg(6).astype(BF16)
        lane = lax.broadcasted_iota(jnp.int32, g.shape, 1)
        raw = (lane >= GATE_ML_I) & (lane < GATE_ML_F)
        gt = jnp.where(raw, g, jax.nn.log_sigmoid(g))
        gt_ref[...] = gt
        fcol, cur = _cumsum_rows(gt, carry[...])
        carry[...] = cur
        fc_ref[...] = fcol

        v = seg(2)
        q = headnorm(seg(0), gq_ref[...]) * (FOX_HEAD_DIM ** -0.5)
        k = headnorm(seg(1), gk_ref[...])
        def store_kv():
            if kv_t:
                store_shifted(k.T, kcar, kf_ref)
                store_shifted(v.T, vcar, vf_ref)
                store_shifted(gt.T[:FOX_HEADS], lcar, lt_ref)
            else:
                kf_ref[...] = k
                vf_ref[...] = v

        def store_xc():
            tm = x_ref.shape[0]
            xbuf[8:8 + tm, :] = seg(3)
            xc = bc_ref[...]
            for i in range(CONV_WIDTH):
                xc = xc + xbuf[5 + i:5 + i + tm, :] * wc_ref[i:i + 1, :]
            xc_ref[...] = (xc * jax.nn.sigmoid(xc)).astype(BF16)
            ct_ref[...] = xbuf[conv_valid:conv_valid + 8, :]
            xbuf[0:8, :] = xbuf[tm:tm + 8, :]

        def store_mv():
            mv_ref[...] = seg(4).astype(BF16)

        def store_mo():
            mo_ref[...] = jax.nn.sigmoid(seg(5)).astype(BF16)

        def store_gb():
            gb_ref[...] = seg(7).astype(BF16)

        _store_attention_operands(q, k, v, fcol, q_ref, kb_ref, vb_ref,
                                  fillers=(store_xc, store_mv, store_mo, store_gb, store_kv))

    if kv_t:
        @pl.when(r == nt)
        def _():
            kf_ref[:, :LANES] = kcar[...]
            vf_ref[:, :LANES] = vcar[...]
            lt_ref[:, :LANES] = lcar[...]


def _inproj(x, gpre, w, b, gq, gk, hs, c0, cp, wc, bc, tm, nt, meta_t=None, conv_valid=None):
    n, d = x.shape
    nb = n // (nt * tm)
    dinp = w.shape[1]
    fw, mw = FOX_WIDTH, ML_WIDTH
    aw = FOX_HEADS * LANES
    kv_t = meta_t is not None
    tile = lambda b, r: b * nt + jnp.minimum(r, nt - 1)
    row = lambda c: pl.BlockSpec((tm, c), lambda b, r: (tile(b, r), 0))
    cb = (lambda b: b) if c0.shape[0] > 1 else (lambda b: 0)
    pb = (lambda b: b) if cp.shape[0] > 1 else (lambda b: 0)
    lp = N_META + nt * tm
    tmin = lambda c: pl.BlockSpec((None, c, tm), lambda b, r: (b, 0, r))
    flat = lambda c, t: (row(c), jax.ShapeDtypeStruct((n, c), t))
    tok_minor = lambda c: (tmin(c), jax.ShapeDtypeStruct((nb, c, lp), F32))
    outs = [flat(aw, BF16), tok_minor(fw) if kv_t else flat(fw, F32), flat(aw, BF16),
            tok_minor(fw) if kv_t else flat(fw, F32), flat(aw, BF16), flat(mw, BF16), flat(mw, BF16), flat(mw, BF16),
            flat(d, BF16), flat(d, BF16), flat(LANES, F32), flat(LANES, F32),
            (pl.BlockSpec((None, 8, mw), lambda b, r: (b, 0, 0)), jax.ShapeDtypeStruct((nb, 8, mw), F32))]
    in_specs = [row(d), _const_spec((1, d)), _const_spec((d, dinp)), _const_spec((1, dinp)),
                _const_spec((1, fw)), _const_spec((1, fw)), _const_spec((fw, fw)),
                pl.BlockSpec((None, 1, LANES), lambda b, r: (cb(b), 0, 0)),
                pl.BlockSpec((None, 8, mw), lambda b, r: (pb(b), 0, 0)),
                _const_spec((CONV_WIDTH, mw)), _const_spec((1, mw))]
    scratch = [pltpu.VMEM((1, LANES), F32), pltpu.VMEM((8 + tm, mw), F32)]
    args = [x, gpre, w, b, gq, gk, hs, c0, cp, wc, bc]
    if kv_t:
        outs.append(tok_minor(FOX_HEADS))
        in_specs += [_const_spec((fw, LANES)), _const_spec((fw, LANES)), _const_spec((FOX_HEADS, LANES))]
        scratch += [pltpu.VMEM((fw, LANES), F32), pltpu.VMEM((fw, LANES), F32), pltpu.VMEM((FOX_HEADS, LANES), F32)]
        args += list(meta_t)
    return pl.pallas_call(
        functools.partial(_inproj_kernel, d_model=d, nt=nt, kv_t=kv_t,
                          conv_valid=tm if conv_valid is None else conv_valid),
        grid=(nb, nt + (1 if kv_t else 0)),
        in_specs=in_specs,
        out_specs=[o[0] for o in outs],
        out_shape=[o[1] for o in outs],
        scratch_shapes=scratch,
        compiler_params=pltpu.CompilerParams(dimension_semantics=("arbitrary", "arbitrary"),
                                             vmem_limit_bytes=VMEM_LIMIT),
        name="inproj",
    )(*args)


def _cacheprep_kernel(k_ref, v_ref, lf_ref, ka_ref, va_ref, fl_ref, carry):
    @pl.when(pl.program_id(1) == 0)
    def _():
        carry[...] = jnp.zeros_like(carry)

    fcol, cur = _cumsum_rows(lf_ref[...], carry[...])
    carry[...] = cur
    fl_ref[...] = cur
    _store_attention_operands(None, k_ref[...].T, v_ref[...].T, fcol, None, ka_ref, va_ref)


def _cacheprep(ck, cv, clf, tp):
    bsz, _, plen = ck.shape
    aw = FOX_HEADS * LANES
    return pl.pallas_call(
        _cacheprep_kernel,
        grid=(bsz, plen // tp),
        in_specs=[pl.BlockSpec((None, FOX_WIDTH, tp), lambda b, i: (b, 0, i)),
                  pl.BlockSpec((None, FOX_WIDTH, tp), lambda b, i: (b, 0, i)),
                  pl.BlockSpec((None, tp, LANES), lambda b, i: (b, i, 0))],
        out_specs=[pl.BlockSpec((None, tp, aw), lambda b, i: (b, i, 0)),
                   pl.BlockSpec((None, tp, aw), lambda b, i: (b, i, 0)),
                   pl.BlockSpec((None, 1, LANES), lambda b, i: (b, 0, 0))],
        out_shape=[jax.ShapeDtypeStruct((bsz, plen, aw), BF16), jax.ShapeDtypeStruct((bsz, plen, aw), BF16),
                   jax.ShapeDtypeStruct((bsz, 1, LANES), F32)],
        scratch_shapes=[pltpu.VMEM((1, LANES), F32)],
        compiler_params=pltpu.CompilerParams(dimension_semantics=("arbitrary", "arbitrary")),
        name="cache_prep",
    )(ck, cv, clf)


def _fox_kernel(q_ref, kp_ref, vp_ref, k_ref, v_ref, o_ref, *, tq, tr, tkb, nq, hps, pblk, n_pblk, pvalid):
    i = pl.program_id(2)
    nr = tq // tr
    chains = [(hh, r) for hh in range(hps) for r in range(nr)]
    qs = [q_ref[r * tr:(r + 1) * tr, hh * LANES:(hh + 1) * LANES] for hh, r in chains]

    def upd1(n, state, kb, vb, mask=None):
        m, acc = state
        hh = chains[n][0]
        hs = slice(hh * LANES, (hh + 1) * LANES)
        s = lax.dot_general(qs[n], kb[:, hs], (((1,), (1,)), ((), ())), preferred_element_type=F32)
        if mask is not None:
            cut = s.shape[1] - mask.shape[1]
            tail = jnp.where(mask, s[:, cut:], NEG_BIG)
            s = tail if cut == 0 else jnp.concatenate([s[:, :cut], tail], axis=1)
        m_new = jnp.maximum(m, jnp.max(s, axis=-1, keepdims=True))
        alpha = jnp.exp(m - m_new)
        pm = jnp.exp(s - m_new)
        acc = alpha * acc + jnp.dot(pm.astype(BF16), vb[:, hs], preferred_element_type=F32)
        return m_new, acc

    def upd(carry, kb, vb):
        return tuple(upd1(n, carry[n], kb, vb) for n in range(len(chains)))

    def start():
        carry = tuple((jnp.full((tr, 1), NEG_BIG, F32), jnp.zeros((tr, LANES), F32)) for _ in chains)
        if pvalid is None:
            for jb in range(n_pblk):
                sl = slice(jb * pblk, (jb + 1) * pblk)
                carry = upd(carry, kp_ref[sl, :], vp_ref[sl, :])
        return carry

    def finish(carry, off):
        tail_w = tr + (LANES if pvalid is not None else 0)
        rj = lax.broadcasted_iota(jnp.int32, (tr, tail_w), 0)
        cj = lax.broadcasted_iota(jnp.int32, (tr, tail_w), 1)
        mask = cj <= rj
        if pvalid is not None:
            mask = (mask | (cj >= tr)) & (cj < tr + pvalid)
        carry = list(carry)
        for r in reversed(range(nr)):
            width = (r + 1) * tr
            kb = k_ref[pl.ds(off, width), :]
            vb = v_ref[pl.ds(off, width), :]
            if pvalid is not None:
                kb = jnp.concatenate([kb, kp_ref[...]], axis=0)
                vb = jnp.concatenate([vb, vp_ref[...]], axis=0)
            for n, (hh, rr) in enumerate(chains):
                if rr == r:
                    carry[n] = upd1(n, carry[n], kb, vb, mask=mask)
        lane = lax.broadcasted_iota(jnp.int32, (tr, LANES), 1)
        outs = []
        for n in range(len(chains)):
            acc = carry[n][1]
            l = jnp.sum(jnp.where(lane == FOX_HEAD_DIM, acc, 0.0), axis=-1, keepdims=True)
            outs.append(acc / l)
        for hp in range(hps // 2):
            for r in range(nr):
                even, odd = outs[2 * hp * nr + r], outs[(2 * hp + 1) * nr + r]
                o_ref[r * tr:(r + 1) * tr, hp * LANES:(hp + 1) * LANES] = jnp.where(
                    lane < FOX_HEAD_DIM, even, pltpu.roll(odd, FOX_HEAD_DIM, 1)).astype(o_ref.dtype)

    if tq % tkb == 0 and nq <= FOX_MAX_STATIC_BLOCKS:
        for qi in range(nq):
            @pl.when(i == qi)
            def _():
                carry = start()
                for j in range(qi * tq // tkb):
                    carry = upd(carry, k_ref[j * tkb:(j + 1) * tkb, :], v_ref[j * tkb:(j + 1) * tkb, :])
                finish(carry, qi * tq)
    else:
        tkw = max(tkb, tq)

        def body_wide(j, carry):
            off = pl.multiple_of(j * tkw, tkw)
            return upd(carry, k_ref[pl.ds(off, tkw), :], v_ref[pl.ds(off, tkw), :])

        def body(j, carry):
            off = pl.multiple_of(j * tq, tq)
            return upd(carry, k_ref[pl.ds(off, tq), :], v_ref[pl.ds(off, tq), :])

        n_wide = i // (tkw // tq)
        carry = lax.fori_loop(0, n_wide, body_wide, start())
        if tkw > tq:
            carry = lax.fori_loop(n_wide * (tkw // tq), i, body, carry)
        finish(carry, pl.multiple_of(i * tq, tq))


def _fox(q, kp, vp, k, v, tq, tr, pblk, pvalid=None, hps=2):
    bsz, t, _ = q.shape
    bp, plen, _ = kp.shape
    nq = t // tq
    n_pblk = plen // pblk
    pw = hps * LANES
    pb = (lambda b: b) if bp == bsz else (lambda b: 0)
    return pl.pallas_call(
        functools.partial(_fox_kernel, tq=tq, tr=tr, tkb=_row_tile(t, FOX_WIDE_KEYS), nq=nq, hps=hps, pblk=pblk,
                          n_pblk=n_pblk, pvalid=pvalid),
        grid=(bsz, FOX_HEADS // hps, nq),
        in_specs=[pl.BlockSpec((None, tq, pw), lambda b, p, i: (b, i, p)),
                  pl.BlockSpec((None, plen, pw), lambda b, p, i: (pb(b), 0, p)),
                  pl.BlockSpec((None, plen, pw), lambda b, p, i: (pb(b), 0, p)),
                  pl.BlockSpec((None, t, pw), lambda b, p, i: (b, 0, p)),
                  pl.BlockSpec((None, t, pw), lambda b, p, i: (b, 0, p))],
        out_specs=pl.BlockSpec((None, tq, hps * FOX_HEAD_DIM), lambda b, p, i: (b, i, p)),
        out_shape=jax.ShapeDtypeStruct((bsz, t, FOX_WIDTH), BF16),
        compiler_params=pltpu.CompilerParams(dimension_semantics=("arbitrary", "arbitrary", "arbitrary"),
                                             vmem_limit_bytes=VMEM_LIMIT),
        name="fox_attention",
    )(q, kp, vp, k, v)


def _mlstm_kernel(xc_ref, mv_ref, mo_ref, gt_ref, c0_ref, n0_ref, m0_ref, wq_ref, wk_ref, gn_ref, sk_ref,
                  hm_ref, cout_ref, nout_ref, mout_ref,
                  c_scr, n_scr, m_scr, *, tc, n_valid, nb, shared_state):
    c = pl.program_id(1)

    @pl.when(c == 0)
    def _():
        for bi in range(nb):
            si = 0 if shared_state else bi
            c_scr[bi] = c0_ref[si]
            n_scr[bi] = n0_ref[si]
            m_scr[bi] = m0_ref[si]

    for bi in range(nb):
        _mlstm_chunk(xc_ref.at[bi], mv_ref.at[bi], mo_ref.at[bi], gt_ref.at[bi], wq_ref, wk_ref,
                     gn_ref, sk_ref, hm_ref.at[bi], c_scr.at[bi], n_scr.at[bi], m_scr.at[bi],
                     tc=tc, n_valid=n_valid)

    @pl.when(c == pl.num_programs(1) - 1)
    def _():
        cout_ref[...] = c_scr[...]
        nout_ref[...] = n_scr[...]
        mout_ref[...] = m_scr[...]


def _mlstm_chunk(xc_ref, mv_ref, mo_ref, gt_ref, wq_ref, wk_ref, gn_ref, sk_ref, hm_ref,
                 c_scr, n_scr, m_scr, *, tc, n_valid):
    gt = gt_ref[...]
    lane = lax.broadcasted_iota(jnp.int32, (tc, LANES), 1)
    if n_valid is not None:
        rowi = lax.broadcasted_iota(jnp.int32, (tc, LANES), 0)
        is_i = (lane >= GATE_ML_I) & (lane < GATE_ML_F)
        gt = jnp.where(rowi < n_valid, gt, jnp.where(is_i, PAD_LOG_GATE, 0.0))
    ri = lax.broadcasted_iota(jnp.int32, (tc, tc), 0)
    ci = lax.broadcasted_iota(jnp.int32, (tc, tc), 1)
    causal = ci <= ri
    b_all = _cumsum_rows(gt, jnp.zeros((1, LANES), F32))[0]
    r_t = (pltpu.roll(gt, GATE_ML_F - GATE_ML_I, 1) - b_all).T
    heads = range(ML_HEADS)
    hsl = [slice(h * ML_HEAD_DIM, (h + 1) * ML_HEAD_DIM) for h in heads]
    b_col = [b_all[:, GATE_ML_F + h:GATE_ML_F + h + 1] for h in heads]
    ig_col = [gt[:, GATE_ML_I + h:GATE_ML_I + h + 1] for h in heads]
    m_prev = [m_scr[h:h + 1, 0:1] for h in heads]
    dmat = [jnp.where(causal, b_col[h] + r_t[GATE_ML_F + h:GATE_ML_F + h + 1, :], NEG_BIG) for h in heads]
    m_inter = [b_col[h] + m_prev[h] for h in heads]
    m_t = [jnp.maximum(m_inter[h], jnp.max(dmat[h], axis=-1, keepdims=True)) for h in heads]
    w_inter = [jnp.exp(m_inter[h] - m_t[h]) for h in heads]

    xhb = [xc_ref[:, hsl[h]] for h in heads]
    q = [jnp.dot(xhb[h], wq_ref[h], preferred_element_type=F32) for h in heads]
    k = [jnp.dot(xhb[h], wk_ref[h], preferred_element_type=F32) * (ML_HEAD_DIM ** -0.5) for h in heads]
    qb = [x.astype(BF16) for x in q]
    kb = [x.astype(BF16) for x in k]
    v = [mv_ref[:, hsl[h]] for h in heads]
    smat = [lax.dot_general(qb[h], kb[h], (((1,), (1,)), ((), ())), preferred_element_type=F32) for h in heads]
    amat = [jnp.exp(dmat[h] - m_t[h]) * smat[h] for h in heads]
    ch = [c_scr[h] for h in heads]
    nh = [n_scr[h:h + 1, :] for h in heads]
    num = [jnp.dot(amat[h].astype(BF16), v[h], preferred_element_type=F32)
           + w_inter[h] * jnp.dot(qb[h], ch[h].astype(BF16), preferred_element_type=F32) for h in heads]
    den = [jnp.sum(amat[h], axis=-1, keepdims=True) + w_inter[h] * jnp.sum(q[h] * nh[h], axis=-1, keepdims=True)
           for h in heads]
    hcell = [num[h] / jnp.maximum(jnp.abs(den[h]), jnp.exp(-m_t[h])) for h in heads]

    b_last = [b_col[h][tc - 1:tc, :] for h in heads]
    g = [b_last[h] - b_col[h] + ig_col[h] for h in heads]
    m_new = [jnp.maximum(b_last[h] + m_prev[h], jnp.max(g[h], axis=0, keepdims=True)) for h in heads]
    decay = [jnp.exp(b_last[h] + m_prev[h] - m_new[h]) for h in heads]
    kw = [k[h] * jnp.exp(g[h] - m_new[h]) for h in heads]
    for h in heads:
        c_scr[h] = decay[h] * ch[h] + lax.dot_general(kw[h].astype(BF16), v[h], (((0,), (0,)), ((), ())),
                                                       preferred_element_type=F32)
        n_scr[h:h + 1, :] = decay[h] * nh[h] + jnp.sum(kw[h], axis=0, keepdims=True)
        m_scr[h:h + 1, :] = jnp.broadcast_to(m_new[h], (1, LANES))
        y = mo_ref[:, hsl[h]].astype(F32) * hcell[h]
        hm_ref[:, hsl[h]] = (_rms(y, gn_ref[:, hsl[h]]) + sk_ref[:, hsl[h]] * xhb[h].astype(F32)).astype(hm_ref.dtype)


def _mlstm(xc, mv, mo, gt, c0, n0, m0, wq, wk, gn, sk, tc, nb=1, n_valid=None):
    bsz, t, _ = xc.shape
    shared = c0.shape[0] != bsz
    sn = 1 if shared else nb
    sb = (lambda b: 0) if shared else (lambda b: b)
    tok = lambda w: pl.BlockSpec((nb, tc, w), lambda b, c: (b, c, 0))
    hd = ML_HEAD_DIM
    st_specs = [pl.BlockSpec((nb, ML_HEADS, hd, hd), lambda b, c: (b, 0, 0, 0)),
                pl.BlockSpec((nb, 8, LANES), lambda b, c: (b, 0, 0)),
                pl.BlockSpec((nb, 8, LANES), lambda b, c: (b, 0, 0))]
    return pl.pallas_call(
        functools.partial(_mlstm_kernel, tc=tc, n_valid=n_valid, nb=nb, shared_state=shared),
        grid=(bsz // nb, t // tc),
        in_specs=[tok(ML_WIDTH), tok(ML_WIDTH), tok(ML_WIDTH), tok(LANES),
                  pl.BlockSpec((sn, ML_HEADS, hd, hd), lambda b, c: (sb(b), 0, 0, 0)),
                  pl.BlockSpec((sn, 8, LANES), lambda b, c: (sb(b), 0, 0)),
                  pl.BlockSpec((sn, 8, LANES), lambda b, c: (sb(b), 0, 0)),
                  _const_spec((ML_HEADS, hd, hd)), _const_spec((ML_HEADS, hd, hd)),
                  _const_spec((1, ML_WIDTH)), _const_spec((1, ML_WIDTH))],
        out_specs=[tok(ML_WIDTH)] + st_specs,
        out_shape=[jax.ShapeDtypeStruct((bsz, t, ML_WIDTH), BF16),
                   jax.ShapeDtypeStruct((bsz, ML_HEADS, hd, hd), F32),
                   jax.ShapeDtypeStruct((bsz, 8, LANES), F32),
                   jax.ShapeDtypeStruct((bsz, 8, LANES), F32)],
        scratch_shapes=[pltpu.VMEM((nb, ML_HEADS, hd, hd), F32),
                        pltpu.VMEM((nb, 8, LANES), F32), pltpu.VMEM((nb, 8, LANES), F32)],
        compiler_params=pltpu.CompilerParams(dimension_semantics=("arbitrary", "arbitrary"),
                                             vmem_limit_bytes=VMEM_LIMIT),
        name="mlstm",
    )(xc, mv, mo, gt, c0, n0, m0, wq, wk, gn, sk)


def _post_kernel(a_ref, hm_ref, ga_ref, gb_ref, x_ref, wfp_ref, wmp_ref, wo_ref, wgu_ref, wd_ref,
                 g1_ref, g2_ref, g3_ref, y_ref, *, d_ff, ff_cuts):
    ya = jnp.dot(a_ref[...], wfp_ref[...], preferred_element_type=F32)
    yb = jnp.dot(hm_ref[...], wmp_ref[...], preferred_element_type=F32)
    merged = jax.nn.sigmoid(ga_ref[...].astype(F32)) * ya + jax.nn.sigmoid(gb_ref[...].astype(F32)) * yb
    mix = jnp.dot(merged.astype(BF16), wo_ref[...], preferred_element_type=F32)
    x1 = x_ref[...] + _rms(mix, g1_ref[...])
    h2 = _rms(x1, g2_ref[...]).astype(BF16)
    f = None
    for lo, hi in zip(ff_cuts[:-1], ff_cuts[1:]):
        g = jnp.dot(h2, wgu_ref[:, lo:hi], preferred_element_type=F32)
        u = jnp.dot(h2, wgu_ref[:, d_ff + lo:d_ff + hi], preferred_element_type=F32)
        t = (g * jax.nn.sigmoid(g) * u).astype(BF16)
        fj = jnp.dot(t, wd_ref[lo:hi, :], preferred_element_type=F32)
        f = fj if f is None else f + fj
    y_ref[...] = x1 + _rms(f, g3_ref[...])


def _post(a, hm, ga, gb, x, wfp, wmp, wo, wgu, wd, g1, g2, g3, tm):
    n, d = x.shape
    d_ff = wd.shape[0]
    nmt = d_ff // MXU_TILE
    ff_cuts = tuple(MXU_TILE * ((nmt * j) // FF_CHUNKS) for j in range(FF_CHUNKS)) + (d_ff,) \
        if d_ff % MXU_TILE == 0 else (0, d_ff)
    row = lambda c: pl.BlockSpec((tm, c), lambda i: (i, 0))
    return pl.pallas_call(
        functools.partial(_post_kernel, d_ff=d_ff, ff_cuts=ff_cuts),
        grid=(n // tm,),
        in_specs=[row(FOX_WIDTH), row(ML_WIDTH), row(d), row(d), row(d),
                  _const_spec(wfp.shape), _const_spec(wmp.shape), _const_spec(wo.shape),
                  _const_spec(wgu.shape), _const_spec(wd.shape),
                  _const_spec((1, d)), _const_spec((1, d)), _const_spec((1, d))],
        out_specs=row(d),
        out_shape=jax.ShapeDtypeStruct((n, d), F32),
        compiler_params=pltpu.CompilerParams(dimension_semantics=("arbitrary",), vmem_limit_bytes=VMEM_LIMIT),
        name="post_mix_ffn",
    )(a, hm, ga, gb, x, wfp, wmp, wo, wgu, wd, g1, g2, g3)


def _pad_rows(a, rows, axis):
    pad = [(0, 0)] * a.ndim
    pad[axis] = (0, rows - a.shape[axis])
    return jnp.pad(a, pad)


def _row_tile(n, pref):
    t = min(pref, n)
    while n % t:
        t //= 2
    return t


def kernel(x_prompt, x_sample, cache_fox_k, cache_fox_v, cache_fox_logf, state_mlstm_C, state_mlstm_n,
           state_mlstm_m, state_mlstm_conv, meta_tokens, w_in, b_in, g_fq, g_fk, w_conv, b_conv, w_mq, w_mk,
           g_mnorm, skip_m, w_fox_proj, w_ml_proj, w_out, g_pre_mix, g_post_mix, g_pre_ffn, g_post_ffn,
           w_gate_up, w_down):
    assert w_in.shape[0] == 1, "single-layer trunk"
    bsz, seq, d = x_prompt.shape
    dbsz, dseq, _ = x_sample.shape
    past = cache_fox_k.shape[2]
    fw, mw = FOX_WIDTH, ML_WIDTH

    w, b = w_in[0], b_in[0]
    o_ff = 3 * fw
    o_mx = o_ff + FOX_HEADS
    o_mi = o_mx + 3 * mw
    o_mf = o_mi + ML_HEADS
    o_ga = o_mf + ML_HEADS

    def relayout(a):
        gates = jnp.concatenate([a[..., o_ff:o_mx], a[..., o_mi:o_mf], a[..., o_mf:o_ga]], axis=-1)
        gates = _pad_rows(gates, LANES, gates.ndim - 1)
        return jnp.concatenate([a[..., :o_ff], a[..., o_mx:o_mi], a[..., o_ga:], gates], axis=-1)

    w_p = relayout(w).astype(BF16)
    b_p = relayout(b)[None, :]
    hid = jnp.arange(fw) // FOX_HEAD_DIM
    hsum = (hid[:, None] == hid[None, :]).astype(BF16)
    gq = g_fq[0].reshape(1, fw)
    gk = g_fk[0].reshape(1, fw)
    gpre = g_pre_mix[0][None, :]
    wc, bc = w_conv[0], b_conv[0][None, :]
    inproj = lambda x, c0, cp, tm, nt, **kw: _inproj(x, gpre, w_p, b_p, gq, gk, hsum, c0, cp, wc, bc, tm, nt, **kw)

    wq, wk = w_mq[0].astype(BF16), w_mk[0].astype(BF16)
    gn, sk = g_mnorm[0].reshape(1, mw), skip_m[0][None, :]
    mlstm = lambda *a, **kw: _mlstm(*a, wq, wk, gn, sk, **kw)

    def conv_hist(rows):
        return jnp.pad(rows, ((0, 0), (8 - (CONV_WIDTH - 1), 0), (0, 0)))

    def m_rows(m):
        return jnp.broadcast_to(_pad_rows(m, 8, 1)[:, :, None], (m.shape[0], 8, LANES))

    mt = 64
    xm = _pad_rows(meta_tokens.astype(F32), mt, 0)
    (_, kf_m, ka_m, vf_m, va_m, xc_m, mv_m, mo_m, _, _, gt_m, fc_m, ct_m) = inproj(
        xm, jnp.zeros((1, 1, LANES), F32), jnp.zeros((1, 8, mw), F32), mt, 1, conv_valid=N_META)
    zc = jnp.zeros((1, ML_HEADS, ML_HEAD_DIM, ML_HEAD_DIM), F32)
    z8 = jnp.zeros((1, 8, LANES), F32)
    _, c_m, n_m, m_m = mlstm(xc_m[None], mv_m[None], mo_m[None], gt_m[None], zc, z8, z8, tc=mt, n_valid=N_META)

    xp = x_prompt.reshape(bsz * seq, d)
    tm = _row_tile(seq, 512)
    lanes_t = lambda rows: _pad_rows(rows[:N_META].T, LANES, 1)
    (qa_p, kt_p, ka_p, vt_p, va_p, xc_p, mv_p, mo_p, ga_p, gb_p, gt_p, _, ct_p, lt_p) = inproj(
        xp, fc_m[None, N_META - 1:N_META], ct_m, tm, seq // tm,
        meta_t=(lanes_t(kf_m), lanes_t(vf_m), lanes_t(gt_m[:, :FOX_HEADS])))
    r3 = lambda a: a.reshape(bsz, seq, a.shape[-1])
    hm_p, c_p, n_p, m_p = mlstm(r3(xc_p), r3(mv_p), r3(mo_p), r3(gt_p), c_m, n_m, m_m,
                                tc=_row_tile(seq, 1024), nb=_row_tile(bsz, 2))
    a_p = _fox(r3(qa_p), _pad_rows(ka_m, LANES, 0)[None], _pad_rows(va_m, LANES, 0)[None], r3(ka_p), r3(va_p),
               tq=_row_tile(seq, 4096), tr=_row_tile(seq, 512), pblk=LANES, pvalid=N_META)

    tok_minor = lambda c: c[0].transpose(0, 2, 3, 1).reshape(dbsz, fw, past)
    kc, vc, f_c = _cacheprep(tok_minor(cache_fox_k), tok_minor(cache_fox_v),
                             _pad_rows(cache_fox_logf[0].astype(F32), LANES, 2), _row_tile(past, 512))
    xs = x_sample.reshape(dbsz * dseq, d)
    (qa_s, kf_s, ka_s, vf_s, va_s, xc_s, mv_s, mo_s, ga_s, gb_s, gt_s, _, ct_s) = inproj(
        xs, f_c, conv_hist(state_mlstm_conv[0]), dseq, 1)
    r3s = lambda a: a.reshape(dbsz, dseq, a.shape[-1])
    hm_s, c_s, n_s, m_s = mlstm(r3s(xc_s), r3s(mv_s), r3s(mo_s), r3s(gt_s), state_mlstm_C[0], _pad_rows(state_mlstm_n[0], 8, 1), m_rows(state_mlstm_m[0]),
                                tc=dseq, nb=_row_tile(dbsz, 4))
    a_s = _fox(r3s(qa_s), kc, vc, r3s(ka_s), r3s(va_s), tq=dseq, tr=dseq, pblk=_row_tile(past, 512), hps=FOX_HEADS)

    wfp, wmp, wo = w_fox_proj[0].astype(BF16), w_ml_proj[0].astype(BF16), w_out[0].astype(BF16)
    wgu, wd = w_gate_up[0].astype(BF16), w_down[0].astype(BF16)
    g1, g2, g3 = g_post_mix[0][None, :], g_pre_ffn[0][None, :], g_post_ffn[0][None, :]
    post = lambda a, hm, ga, gb, x, tm: _post(a, hm, ga, gb, x, wfp, wmp, wo, wgu, wd, g1, g2, g3, tm)
    y_p = post(a_p.reshape(bsz * seq, fw), hm_p.reshape(bsz * seq, mw), ga_p, gb_p, xp, _row_tile(bsz * seq, 512))
    y_s = post(a_s.reshape(dbsz * dseq, fw), hm_s.reshape(dbsz * dseq, mw), ga_s, gb_s, xs,
               _row_tile(dbsz * dseq, 512))

    def tokens_major(a):
        return a.reshape(bsz, FOX_HEADS, FOX_HEAD_DIM, a.shape[-1]).transpose(0, 3, 1, 2)[None]

    heads = lambda a: a.reshape(a.shape[:-1] + (FOX_HEADS, FOX_HEAD_DIM))
    tail = CONV_WIDTH - 1
    return (y_p.reshape(bsz, seq, d), y_s.reshape(dbsz, dseq, d),
            tokens_major(kt_p), tokens_major(vt_p), lt_p.transpose(0, 2, 1)[None],
            c_p[None], n_p[None, :, :ML_HEADS], m_p[None, :, :ML_HEADS, 0], ct_p[None, :, 8 - tail:],
            heads(r3s(kf_s))[None], heads(r3s(vf_s))[None], r3s(gt_s)[None, ..., :FOX_HEADS],
            c_s[None], n_s[None, :, :ML_HEADS], m_s[None, :, :ML_HEADS, 0], ct_s[None, :, 8 - tail:])
```

```python
import functools

import jax
import jax.numpy as jnp
from jax import lax
from jax.experimental import pallas as pl
from jax.experimental.pallas import tpu as pltpu

F32 = jnp.float32
BF16 = jnp.bfloat16

N_META = 16
FOX_HEADS = 8
FOX_HEAD_DIM = 64
FOX_WIDTH = FOX_HEADS * FOX_HEAD_DIM
ML_HEADS = 4
ML_HEAD_DIM = 128
ML_WIDTH = ML_HEADS * ML_HEAD_DIM
CONV_WIDTH = 4
RMS_EPS = 1e-6
PAD_LOG_GATE = -1e30
NEG_BIG = -1e30

LANES = 128
GATE_ML_I = FOX_HEADS
GATE_ML_F = FOX_HEADS + ML_HEADS

AUG_STEP = FOX_HEADS
AUG_QF = FOX_HEAD_DIM
AUG_KF = FOX_HEAD_DIM + 24

VMEM_LIMIT = 56 * 1024 * 1024
FOX_WIDE_KEYS = 1024
FOX_MAX_STATIC_BLOCKS = 4
MXU_TILE = 256
FF_CHUNKS = 2


def _const_spec(shape):
    nd = len(shape)
    return pl.BlockSpec(shape, lambda *_: (0,) * nd, pipeline_mode=pl.Buffered(1))


def _rms(x, g):
    return x * lax.rsqrt(jnp.mean(x * x, axis=-1, keepdims=True) + RMS_EPS) * g


def _split3(x):
    hi = x.astype(BF16).astype(F32)
    r = x - hi
    mid = r.astype(BF16).astype(F32)
    lo = (r - mid).astype(BF16).astype(F32)
    return hi, mid, lo


def _cumsum_rows(lg, cur):
    tm = lg.shape[0]
    blk = min(tm, LANES)
    r = lax.broadcasted_iota(jnp.int32, (blk, blk), 0)
    c = lax.broadcasted_iota(jnp.int32, (blk, blk), 1)
    tril = (c <= r).astype(BF16)
    outs = []
    for s in range(tm // blk):
        hi, mid, lo = _split3(lg[s * blk:(s + 1) * blk])
        f = (jnp.dot(tril, lo.astype(BF16), preferred_element_type=F32)
             + jnp.dot(tril, mid.astype(BF16), preferred_element_type=F32)
             + jnp.dot(tril, hi.astype(BF16), preferred_element_type=F32)) + cur
        outs.append(f)
        cur = f[blk - 1:blk]
    return (outs[0] if len(outs) == 1 else jnp.concatenate(outs, axis=0)), cur


def _store_attention_operands(qn, kn, v, fcol, q_ref, k_ref, v_ref, fillers=()):
    tm = fcol.shape[0]
    lane = lax.broadcasted_iota(jnp.int32, (tm, LANES), 1)
    data = lane < FOX_HEAD_DIM
    hi, mid, lo = _split3(fcol)
    comb = jnp.where(lane < AUG_STEP, hi, jnp.where(lane < 2 * AUG_STEP, pltpu.roll(mid, AUG_STEP, 1),
                                                     pltpu.roll(lo, 2 * AUG_STEP, 1)))
    ncomb = -comb
    is_qf = (lane == AUG_QF) | (lane == AUG_QF + AUG_STEP) | (lane == AUG_QF + 2 * AUG_STEP)
    is_kf = (lane == AUG_KF) | (lane == AUG_KF + AUG_STEP) | (lane == AUG_KF + 2 * AUG_STEP)
    base_q = jnp.where(is_kf, 1.0, 0.0)
    base_k = jnp.where(is_qf, 1.0, 0.0)
    base_v = jnp.where(lane == FOX_HEAD_DIM, 1.0, 0.0)
    fillers = list(fillers)
    for h in range(FOX_HEADS):
        t, odd = divmod(h, 2)

        def head_tile(z):
            x = z[:, t * LANES:(t + 1) * LANES]
            return pltpu.roll(x, FOX_HEAD_DIM, 1) if odd else x

        hs = slice(h * LANES, (h + 1) * LANES)
        if q_ref is not None:
            aug = jnp.where(is_qf, pltpu.roll(comb, AUG_QF - h, 1), base_q)
            q_ref[:, hs] = jnp.where(data, head_tile(qn), aug).astype(BF16)
        aug = jnp.where(is_kf, pltpu.roll(ncomb, AUG_KF - h, 1), base_k)
        k_ref[:, hs] = jnp.where(data, head_tile(kn), aug).astype(BF16)
        v_ref[:, hs] = jnp.where(data, head_tile(v), base_v).astype(BF16)
        if fillers:
            fillers.pop(0)()
    for f in fillers:
        f()


def _inproj_kernel(*refs, d_model, nt, kv_t, conv_valid):
    (x_ref, gpre_ref, w_ref, b_ref, gq_ref, gk_ref, hs_ref, c0_ref, cp_ref, wc_ref, bc_ref), rest = refs[:11], refs[11:]
    if kv_t:
        kmeta_ref, vmeta_ref, lmeta_ref = rest[:3]
        (q_ref, kf_ref, kb_ref, vf_ref, vb_ref, xc_ref, mv_ref, mo_ref, ga_ref, gb_ref, gt_ref, fc_ref, ct_ref,
         lt_ref, carry, xbuf, kcar, vcar, lcar) = rest[3:]
    else:
        (q_ref, kf_ref, kb_ref, vf_ref, vb_ref, xc_ref, mv_ref, mo_ref, ga_ref, gb_ref, gt_ref, fc_ref, ct_ref,
         carry, xbuf) = rest
    fw, mw = FOX_WIDTH, ML_WIDTH
    offs = [0, fw, 2 * fw, 3 * fw, 3 * fw + mw, 3 * fw + 2 * mw, 3 * fw + 3 * mw,
            3 * fw + 3 * mw + d_model, 3 * fw + 3 * mw + 2 * d_model, 3 * fw + 3 * mw + 2 * d_model + LANES]
    r = pl.program_id(1)

    @pl.when(r == 0)
    def _():
        carry[...] = c0_ref[...]
        xbuf[0:8, :] = cp_ref[...]
        if kv_t:
            kcar[...] = kmeta_ref[...]
            vcar[...] = vmeta_ref[...]
            lcar[...] = lmeta_ref[...]

    def store_shifted(xt, car, out_ref):
        sh = pltpu.roll(xt, N_META, 1)
        lane = lax.broadcasted_iota(jnp.int32, (xt.shape[0], LANES), 1)
        out_ref[:, :LANES] = jnp.where(lane < N_META, car[...], sh[:, :LANES])
        out_ref[:, LANES:] = sh[:, LANES:]
        car[...] = sh[:, :LANES]

    @pl.when(r < nt)
    def _():
        h = _rms(x_ref[...], gpre_ref[...]).astype(BF16)

        def seg(i):
            a, b = offs[i], offs[i + 1]
            return jnp.dot(h, w_ref[:, a:b], preferred_element_type=F32) + b_ref[:, a:b]

        def headnorm(z, g):
            ms = jnp.dot((z * z).astype(BF16), hs_ref[...], preferred_element_type=F32) * (1.0 / FOX_HEAD_DIM)
            return z * lax.rsqrt(ms + RMS_EPS) * g

        g = seg(8)
        ga_ref[...] = seg(6).astype(BF16)
        lane = lax.broadcasted_iota(jnp.int32, g.shape, 1)
        raw = (lane >= GATE_ML_I) & (lane < GATE_ML_F)
        gt = jnp.where(raw, g, jax.nn.log_sigmoid(g))
        gt_ref[...] = gt
        fcol, cur = _cumsum_rows(gt, carry[...])
        carry[...] = cur
        fc_ref[...] = fcol

        v = seg(2)
        q = headnorm(seg(0), gq_ref[...]) * (FOX_HEAD_DIM ** -0.5)
        k = headnorm(seg(1), gk_ref[...])
        def store_kv():
            if kv_t:
                store_shifted(k.T, kcar, kf_ref)
                store_shifted(v.T, vcar, vf_ref)
                store_shifted(gt.T[:FOX_HEADS], lcar, lt_ref)
            else:
                kf_ref[...] = k
                vf_ref[...] = v

        def store_xc():
            tm = x_ref.shape[0]
            xbuf[8:8 + tm, :] = seg(3)
            xc = bc_ref[...]
            for i in range(CONV_WIDTH):
                xc = xc + xbuf[5 + i:5 + i + tm, :] * wc_ref[i:i + 1, :]
            xc_ref[...] = (xc * jax.nn.sigmoid(xc)).astype(BF16)
            ct_ref[...] = xbuf[conv_valid:conv_valid + 8, :]
            xbuf[0:8, :] = xbuf[tm:tm + 8, :]

        def store_mv():
            mv_ref[...] = seg(4).astype(BF16)

        def store_mo():
            mo_ref[...] = jax.nn.sigmoid(seg(5)).astype(BF16)

        def store_gb():
            gb_ref[...] = seg(7).astype(BF16)

        _store_attention_operands(q, k, v, fcol, q_ref, kb_ref, vb_ref,
                                  fillers=(store_xc, store_mv, store_mo, store_gb, store_kv))

    if kv_t:
        @pl.when(r == nt)
        def _():
            kf_ref[:, :LANES] = kcar[...]
            vf_ref[:, :LANES] = vcar[...]
            lt_ref[:, :LANES] = lcar[...]


def _inproj(x, gpre, w, b, gq, gk, hs, c0, cp, wc, bc, tm, nt, meta_t=None, conv_valid=None):
    n, d = x.shape
    nb = n // (nt * tm)
    dinp = w.shape[1]
    fw, mw = FOX_WIDTH, ML_WIDTH
    aw = FOX_HEADS * LANES
    kv_t = meta_t is not None
    tile = lambda b, r: b * nt + jnp.minimum(r, nt - 1)
    row = lambda c: pl.BlockSpec((tm, c), lambda b, r: (tile(b, r), 0))
    cb = (lambda b: b) if c0.shape[0] > 1 else (lambda b: 0)
    pb = (lambda b: b) if cp.shape[0] > 1 else (lambda b: 0)
    lp = N_META + nt * tm
    tmin = lambda c: pl.BlockSpec((None, c, tm), lambda b, r: (b, 0, r))
    flat = lambda c, t: (row(c), jax.ShapeDtypeStruct((n, c), t))
    tok_minor = lambda c: (tmin(c), jax.ShapeDtypeStruct((nb, c, lp), F32))
    outs = [flat(aw, BF16), tok_minor(fw) if kv_t else flat(fw, F32), flat(aw, BF16),
            tok_minor(fw) if kv_t else flat(fw, F32), flat(aw, BF16), flat(mw, BF16), flat(mw, BF16), flat(mw, BF16),
            flat(d, BF16), flat(d, BF16), flat(LANES, F32), flat(LANES, F32),
            (pl.BlockSpec((None, 8, mw), lambda b, r: (b, 0, 0)), jax.ShapeDtypeStruct((nb, 8, mw), F32))]
    in_specs = [row(d), _const_spec((1, d)), _const_spec((d, dinp)), _const_spec((1, dinp)),
                _const_spec((1, fw)), _const_spec((1, fw)), _const_spec((fw, fw)),
                pl.BlockSpec((None, 1, LANES), lambda b, r: (cb(b), 0, 0)),
                pl.BlockSpec((None, 8, mw), lambda b, r: (pb(b), 0, 0)),
                _const_spec((CONV_WIDTH, mw)), _const_spec((1, mw))]
    scratch = [pltpu.VMEM((1, LANES), F32), pltpu.VMEM((8 + tm, mw), F32)]
    args = [x, gpre, w, b, gq, gk, hs, c0, cp, wc, bc]
    if kv_t:
        outs.append(tok_minor(FOX_HEADS))
        in_specs += [_const_spec((fw, LANES)), _const_spec((fw, LANES)), _const_spec((FOX_HEADS, LANES))]
        scratch += [pltpu.VMEM((fw, LANES), F32), pltpu.VMEM((fw, LANES), F32), pltpu.VMEM((FOX_HEADS, LANES), F32)]
        args += list(meta_t)
    return pl.pallas_call(
        functools.partial(_inproj_kernel, d_model=d, nt=nt, kv_t=kv_t,
                          conv_valid=tm if conv_valid is None else conv_valid),
        grid=(nb, nt + (1 if kv_t else 0)),
        in_specs=in_specs,
        out_specs=[o[0] for o in outs],
        out_shape=[o[1] for o in outs],
        scratch_shapes=scratch,
        compiler_params=pltpu.CompilerParams(dimension_semantics=("arbitrary", "arbitrary"),
                                             vmem_limit_bytes=VMEM_LIMIT),
        name="inproj",
    )(*args)


def _cacheprep_kernel(k_ref, v_ref, lf_ref, ka_ref, va_ref, fl_ref, carry):
    @pl.when(pl.program_id(1) == 0)
    def _():
        carry[...] = jnp.zeros_like(carry)

    fcol, cur = _cumsum_rows(lf_ref[...], carry[...])
    carry[...] = cur
    fl_ref[...] = cur
    _store_attention_operands(None, k_ref[...].T, v_ref[...].T, fcol, None, ka_ref, va_ref)


def _cacheprep(ck, cv, clf, tp):
    bsz, _, plen = ck.shape
    aw = FOX_HEADS * LANES
    return pl.pallas_call(
        _cacheprep_kernel,
        grid=(bsz, plen // tp),
        in_specs=[pl.BlockSpec((None, FOX_WIDTH, tp), lambda b, i: (b, 0, i)),
                  pl.BlockSpec((None, FOX_WIDTH, tp), lambda b, i: (b, 0, i)),
                  pl.BlockSpec((None, tp, LANES), lambda b, i: (b, i, 0))],
        out_specs=[pl.BlockSpec((None, tp, aw), lambda b, i: (b, i, 0)),
                   pl.BlockSpec((None, tp, aw), lambda b, i: (b, i, 0)),
                   pl.BlockSpec((None, 1, LANES), lambda b, i: (b, 0, 0))],
        out_shape=[jax.ShapeDtypeStruct((bsz, plen, aw), BF16), jax.ShapeDtypeStruct((bsz, plen, aw), BF16),
                   jax.ShapeDtypeStruct((bsz, 1, LANES), F32)],
        scratch_shapes=[pltpu.VMEM((1, LANES), F32)],
        compiler_params=pltpu.CompilerParams(dimension_semantics=("arbitrary", "arbitrary")),
        name="cache_prep",
    )(ck, cv, clf)


def _fox_kernel(q_ref, kp_ref, vp_ref, k_ref, v_ref, o_ref, *, tq, tr, tkb, nq, hps, pblk, n_pblk, pvalid):
    i = pl.program_id(2)
    nr = tq // tr
    chains = [(hh, r) for hh in range(hps) for r in range(nr)]
    qs = [q_ref[r * tr:(r + 1) * tr, hh * LANES:(hh + 1) * LANES] for hh, r in chains]

    def upd1(n, state, kb, vb, mask=None):
        m, acc = state
        hh = chains[n][0]
        hs = slice(hh * LANES, (hh + 1) * LANES)
        s = lax.dot_general(qs[n], kb[:, hs], (((1,), (1,)), ((), ())), preferred_element_type=F32)
        if mask is not None:
            cut = s.shape[1] - mask.shape[1]
            tail = jnp.where(mask, s[:, cut:], NEG_BIG)
            s = tail if cut == 0 else jnp.concatenate([s[:, :cut], tail], axis=1)
        m_new = jnp.maximum(m, jnp.max(s, axis=-1, keepdims=True))
        alpha = jnp.exp(m - m_new)
        pm = jnp.exp(s - m_new)
        acc = alpha * acc + jnp.dot(pm.astype(BF16), vb[:, hs], preferred_element_type=F32)
        return m_new, acc

    def upd(carry, kb, vb):
        return tuple(upd1(n, carry[n], kb, vb) for n in range(len(chains)))

    def start():
        carry = tuple((jnp.full((tr, 1), NEG_BIG, F32), jnp.zeros((tr, LANES), F32)) for _ in chains)
        if pvalid is None:
            for jb in range(n_pblk):
                sl = slice(jb * pblk, (jb + 1) * pblk)
                carry = upd(carry, kp_ref[sl, :], vp_ref[sl, :])
        return carry

    def finish(carry, off):
        tail_w = tr + (LANES if pvalid is not None else 0)
        rj = lax.broadcasted_iota(jnp.int32, (tr, tail_w), 0)
        cj = lax.broadcasted_iota(jnp.int32, (tr, tail_w), 1)
        mask = cj <= rj
        if pvalid is not None:
            mask = (mask | (cj >= tr)) & (cj < tr + pvalid)
        carry = list(carry)
        for r in reversed(range(nr)):
            width = (r + 1) * tr
            kb = k_ref[pl.ds(off, width), :]
            vb = v_ref[pl.ds(off, width), :]
            if pvalid is not None:
                kb = jnp.concatenate([kb, kp_ref[...]], axis=0)
                vb = jnp.concatenate([vb, vp_ref[...]], axis=0)
            for n, (hh, rr) in enumerate(chains):
                if rr == r:
                    carry[n] = upd1(n, carry[n], kb, vb, mask=mask)
        lane = lax.broadcasted_iota(jnp.int32, (tr, LANES), 1)
        outs = []
        for n in range(len(chains)):
            acc = carry[n][1]
            l = jnp.sum(jnp.where(lane == FOX_HEAD_DIM, acc, 0.0), axis=-1, keepdims=True)
            outs.append(acc / l)
        for hp in range(hps // 2):
            for r in range(nr):
                even, odd = outs[2 * hp * nr + r], outs[(2 * hp + 1) * nr + r]
                o_ref[r * tr:(r + 1) * tr, hp * LANES:(hp + 1) * LANES] = jnp.where(
                    lane < FOX_HEAD_DIM, even, pltpu.roll(odd, FOX_HEAD_DIM, 1)).astype(o_ref.dtype)

    if tq % tkb == 0 and nq <= FOX_MAX_STATIC_BLOCKS:
        for qi in range(nq):
            @pl.when(i == qi)
            def _():
                carry = start()
                for j in range(qi * tq // tkb):
                    carry = upd(carry, k_ref[j * tkb:(j + 1) * tkb, :], v_ref[j * tkb:(j + 1) * tkb, :])
                finish(carry, qi * tq)
    else:
        tkw = max(tkb, tq)

        def body_wide(j, carry):
            off = pl.multiple_of(j * tkw, tkw)
            return upd(carry, k_ref[pl.ds(off, tkw), :], v_ref[pl.ds(off, tkw), :])

        def body(j, carry):
            off = pl.multiple_of(j * tq, tq)
            return upd(carry, k_ref[pl.ds(off, tq), :], v_ref[pl.ds(off, tq), :])

        n_wide = i // (tkw // tq)
        carry = lax.fori_loop(0, n_wide, body_wide, start())
        if tkw > tq:
            carry = lax.fori_loop(n_wide * (tkw // tq), i, body, carry)
        finish(carry, pl.multiple_of(i * tq, tq))


def _fox(q, kp, vp, k, v, tq, tr, pblk, pvalid=None, hps=2):
    bsz, t, _ = q.shape
    bp, plen, _ = kp.shape
    nq = t // tq
    n_pblk = plen // pblk
    pw = hps * LANES
    pb = (lambda b: b) if bp == bsz else (lambda b: 0)
    return pl.pallas_call(
        functools.partial(_fox_kernel, tq=tq, tr=tr, tkb=_row_tile(t, FOX_WIDE_KEYS), nq=nq, hps=hps, pblk=pblk,
                          n_pblk=n_pblk, pvalid=pvalid),
        grid=(bsz, FOX_HEADS // hps, nq),
        in_specs=[pl.BlockSpec((None, tq, pw), lambda b, p, i: (b, i, p)),
                  pl.BlockSpec((None, plen, pw), lambda b, p, i: (pb(b), 0, p)),
                  pl.BlockSpec((None, plen, pw), lambda b, p, i: (pb(b), 0, p)),
                  pl.BlockSpec((None, t, pw), lambda b, p, i: (b, 0, p)),
                  pl.BlockSpec((None, t, pw), lambda b, p, i: (b, 0, p))],
        out_specs=pl.BlockSpec((None, tq, hps * FOX_HEAD_DIM), lambda b, p, i: (b, i, p)),
        out_shape=jax.ShapeDtypeStruct((bsz, t, FOX_WIDTH), BF16),
        compiler_params=pltpu.CompilerParams(dimension_semantics=("arbitrary", "arbitrary", "arbitrary"),
                                             vmem_limit_bytes=VMEM_LIMIT),
        name="fox_attention",
    )(q, kp, vp, k, v)


def _mlstm_kernel(xc_ref, mv_ref, mo_ref, gt_ref, c0_ref, n0_ref, m0_ref, wq_ref, wk_ref, gn_ref, sk_ref,
                  hm_ref, cout_ref, nout_ref, mout_ref,
                  c_scr, n_scr, m_scr, *, tc, n_valid, nb, shared_state):
    c = pl.program_id(1)

    @pl.when(c == 0)
    def _():
        for bi in range(nb):
            si = 0 if shared_state else bi
            c_scr[bi] = c0_ref[si]
            n_scr[bi] = n0_ref[si]
            m_scr[bi] = m0_ref[si]

    for bi in range(nb):
        _mlstm_chunk(xc_ref.at[bi], mv_ref.at[bi], mo_ref.at[bi], gt_ref.at[bi], wq_ref, wk_ref,
                     gn_ref, sk_ref, hm_ref.at[bi], c_scr.at[bi], n_scr.at[bi], m_scr.at[bi],
                     tc=tc, n_valid=n_valid)

    @pl.when(c == pl.num_programs(1) - 1)
    def _():
        cout_ref[...] = c_scr[...]
        nout_ref[...] = n_scr[...]
        mout_ref[...] = m_scr[...]


def _mlstm_chunk(xc_ref, mv_ref, mo_ref, gt_ref, wq_ref, wk_ref, gn_ref, sk_ref, hm_ref,
                 c_scr, n_scr, m_scr, *, tc, n_valid):
    gt = gt_ref[...]
    lane = lax.broadcasted_iota(jnp.int32, (tc, LANES), 1)
    if n_valid is not None:
        rowi = lax.broadcasted_iota(jnp.int32, (tc, LANES), 0)
        is_i = (lane >= GATE_ML_I) & (lane < GATE_ML_F)
        gt = jnp.where(rowi < n_valid, gt, jnp.where(is_i, PAD_LOG_GATE, 0.0))
    ri = lax.broadcasted_iota(jnp.int32, (tc, tc), 0)
    ci = lax.broadcasted_iota(jnp.int32, (tc, tc), 1)
    causal = ci <= ri
    b_all = _cumsum_rows(gt, jnp.zeros((1, LANES), F32))[0]
    r_t = (pltpu.roll(gt, GATE_ML_F - GATE_ML_I, 1) - b_all).T
    heads = range(ML_HEADS)
    hsl = [slice(h * ML_HEAD_DIM, (h + 1) * ML_HEAD_DIM) for h in heads]
    b_col = [b_all[:, GATE_ML_F + h:GATE_ML_F + h + 1] for h in heads]
    ig_col = [gt[:, GATE_ML_I + h:GATE_ML_I + h + 1] for h in heads]
    m_prev = [m_scr[h:h + 1, 0:1] for h in heads]
    dmat = [jnp.where(causal, b_col[h] + r_t[GATE_ML_F + h:GATE_ML_F + h + 1, :], NEG_BIG) for h in heads]
    m_inter = [b_col[h] + m_prev[h] for h in heads]
    m_t = [jnp.maximum(m_inter[h], jnp.max(dmat[h], axis=-1, keepdims=True)) for h in heads]
    w_inter = [jnp.exp(m_inter[h] - m_t[h]) for h in heads]

    xhb = [xc_ref[:, hsl[h]] for h in heads]
    q = [jnp.dot(xhb[h], wq_ref[h], preferred_element_type=F32) for h in heads]
    k = [jnp.dot(xhb[h], wk_ref[h], preferred_element_type=F32) * (ML_HEAD_DIM ** -0.5) for h in heads]
    qb = [x.astype(BF16) for x in q]
    kb = [x.astype(BF16) for x in k]
    v = [mv_ref[:, hsl[h]] for h in heads]
    smat = [lax.dot_general(qb[h], kb[h], (((1,), (1,)), ((), ())), preferred_element_type=F32) for h in heads]
    amat = [jnp.exp(dmat[h] - m_t[h]) * smat[h] for h in heads]
    ch = [c_scr[h] for h in heads]
    nh = [n_scr[h:h + 1, :] for h in heads]
    num = [jnp.dot(amat[h].astype(BF16), v[h], preferred_element_type=F32)
           + w_inter[h] * jnp.dot(qb[h], ch[h].astype(BF16), preferred_element_type=F32) for h in heads]
    den = [jnp.sum(amat[h], axis=-1, keepdims=True) + w_inter[h] * jnp.sum(q[h] * nh[h], axis=-1, keepdims=True)
           for h in heads]
    hcell = [num[h] / jnp.maximum(jnp.abs(den[h]), jnp.exp(-m_t[h])) for h in heads]

    b_last = [b_col[h][tc - 1:tc, :] for h in heads]
    g = [b_last[h] - b_col[h] + ig_col[h] for h in heads]
    m_new = [jnp.maximum(b_last[h] + m_prev[h], jnp.max(g[h], axis=0, keepdims=True)) for h in heads]
    decay = [jnp.exp(b_last[h] + m_prev[h] - m_new[h]) for h in heads]
    kw = [k[h] * jnp.exp(g[h] - m_new[h]) for h in heads]
    for h in heads:
        c_scr[h] = decay[h] * ch[h] + lax.dot_general(kw[h].astype(BF16), v[h], (((0,), (0,)), ((), ())),
                                                       preferred_element_type=F32)
        n_scr[h:h + 1, :] = decay[h] * nh[h] + jnp.sum(kw[h], axis=0, keepdims=True)
        m_scr[h:h + 1, :] = jnp.broadcast_to(m_new[h], (1, LANES))
        y = mo_ref[:, hsl[h]].astype(F32) * hcell[h]
        hm_ref[:, hsl[h]] = (_rms(y, gn_ref[:, hsl[h]]) + sk_ref[:, hsl[h]] * xhb[h].astype(F32)).astype(hm_ref.dtype)


def _mlstm(xc, mv, mo, gt, c0, n0, m0, wq, wk, gn, sk, tc, nb=1, n_valid=None):
    bsz, t, _ = xc.shape
    shared = c0.shape[0] != bsz
    sn = 1 if shared else nb
    sb = (lambda b: 0) if shared else (lambda b: b)
    tok = lambda w: pl.BlockSpec((nb, tc, w), lambda b, c: (b, c, 0))
    hd = ML_HEAD_DIM
    st_specs = [pl.BlockSpec((nb, ML_HEADS, hd, hd), lambda b, c: (b, 0, 0, 0)),
                pl.BlockSpec((nb, 8, LANES), lambda b, c: (b, 0, 0)),
                pl.BlockSpec((nb, 8, LANES), lambda b, c: (b, 0, 0))]
    return pl.pallas_call(
        functools.partial(_mlstm_kernel, tc=tc, n_valid=n_valid, nb=nb, shared_state=shared),
        grid=(bsz // nb, t // tc),
        in_specs=[tok(ML_WIDTH), tok(ML_WIDTH), tok(ML_WIDTH), tok(LANES),
                  pl.BlockSpec((sn, ML_HEADS, hd, hd), lambda b, c: (sb(b), 0, 0, 0)),
                  pl.BlockSpec((sn, 8, LANES), lambda b, c: (sb(b), 0, 0)),
                  pl.BlockSpec((sn, 8, LANES), lambda b, c: (sb(b), 0, 0)),
                  _const_spec((ML_HEADS, hd, hd)), _const_spec((ML_HEADS, hd, hd)),
                  _const_spec((1, ML_WIDTH)), _const_spec((1, ML_WIDTH))],
        out_specs=[tok(ML_WIDTH)] + st_specs,
        out_shape=[jax.ShapeDtypeStruct((bsz, t, ML_WIDTH), BF16),
                   jax.ShapeDtypeStruct((bsz, ML_HEADS, hd, hd), F32),
                   jax.ShapeDtypeStruct((bsz, 8, LANES), F32),
                   jax.ShapeDtypeStruct((bsz, 8, LANES), F32)],
        scratch_shapes=[pltpu.VMEM((nb, ML_HEADS, hd, hd), F32),
                        pltpu.VMEM((nb, 8, LANES), F32), pltpu.VMEM((nb, 8, LANES), F32)],
        compiler_params=pltpu.CompilerParams(dimension_semantics=("arbitrary", "arbitrary")),
        name="mlstm",
    )(xc, mv, mo, gt, c0, n0, m0, wq, wk, gn, sk)


def _post_kernel(a_ref, hm_ref, ga_ref, gb_ref, x_ref, wfp_ref, wmp_ref, wo_ref, wgu_ref, wd_ref,
                 g1_ref, g2_ref, g3_ref, y_ref, *, d_ff, ff_cuts):
    ya = jnp.dot(a_ref[...], wfp_ref[...], preferred_element_type=F32)
    yb = jnp.dot(hm_ref[...], wmp_ref[...], preferred_element_type=F32)
    merged = jax.nn.sigmoid(ga_ref[...].astype(F32)) * ya + jax.nn.sigmoid(gb_ref[...].astype(F32)) * yb
    mix = jnp.dot(merged.astype(BF16), wo_ref[...], preferred_element_type=F32)
    x1 = x_ref[...] + _rms(mix, g1_ref[...])
    h2 = _rms(x1, g2_ref[...]).astype(BF16)
    f = None
    for lo, hi in zip(ff_cuts[:-1], ff_cuts[1:]):
        g = jnp.dot(h2, wgu_ref[:, lo:hi], preferred_element_type=F32)
        u = jnp.dot(h2, wgu_ref[:, d_ff + lo:d_ff + hi], preferred_element_type=F32)
        t = (g * jax.nn.sigmoid(g) * u).astype(BF16)
        fj = jnp.dot(t, wd_ref[lo:hi, :], preferred_element_type=F32)
        f = fj if f is None else f + fj
    y_ref[...] = x1 + _rms(f, g3_ref[...])


def _post(a, hm, ga, gb, x, wfp, wmp, wo, wgu, wd, g1, g2, g3, tm):
    n, d = x.shape
    d_ff = wd.shape[0]
    nmt = d_ff // MXU_TILE
    ff_cuts = tuple(MXU_TILE * ((nmt * j) // FF_CHUNKS) for j in range(FF_CHUNKS)) + (d_ff,) \
        if d_ff % MXU_TILE == 0 else (0, d_ff)
    row = lambda c: pl.BlockSpec((tm, c), lambda i: (i, 0))
    return pl.pallas_call(
        functools.partial(_post_kernel, d_ff=d_ff, ff_cuts=ff_cuts),
        grid=(n // tm,),
        in_specs=[row(FOX_WIDTH), row(ML_WIDTH), row(d), row(d), row(d),
                  _const_spec(wfp.shape), _const_spec(wmp.shape), _const_spec(wo.shape),
                  _const_spec(wgu.shape), _const_spec(wd.shape),
                  _const_spec((1, d)), _const_spec((1, d)), _const_spec((1, d))],
        out_specs=row(d),
        out_shape=jax.ShapeDtypeStruct((n, d), F32),
        compiler_params=pltpu.CompilerParams(dimension_semantics=("arbitrary",), vmem_limit_bytes=VMEM_LIMIT),
        name="post_mix_ffn",
    )(a, hm, ga, gb, x, wfp, wmp, wo, wgu, wd, g1, g2, g3)


def _pad_rows(a, rows, axis):
    pad = [(0, 0)] * a.ndim
    pad[axis] = (0, rows - a.shape[axis])
    return jnp.pad(a, pad)


def _row_tile(n, pref):
    t = min(pref, n)
    while n % t:
        t //= 2
    return t


def kernel(x_prompt, x_sample, cache_fox_k, cache_fox_v, cache_fox_logf, state_mlstm_C, state_mlstm_n,
           state_mlstm_m, state_mlstm_conv, meta_tokens, w_in, b_in, g_fq, g_fk, w_conv, b_conv, w_mq, w_mk,
           g_mnorm, skip_m, w_fox_proj, w_ml_proj, w_out, g_pre_mix, g_post_mix, g_pre_ffn, g_post_ffn,
           w_gate_up, w_down):
    assert w_in.shape[0] == 1, "single-layer trunk"
    bsz, seq, d = x_prompt.shape
    dbsz, dseq, _ = x_sample.shape
    past = cache_fox_k.shape[2]
    fw, mw = FOX_WIDTH, ML_WIDTH

    w, b = w_in[0], b_in[0]
    o_ff = 3 * fw
    o_mx = o_ff + FOX_HEADS
    o_mi = o_mx + 3 * mw
    o_mf = o_mi + ML_HEADS
    o_ga = o_mf + ML_HEADS

    def relayout(a):
        gates = jnp.concatenate([a[..., o_ff:o_mx], a[..., o_mi:o_mf], a[..., o_mf:o_ga]], axis=-1)
        gates = _pad_rows(gates, LANES, gates.ndim - 1)
        return jnp.concatenate([a[..., :o_ff], a[..., o_mx:o_mi], a[..., o_ga:], gates], axis=-1)

    w_p = relayout(w).astype(BF16)
    b_p = relayout(b)[None, :]
    hid = jnp.arange(fw) // FOX_HEAD_DIM
    hsum = (hid[:, None] == hid[None, :]).astype(BF16)
    gq = g_fq[0].reshape(1, fw)
    gk = g_fk[0].reshape(1, fw)
    gpre = g_pre_mix[0][None, :]
    wc, bc = w_conv[0], b_conv[0][None, :]
    inproj = lambda x, c0, cp, tm, nt, **kw: _inproj(x, gpre, w_p, b_p, gq, gk, hsum, c0, cp, wc, bc, tm, nt, **kw)

    wq, wk = w_mq[0].astype(BF16), w_mk[0].astype(BF16)
    gn, sk = g_mnorm[0].reshape(1, mw), skip_m[0][None, :]
    mlstm = lambda *a, **kw: _mlstm(*a, wq, wk, gn, sk, **kw)

    def conv_hist(rows):
        return jnp.pad(rows, ((0, 0), (8 - (CONV_WIDTH - 1), 0), (0, 0)))

    def m_rows(m):
        return jnp.broadcast_to(_pad_rows(m, 8, 1)[:, :, None], (m.shape[0], 8, LANES))

    mt = 64
    xm = _pad_rows(meta_tokens.astype(F32), mt, 0)
    (_, kf_m, ka_m, vf_m, va_m, xc_m, mv_m, mo_m, _, _, gt_m, fc_m, ct_m) = inproj(
        xm, jnp.zeros((1, 1, LANES), F32), jnp.zeros((1, 8, mw), F32), mt, 1, conv_valid=N_META)
    zc = jnp.zeros((1, ML_HEADS, ML_HEAD_DIM, ML_HEAD_DIM), F32)
    z8 = jnp.zeros((1, 8, LANES), F32)
    _, c_m, n_m, m_m = mlstm(xc_m[None], mv_m[None], mo_m[None], gt_m[None], zc, z8, z8, tc=mt, n_valid=N_META)

    xp = x_prompt.reshape(bsz * seq, d)
    tm = _row_tile(seq, 512)
    lanes_t = lambda rows: _pad_rows(rows[:N_META].T, LANES, 1)
    (qa_p, kt_p, ka_p, vt_p, va_p, xc_p, mv_p, mo_p, ga_p, gb_p, gt_p, _, ct_p, lt_p) = inproj(
        xp, fc_m[None, N_META - 1:N_META], ct_m, tm, seq // tm,
        meta_t=(lanes_t(kf_m), lanes_t(vf_m), lanes_t(gt_m[:, :FOX_HEADS])))
    r3 = lambda a: a.reshape(bsz, seq, a.shape[-1])
    hm_p, c_p, n_p, m_p = mlstm(r3(xc_p), r3(mv_p), r3(mo_p), r3(gt_p), c_m, n_m, m_m,
                                tc=_row_tile(seq, 512), nb=_row_tile(bsz, 4))
    a_p = _fox(r3(qa_p), _pad_rows(ka_m, LANES, 0)[None], _pad_rows(va_m, LANES, 0)[None], r3(ka_p), r3(va_p),
               tq=_row_tile(seq, 4096), tr=_row_tile(seq, 512), pblk=LANES, pvalid=N_META)

    tok_minor = lambda c: c[0].transpose(0, 2, 3, 1).reshape(dbsz, fw, past)
    kc, vc, f_c = _cacheprep(tok_minor(cache_fox_k), tok_minor(cache_fox_v),
                             _pad_rows(cache_fox_logf[0].astype(F32), LANES, 2), _row_tile(past, 512))
    xs = x_sample.reshape(dbsz * dseq, d)
    (qa_s, kf_s, ka_s, vf_s, va_s, xc_s, mv_s, mo_s, ga_s, gb_s, gt_s, _, ct_s) = inproj(
        xs, f_c, conv_hist(state_mlstm_conv[0]), dseq, 1)
    r3s = lambda a: a.reshape(dbsz, dseq, a.shape[-1])
    hm_s, c_s, n_s, m_s = mlstm(r3s(xc_s), r3s(mv_s), r3s(mo_s), r3s(gt_s), state_mlstm_C[0], _pad_rows(state_mlstm_n[0], 8, 1), m_rows(state_mlstm_m[0]),
                                tc=dseq, nb=_row_tile(dbsz, 4))
    a_s = _fox(r3s(qa_s), kc, vc, r3s(ka_s), r3s(va_s), tq=dseq, tr=dseq, pblk=_row_tile(past, FOX_WIDE_KEYS), hps=FOX_HEADS)

    wfp, wmp, wo = w_fox_proj[0].astype(BF16), w_ml_proj[0].astype(BF16), w_out[0].astype(BF16)
    wgu, wd = w_gate_up[0].astype(BF16), w_down[0].astype(BF16)
    g1, g2, g3 = g_post_mix[0][None, :], g_pre_ffn[0][None, :], g_post_ffn[0][None, :]
    post = lambda a, hm, ga, gb, x, tm: _post(a, hm, ga, gb, x, wfp, wmp, wo, wgu, wd, g1, g2, g3, tm)
    y_p = post(a_p.reshape(bsz * seq, fw), hm_p.reshape(bsz * seq, mw), ga_p, gb_p, xp, _row_tile(bsz * seq, 512))
    y_s = post(a_s.reshape(dbsz * dseq, fw), hm_s.reshape(dbsz * dseq, mw), ga_s, gb_s, xs,
               _row_tile(dbsz * dseq, 512))

    def tokens_major(a):
        return a.reshape(bsz, FOX_HEADS, FOX_HEAD_DIM, a.shape[-1]).transpose(0, 3, 1, 2)[None]

    heads = lambda a: a.reshape(a.shape[:-1] + (FOX_HEADS, FOX_HEAD_DIM))
    tail = CONV_WIDTH - 1
    return (y_p.reshape(bsz, seq, d), y_s.reshape(dbsz, dseq, d),
            tokens_major(kt_p), tokens_major(vt_p), lt_p.transpose(0, 2, 1)[None],
            c_p[None], n_p[None, :, :ML_HEADS], m_p[None, :, :ML_HEADS, 0], ct_p[None, :, 8 - tail:],
            heads(r3s(kf_s))[None], heads(r3s(vf_s))[None], r3s(gt_s)[None, ..., :FOX_HEADS],
            c_s[None], n_s[None, :, :ML_HEADS], m_s[None, :, :ML_HEADS, 0], ct_s[None, :, 8 - tail:])
```
